```python
import jax, jax.numpy as jnp
from jax import lax
import numpy as np

D_MODEL = 1024
BATCH = 4
SEQ = 4096
DEPTH = 2

CHUNK = 64
RMS_EPS = 1e-6
MLSTM_WIDTH = D_MODEL // 2
MLSTM_HEADS = 4
MLSTM_HEAD_DIM = MLSTM_WIDTH // MLSTM_HEADS
MLSTM_STAB_INIT = -1e30
SCONV_WIDTH = D_MODEL - MLSTM_WIDTH
SCONV_KERNEL = 3
S5_WIDTH = D_MODEL // 2
S5_GROUP_CH = 16
S5_GROUPS = S5_WIDTH // S5_GROUP_CH
S5_STATE = 64
HGRN_WIDTH = D_MODEL - S5_WIDTH
HGRN_HEADS = 4
HGRN_HEAD_DIM = HGRN_WIDTH // HGRN_HEADS
MOE_GROUPS = 4
MOE_EXPERTS_PER_GROUP = 8
MOE_EXPERTS = MOE_GROUPS * MOE_EXPERTS_PER_GROUP
MOE_TOP_K = 2
MOE_FF = D_MODEL // 4
AB_SIZES = (MLSTM_WIDTH,) * 4 + (MLSTM_HEADS,) * 2 + (SCONV_WIDTH,) * 3
CD_SIZES = (S5_WIDTH,) + (HGRN_WIDTH,) * 4

kernel_name = "hybrid_mlstm_conv_s5_hgrn2_hmoe"


def _offsets(sizes):
    return [int(v) for v in np.cumsum(sizes)[:-1]]


def rmsnorm(x, w):
    xf = x.astype(jnp.float32)
    y = xf * lax.rsqrt(jnp.mean(xf * xf, axis=-1, keepdims=True) + RMS_EPS)
    return (y * w.astype(jnp.float32)).astype(x.dtype)


def head_rmsnorm(h, w):
    return h * lax.rsqrt(jnp.mean(h * h, axis=-1, keepdims=True) + RMS_EPS) * w.astype(jnp.float32)


def to_chunks(t):
    b, s, h = t.shape[:3]
    t = t.reshape((b, s // CHUNK, CHUNK, h) + t.shape[3:])
    return t.transpose((1, 0, 3, 2) + tuple(range(4, t.ndim)))


def from_chunks(t):
    nc, b, h, l = t.shape[:4]
    t = t.transpose((1, 0, 3, 2) + tuple(range(4, t.ndim)))
    return t.reshape((b, nc * l, h) + t.shape[4:])


def mlstm_chunkwise(q, k, v, i_pre, f_pre):
    b, s, h, dh = q.shape
    causal = jnp.tril(jnp.ones((CHUNK, CHUNK), dtype=bool))
    xs = (to_chunks(q), to_chunks(k), to_chunks(v), to_chunks(i_pre),
          to_chunks(jax.nn.log_sigmoid(f_pre)))

    def step(carry, chunk):
        c_mat, n_vec, m = carry
        q_, k_, v_, i_, lf = chunk
        bcum = jnp.cumsum(lf, axis=-1)
        logw = jnp.where(causal, bcum[..., :, None] - bcum[..., None, :] + i_[..., None, :], -jnp.inf)
        inter = bcum + m[..., None]
        m_row = jnp.maximum(jnp.max(logw, axis=-1), inter)
        scores = jnp.einsum('bhld,bhsd->bhls', q_, k_) * jnp.exp(logw - m_row[..., None])
        inter_scale = jnp.exp(inter - m_row)
        num = (jnp.einsum('bhls,bhsd->bhld', scores, v_)
               + inter_scale[..., None] * jnp.einsum('bhek,bhlk->bhle', c_mat, q_))
        den = jnp.sum(scores, axis=-1) + inter_scale * jnp.einsum('bhk,bhlk->bhl', n_vec, q_)
        h_out = num / jnp.maximum(jnp.abs(den), jnp.exp(-m_row))[..., None]
        b_end = bcum[..., -1]
        logg = b_end[..., None] - bcum + i_
        m_new = jnp.maximum(b_end + m, jnp.max(logg, axis=-1))
        wk = jnp.exp(logg - m_new[..., None])
        decay = jnp.exp(b_end + m - m_new)
        c_new = decay[..., None, None] * c_mat + jnp.einsum('bhl,bhle,bhlk->bhek', wk, v_, k_)
        n_new = decay[..., None] * n_vec + jnp.einsum('bhl,bhlk->bhk', wk, k_)
        return (c_new, n_new, m_new), h_out

    init = (jnp.zeros((b, h, dh, dh), jnp.float32), jnp.zeros((b, h, dh), jnp.float32),
            jnp.full((b, h), MLSTM_STAB_INIT, jnp.float32))
    _, hs = lax.scan(step, init, xs)
    return from_chunks(hs)


def causal_dwconv(z, w):
    c = z.shape[-1]
    return lax.conv_general_dilated(z, w[:, None, :].astype(z.dtype), window_strides=(1,),
                                    padding=[(SCONV_KERNEL - 1, 0)],
                                    dimension_numbers=('NWC', 'WIO', 'NWC'),
                                    feature_group_count=c)


def mixer_ab(xn, w_in, gate_bias, head_norm, conv_w, w_out):
    bsz, s, _ = xn.shape
    proj = xn @ w_in
    q, k, v, o, ig, fg, gb, gc, xin = jnp.split(proj, _offsets(AB_SIZES), axis=-1)
    heads = lambda t: t.astype(jnp.float32).reshape(bsz, s, MLSTM_HEADS, MLSTM_HEAD_DIM)
    gbias = gate_bias.astype(jnp.float32)
    i_pre = ig.astype(jnp.float32) + gbias[:MLSTM_HEADS]
    f_pre = fg.astype(jnp.float32) + gbias[MLSTM_HEADS:]
    hm = mlstm_chunkwise(heads(q), heads(k) * (MLSTM_HEAD_DIM ** -0.5), heads(v), i_pre, f_pre)
    hm = head_rmsnorm(hm, head_norm.reshape(MLSTM_HEADS, MLSTM_HEAD_DIM)) * jax.nn.sigmoid(heads(o))
    hm = hm.reshape(bsz, s, MLSTM_WIDTH).astype(xn.dtype)
    yc = gb * causal_dwconv(gc * xin, conv_w)
    return jnp.concatenate([hm, yc], axis=-1) @ w_out


def s5_mixer(u, lam_re, lam_im, b_re, b_im, c_re, c_im, d_skip, log_dt, w_glu, b_glu):
    f32 = jnp.float32
    bsz, s, _ = u.shape
    uf = u.astype(f32).reshape(bsz, s, S5_GROUPS, S5_GROUP_CH)
    lam = lax.complex(lam_re.astype(f32), lam_im.astype(f32))
    dt = jnp.exp(log_dt.astype(f32))[:, None]
    lam_bar = jnp.exp(lam * dt)
    b_bar = ((lam_bar - 1.0) / lam)[..., None] * lax.complex(b_re.astype(f32), b_im.astype(f32))
    bu = jnp.einsum('bsgc,gpc->bsgp', uf.astype(jnp.complex64), b_bar)
    a = jnp.broadcast_to(lam_bar, bu.shape)

    def combine(e1, e2):
        a1, x1 = e1
        a2, x2 = e2
        return a2 * a1, a2 * x1 + x2

    _, states = lax.associative_scan(combine, (a, bu), axis=1)
    cmat = lax.complex(c_re.astype(f32), c_im.astype(f32))
    y = jnp.real(jnp.einsum('bsgp,gcp->bsgc', states, cmat)) + d_skip.astype(f32).reshape(S5_GROUPS, S5_GROUP_CH) * uf
    z = jax.nn.gelu(y.reshape(bsz, s, S5_WIDTH))
    return z * jax.nn.sigmoid(z @ w_glu.astype(f32) + b_glu.astype(f32))


def hgrn2_chunkwise(q, log_f, k, v):
    b, s, h, dk = q.shape
    dv = v.shape[-1]
    causal = jnp.tril(jnp.ones((CHUNK, CHUNK), dtype=bool))

    def step(state, chunk):
        q_, lf, k_, v_ = chunk
        bcum = jnp.cumsum(lf, axis=-2)
        diff = bcum[..., :, None, :] - bcum[..., None, :, :]
        decay = jnp.exp(jnp.where(causal[:, :, None], diff, -jnp.inf))
        attn = jnp.einsum('bhjk,bhjsk,bhsk->bhjs', q_, decay, k_)
        o = (jnp.einsum('bhjs,bhsv->bhjv', attn, v_)
             + jnp.einsum('bhjk,bhkv->bhjv', q_ * jnp.exp(bcum), state))
        b_end = bcum[..., -1, :]
        state_new = (jnp.exp(b_end)[..., None] * state
                     + jnp.einsum('bhsk,bhsv->bhkv', k_ * jnp.exp(b_end[..., None, :] - bcum), v_))
        return state_new, o

    init = jnp.zeros((b, h, dk, dv), jnp.float32)
    _, os_ = lax.scan(step, init, (to_chunks(q), to_chunks(log_f), to_chunks(k), to_chunks(v)))
    return from_chunks(os_)


def hgrn_lower_bound(lb_param, layer):
    sm = jax.nn.softmax(lb_param.astype(jnp.float32), axis=0)
    return jnp.cumsum(sm, axis=0)[layer] - sm[0]


def mixer_cd(xn, w_in, lam_re, lam_im, b_re, b_im, c_re, c_im, d_skip, log_dt, w_glu, b_glu,
             lower_bound, out_norm, w_out):
    bsz, s, _ = xn.shape
    proj = xn @ w_in
    u, q, fg, ig, og = jnp.split(proj, _offsets(CD_SIZES), axis=-1)
    y_s5 = s5_mixer(u, lam_re, lam_im, b_re, b_im, c_re, c_im, d_skip, log_dt, w_glu, b_glu)
    heads = lambda t: t.astype(jnp.float32).reshape(bsz, s, HGRN_HEADS, HGRN_HEAD_DIM)
    lb = lower_bound.reshape(HGRN_HEADS, HGRN_HEAD_DIM)
    log_f = jnp.logaddexp(jnp.log(lb), jnp.log1p(-lb) + jax.nn.log_sigmoid(heads(fg)))
    k = -jnp.expm1(log_f)
    o = hgrn2_chunkwise(heads(q), log_f, k, heads(ig))
    o = head_rmsnorm(o, out_norm.reshape(HGRN_HEADS, HGRN_HEAD_DIM)) * jax.nn.silu(heads(og))
    o = o.reshape(bsz, s, HGRN_WIDTH)
    mixed = jnp.concatenate([y_s5.astype(xn.dtype), o.astype(xn.dtype)], axis=-1)
    return mixed @ w_out


def hier_moe(xn, w_group, b_group, w_router, b_router, w_gate, w_up, w_down):
    f32 = jnp.float32
    bsz, s, d = xn.shape
    xf = xn.reshape(-1, d)
    t = xf.shape[0]
    grp_prob = jax.nn.softmax(xf.astype(f32) @ w_group.astype(f32) + b_group.astype(f32), axis=-1)
    g_val, g_idx = lax.top_k(grp_prob, 1)
    exp_logits = (xf.astype(f32) @ w_router.astype(f32) + b_router.astype(f32)).reshape(
        t, MOE_GROUPS, MOE_EXPERTS_PER_GROUP)
    sel = jnp.take_along_axis(exp_logits, g_idx[:, :, None], axis=1)[:, 0]
    e_val, e_idx = lax.top_k(jax.nn.softmax(sel, axis=-1), MOE_TOP_K)
    weights = g_val * e_val / jnp.sum(e_val, axis=-1, keepdims=True)
    expert_id = g_idx * MOE_EXPERTS_PER_GROUP + e_idx
    combine = jnp.einsum('tk,tke->te', weights, jax.nn.one_hot(expert_id, MOE_EXPERTS, dtype=f32))
    y = jnp.zeros((t, d), f32)
    for e in range(MOE_EXPERTS):
        hid = jax.nn.silu(xf @ w_gate[e]) * (xf @ w_up[e])
        y = y + combine[:, e:e + 1] * (hid @ w_down[e]).astype(f32)
    return y.astype(xn.dtype).reshape(bsz, s, d)


def setup_inputs(seed: int = 0) -> dict:
    key = jax.random.key(seed)
    ks = jax.random.split(key, 32)
    f32 = jnp.float32
    n_even = (DEPTH + 1) // 2
    n_odd = DEPTH // 2
    nrm = lambda k, shape, scale: jax.random.normal(k, shape, f32) * scale
    gain = lambda k, shape: 1.0 + nrm(k, shape, 0.05)
    d = D_MODEL
    gate_bias = jnp.concatenate(
        [nrm(ks[5], (n_even, MLSTM_HEADS), 0.1),
         jnp.linspace(3.0, 6.0, MLSTM_HEADS, dtype=f32)[None] + nrm(ks[6], (n_even, MLSTM_HEADS), 0.1)], axis=-1)
    lam_im = jnp.pi * jnp.arange(S5_STATE, dtype=f32)
    return {
        "x": nrm(ks[0], (BATCH, SEQ, d), 1.0),
        "norm_mix": gain(ks[1], (DEPTH, d)),
        "norm_ffn": gain(ks[2], (DEPTH, d)),
        "norm_final": gain(ks[3], (d,)),
        "ab_w_in": nrm(ks[4], (n_even, d, sum(AB_SIZES)), d ** -0.5),
        "ab_gate_bias": gate_bias,
        "ab_head_norm": gain(ks[7], (n_even, MLSTM_WIDTH)),
        "ab_conv_w": nrm(ks[8], (n_even, SCONV_KERNEL, SCONV_WIDTH), SCONV_KERNEL ** -0.5),
        "ab_w_out": nrm(ks[9], (n_even, MLSTM_WIDTH + SCONV_WIDTH, d), d ** -0.5),
        "cd_w_in": nrm(ks[10], (n_odd, d, sum(CD_SIZES)), d ** -0.5),
        "s5_lambda_re": -0.5 + nrm(ks[11], (n_odd, S5_GROUPS, S5_STATE), 0.01),
        "s5_lambda_im": lam_im + nrm(ks[12], (n_odd, S5_GROUPS, S5_STATE), 0.01),
        "s5_b_re": nrm(ks[13], (n_odd, S5_GROUPS, S5_STATE, S5_GROUP_CH), (2 * S5_GROUP_CH) ** -0.5),
        "s5_b_im": nrm(ks[14], (n_odd, S5_GROUPS, S5_STATE, S5_GROUP_CH), (2 * S5_GROUP_CH) ** -0.5),
        "s5_c_re": nrm(ks[15], (n_odd, S5_GROUPS, S5_GROUP_CH, S5_STATE), 0.5),
        "s5_c_im": nrm(ks[16], (n_odd, S5_GROUPS, S5_GROUP_CH, S5_STATE), 0.5),
        "s5_d": nrm(ks[17], (n_odd, S5_WIDTH), 1.0),
        "s5_log_dt": jax.random.uniform(ks[18], (n_odd, S5_GROUPS), f32, np.log(0.001), np.log(0.1)),
        "s5_w_glu": nrm(ks[19], (n_odd, S5_WIDTH, S5_WIDTH), S5_WIDTH ** -0.5),
        "s5_b_glu": nrm(ks[20], (n_odd, S5_WIDTH), 0.01),
        "hgrn_lb": nrm(ks[21], (DEPTH, HGRN_WIDTH), 0.1),
        "hgrn_out_norm": gain(ks[22], (n_odd, HGRN_WIDTH)),
        "cd_w_out": nrm(ks[23], (n_odd, S5_WIDTH + HGRN_WIDTH, d), d ** -0.5),
        "moe_w_group": nrm(ks[24], (DEPTH, d, MOE_GROUPS), d ** -0.5),
        "moe_b_group": nrm(ks[25], (DEPTH, MOE_GROUPS), 0.01),
        "moe_w_router": nrm(ks[26], (DEPTH, d, MOE_EXPERTS), d ** -0.5),
        "moe_b_router": nrm(ks[27], (DEPTH, MOE_EXPERTS), 0.01),
        "moe_w_gate": nrm(ks[28], (DEPTH, MOE_EXPERTS, d, MOE_FF), d ** -0.5),
        "moe_w_up": nrm(ks[29], (DEPTH, MOE_EXPERTS, d, MOE_FF), d ** -0.5),
        "moe_w_down": nrm(ks[30], (DEPTH, MOE_EXPERTS, MOE_FF, d), MOE_FF ** -0.5),
    }


def reference(x, norm_mix, norm_ffn, norm_final, ab_w_in, ab_gate_bias, ab_head_norm, ab_conv_w,
              ab_w_out, cd_w_in, s5_lambda_re, s5_lambda_im, s5_b_re, s5_b_im, s5_c_re, s5_c_im,
              s5_d, s5_log_dt, s5_w_glu, s5_b_glu, hgrn_lb, hgrn_out_norm, cd_w_out,
              moe_w_group, moe_b_group, moe_w_router, moe_b_router, moe_w_gate, moe_w_up,
              moe_w_down):
    h = x
    for layer in range(DEPTH):
        j = layer // 2
        hn = rmsnorm(h, norm_mix[layer])
        if layer % 2 == 0:
            mix = mixer_ab(hn, ab_w_in[j], ab_gate_bias[j], ab_head_norm[j], ab_conv_w[j], ab_w_out[j])
        else:
            mix = mixer_cd(hn, cd_w_in[j], s5_lambda_re[j], s5_lambda_im[j], s5_b_re[j], s5_b_im[j],
                           s5_c_re[j], s5_c_im[j], s5_d[j], s5_log_dt[j], s5_w_glu[j], s5_b_glu[j],
                           hgrn_lower_bound(hgrn_lb, layer), hgrn_out_norm[j], cd_w_out[j])
        h = h + mix
        h = h + hier_moe(rmsnorm(h, norm_ffn[layer]), moe_w_group[layer], moe_b_group[layer],
                         moe_w_router[layer], moe_b_router[layer], moe_w_gate[layer],
                         moe_w_up[layer], moe_w_down[layer])
    return rmsnorm(h, norm_final)
```

```python
import functools

import numpy as np
import jax
import jax.numpy as jnp
from jax import lax
from jax.experimental import pallas as pl
from jax.experimental.pallas import tpu as pltpu

F32 = jnp.float32
BF16 = jnp.bfloat16
RMS_EPS = 1e-6
CHUNK = 64
S5_CHUNK = 16
S5_GROUP_CH = 16
S5_STATE = 64
MLSTM_HEADS = 4
HGRN_HEADS = 4
MOE_GROUPS = 4
MOE_EPG = 8
ROUTE_LANES = 128
TM_PROJ = 512
TM_MOE = 1024
VMEM_LIMIT = 56 * 1024 * 1024

_NT = (((1,), (1,)), ((), ()))
_TN = (((0,), (0,)), ((), ()))


def _cparams(sem):
    return pltpu.CompilerParams(dimension_semantics=sem, vmem_limit_bytes=VMEM_LIMIT)


def _rms(x, w):
    return x * lax.rsqrt(jnp.mean(x * x, axis=-1, keepdims=True) + RMS_EPS) * w


def _split3(x):
    hi = x.astype(BF16)
    r = x - hi.astype(F32)
    mid = r.astype(BF16)
    lo = (r - mid.astype(F32)).astype(BF16)
    return hi, mid, lo


def _dot(a, b):
    return jnp.dot(a, b, preferred_element_type=F32)


def _dot_nt(a, b):
    return lax.dot_general(a, b, _NT, preferred_element_type=F32)


def _dot_tn(a, b):
    return lax.dot_general(a, b, _TN, preferred_element_type=F32)


def _proj_gates_body(x_ref, nw_ref, w_ref, wg_ref, wgt_ref, main_ref, g_ref, gt_ref):
    xn = _rms(x_ref[...], nw_ref[...]).astype(BF16)
    main_ref[...] = _dot(xn, w_ref[...])
    g_ref[...] = _dot(xn, wg_ref[...])[:, : g_ref.shape[1]]
    gt_ref[...] = _dot_nt(wgt_ref[...], xn)


def _proj_body(x_ref, nw_ref, w_ref, main_ref):
    xn = _rms(x_ref[...], nw_ref[...]).astype(BF16)
    main_ref[...] = _dot(xn, w_ref[...])


def _proj(h, nw, w_main, w_gates=None):
    t, d = h.shape
    n = w_main.shape[1]
    tm = TM_PROJ
    x_spec = pl.BlockSpec((tm, d), lambda i: (i, 0))
    nw_spec = pl.BlockSpec((1, d), lambda i: (0, 0))
    w_spec = pl.BlockSpec((d, n), lambda i: (0, 0))
    main_spec = pl.BlockSpec((tm, n), lambda i: (i, 0))
    main_shape = jax.ShapeDtypeStruct((t, n), F32)
    if w_gates is None:
        return pl.pallas_call(
            _proj_body, grid=(t // tm,), in_specs=[x_spec, nw_spec, w_spec], out_specs=main_spec,
            out_shape=main_shape, compiler_params=_cparams(("parallel",)), name="proj",
        )(h, nw.reshape(1, d), w_main)
    ng = w_gates.shape[1]
    wg_pad = jnp.zeros((d, 128), BF16).at[:, :ng].set(w_gates)
    return pl.pallas_call(
        _proj_gates_body, grid=(t // tm,),
        in_specs=[x_spec, nw_spec, w_spec, pl.BlockSpec((d, 128), lambda i: (0, 0)),
                  pl.BlockSpec((ng, d), lambda i: (0, 0))],
        out_specs=[main_spec, pl.BlockSpec((tm, ng), lambda i: (i, 0)), pl.BlockSpec((ng, tm), lambda i: (0, i))],
        out_shape=[main_shape, jax.ShapeDtypeStruct((t, ng), F32), jax.ShapeDtypeStruct((ng, t), F32)],
        compiler_params=_cparams(("parallel",)), name="proj_gates",
    )(h, nw.reshape(1, d), w_main, wg_pad, w_gates.T)


def _mlstm_body(main_ref, g_ref, gt_ref, br_ref, bc_ref, hn_ref, out_ref, c_ref, n_ref, m_ref, *, nb, nh, dh):
    L = CHUNK
    w = nh * dh

    @pl.when(pl.program_id(0) == 0)
    def _init():
        c_ref[...] = jnp.zeros_like(c_ref)
        n_ref[...] = jnp.zeros_like(n_ref)
        m_ref[...] = jnp.full_like(m_ref, -1e30)

    row = lax.broadcasted_iota(jnp.int32, (L, L), 0)
    col = lax.broadcasted_iota(jnp.int32, (L, L), 1)
    causal = col <= row
    tril = causal.astype(BF16)
    triu = (row <= col).astype(BF16)
    scale = dh ** -0.5
    for b in range(nb):
        g = g_ref[b] + br_ref[...]
        gt = gt_ref[b, 0] + bc_ref[...]
        i_c = g[:, :nh]
        i_r = gt[:nh, :]
        lfc = _split3(jax.nn.log_sigmoid(g[:, nh:]))
        lfr = _split3(jax.nn.log_sigmoid(gt[nh:, :]))
        bc_all = _dot(tril, lfc[0]) + _dot(tril, lfc[1]) + _dot(tril, lfc[2])
        br_all = _dot(lfr[0], triu) + _dot(lfr[1], triu) + _dot(lfr[2], triu)
        for h in range(nh):
            idx = b * nh + h
            q = main_ref[b, :, h * dh:(h + 1) * dh]
            k = main_ref[b, :, w + h * dh:w + (h + 1) * dh] * scale
            v = main_ref[b, :, 2 * w + h * dh:2 * w + (h + 1) * dh]
            o = main_ref[b, :, 3 * w + h * dh:3 * w + (h + 1) * dh]
            bc = bc_all[:, h:h + 1]
            br = br_all[h:h + 1, :]
            ir = i_r[h:h + 1, :]
            ic = i_c[:, h:h + 1]
            m_prev = m_ref[idx]
            c_prev = c_ref[idx]
            n_prev = n_ref[idx]
            logw = jnp.where(causal, bc - br + ir, -jnp.inf)
            inter = bc + m_prev
            m_row = jnp.maximum(jnp.max(logw, axis=-1, keepdims=True), inter)
            qb = q.astype(BF16)
            kb = k.astype(BF16)
            vb = v.astype(BF16)
            s = _dot_nt(qb, kb) * jnp.exp(logw - m_row)
            isc = jnp.exp(inter - m_row)
            num = _dot(s.astype(BF16), vb) + isc * _dot_nt(qb, c_prev.astype(BF16))
            den = jnp.sum(s, axis=-1, keepdims=True) + isc * jnp.sum(q * n_prev, axis=-1, keepdims=True)
            hout = num / jnp.maximum(jnp.abs(den), jnp.exp(-m_row))
            b_end = bc[L - 1:L, :]
            logg = b_end - bc + ic
            m_new = jnp.maximum(b_end + m_prev, jnp.max(logg, axis=0, keepdims=True))
            wk = jnp.exp(logg - m_new)
            decay = jnp.exp(b_end + m_prev - m_new)
            c_ref[idx] = decay * c_prev + _dot_tn((v * wk).astype(BF16), kb)
            n_ref[idx] = decay * n_prev + jnp.sum(wk * k, axis=0, keepdims=True)
            m_ref[idx] = m_new
            hn = hout * lax.rsqrt(jnp.mean(hout * hout, axis=-1, keepdims=True) + RMS_EPS)
            out_ref[b, :, h * dh:(h + 1) * dh] = hn * hn_ref[:, h * dh:(h + 1) * dh] * jax.nn.sigmoid(o)


def _mlstm(main, g, gt, gate_bias, head_norm, nb, seq):
    t, n = main.shape
    nh = MLSTM_HEADS
    w = head_norm.shape[0]
    dh = w // nh
    nc = seq // CHUNK
    main3 = main.reshape(nb, seq, n)
    g3 = g.reshape(nb, seq, 2 * nh)
    gt4 = gt.reshape(2 * nh, nb, nc, CHUNK).transpose(1, 2, 0, 3)
    body = functools.partial(_mlstm_body, nb=nb, nh=nh, dh=dh)
    out = pl.pallas_call(
        body, grid=(nc,),
        in_specs=[pl.BlockSpec((nb, CHUNK, 4 * w), lambda c: (0, c, 0)),
                  pl.BlockSpec((nb, CHUNK, 2 * nh), lambda c: (0, c, 0)),
                  pl.BlockSpec((nb, 1, 2 * nh, CHUNK), lambda c: (0, c, 0, 0)),
                  pl.BlockSpec((1, 2 * nh), lambda c: (0, 0)),
                  pl.BlockSpec((2 * nh, 1), lambda c: (0, 0)),
                  pl.BlockSpec((1, w), lambda c: (0, 0))],
        out_specs=pl.BlockSpec((nb, CHUNK, w), lambda c: (0, c, 0)),
        out_shape=jax.ShapeDtypeStruct((nb, seq, w), F32),
        scratch_shapes=[pltpu.VMEM((nb * nh, dh, dh), F32), pltpu.VMEM((nb * nh, 1, dh), F32),
                        pltpu.VMEM((nb * nh, 1, 1), F32)],
        compiler_params=_cparams(("arbitrary",)), name="mlstm",
    )(main3, g3, gt4, gate_bias.reshape(1, 2 * nh), gate_bias.reshape(2 * nh, 1), head_norm.reshape(1, w))
    return out.reshape(t, w)


def _out_a_body(h_ref, hm_ref, gb_ref, gc_ref, xin_ref, pgc_ref, pxin_ref, cw_ref, w_ref, out_ref, *, tm, seq):
    i = pl.program_id(0)
    wm = hm_ref.shape[1]
    p = gc_ref[...] * xin_ref[...]
    first = (i * tm) % seq == 0
    pp = jnp.where(first, 0.0, pgc_ref[...] * pxin_ref[...])
    rowi = lax.broadcasted_iota(jnp.int32, p.shape, 0)
    p1 = jnp.where(rowi == 0, pp[7:8, :], pltpu.roll(p, 1, axis=0))
    p2 = jnp.where(rowi == 0, pp[6:7, :], jnp.where(rowi == 1, pp[7:8, :], pltpu.roll(p, 2, axis=0)))
    yc = gb_ref[...] * (cw_ref[0:1, :] * p2 + cw_ref[1:2, :] * p1 + cw_ref[2:3, :] * p)
    out_ref[...] = (h_ref[...] + _dot(hm_ref[...].astype(BF16), w_ref[:wm, :])
                    + _dot(yc.astype(BF16), w_ref[wm:, :]))


def _out_a(h, hm, main, conv_w, w_out, seq):
    t, d = h.shape
    wm = hm.shape[1]
    wc = conv_w.shape[1]
    tm = TM_PROJ
    cb = (4 * wm) // wc
    rb = tm // 8
    prev = lambda i: jnp.maximum(i * rb - 1, 0)
    body = functools.partial(_out_a_body, tm=tm, seq=seq)
    return pl.pallas_call(
        body, grid=(t // tm,),
        in_specs=[pl.BlockSpec((tm, d), lambda i: (i, 0)),
                  pl.BlockSpec((tm, wm), lambda i: (i, 0)),
                  pl.BlockSpec((tm, wc), lambda i: (i, cb)),
                  pl.BlockSpec((tm, wc), lambda i: (i, cb + 1)),
                  pl.BlockSpec((tm, wc), lambda i: (i, cb + 2)),
                  pl.BlockSpec((8, wc), lambda i: (prev(i), cb + 1)),
                  pl.BlockSpec((8, wc), lambda i: (prev(i), cb + 2)),
                  pl.BlockSpec(conv_w.shape, lambda i: (0, 0)),
                  pl.BlockSpec(w_out.shape, lambda i: (0, 0))],
        out_specs=pl.BlockSpec((tm, d), lambda i: (i, 0)),
        out_shape=jax.ShapeDtypeStruct((t, d), F32),
        compiler_params=_cparams(("parallel",)), name="out_a",
    )(h, hm, main, main, main, main, main, conv_w, w_out)


def _route(logits):
    ne = MOE_GROUPS * MOE_EPG
    lane = lax.broadcasted_iota(jnp.int32, logits.shape, 1).astype(F32)
    big = 1e9
    gl = jnp.where((lane >= ne) & (lane < ne + MOE_GROUPS), logits, -jnp.inf)
    gmax = jnp.max(gl, axis=-1, keepdims=True)
    gidx = jnp.min(jnp.where(gl == gmax, lane - ne, big), axis=-1, keepdims=True)
    gval = 1.0 / jnp.sum(jnp.exp(gl - gmax), axis=-1, keepdims=True)
    lo = gidx * MOE_EPG
    sel = jnp.where((lane >= lo) & (lane < lo + MOE_EPG), logits, -jnp.inf)
    l1 = jnp.max(sel, axis=-1, keepdims=True)
    i1 = jnp.min(jnp.where(sel == l1, lane, big), axis=-1, keepdims=True)
    sel2 = jnp.where(lane == i1, -jnp.inf, sel)
    l2 = jnp.max(sel2, axis=-1, keepdims=True)
    i2 = jnp.min(jnp.where(sel2 == l2, lane, big), axis=-1, keepdims=True)
    r = jnp.exp(l2 - l1)
    w1 = gval / (1.0 + r)
    return jnp.where(lane == i1, w1, 0.0) + jnp.where(lane == i2, w1 * r, 0.0)


def _moe_body(h_ref, nw_ref, wr_ref, br_ref, wg_ref, wu_ref, wd_ref, fw_ref, out_ref, xn_ref, cmb_ref, acc_ref, *,
              final):
    e = pl.program_id(1)

    @pl.when(e == 0)
    def _start():
        xn = _rms(h_ref[...], nw_ref[...])
        hi = xn.astype(BF16)
        lo = (xn - hi.astype(F32)).astype(BF16)
        xn_ref[...] = hi
        logits = _dot(jnp.concatenate([hi, lo, hi], axis=1), wr_ref[...]) + br_ref[...]
        cmb_ref[...] = _route(logits)
        acc_ref[...] = jnp.zeros_like(acc_ref)

    x = xn_ref[...]
    hid = jax.nn.silu(_dot(x, wg_ref[0])) * _dot(x, wu_ref[0])
    y = _dot(hid.astype(BF16), wd_ref[0])
    lane = lax.broadcasted_iota(jnp.int32, cmb_ref.shape, 1)
    ce = jnp.sum(jnp.where(lane == e, cmb_ref[...], 0.0), axis=-1, keepdims=True)
    acc_ref[...] += ce * y

    @pl.when(e == pl.num_programs(1) - 1)
    def _finish():
        o = h_ref[...] + acc_ref[...]
        if final:
            o = _rms(o, fw_ref[...])
        out_ref[...] = o


def _router_weights(w_group, b_group, w_router, b_router):
    d, ne = w_router.shape
    ng = w_group.shape[1]
    w = jnp.zeros((d, ROUTE_LANES), F32).at[:, :ne].set(w_router).at[:, ne:ne + ng].set(w_group)
    hi = w.astype(BF16)
    lo = (w - hi.astype(F32)).astype(BF16)
    bias = jnp.zeros((1, ROUTE_LANES), F32).at[0, :ne].set(b_router).at[0, ne:ne + ng].set(b_group)
    return jnp.concatenate([hi, hi, lo], axis=0), bias


def _moe(h, nw, w_group, b_group, w_router, b_router, w_gate, w_up, w_down, final_w):
    t, d = h.shape
    ne, _, ff = w_gate.shape
    tm = TM_MOE
    wr3, bias = _router_weights(w_group, b_group, w_router, b_router)
    final = final_w is not None
    fw = (final_w if final else nw).reshape(1, d)
    body = functools.partial(_moe_body, final=final)
    return pl.pallas_call(
        body, grid=(t // tm, ne),
        in_specs=[pl.BlockSpec((tm, d), lambda i, e: (i, 0)),
                  pl.BlockSpec((1, d), lambda i, e: (0, 0)),
                  pl.BlockSpec((3 * d, ROUTE_LANES), lambda i, e: (0, 0)),
                  pl.BlockSpec((1, ROUTE_LANES), lambda i, e: (0, 0)),
                  pl.BlockSpec((1, d, ff), lambda i, e: (e, 0, 0)),
                  pl.BlockSpec((1, d, ff), lambda i, e: (e, 0, 0)),
                  pl.BlockSpec((1, ff, d), lambda i, e: (e, 0, 0)),
                  pl.BlockSpec((1, d), lambda i, e: (0, 0))],
        out_specs=pl.BlockSpec((tm, d), lambda i, e: (i, 0)),
        out_shape=jax.ShapeDtypeStruct((t, d), F32),
        scratch_shapes=[pltpu.VMEM((tm, d), BF16), pltpu.VMEM((tm, ROUTE_LANES), F32), pltpu.VMEM((tm, d), F32)],
        compiler_params=_cparams(("parallel", "arbitrary")), name="moe",
    )(h, nw.reshape(1, d), wr3, bias, w_gate.astype(BF16), w_up.astype(BF16), w_down.astype(BF16), fw)


def _s5_weights(lam_re, lam_im, b_re, b_im, c_re, c_im, log_dt, nsteps):
    hp = lax.Precision.HIGHEST
    L = S5_CHUNK
    g, p = lam_re.shape
    ch = b_re.shape[-1]
    lam = lax.complex(lam_re.astype(F32), lam_im.astype(F32))
    dt = jnp.exp(log_dt.astype(F32))[:, None]
    lam_bar = jnp.exp(lam * dt)
    b_bar = ((lam_bar - 1.0) / lam)[..., None] * lax.complex(b_re.astype(F32), b_im.astype(F32))
    cmat = lax.complex(c_re.astype(F32), c_im.astype(F32))
    pows = [jnp.ones_like(lam_bar)]
    for _ in range(L):
        pows.append(pows[-1] * lam_bar)
    pw = jnp.stack(pows, axis=1)
    kern = jnp.real(jnp.einsum('gop,gtp,gpi->gtoi', cmat, pw[:, :L], b_bar, precision=hp))
    lag = np.arange(L)[None, :] - np.arange(L)[:, None]
    toep = jnp.where((lag >= 0)[None, :, :, None, None], kern[:, np.maximum(lag, 0)], 0.0)
    toep = toep.transpose(0, 1, 4, 2, 3).reshape(g, L * ch, L * ch)
    wst = pw[:, L - 1 - np.arange(L)][:, :, :, None] * b_bar[:, None, :, :]
    wst = wst.transpose(0, 1, 3, 2).reshape(g, L * ch, p)
    wst = jnp.concatenate([jnp.real(wst), jnp.imag(wst)], axis=-1)
    mo = cmat.transpose(0, 2, 1)[:, :, None, :] * pw[:, 1:L + 1].transpose(0, 2, 1)[:, :, :, None]
    mo = mo.reshape(g, p, L * ch)
    wout = jnp.concatenate([jnp.real(mo), -jnp.imag(mo)], axis=1)
    a = pw[:, L]
    ars, ais = [], []
    for _ in range(nsteps):
        ars.append(jnp.concatenate([jnp.real(a), jnp.real(a)], axis=-1))
        ais.append(jnp.concatenate([-jnp.imag(a), jnp.imag(a)], axis=-1))
        a = a * a
    wcat = jnp.concatenate([toep, wst], axis=-1).astype(BF16)
    return wcat, wout.astype(BF16), jnp.stack(ars, axis=1), jnp.stack(ais, axis=1)


def _s5_body(u_ref, wcat_ref, wout_ref, ar_ref, ai_ref, y_ref, *, nb, nsteps):
    ny = y_ref.shape[2]
    r = _dot(u_ref[0], wcat_ref[0])
    y1 = r[:, :ny]
    z = r[:, ny:]
    rows, lanes = z.shape
    ridx = lax.broadcasted_iota(jnp.int32, z.shape, 0)

    def shift(x, k):
        return jnp.where(ridx >= k, pltpu.roll(x, k, axis=0), 0.0)

    w = shift(z, nb)
    for k in range(nsteps):
        if (nb << k) >= rows:
            break
        s = shift(w, nb << k)
        w = w + s * ar_ref[0, k:k + 1, :] + pltpu.roll(s, lanes // 2, axis=1) * ai_ref[0, k:k + 1, :]
    y_ref[0] = y1 + _dot(w.astype(BF16), wout_ref[0])


def _s5(u, nb, seq, lam_re, lam_im, b_re, b_im, c_re, c_im, log_dt):
    t, width = u.shape
    L = S5_CHUNK
    ch = S5_GROUP_CH
    g = width // ch
    nc = seq // L
    nsteps = max(1, (nc - 1).bit_length())
    wcat, wout, ar, ai = _s5_weights(lam_re, lam_im, b_re, b_im, c_re, c_im, log_dt, nsteps)
    ug = u.reshape(nb, nc, L, g, ch).transpose(3, 1, 0, 2, 4).reshape(g, nc * nb, L * ch).astype(BF16)
    rows = nc * nb
    body = functools.partial(_s5_body, nb=nb, nsteps=nsteps)
    y = pl.pallas_call(
        body, grid=(g,),
        in_specs=[pl.BlockSpec((1, rows, L * ch), lambda i: (i, 0, 0)),
                  pl.BlockSpec((1,) + wcat.shape[1:], lambda i: (i, 0, 0)),
                  pl.BlockSpec((1,) + wout.shape[1:], lambda i: (i, 0, 0)),
                  pl.BlockSpec((1,) + ar.shape[1:], lambda i: (i, 0, 0)),
                  pl.BlockSpec((1,) + ai.shape[1:], lambda i: (i, 0, 0))],
        out_specs=pl.BlockSpec((1, rows, L * ch), lambda i: (i, 0, 0)),
        out_shape=jax.ShapeDtypeStruct((g, rows, L * ch), F32),
        compiler_params=_cparams(("parallel",)), name="s5",
    )(ug, wcat, wout, ar, ai)
    return y.reshape(g, nc, nb, L, ch).transpose(2, 1, 3, 0, 4).reshape(t, width)


def _hgrn_gmat():
    L = CHUNK
    blocks = 2 + int(np.log2(L))
    gm = np.zeros((blocks * L, L), np.float32)
    for j in range(L):
        gm[j, :j + 1] = 1.0
        gm[L + j, j + 1:] = 1.0
    li, m = 2, L
    while m >= 2:
        half = m // 2
        for j in range(L):
            pos = j % m
            r = j - pos + half - 1
            if pos >= half:
                gm[li * L + j, r + 1:j + 1] = 1.0
            else:
                gm[li * L + j, j + 1:r + 1] = 1.0
        li += 1
        m //= 2
    return gm


def _hgrn_body(main_ref, gm_ref, lb_ref, nw_ref, out_ref, st_ref, *, nb, nh, dh):
    L = CHUNK
    w = nh * dh

    @pl.when(pl.program_id(0) == 0)
    def _init():
        st_ref[...] = jnp.zeros_like(st_ref)

    row = lax.broadcasted_iota(jnp.int32, (L, L), 0)
    col = lax.broadcasted_iota(jnp.int32, (L, L), 1)
    rowd = lax.broadcasted_iota(jnp.int32, (L, dh), 0)
    eye = row == col
    gm = gm_ref[...]
    lb = lb_ref[...]
    for b in range(nb):
        fg = main_ref[b, :, w:2 * w]
        f = lb + (1.0 - lb) * jax.nn.sigmoid(fg)
        kk = (1.0 - lb) * jax.nn.sigmoid(-fg)
        lf = _split3(jnp.log(f))
        p_all = jnp.exp(_dot(gm, lf[0]) + _dot(gm, lf[1]) + _dot(gm, lf[2]))
        for h in range(nh):
            idx = b * nh + h
            hs = slice(h * dh, (h + 1) * dh)
            q = main_ref[b, :, h * dh:(h + 1) * dh]
            v = main_ref[b, :, 2 * w + h * dh:2 * w + (h + 1) * dh]
            og = main_ref[b, :, 3 * w + h * dh:3 * w + (h + 1) * dh]
            k = kk[:, hs]
            pb = p_all[0:L, hs]
            pe = p_all[L:2 * L, hs]
            st = st_ref[idx]
            o = _dot_nt((q * pb).astype(BF16), st.astype(BF16))
            attn = jnp.where(eye, _dot_nt(q.astype(BF16), k.astype(BF16)), 0.0)
            li, m = 2, L
            while m >= 2:
                pl_ = p_all[li * L:(li + 1) * L, hs]
                up = (rowd & (m - 1)) >= (m // 2)
                ql = jnp.where(up, q * pl_, 0.0).astype(BF16)
                kl = jnp.where(up, 0.0, k * pl_).astype(BF16)
                same = (row & ~(m - 1)) == (col & ~(m - 1))
                attn = attn + jnp.where(same, _dot_nt(ql, kl), 0.0)
                li += 1
                m //= 2
            vb = v.astype(BF16)
            o = o + _dot(attn.astype(BF16), vb)
            st_ref[idx] = st * pb[L - 1:L, :] + _dot_tn(vb, (k * pe).astype(BF16))
            on = o * lax.rsqrt(jnp.mean(o * o, axis=-1, keepdims=True) + RMS_EPS)
            out_ref[b, :, hs] = on * nw_ref[:, hs] * jax.nn.silu(og)


def _hgrn(main, lower_bound, out_norm, nb, seq):
    t, n = main.shape
    nh = HGRN_HEADS
    w = out_norm.shape[0]
    dh = w // nh
    nc = seq // CHUNK
    gm = jnp.asarray(_hgrn_gmat(), BF16)
    body = functools.partial(_hgrn_body, nb=nb, nh=nh, dh=dh)
    out = pl.pallas_call(
        body, grid=(nc,),
        in_specs=[pl.BlockSpec((nb, CHUNK, 4 * w), lambda c: (0, c, 0)),
                  pl.BlockSpec(gm.shape, lambda c: (0, 0)),
                  pl.BlockSpec((1, w), lambda c: (0, 0)),
                  pl.BlockSpec((1, w), lambda c: (0, 0))],
        out_specs=pl.BlockSpec((nb, CHUNK, w), lambda c: (0, c, 0)),
        out_shape=jax.ShapeDtypeStruct((nb, seq, w), F32),
        scratch_shapes=[pltpu.VMEM((nb * nh, dh, dh), F32)],
        compiler_params=_cparams(("arbitrary",)), name="hgrn",
    )(main.reshape(nb, seq, n), gm, lower_bound.reshape(1, w), out_norm.reshape(1, w))
    return out.reshape(t, w)


def _out_c_body(h_ref, ys_ref, u_ref, oh_ref, d_ref, wglu_ref, bglu_ref, w_ref, out_ref):
    ws = ys_ref.shape[1]
    z = jax.nn.gelu(ys_ref[...] + d_ref[...] * u_ref[...])
    gate = jax.nn.sigmoid(_dot(z.astype(BF16), wglu_ref[...]) + bglu_ref[...])
    out_ref[...] = (h_ref[...] + _dot((z * gate).astype(BF16), w_ref[:ws, :])
                    + _dot(oh_ref[...].astype(BF16), w_ref[ws:, :]))


def _out_c(h, ys, main, oh, d_skip, w_glu, b_glu, w_out):
    t, d = h.shape
    ws = ys.shape[1]
    wh = oh.shape[1]
    tm = TM_PROJ
    ub = (main.shape[1] - ws) // ws
    return pl.pallas_call(
        _out_c_body, grid=(t // tm,),
        in_specs=[pl.BlockSpec((tm, d), lambda i: (i, 0)),
                  pl.BlockSpec((tm, ws), lambda i: (i, 0)),
                  pl.BlockSpec((tm, ws), lambda i: (i, ub)),
                  pl.BlockSpec((tm, wh), lambda i: (i, 0)),
                  pl.BlockSpec((1, ws), lambda i: (0, 0)),
                  pl.BlockSpec(w_glu.shape, lambda i: (0, 0)),
                  pl.BlockSpec((1, ws), lambda i: (0, 0)),
                  pl.BlockSpec(w_out.shape, lambda i: (0, 0))],
        out_specs=pl.BlockSpec((tm, d), lambda i: (i, 0)),
        out_shape=jax.ShapeDtypeStruct((t, d), F32),
        compiler_params=_cparams(("parallel",)), name="out_c",
    )(h, ys, main, oh, d_skip.reshape(1, ws), w_glu, b_glu.reshape(1, ws), w_out)


def kernel(x, norm_mix, norm_ffn, norm_final, ab_w_in, ab_gate_bias, ab_head_norm, ab_conv_w, ab_w_out, cd_w_in, s5_lambda_re, s5_lambda_im, s5_b_re, s5_b_im, s5_c_re, s5_c_im, s5_d, s5_log_dt, s5_w_glu, s5_b_glu, hgrn_lb, hgrn_out_norm, cd_w_out, moe_w_group, moe_b_group, moe_w_router, moe_b_router, moe_w_gate, moe_w_up, moe_w_down):
    nb, seq, d = x.shape
    depth = norm_mix.shape[0]
    h = x.reshape(nb * seq, d)
    for layer in range(depth):
        j = layer // 2
        if layer % 2 == 0:
            wm = ab_head_norm.shape[1]
            ng = ab_gate_bias.shape[1]
            w_in = ab_w_in[j]
            w_main = jnp.concatenate([w_in[:, :4 * wm], w_in[:, 4 * wm + ng:]], axis=1).astype(BF16)
            w_gates = w_in[:, 4 * wm:4 * wm + ng].astype(BF16)
            main, g, gt = _proj(h, norm_mix[layer], w_main, w_gates)
            hm = _mlstm(main, g, gt, ab_gate_bias[j], ab_head_norm[j], nb, seq)
            h = _out_a(h, hm, main, ab_conv_w[j], ab_w_out[j].astype(BF16), seq)
        else:
            ws = s5_d.shape[1]
            w_in = cd_w_in[j]
            w_main = jnp.concatenate([w_in[:, ws:], w_in[:, :ws]], axis=1).astype(BF16)
            main = _proj(h, norm_mix[layer], w_main)
            sm = jax.nn.softmax(hgrn_lb.astype(F32), axis=0)
            lower_bound = jnp.cumsum(sm, axis=0)[layer] - sm[0]
            ys = _s5(main[:, main.shape[1] - ws:], nb, seq, s5_lambda_re[j], s5_lambda_im[j], s5_b_re[j], s5_b_im[j],
                     s5_c_re[j], s5_c_im[j], s5_log_dt[j])
            oh = _hgrn(main, lower_bound, hgrn_out_norm[j], nb, seq)
            h = _out_c(h, ys, main, oh, s5_d[j], s5_w_glu[j].astype(BF16), s5_b_glu[j], cd_w_out[j].astype(BF16))
        h = _moe(h, norm_ffn[layer], moe_w_group[layer], moe_b_group[layer], moe_w_router[layer], moe_b_router[layer],
                 moe_w_gate[layer], moe_w_up[layer], moe_w_down[layer],
                 norm_final if layer == depth - 1 else None)
    return h.reshape(nb, seq, d)
```

```python
import functools

import numpy as np
import jax
import jax.numpy as jnp
from jax import lax
from jax.experimental import pallas as pl
from jax.experimental.pallas import tpu as pltpu

F32 = jnp.float32
BF16 = jnp.bfloat16
RMS_EPS = 1e-6
CHUNK = 64
S5_CHUNK = 16
S5_GROUP_CH = 16
S5_STATE = 64
MLSTM_HEADS = 4
HGRN_HEADS = 4
MOE_GROUPS = 4
MOE_EPG = 8
ROUTE_LANES = 128
TM_PROJ = 512
TM_MOE = 512
TM_EXPERT = 512
MOE_TOP_K = 2
VMEM_LIMIT = 56 * 1024 * 1024

_NT = (((1,), (1,)), ((), ()))
_TN = (((0,), (0,)), ((), ()))


def _cparams(sem):
    return pltpu.CompilerParams(dimension_semantics=sem, vmem_limit_bytes=VMEM_LIMIT)


def _rms(x, w):
    return x * lax.rsqrt(jnp.mean(x * x, axis=-1, keepdims=True) + RMS_EPS) * w


def _split3(x):
    hi = x.astype(BF16)
    r = x - hi.astype(F32)
    mid = r.astype(BF16)
    lo = (r - mid.astype(F32)).astype(BF16)
    return hi, mid, lo


def _dot(a, b):
    return jnp.dot(a, b, preferred_element_type=F32)


def _dot_nt(a, b):
    return lax.dot_general(a, b, _NT, preferred_element_type=F32)


def _dot_tn(a, b):
    return lax.dot_general(a, b, _TN, preferred_element_type=F32)


def _proj_gates_body(x_ref, nw_ref, w_ref, wg_ref, wgt_ref, main_ref, g_ref, gt_ref):
    xn = _rms(x_ref[...], nw_ref[...]).astype(BF16)
    main_ref[...] = _dot(xn, w_ref[...])
    g_ref[...] = _dot(xn, wg_ref[...])[:, : g_ref.shape[1]]
    gt_ref[...] = _dot_nt(wgt_ref[...], xn)


def _proj_body(x_ref, nw_ref, w_ref, main_ref):
    xn = _rms(x_ref[...], nw_ref[...]).astype(BF16)
    main_ref[...] = _dot(xn, w_ref[...])


def _proj(h, nw, w_main, w_gates=None):
    t, d = h.shape
    n = w_main.shape[1]
    tm = TM_PROJ
    x_spec = pl.BlockSpec((tm, d), lambda i: (i, 0))
    nw_spec = pl.BlockSpec((1, d), lambda i: (0, 0))
    w_spec = pl.BlockSpec((d, n), lambda i: (0, 0))
    main_spec = pl.BlockSpec((tm, n), lambda i: (i, 0))
    main_shape = jax.ShapeDtypeStruct((t, n), F32)
    if w_gates is None:
        return pl.pallas_call(
            _proj_body, grid=(t // tm,), in_specs=[x_spec, nw_spec, w_spec], out_specs=main_spec,
            out_shape=main_shape, compiler_params=_cparams(("parallel",)), name="proj",
        )(h, nw.reshape(1, d), w_main)
    ng = w_gates.shape[1]
    wg_pad = jnp.zeros((d, 128), BF16).at[:, :ng].set(w_gates)
    return pl.pallas_call(
        _proj_gates_body, grid=(t // tm,),
        in_specs=[x_spec, nw_spec, w_spec, pl.BlockSpec((d, 128), lambda i: (0, 0)),
                  pl.BlockSpec((ng, d), lambda i: (0, 0))],
        out_specs=[main_spec, pl.BlockSpec((tm, ng), lambda i: (i, 0)), pl.BlockSpec((ng, tm), lambda i: (0, i))],
        out_shape=[main_shape, jax.ShapeDtypeStruct((t, ng), F32), jax.ShapeDtypeStruct((ng, t), F32)],
        compiler_params=_cparams(("parallel",)), name="proj_gates",
    )(h, nw.reshape(1, d), w_main, wg_pad, w_gates.T)


def _mlstm_body(main_ref, g_ref, gt_ref, br_ref, bc_ref, hn_ref, out_ref, c_ref, n_ref, m_ref, *, nb, nh, dh):
    L = CHUNK
    w = nh * dh

    @pl.when(pl.program_id(0) == 0)
    def _init():
        c_ref[...] = jnp.zeros_like(c_ref)
        n_ref[...] = jnp.zeros_like(n_ref)
        m_ref[...] = jnp.full_like(m_ref, -1e30)

    row = lax.broadcasted_iota(jnp.int32, (L, L), 0)
    col = lax.broadcasted_iota(jnp.int32, (L, L), 1)
    causal = col <= row
    tril = causal.astype(BF16)
    triu = (row <= col).astype(BF16)
    scale = dh ** -0.5
    for b in range(nb):
        g = g_ref[b] + br_ref[...]
        gt = gt_ref[b, 0] + bc_ref[...]
        i_c = g[:, :nh]
        i_r = gt[:nh, :]
        lfc = _split3(jax.nn.log_sigmoid(g[:, nh:]))
        lfr = _split3(jax.nn.log_sigmoid(gt[nh:, :]))
        bc_all = _dot(tril, lfc[0]) + _dot(tril, lfc[1]) + _dot(tril, lfc[2])
        br_all = _dot(lfr[0], triu) + _dot(lfr[1], triu) + _dot(lfr[2], triu)
        for h in range(nh):
            idx = b * nh + h
            q = main_ref[b, :, h * dh:(h + 1) * dh]
            k = main_ref[b, :, w + h * dh:w + (h + 1) * dh] * scale
            v = main_ref[b, :, 2 * w + h * dh:2 * w + (h + 1) * dh]
            o = main_ref[b, :, 3 * w + h * dh:3 * w + (h + 1) * dh]
            bc = bc_all[:, h:h + 1]
            br = br_all[h:h + 1, :]
            ir = i_r[h:h + 1, :]
            ic = i_c[:, h:h + 1]
            m_prev = m_ref[idx]
            c_prev = c_ref[idx]
            n_prev = n_ref[idx]
            logw = jnp.where(causal, bc - br + ir, -jnp.inf)
            inter = bc + m_prev
            m_row = jnp.maximum(jnp.max(logw, axis=-1, keepdims=True), inter)
            qb = q.astype(BF16)
            kb = k.astype(BF16)
            vb = v.astype(BF16)
            s = _dot_nt(qb, kb) * jnp.exp(logw - m_row)
            isc = jnp.exp(inter - m_row)
            num = _dot(s.astype(BF16), vb) + isc * _dot_nt(qb, c_prev.astype(BF16))
            den = jnp.sum(s, axis=-1, keepdims=True) + isc * jnp.sum(q * n_prev, axis=-1, keepdims=True)
            hout = num / jnp.maximum(jnp.abs(den), jnp.exp(-m_row))
            b_end = bc[L - 1:L, :]
            logg = b_end - bc + ic
            m_new = jnp.maximum(b_end + m_prev, jnp.max(logg, axis=0, keepdims=True))
            wk = jnp.exp(logg - m_new)
            decay = jnp.exp(b_end + m_prev - m_new)
            c_ref[idx] = decay * c_prev + _dot_tn((v * wk).astype(BF16), kb)
            n_ref[idx] = decay * n_prev + jnp.sum(wk * k, axis=0, keepdims=True)
            m_ref[idx] = m_new
            hn = hout * lax.rsqrt(jnp.mean(hout * hout, axis=-1, keepdims=True) + RMS_EPS)
            out_ref[b, :, h * dh:(h + 1) * dh] = hn * hn_ref[:, h * dh:(h + 1) * dh] * jax.nn.sigmoid(o)


def _mlstm(main, g, gt, gate_bias, head_norm, nb, seq):
    t, n = main.shape
    nh = MLSTM_HEADS
    w = head_norm.shape[0]
    dh = w // nh
    nc = seq // CHUNK
    main3 = main.reshape(nb, seq, n)
    g3 = g.reshape(nb, seq, 2 * nh)
    gt4 = gt.reshape(2 * nh, nb, nc, CHUNK).transpose(1, 2, 0, 3)
    body = functools.partial(_mlstm_body, nb=nb, nh=nh, dh=dh)
    out = pl.pallas_call(
        body, grid=(nc,),
        in_specs=[pl.BlockSpec((nb, CHUNK, 4 * w), lambda c: (0, c, 0)),
                  pl.BlockSpec((nb, CHUNK, 2 * nh), lambda c: (0, c, 0)),
                  pl.BlockSpec((nb, 1, 2 * nh, CHUNK), lambda c: (0, c, 0, 0)),
                  pl.BlockSpec((1, 2 * nh), lambda c: (0, 0)),
                  pl.BlockSpec((2 * nh, 1), lambda c: (0, 0)),
                  pl.BlockSpec((1, w), lambda c: (0, 0))],
        out_specs=pl.BlockSpec((nb, CHUNK, w), lambda c: (0, c, 0)),
        out_shape=jax.ShapeDtypeStruct((nb, seq, w), F32),
        scratch_shapes=[pltpu.VMEM((nb * nh, dh, dh), F32), pltpu.VMEM((nb * nh, 1, dh), F32),
                        pltpu.VMEM((nb * nh, 1, 1), F32)],
        compiler_params=_cparams(("arbitrary",)), name="mlstm",
    )(main3, g3, gt4, gate_bias.reshape(1, 2 * nh), gate_bias.reshape(2 * nh, 1), head_norm.reshape(1, w))
    return out.reshape(t, w)


def _out_a_body(h_ref, hm_ref, gb_ref, gc_ref, xin_ref, pgc_ref, pxin_ref, cw_ref, w_ref, out_ref, *, tm, seq):
    i = pl.program_id(0)
    wm = hm_ref.shape[1]
    p = gc_ref[...] * xin_ref[...]
    first = (i * tm) % seq == 0
    pp = jnp.where(first, 0.0, pgc_ref[...] * pxin_ref[...])
    rowi = lax.broadcasted_iota(jnp.int32, p.shape, 0)
    p1 = jnp.where(rowi == 0, pp[7:8, :], pltpu.roll(p, 1, axis=0))
    p2 = jnp.where(rowi == 0, pp[6:7, :], jnp.where(rowi == 1, pp[7:8, :], pltpu.roll(p, 2, axis=0)))
    yc = gb_ref[...] * (cw_ref[0:1, :] * p2 + cw_ref[1:2, :] * p1 + cw_ref[2:3, :] * p)
    out_ref[...] = (h_ref[...] + _dot(hm_ref[...].astype(BF16), w_ref[:wm, :])
                    + _dot(yc.astype(BF16), w_ref[wm:, :]))


def _out_a(h, hm, main, conv_w, w_out, seq):
    t, d = h.shape
    wm = hm.shape[1]
    wc = conv_w.shape[1]
    tm = TM_PROJ
    cb = (4 * wm) // wc
    rb = tm // 8
    prev = lambda i: jnp.maximum(i * rb - 1, 0)
    body = functools.partial(_out_a_body, tm=tm, seq=seq)
    return pl.pallas_call(
        body, grid=(t // tm,),
        in_specs=[pl.BlockSpec((tm, d), lambda i: (i, 0)),
                  pl.BlockSpec((tm, wm), lambda i: (i, 0)),
                  pl.BlockSpec((tm, wc), lambda i: (i, cb)),
                  pl.BlockSpec((tm, wc), lambda i: (i, cb + 1)),
                  pl.BlockSpec((tm, wc), lambda i: (i, cb + 2)),
                  pl.BlockSpec((8, wc), lambda i: (prev(i), cb + 1)),
                  pl.BlockSpec((8, wc), lambda i: (prev(i), cb + 2)),
                  pl.BlockSpec(conv_w.shape, lambda i: (0, 0)),
                  pl.BlockSpec(w_out.shape, lambda i: (0, 0))],
        out_specs=pl.BlockSpec((tm, d), lambda i: (i, 0)),
        out_shape=jax.ShapeDtypeStruct((t, d), F32),
        compiler_params=_cparams(("parallel",)), name="out_a",
    )(h, hm, main, main, main, main, main, conv_w, w_out)


def _route(logits, lane):
    ne = MOE_GROUPS * MOE_EPG
    big = 1e9
    gl = jnp.where((lane >= ne) & (lane < ne + MOE_GROUPS), logits, -jnp.inf)
    gmax = jnp.max(gl, axis=-1, keepdims=True)
    gidx = jnp.min(jnp.where(gl == gmax, lane - ne, big), axis=-1, keepdims=True)
    gval = 1.0 / jnp.sum(jnp.exp(gl - gmax), axis=-1, keepdims=True)
    lo = gidx * MOE_EPG
    sel = jnp.where((lane >= lo) & (lane < lo + MOE_EPG), logits, -jnp.inf)
    l1 = jnp.max(sel, axis=-1, keepdims=True)
    i1 = jnp.min(jnp.where(sel == l1, lane, big), axis=-1, keepdims=True)
    sel2 = jnp.where(lane == i1, -jnp.inf, sel)
    l2 = jnp.max(sel2, axis=-1, keepdims=True)
    i2 = jnp.min(jnp.where(sel2 == l2, lane, big), axis=-1, keepdims=True)
    r = jnp.exp(l2 - l1)
    w1 = gval / (1.0 + r)
    return i1, i2, w1, w1 * r


def _lane_put(lane, cols):
    out = jnp.where(lane == 0.0, cols[0], 0.0)
    for k in range(1, len(cols)):
        out = out + jnp.where(lane == float(k), cols[k], 0.0)
    return out


def _lane_get(lane, x, idx_col):
    return jnp.sum(jnp.where(lane == idx_col, x, 0.0), axis=-1, keepdims=True)


def _route_body(h_ref, nw_ref, wr_ref, br_ref, tri_ref, info_ref, cnt_ref):
    @pl.when(pl.program_id(0) == 0)
    def _init():
        cnt_ref[...] = jnp.zeros_like(cnt_ref)

    xn = _rms(h_ref[...], nw_ref[...])
    hi = xn.astype(BF16)
    lo = (xn - hi.astype(F32)).astype(BF16)
    logits = _dot(jnp.concatenate([hi, lo, hi], axis=1), wr_ref[...]) + br_ref[...]
    lane = lax.broadcasted_iota(jnp.int32, logits.shape, 1).astype(F32)
    i1, i2, w1, w2 = _route(logits, lane)
    ind = jnp.where((lane == i1) | (lane == i2), 1.0, 0.0)
    before = _dot(tri_ref[...], ind.astype(BF16)) + cnt_ref[0:1, :]
    info_ref[...] = _lane_put(lane, [w1, w2, i1, i2, _lane_get(lane, before, i1), _lane_get(lane, before, i2)])
    cnt_ref[0:1, :] = cnt_ref[0:1, :] + jnp.sum(ind, axis=0, keepdims=True)


def _dispatch_body(cnt_s, off_s, tot_s, h_ref, nw_ref, info_ref, offrow_ref, sel_ref, xs_ref, pos_ref,
                   xn_buf, pos_v, pos_s, zbuf, sem, *, tm, tile_rows, ne, n_tiles):
    i = pl.program_id(0)
    xn_buf[...] = _rms(h_ref[...], nw_ref[...])
    info = info_ref[...]
    lane = lax.broadcasted_iota(jnp.int32, info.shape, 1).astype(F32)
    offrow = offrow_ref[...]
    p1 = _lane_get(lane, offrow, info[:, 2:3]) + info[:, 4:5]
    p2 = _lane_get(lane, offrow, info[:, 3:4]) + info[:, 5:6]
    h1 = jnp.floor(p1 * (1.0 / 256.0))
    h2 = jnp.floor(p2 * (1.0 / 256.0))
    pieces = _lane_put(lane, [h1, p1 - 256.0 * h1, h2, p2 - 256.0 * h2]).astype(BF16)
    rows = _dot_nt(sel_ref[...], pieces)
    sub = lax.broadcasted_iota(jnp.int32, rows.shape, 0)
    r1 = rows[0:1, :] * 256.0 + rows[1:2, :]
    r2 = rows[2:3, :] * 256.0 + rows[3:4, :]
    posall = jnp.where(sub == 0, r1, jnp.where(sub == 1, r2, 0.0)).astype(jnp.int32)
    pos_ref[...] = posall
    pos_v[...] = posall
    cp = pltpu.make_async_copy(pos_v, pos_s, sem.at[2])
    cp.start()
    cp.wait()

    def issue(t, carry):
        pltpu.make_async_copy(xn_buf.at[pl.ds(t, 1)], xs_ref.at[pl.ds(pos_s[0, t], 1)], sem.at[0]).start()
        pltpu.make_async_copy(xn_buf.at[pl.ds(t, 1)], xs_ref.at[pl.ds(pos_s[1, t], 1)], sem.at[1]).start()
        return carry

    lax.fori_loop(0, tm, issue, 0, unroll=8)
    pltpu.make_async_copy(xn_buf, xs_ref.at[pl.ds(0, tm)], sem.at[0]).wait()
    pltpu.make_async_copy(xn_buf, xs_ref.at[pl.ds(0, tm)], sem.at[1]).wait()

    @pl.when(i == pl.num_programs(0) - 1)
    def _zero_pad_rows():
        zbuf[...] = jnp.zeros_like(zbuf)
        for e in range(ne):
            n_pad = (tile_rows - cnt_s[e] % tile_rows) % tile_rows
            start = off_s[e] + cnt_s[e]
            head = (8 - start % 8) % 8
            for j in range(7):
                @pl.when(j < head)
                def _fill_row(j=j, start=start):
                    c = pltpu.make_async_copy(zbuf.at[pl.ds(0, 1)], xs_ref.at[pl.ds(start + j, 1)], sem.at[2])
                    c.start()
                    c.wait()
            body = n_pad - head
            base = start + head
            size = tile_rows // 2
            while size >= 8:
                @pl.when((body & size) != 0)
                def _fill(size=size, body=body, base=base):
                    at = pl.multiple_of(base + (body & ~(2 * size - 1)), 8)
                    c = pltpu.make_async_copy(zbuf.at[pl.ds(0, size)], xs_ref.at[pl.ds(at, size)], sem.at[2])
                    c.start()
                    c.wait()
                size //= 2

        half = tile_rows // 2

        def zero_tile(j, carry):
            for k in range(2):
                at = pl.multiple_of(j * tile_rows + k * half, 8)
                c = pltpu.make_async_copy(zbuf, xs_ref.at[pl.ds(at, half)], sem.at[2])
                c.start()
                c.wait()
            return carry

        lax.fori_loop(tot_s[0], n_tiles, zero_tile, 0)


def _expert_body(te_s, blk_s, tot_s, xs_ref, wg_ref, wu_ref, wd_ref, ys_ref):
    valid = pl.program_id(0) < tot_s[0]

    @pl.when(valid)
    def _run():
        x = xs_ref[...].astype(BF16)
        hid = jax.nn.silu(_dot(x, wg_ref[0].astype(BF16))) * _dot(x, wu_ref[0].astype(BF16))
        ys_ref[...] = _dot(hid.astype(BF16), wd_ref[0].astype(BF16))

    @pl.when(jnp.logical_not(valid))
    def _unused_tile():
        ys_ref[...] = jnp.zeros_like(ys_ref)


def _combine_body(h_ref, info_ref, pos_s, ys_ref, fw_ref, out_ref, y1, y2, sem, *, tm, final):
    def issue(t, carry):
        pltpu.make_async_copy(ys_ref.at[pl.ds(pos_s[0, t], 1)], y1.at[pl.ds(t, 1)], sem.at[0]).start()
        pltpu.make_async_copy(ys_ref.at[pl.ds(pos_s[1, t], 1)], y2.at[pl.ds(t, 1)], sem.at[1]).start()
        return carry

    lax.fori_loop(0, tm, issue, 0, unroll=8)
    pltpu.make_async_copy(ys_ref.at[pl.ds(0, tm)], y1, sem.at[0]).wait()
    pltpu.make_async_copy(ys_ref.at[pl.ds(0, tm)], y2, sem.at[1]).wait()
    o = h_ref[...] + info_ref[:, 0:1] * y1[...] + info_ref[:, 1:2] * y2[...]
    if final:
        o = _rms(o, fw_ref[...])
    out_ref[...] = o


def _router_weights(w_group, b_group, w_router, b_router):
    d, ne = w_router.shape
    ng = w_group.shape[1]
    w = jnp.zeros((d, ROUTE_LANES), F32).at[:, :ne].set(w_router).at[:, ne:ne + ng].set(w_group)
    hi = w.astype(BF16)
    lo = (w - hi.astype(F32)).astype(BF16)
    bias = jnp.zeros((1, ROUTE_LANES), F32).at[0, :ne].set(b_router).at[0, ne:ne + ng].set(b_group)
    return jnp.concatenate([hi, hi, lo], axis=0), bias


def _moe(h, nw, w_group, b_group, w_router, b_router, w_gate, w_up, w_down, final_w):
    t, d = h.shape
    ne, _, ff = w_gate.shape
    tm = TM_MOE
    te_rows = TM_EXPERT
    nw2 = nw.reshape(1, d)
    wr3, bias = _router_weights(w_group, b_group, w_router, b_router)
    tri = jnp.asarray(np.tril(np.ones((tm, tm), np.float32), -1), BF16)
    info, cnt = pl.pallas_call(
        _route_body, grid=(t // tm,),
        in_specs=[pl.BlockSpec((tm, d), lambda i: (i, 0)),
                  pl.BlockSpec((1, d), lambda i: (0, 0)),
                  pl.BlockSpec((3 * d, ROUTE_LANES), lambda i: (0, 0)),
                  pl.BlockSpec((1, ROUTE_LANES), lambda i: (0, 0)),
                  pl.BlockSpec((tm, tm), lambda i: (0, 0))],
        out_specs=[pl.BlockSpec((tm, ROUTE_LANES), lambda i: (i, 0)), pl.BlockSpec((8, ROUTE_LANES), lambda i: (0, 0))],
        out_shape=[jax.ShapeDtypeStruct((t, ROUTE_LANES), F32), jax.ShapeDtypeStruct((8, ROUTE_LANES), F32)],
        compiler_params=_cparams(("arbitrary",)), name="moe_route",
    )(h, nw2, wr3, bias, tri)

    cnt_i = cnt[0, :ne].astype(jnp.int32)
    ntile = (cnt_i + te_rows - 1) // te_rows
    tile_end = jnp.cumsum(ntile)
    off_i = (tile_end - ntile) * te_rows
    n_tiles = (MOE_TOP_K * t) // te_rows + ne
    rows_total = n_tiles * te_rows
    ti = jnp.arange(n_tiles, dtype=jnp.int32)
    tot = tile_end[-1:]
    ti_c = jnp.minimum(ti, tot[0] - 1)
    tile_e = jnp.sum((ti_c[:, None] >= tile_end[None, :]).astype(jnp.int32), axis=1)
    off_row = jnp.zeros((1, ROUTE_LANES), F32).at[0, :ne].set(off_i.astype(F32))
    sel = jnp.asarray(np.eye(8, ROUTE_LANES, dtype=np.float32), BF16)

    xs, pos = pl.pallas_call(
        functools.partial(_dispatch_body, tm=tm, tile_rows=te_rows, ne=ne, n_tiles=n_tiles),
        grid_spec=pltpu.PrefetchScalarGridSpec(
            num_scalar_prefetch=3, grid=(t // tm,),
            in_specs=[pl.BlockSpec((tm, d), lambda i, *_: (i, 0)),
                      pl.BlockSpec((1, d), lambda i, *_: (0, 0)),
                      pl.BlockSpec((tm, ROUTE_LANES), lambda i, *_: (i, 0)),
                      pl.BlockSpec((1, ROUTE_LANES), lambda i, *_: (0, 0)),
                      pl.BlockSpec((8, ROUTE_LANES), lambda i, *_: (0, 0))],
            out_specs=[pl.BlockSpec(memory_space=pl.ANY), pl.BlockSpec((8, tm), lambda i, *_: (0, i))],
            scratch_shapes=[pltpu.VMEM((tm, d), F32), pltpu.VMEM((8, tm), jnp.int32), pltpu.SMEM((8, tm), jnp.int32),
                            pltpu.VMEM((te_rows // 2, d), F32), pltpu.SemaphoreType.DMA((3,))]),
        out_shape=[jax.ShapeDtypeStruct((rows_total, d), F32), jax.ShapeDtypeStruct((8, t), jnp.int32)],
        compiler_params=_cparams(("arbitrary",)), name="moe_dispatch",
    )(cnt_i, off_i, tot, h, nw2, info, off_row, sel)

    ys = pl.pallas_call(
        _expert_body,
        grid_spec=pltpu.PrefetchScalarGridSpec(
            num_scalar_prefetch=3, grid=(n_tiles,),
            in_specs=[pl.BlockSpec((te_rows, d), lambda i, e, b, v: (b[i], 0)),
                      pl.BlockSpec((1, d, ff), lambda i, e, b, v: (e[i], 0, 0)),
                      pl.BlockSpec((1, d, ff), lambda i, e, b, v: (e[i], 0, 0)),
                      pl.BlockSpec((1, ff, d), lambda i, e, b, v: (e[i], 0, 0))],
            out_specs=pl.BlockSpec((te_rows, d), lambda i, e, b, v: (i, 0))),
        out_shape=jax.ShapeDtypeStruct((rows_total, d), F32),
        compiler_params=_cparams(("arbitrary",)), name="moe_expert",
    )(tile_e, ti_c, tot, xs, w_gate, w_up, w_down)

    final = final_w is not None
    fw = (final_w if final else nw).reshape(1, d)
    return pl.pallas_call(
        functools.partial(_combine_body, tm=tm, final=final), grid=(t // tm,),
        in_specs=[pl.BlockSpec((tm, d), lambda i: (i, 0)),
                  pl.BlockSpec((tm, ROUTE_LANES), lambda i: (i, 0)),
                  pl.BlockSpec((8, tm), lambda i: (0, i), memory_space=pltpu.SMEM),
                  pl.BlockSpec(memory_space=pl.ANY),
                  pl.BlockSpec((1, d), lambda i: (0, 0))],
        out_specs=pl.BlockSpec((tm, d), lambda i: (i, 0)),
        out_shape=jax.ShapeDtypeStruct((t, d), F32),
        scratch_shapes=[pltpu.VMEM((tm, d), F32), pltpu.VMEM((tm, d), F32), pltpu.SemaphoreType.DMA((2,))],
        compiler_params=_cparams(("arbitrary",)), name="moe_combine",
    )(h, info, pos, ys, fw)


def _s5_weights(lam_re, lam_im, b_re, b_im, c_re, c_im, log_dt, nsteps):
    hp = lax.Precision.HIGHEST
    L = S5_CHUNK
    g, p = lam_re.shape
    ch = b_re.shape[-1]
    lam = lax.complex(lam_re.astype(F32), lam_im.astype(F32))
    dt = jnp.exp(log_dt.astype(F32))[:, None]
    lam_bar = jnp.exp(lam * dt)
    b_bar = ((lam_bar - 1.0) / lam)[..., None] * lax.complex(b_re.astype(F32), b_im.astype(F32))
    cmat = lax.complex(c_re.astype(F32), c_im.astype(F32))
    pows = [jnp.ones_like(lam_bar)]
    for _ in range(L):
        pows.append(pows[-1] * lam_bar)
    pw = jnp.stack(pows, axis=1)
    kern = jnp.real(jnp.einsum('gop,gtp,gpi->gtoi', cmat, pw[:, :L], b_bar, precision=hp))
    lag = np.arange(L)[None, :] - np.arange(L)[:, None]
    toep = jnp.where((lag >= 0)[None, :, :, None, None], kern[:, np.maximum(lag, 0)], 0.0)
    toep = toep.transpose(0, 1, 4, 2, 3).reshape(g, L * ch, L * ch)
    wst = pw[:, L - 1 - np.arange(L)][:, :, :, None] * b_bar[:, None, :, :]
    wst = wst.transpose(0, 1, 3, 2).reshape(g, L * ch, p)
    wst = jnp.concatenate([jnp.real(wst), jnp.imag(wst)], axis=-1)
    mo = cmat.transpose(0, 2, 1)[:, :, None, :] * pw[:, 1:L + 1].transpose(0, 2, 1)[:, :, :, None]
    mo = mo.reshape(g, p, L * ch)
    wout = jnp.concatenate([jnp.real(mo), -jnp.imag(mo)], axis=1)
    a = pw[:, L]
    ars, ais = [], []
    for _ in range(nsteps):
        ars.append(jnp.concatenate([jnp.real(a), jnp.real(a)], axis=-1))
        ais.append(jnp.concatenate([-jnp.imag(a), jnp.imag(a)], axis=-1))
        a = a * a
    wcat = jnp.concatenate([toep, wst], axis=-1).astype(BF16)
    return wcat, wout.astype(BF16), jnp.stack(ars, axis=1), jnp.stack(ais, axis=1)


def _s5_body(u_ref, wcat_ref, wout_ref, ar_ref, ai_ref, y_ref, *, nb, nsteps):
    ny = y_ref.shape[2]
    r = _dot(u_ref[0], wcat_ref[0])
    y1 = r[:, :ny]
    z = r[:, ny:]
    rows, lanes = z.shape
    ridx = lax.broadcasted_iota(jnp.int32, z.shape, 0)

    def shift(x, k):
        return jnp.where(ridx >= k, pltpu.roll(x, k, axis=0), 0.0)

    w = shift(z, nb)
    for k in range(nsteps):
        if (nb << k) >= rows:
            break
        s = shift(w, nb << k)
        w = w + s * ar_ref[0, k:k + 1, :] + pltpu.roll(s, lanes // 2, axis=1) * ai_ref[0, k:k + 1, :]
    y_ref[0] = y1 + _dot(w.astype(BF16), wout_ref[0])


def _s5(u, nb, seq, lam_re, lam_im, b_re, b_im, c_re, c_im, log_dt):
    t, width = u.shape
    L = S5_CHUNK
    ch = S5_GROUP_CH
    g = width // ch
    nc = seq // L
    nsteps = max(1, (nc - 1).bit_length())
    wcat, wout, ar, ai = _s5_weights(lam_re, lam_im, b_re, b_im, c_re, c_im, log_dt, nsteps)
    ug = u.reshape(nb, nc, L, g, ch).transpose(3, 1, 0, 2, 4).reshape(g, nc * nb, L * ch).astype(BF16)
    rows = nc * nb
    body = functools.partial(_s5_body, nb=nb, nsteps=nsteps)
    y = pl.pallas_call(
        body, grid=(g,),
        in_specs=[pl.BlockSpec((1, rows, L * ch), lambda i: (i, 0, 0)),
                  pl.BlockSpec((1,) + wcat.shape[1:], lambda i: (i, 0, 0)),
                  pl.BlockSpec((1,) + wout.shape[1:], lambda i: (i, 0, 0)),
                  pl.BlockSpec((1,) + ar.shape[1:], lambda i: (i, 0, 0)),
                  pl.BlockSpec((1,) + ai.shape[1:], lambda i: (i, 0, 0))],
        out_specs=pl.BlockSpec((1, rows, L * ch), lambda i: (i, 0, 0)),
        out_shape=jax.ShapeDtypeStruct((g, rows, L * ch), F32),
        compiler_params=_cparams(("parallel",)), name="s5",
    )(ug, wcat, wout, ar, ai)
    return y.reshape(g, nc, nb, L, ch).transpose(2, 1, 3, 0, 4).reshape(t, width)


def _hgrn_gmat():
    L = CHUNK
    blocks = 2 + int(np.log2(L))
    gm = np.zeros((blocks * L, L), np.float32)
    for j in range(L):
        gm[j, :j + 1] = 1.0
        gm[L + j, j + 1:] = 1.0
    li, m = 2, L
    while m >= 2:
        half = m // 2
        for j in range(L):
            pos = j % m
            r = j - pos + half - 1
            if pos >= half:
                gm[li * L + j, r + 1:j + 1] = 1.0
            else:
                gm[li * L + j, j + 1:r + 1] = 1.0
        li += 1
        m //= 2
    return gm


def _hgrn_body(main_ref, gm_ref, lb_ref, nw_ref, out_ref, st_ref, *, nb, nh, dh):
    L = CHUNK
    w = nh * dh

    @pl.when(pl.program_id(0) == 0)
    def _init():
        st_ref[...] = jnp.zeros_like(st_ref)

    row = lax.broadcasted_iota(jnp.int32, (L, L), 0)
    col = lax.broadcasted_iota(jnp.int32, (L, L), 1)
    rowd = lax.broadcasted_iota(jnp.int32, (L, dh), 0)
    eye = row == col
    gm = gm_ref[...]
    lb = lb_ref[...]
    for b in range(nb):
        fg = main_ref[b, :, w:2 * w]
        f = lb + (1.0 - lb) * jax.nn.sigmoid(fg)
        kk = (1.0 - lb) * jax.nn.sigmoid(-fg)
        lf = _split3(jnp.log(f))
        p_all = jnp.exp(_dot(gm, lf[0]) + _dot(gm, lf[1]) + _dot(gm, lf[2]))
        for h in range(nh):
            idx = b * nh + h
            hs = slice(h * dh, (h + 1) * dh)
            q = main_ref[b, :, h * dh:(h + 1) * dh]
            v = main_ref[b, :, 2 * w + h * dh:2 * w + (h + 1) * dh]
            og = main_ref[b, :, 3 * w + h * dh:3 * w + (h + 1) * dh]
            k = kk[:, hs]
            pb = p_all[0:L, hs]
            pe = p_all[L:2 * L, hs]
            st = st_ref[idx]
            o = _dot_nt((q * pb).astype(BF16), st.astype(BF16))
            attn = jnp.where(eye, _dot_nt(q.astype(BF16), k.astype(BF16)), 0.0)
            li, m = 2, L
            while m >= 2:
                pl_ = p_all[li * L:(li + 1) * L, hs]
                up = (rowd & (m - 1)) >= (m // 2)
                ql = jnp.where(up, q * pl_, 0.0).astype(BF16)
                kl = jnp.where(up, 0.0, k * pl_).astype(BF16)
                same = (row & ~(m - 1)) == (col & ~(m - 1))
                attn = attn + jnp.where(same, _dot_nt(ql, kl), 0.0)
                li += 1
                m //= 2
            vb = v.astype(BF16)
            o = o + _dot(attn.astype(BF16), vb)
            st_ref[idx] = st * pb[L - 1:L, :] + _dot_tn(vb, (k * pe).astype(BF16))
            on = o * lax.rsqrt(jnp.mean(o * o, axis=-1, keepdims=True) + RMS_EPS)
            out_ref[b, :, hs] = on * nw_ref[:, hs] * jax.nn.silu(og)


def _hgrn(main, lower_bound, out_norm, nb, seq):
    t, n = main.shape
    nh = HGRN_HEADS
    w = out_norm.shape[0]
    dh = w // nh
    nc = seq // CHUNK
    gm = jnp.asarray(_hgrn_gmat(), BF16)
    body = functools.partial(_hgrn_body, nb=nb, nh=nh, dh=dh)
    out = pl.pallas_call(
        body, grid=(nc,),
        in_specs=[pl.BlockSpec((nb, CHUNK, 4 * w), lambda c: (0, c, 0)),
                  pl.BlockSpec(gm.shape, lambda c: (0, 0)),
                  pl.BlockSpec((1, w), lambda c: (0, 0)),
                  pl.BlockSpec((1, w), lambda c: (0, 0))],
        out_specs=pl.BlockSpec((nb, CHUNK, w), lambda c: (0, c, 0)),
        out_shape=jax.ShapeDtypeStruct((nb, seq, w), F32),
        scratch_shapes=[pltpu.VMEM((nb * nh, dh, dh), F32)],
        compiler_params=_cparams(("arbitrary",)), name="hgrn",
    )(main.reshape(nb, seq, n), gm, lower_bound.reshape(1, w), out_norm.reshape(1, w))
    return out.reshape(t, w)


def _out_c_body(h_ref, ys_ref, u_ref, oh_ref, d_ref, wglu_ref, bglu_ref, w_ref, out_ref):
    ws = ys_ref.shape[1]
    z = jax.nn.gelu(ys_ref[...] + d_ref[...] * u_ref[...])
    gate = jax.nn.sigmoid(_dot(z.astype(BF16), wglu_ref[...]) + bglu_ref[...])
    out_ref[...] = (h_ref[...] + _dot((z * gate).astype(BF16), w_ref[:ws, :])
                    + _dot(oh_ref[...].astype(BF16), w_ref[ws:, :]))


def _out_c(h, ys, main, oh, d_skip, w_glu, b_glu, w_out):
    t, d = h.shape
    ws = ys.shape[1]
    wh = oh.shape[1]
    tm = TM_PROJ
    ub = (main.shape[1] - ws) // ws
    return pl.pallas_call(
        _out_c_body, grid=(t // tm,),
        in_specs=[pl.BlockSpec((tm, d), lambda i: (i, 0)),
                  pl.BlockSpec((tm, ws), lambda i: (i, 0)),
                  pl.BlockSpec((tm, ws), lambda i: (i, ub)),
                  pl.BlockSpec((tm, wh), lambda i: (i, 0)),
                  pl.BlockSpec((1, ws), lambda i: (0, 0)),
                  pl.BlockSpec(w_glu.shape, lambda i: (0, 0)),
                  pl.BlockSpec((1, ws), lambda i: (0, 0)),
                  pl.BlockSpec(w_out.shape, lambda i: (0, 0))],
        out_specs=pl.BlockSpec((tm, d), lambda i: (i, 0)),
        out_shape=jax.ShapeDtypeStruct((t, d), F32),
        compiler_params=_cparams(("parallel",)), name="out_c",
    )(h, ys, main, oh, d_skip.reshape(1, ws), w_glu, b_glu.reshape(1, ws), w_out)


def kernel(x, norm_mix, norm_ffn, norm_final, ab_w_in, ab_gate_bias, ab_head_norm, ab_conv_w, ab_w_out, cd_w_in, s5_lambda_re, s5_lambda_im, s5_b_re, s5_b_im, s5_c_re, s5_c_im, s5_d, s5_log_dt, s5_w_glu, s5_b_glu, hgrn_lb, hgrn_out_norm, cd_w_out, moe_w_group, moe_b_group, moe_w_router, moe_b_router, moe_w_gate, moe_w_up, moe_w_down):
    nb, seq, d = x.shape
    depth = norm_mix.shape[0]
    h = x.reshape(nb * seq, d)
    for layer in range(depth):
        j = layer // 2
        if layer % 2 == 0:
            wm = ab_head_norm.shape[1]
            ng = ab_gate_bias.shape[1]
            w_in = ab_w_in[j]
            w_main = jnp.concatenate([w_in[:, :4 * wm], w_in[:, 4 * wm + ng:]], axis=1).astype(BF16)
            w_gates = w_in[:, 4 * wm:4 * wm + ng].astype(BF16)
            main, g, gt = _proj(h, norm_mix[layer], w_main, w_gates)
            hm = _mlstm(main, g, gt, ab_gate_bias[j], ab_head_norm[j], nb, seq)
            h = _out_a(h, hm, main, ab_conv_w[j], ab_w_out[j].astype(BF16), seq)
        else:
            ws = s5_d.shape[1]
            w_in = cd_w_in[j]
            w_main = jnp.concatenate([w_in[:, ws:], w_in[:, :ws]], axis=1).astype(BF16)
            main = _proj(h, norm_mix[layer], w_main)
            sm = jax.nn.softmax(hgrn_lb.astype(F32), axis=0)
            lower_bound = jnp.cumsum(sm, axis=0)[layer] - sm[0]
            ys = _s5(main[:, main.shape[1] - ws:], nb, seq, s5_lambda_re[j], s5_lambda_im[j], s5_b_re[j], s5_b_im[j],
                     s5_c_re[j], s5_c_im[j], s5_log_dt[j])
            oh = _hgrn(main, lower_bound, hgrn_out_norm[j], nb, seq)
            h = _out_c(h, ys, main, oh, s5_d[j], s5_w_glu[j].astype(BF16), s5_b_glu[j], cd_w_out[j].astype(BF16))
        h = _moe(h, norm_ffn[layer], moe_w_group[layer], moe_b_group[layer], moe_w_router[layer], moe_b_router[layer],
                 moe_w_gate[layer], moe_w_up[layer], moe_w_down[layer],
                 norm_final if layer == depth - 1 else None)
    return h.reshape(nb, seq, d)
```

```python
import functools

import numpy as np
import jax
import jax.numpy as jnp
from jax import lax
from jax.experimental import pallas as pl
from jax.experimental.pallas import tpu as pltpu

F32 = jnp.float32
BF16 = jnp.bfloat16
RMS_EPS = 1e-6
CHUNK = 64
S5_CHUNK = 16
S5_GROUP_CH = 16
S5_STATE = 64
MLSTM_HEADS = 4
HGRN_HEADS = 4
MOE_GROUPS = 4
MOE_EPG = 8
ROUTE_LANES = 128
TM_PROJ = 512
TM_MOE = 512
TM_EXPERT = 512
MOE_TOP_K = 2
VMEM_LIMIT = 56 * 1024 * 1024

_NT = (((1,), (1,)), ((), ()))
_TN = (((0,), (0,)), ((), ()))


def _cparams(sem):
    return pltpu.CompilerParams(dimension_semantics=sem, vmem_limit_bytes=VMEM_LIMIT)


def _rms(x, w):
    return x * lax.rsqrt(jnp.mean(x * x, axis=-1, keepdims=True) + RMS_EPS) * w


def _split3(x):
    hi = x.astype(BF16)
    r = x - hi.astype(F32)
    mid = r.astype(BF16)
    lo = (r - mid.astype(F32)).astype(BF16)
    return hi, mid, lo


def _dot(a, b):
    return jnp.dot(a, b, preferred_element_type=F32)


def _dot_nt(a, b):
    return lax.dot_general(a, b, _NT, preferred_element_type=F32)


def _dot_tn(a, b):
    return lax.dot_general(a, b, _TN, preferred_element_type=F32)


def _proj_gates_body(x_ref, nw_ref, w_ref, wg_ref, wgt_ref, main_ref, g_ref, gt_ref):
    xn = _rms(x_ref[...], nw_ref[...]).astype(BF16)
    main_ref[...] = _dot(xn, w_ref[...])
    g_ref[...] = _dot(xn, wg_ref[...])[:, : g_ref.shape[1]]
    gt_ref[...] = _dot_nt(wgt_ref[...], xn)


def _proj_body(x_ref, nw_ref, w_ref, main_ref):
    xn = _rms(x_ref[...], nw_ref[...]).astype(BF16)
    main_ref[...] = _dot(xn, w_ref[...])


def _proj(h, nw, w_main, w_gates=None):
    t, d = h.shape
    n = w_main.shape[1]
    tm = TM_PROJ
    x_spec = pl.BlockSpec((tm, d), lambda i: (i, 0))
    nw_spec = pl.BlockSpec((1, d), lambda i: (0, 0))
    w_spec = pl.BlockSpec((d, n), lambda i: (0, 0))
    main_spec = pl.BlockSpec((tm, n), lambda i: (i, 0))
    main_shape = jax.ShapeDtypeStruct((t, n), F32)
    if w_gates is None:
        return pl.pallas_call(
            _proj_body, grid=(t // tm,), in_specs=[x_spec, nw_spec, w_spec], out_specs=main_spec,
            out_shape=main_shape, compiler_params=_cparams(("parallel",)), name="proj",
        )(h, nw.reshape(1, d), w_main)
    ng = w_gates.shape[1]
    wg_pad = jnp.zeros((d, 128), BF16).at[:, :ng].set(w_gates)
    return pl.pallas_call(
        _proj_gates_body, grid=(t // tm,),
        in_specs=[x_spec, nw_spec, w_spec, pl.BlockSpec((d, 128), lambda i: (0, 0)),
                  pl.BlockSpec((ng, d), lambda i: (0, 0))],
        out_specs=[main_spec, pl.BlockSpec((tm, ng), lambda i: (i, 0)), pl.BlockSpec((ng, tm), lambda i: (0, i))],
        out_shape=[main_shape, jax.ShapeDtypeStruct((t, ng), F32), jax.ShapeDtypeStruct((ng, t), F32)],
        compiler_params=_cparams(("parallel",)), name="proj_gates",
    )(h, nw.reshape(1, d), w_main, wg_pad, w_gates.T)


def _mlstm_body(main_ref, g_ref, gt_ref, br_ref, bc_ref, hn_ref, out_ref, c_ref, n_ref, m_ref, *, nb, nh, dh):
    L = CHUNK
    w = nh * dh

    @pl.when(pl.program_id(0) == 0)
    def _init():
        c_ref[...] = jnp.zeros_like(c_ref)
        n_ref[...] = jnp.zeros_like(n_ref)
        m_ref[...] = jnp.full_like(m_ref, -1e30)

    row = lax.broadcasted_iota(jnp.int32, (L, L), 0)
    col = lax.broadcasted_iota(jnp.int32, (L, L), 1)
    causal = col <= row
    tril = causal.astype(BF16)
    triu = (row <= col).astype(BF16)
    scale = dh ** -0.5
    for b in range(nb):
        g = g_ref[b] + br_ref[...]
        gt = gt_ref[b, 0] + bc_ref[...]
        i_c = g[:, :nh]
        i_r = gt[:nh, :]
        lfc = _split3(jax.nn.log_sigmoid(g[:, nh:]))
        lfr = _split3(jax.nn.log_sigmoid(gt[nh:, :]))
        bc_all = _dot(tril, lfc[0]) + _dot(tril, lfc[1]) + _dot(tril, lfc[2])
        br_all = _dot(lfr[0], triu) + _dot(lfr[1], triu) + _dot(lfr[2], triu)
        for h in range(nh):
            idx = b * nh + h
            q = main_ref[b, :, h * dh:(h + 1) * dh]
            k = main_ref[b, :, w + h * dh:w + (h + 1) * dh] * scale
            v = main_ref[b, :, 2 * w + h * dh:2 * w + (h + 1) * dh]
            o = main_ref[b, :, 3 * w + h * dh:3 * w + (h + 1) * dh]
            bc = bc_all[:, h:h + 1]
            br = br_all[h:h + 1, :]
            ir = i_r[h:h + 1, :]
            ic = i_c[:, h:h + 1]
            m_prev = m_ref[idx]
            c_prev = c_ref[idx]
            n_prev = n_ref[idx]
            logw = jnp.where(causal, bc - br + ir, -jnp.inf)
            inter = bc + m_prev
            m_row = jnp.maximum(jnp.max(logw, axis=-1, keepdims=True), inter)
            qb = q.astype(BF16)
            kb = k.astype(BF16)
            vb = v.astype(BF16)
            s = _dot_nt(qb, kb) * jnp.exp(logw - m_row)
            isc = jnp.exp(inter - m_row)
            num = _dot(s.astype(BF16), vb) + isc * _dot_nt(qb, c_prev.astype(BF16))
            den = jnp.sum(s, axis=-1, keepdims=True) + isc * jnp.sum(q * n_prev, axis=-1, keepdims=True)
            hout = num / jnp.maximum(jnp.abs(den), jnp.exp(-m_row))
            b_end = bc[L - 1:L, :]
            logg = b_end - bc + ic
            m_new = jnp.maximum(b_end + m_prev, jnp.max(logg, axis=0, keepdims=True))
            wk = jnp.exp(logg - m_new)
            decay = jnp.exp(b_end + m_prev - m_new)
            c_ref[idx] = decay * c_prev + _dot_tn((v * wk).astype(BF16), kb)
            n_ref[idx] = decay * n_prev + jnp.sum(wk * k, axis=0, keepdims=True)
            m_ref[idx] = m_new
            hn = hout * lax.rsqrt(jnp.mean(hout * hout, axis=-1, keepdims=True) + RMS_EPS)
            out_ref[b, :, h * dh:(h + 1) * dh] = hn * hn_ref[:, h * dh:(h + 1) * dh] * jax.nn.sigmoid(o)


def _mlstm(main, g, gt, gate_bias, head_norm, nb, seq):
    t, n = main.shape
    nh = MLSTM_HEADS
    w = head_norm.shape[0]
    dh = w // nh
    nc = seq // CHUNK
    main3 = main.reshape(nb, seq, n)
    g3 = g.reshape(nb, seq, 2 * nh)
    gt4 = gt.reshape(2 * nh, nb, nc, CHUNK).transpose(1, 2, 0, 3)
    body = functools.partial(_mlstm_body, nb=nb, nh=nh, dh=dh)
    out = pl.pallas_call(
        body, grid=(nc,),
        in_specs=[pl.BlockSpec((nb, CHUNK, 4 * w), lambda c: (0, c, 0)),
                  pl.BlockSpec((nb, CHUNK, 2 * nh), lambda c: (0, c, 0)),
                  pl.BlockSpec((nb, 1, 2 * nh, CHUNK), lambda c: (0, c, 0, 0)),
                  pl.BlockSpec((1, 2 * nh), lambda c: (0, 0)),
                  pl.BlockSpec((2 * nh, 1), lambda c: (0, 0)),
                  pl.BlockSpec((1, w), lambda c: (0, 0))],
        out_specs=pl.BlockSpec((nb, CHUNK, w), lambda c: (0, c, 0)),
        out_shape=jax.ShapeDtypeStruct((nb, seq, w), F32),
        scratch_shapes=[pltpu.VMEM((nb * nh, dh, dh), F32), pltpu.VMEM((nb * nh, 1, dh), F32),
                        pltpu.VMEM((nb * nh, 1, 1), F32)],
        compiler_params=_cparams(("arbitrary",)), name="mlstm",
    )(main3, g3, gt4, gate_bias.reshape(1, 2 * nh), gate_bias.reshape(2 * nh, 1), head_norm.reshape(1, w))
    return out.reshape(t, w)


def _out_a_body(h_ref, hm_ref, gb_ref, gc_ref, xin_ref, pgc_ref, pxin_ref, cw_ref, w_ref, out_ref, *, tm, seq):
    i = pl.program_id(0)
    wm = hm_ref.shape[1]
    p = gc_ref[...] * xin_ref[...]
    first = (i * tm) % seq == 0
    pp = jnp.where(first, 0.0, pgc_ref[...] * pxin_ref[...])
    rowi = lax.broadcasted_iota(jnp.int32, p.shape, 0)
    p1 = jnp.where(rowi == 0, pp[7:8, :], pltpu.roll(p, 1, axis=0))
    p2 = jnp.where(rowi == 0, pp[6:7, :], jnp.where(rowi == 1, pp[7:8, :], pltpu.roll(p, 2, axis=0)))
    yc = gb_ref[...] * (cw_ref[0:1, :] * p2 + cw_ref[1:2, :] * p1 + cw_ref[2:3, :] * p)
    out_ref[...] = (h_ref[...] + _dot(hm_ref[...].astype(BF16), w_ref[:wm, :])
                    + _dot(yc.astype(BF16), w_ref[wm:, :]))


def _out_a(h, hm, main, conv_w, w_out, seq):
    t, d = h.shape
    wm = hm.shape[1]
    wc = conv_w.shape[1]
    tm = TM_PROJ
    cb = (4 * wm) // wc
    rb = tm // 8
    prev = lambda i: jnp.maximum(i * rb - 1, 0)
    body = functools.partial(_out_a_body, tm=tm, seq=seq)
    return pl.pallas_call(
        body, grid=(t // tm,),
        in_specs=[pl.BlockSpec((tm, d), lambda i: (i, 0)),
                  pl.BlockSpec((tm, wm), lambda i: (i, 0)),
                  pl.BlockSpec((tm, wc), lambda i: (i, cb)),
                  pl.BlockSpec((tm, wc), lambda i: (i, cb + 1)),
                  pl.BlockSpec((tm, wc), lambda i: (i, cb + 2)),
                  pl.BlockSpec((8, wc), lambda i: (prev(i), cb + 1)),
                  pl.BlockSpec((8, wc), lambda i: (prev(i), cb + 2)),
                  pl.BlockSpec(conv_w.shape, lambda i: (0, 0)),
                  pl.BlockSpec(w_out.shape, lambda i: (0, 0))],
        out_specs=pl.BlockSpec((tm, d), lambda i: (i, 0)),
        out_shape=jax.ShapeDtypeStruct((t, d), F32),
        compiler_params=_cparams(("parallel",)), name="out_a",
    )(h, hm, main, main, main, main, main, conv_w, w_out)


def _route(logits, lane):
    ne = MOE_GROUPS * MOE_EPG
    big = 1e9
    gl = jnp.where((lane >= ne) & (lane < ne + MOE_GROUPS), logits, -jnp.inf)
    gmax = jnp.max(gl, axis=-1, keepdims=True)
    gidx = jnp.min(jnp.where(gl == gmax, lane - ne, big), axis=-1, keepdims=True)
    gval = 1.0 / jnp.sum(jnp.exp(gl - gmax), axis=-1, keepdims=True)
    lo = gidx * MOE_EPG
    sel = jnp.where((lane >= lo) & (lane < lo + MOE_EPG), logits, -jnp.inf)
    l1 = jnp.max(sel, axis=-1, keepdims=True)
    i1 = jnp.min(jnp.where(sel == l1, lane, big), axis=-1, keepdims=True)
    sel2 = jnp.where(lane == i1, -jnp.inf, sel)
    l2 = jnp.max(sel2, axis=-1, keepdims=True)
    i2 = jnp.min(jnp.where(sel2 == l2, lane, big), axis=-1, keepdims=True)
    r = jnp.exp(l2 - l1)
    w1 = gval / (1.0 + r)
    return i1, i2, w1, w1 * r


ROW_TILE = 8


def _rows_to_tiles(ref, x):
    n = x.shape[0]
    for c in range(ROW_TILE):
        ref[pl.ds(c, n, stride=ROW_TILE), :] = x[:, c * 128:(c + 1) * 128]


def _tiles_to_rows(ref, n):
    return jnp.concatenate([ref[pl.ds(c, n, stride=ROW_TILE), :] for c in range(ROW_TILE)], axis=1)


def _lane_put(lane, cols):
    out = jnp.where(lane == 0.0, cols[0], 0.0)
    for k in range(1, len(cols)):
        out = out + jnp.where(lane == float(k), cols[k], 0.0)
    return out


def _lane_get(lane, x, idx_col):
    return jnp.sum(jnp.where(lane == idx_col, x, 0.0), axis=-1, keepdims=True)


def _route_body(h_ref, nw_ref, wr_ref, br_ref, tri_ref, info_ref, cnt_ref):
    @pl.when(pl.program_id(0) == 0)
    def _init():
        cnt_ref[...] = jnp.zeros_like(cnt_ref)

    xn = _rms(h_ref[...], nw_ref[...])
    hi = xn.astype(BF16)
    lo = (xn - hi.astype(F32)).astype(BF16)
    logits = _dot(jnp.concatenate([hi, lo, hi], axis=1), wr_ref[...]) + br_ref[...]
    lane = lax.broadcasted_iota(jnp.int32, logits.shape, 1).astype(F32)
    i1, i2, w1, w2 = _route(logits, lane)
    ind = jnp.where((lane == i1) | (lane == i2), 1.0, 0.0)
    before = _dot(tri_ref[...], ind.astype(BF16)) + cnt_ref[0:1, :]
    info_ref[...] = _lane_put(lane, [w1, w2, i1, i2, _lane_get(lane, before, i1), _lane_get(lane, before, i2)])
    cnt_ref[0:1, :] = cnt_ref[0:1, :] + jnp.sum(ind, axis=0, keepdims=True)


def _dispatch_body(cnt_s, off_s, tot_s, h_ref, nw_ref, info_ref, offrow_ref, sel_ref, xs_ref, pos_ref,
                   xn_buf, pos_v, pos_s, zbuf, sem, *, tm, tile_rows, ne, n_tiles):
    i = pl.program_id(0)
    _rows_to_tiles(xn_buf, _rms(h_ref[...], nw_ref[...]))
    info = info_ref[...]
    lane = lax.broadcasted_iota(jnp.int32, info.shape, 1).astype(F32)
    offrow = offrow_ref[...]
    p1 = _lane_get(lane, offrow, info[:, 2:3]) + info[:, 4:5]
    p2 = _lane_get(lane, offrow, info[:, 3:4]) + info[:, 5:6]
    h1 = jnp.floor(p1 * (1.0 / 256.0))
    h2 = jnp.floor(p2 * (1.0 / 256.0))
    pieces = _lane_put(lane, [h1, p1 - 256.0 * h1, h2, p2 - 256.0 * h2]).astype(BF16)
    rows = _dot_nt(sel_ref[...], pieces)
    sub = lax.broadcasted_iota(jnp.int32, rows.shape, 0)
    r1 = rows[0:1, :] * 256.0 + rows[1:2, :]
    r2 = rows[2:3, :] * 256.0 + rows[3:4, :]
    posall = jnp.where(sub == 0, r1, jnp.where(sub == 1, r2, 0.0)).astype(jnp.int32)
    pos_ref[...] = posall
    pos_v[...] = posall
    cp = pltpu.make_async_copy(pos_v, pos_s, sem.at[2])
    cp.start()
    cp.wait()

    def issue(t, carry):
        src = xn_buf.at[pl.ds(pl.multiple_of(t * ROW_TILE, ROW_TILE), ROW_TILE)]
        for k in range(MOE_TOP_K):
            at = pl.multiple_of(pos_s[k, t] * ROW_TILE, ROW_TILE)
            pltpu.make_async_copy(src, xs_ref.at[pl.ds(at, ROW_TILE)], sem.at[k]).start()
        return carry

    lax.fori_loop(0, tm, issue, 0, unroll=8)
    for k in range(MOE_TOP_K):
        pltpu.make_async_copy(xn_buf, xs_ref.at[pl.ds(0, tm * ROW_TILE)], sem.at[k]).wait()

    @pl.when(i == pl.num_programs(0) - 1)
    def _zero_unused_rows():
        zbuf[...] = jnp.zeros_like(zbuf)

        def fill(row, nrows):
            at = pl.multiple_of(row * ROW_TILE, ROW_TILE)
            c = pltpu.make_async_copy(zbuf.at[pl.ds(0, nrows * ROW_TILE)], xs_ref.at[pl.ds(at, nrows * ROW_TILE)],
                                      sem.at[2])
            c.start()
            c.wait()

        for e in range(ne):
            n_pad = (tile_rows - cnt_s[e] % tile_rows) % tile_rows
            start = off_s[e] + cnt_s[e]
            size = tile_rows // 2
            while size >= 1:
                @pl.when((n_pad & size) != 0)
                def _fill(size=size, n_pad=n_pad, start=start):
                    fill(start + (n_pad & ~(2 * size - 1)), size)
                size //= 2

        def zero_tile(j, carry):
            for k in range(2):
                fill(j * tile_rows + k * (tile_rows // 2), tile_rows // 2)
            return carry

        lax.fori_loop(tot_s[0], n_tiles, zero_tile, 0)


def _expert_body(te_s, blk_s, tot_s, xs_ref, wg_ref, wu_ref, wd_ref, ys_ref):
    valid = pl.program_id(0) < tot_s[0]
    rows = xs_ref.shape[0] // ROW_TILE

    @pl.when(valid)
    def _run():
        x = _tiles_to_rows(xs_ref, rows).astype(BF16)
        hid = jax.nn.silu(_dot(x, wg_ref[0].astype(BF16))) * _dot(x, wu_ref[0].astype(BF16))
        _rows_to_tiles(ys_ref, _dot(hid.astype(BF16), wd_ref[0].astype(BF16)))

    @pl.when(jnp.logical_not(valid))
    def _unused_tile():
        ys_ref[...] = jnp.zeros_like(ys_ref)


def _combine_body(h_ref, info_ref, pos_s, ys_ref, fw_ref, out_ref, y1, y2, sem, *, tm, final):
    bufs = (y1, y2)

    def issue(t, carry):
        dst = pl.ds(pl.multiple_of(t * ROW_TILE, ROW_TILE), ROW_TILE)
        for k in range(MOE_TOP_K):
            at = pl.multiple_of(pos_s[k, t] * ROW_TILE, ROW_TILE)
            pltpu.make_async_copy(ys_ref.at[pl.ds(at, ROW_TILE)], bufs[k].at[dst], sem.at[k]).start()
        return carry

    lax.fori_loop(0, tm, issue, 0, unroll=8)
    for k in range(MOE_TOP_K):
        pltpu.make_async_copy(ys_ref.at[pl.ds(0, tm * ROW_TILE)], bufs[k], sem.at[k]).wait()
    o = h_ref[...] + info_ref[:, 0:1] * _tiles_to_rows(y1, tm) + info_ref[:, 1:2] * _tiles_to_rows(y2, tm)
    if final:
        o = _rms(o, fw_ref[...])
    out_ref[...] = o


def _router_weights(w_group, b_group, w_router, b_router):
    d, ne = w_router.shape
    ng = w_group.shape[1]
    w = jnp.zeros((d, ROUTE_LANES), F32).at[:, :ne].set(w_router).at[:, ne:ne + ng].set(w_group)
    hi = w.astype(BF16)
    lo = (w - hi.astype(F32)).astype(BF16)
    bias = jnp.zeros((1, ROUTE_LANES), F32).at[0, :ne].set(b_router).at[0, ne:ne + ng].set(b_group)
    return jnp.concatenate([hi, hi, lo], axis=0), bias


def _moe(h, nw, w_group, b_group, w_router, b_router, w_gate, w_up, w_down, final_w):
    t, d = h.shape
    assert d == ROW_TILE * 128, "row-as-tile layout needs d_model == 1024"
    ne, _, ff = w_gate.shape
    tm = TM_MOE
    te_rows = TM_EXPERT
    nw2 = nw.reshape(1, d)
    wr3, bias = _router_weights(w_group, b_group, w_router, b_router)
    tri = jnp.asarray(np.tril(np.ones((tm, tm), np.float32), -1), BF16)
    info, cnt = pl.pallas_call(
        _route_body, grid=(t // tm,),
        in_specs=[pl.BlockSpec((tm, d), lambda i: (i, 0)),
                  pl.BlockSpec((1, d), lambda i: (0, 0)),
                  pl.BlockSpec((3 * d, ROUTE_LANES), lambda i: (0, 0)),
                  pl.BlockSpec((1, ROUTE_LANES), lambda i: (0, 0)),
                  pl.BlockSpec((tm, tm), lambda i: (0, 0))],
        out_specs=[pl.BlockSpec((tm, ROUTE_LANES), lambda i: (i, 0)), pl.BlockSpec((8, ROUTE_LANES), lambda i: (0, 0))],
        out_shape=[jax.ShapeDtypeStruct((t, ROUTE_LANES), F32), jax.ShapeDtypeStruct((8, ROUTE_LANES), F32)],
        compiler_params=_cparams(("arbitrary",)), name="moe_route",
    )(h, nw2, wr3, bias, tri)

    cnt_i = cnt[0, :ne].astype(jnp.int32)
    ntile = (cnt_i + te_rows - 1) // te_rows
    tile_end = jnp.cumsum(ntile)
    off_i = (tile_end - ntile) * te_rows
    n_tiles = (MOE_TOP_K * t) // te_rows + ne
    rows_total = n_tiles * te_rows
    ti = jnp.arange(n_tiles, dtype=jnp.int32)
    tot = tile_end[-1:]
    ti_c = jnp.minimum(ti, tot[0] - 1)
    tile_e = jnp.sum((ti_c[:, None] >= tile_end[None, :]).astype(jnp.int32), axis=1)
    off_row = jnp.zeros((1, ROUTE_LANES), F32).at[0, :ne].set(off_i.astype(F32))
    sel = jnp.asarray(np.eye(8, ROUTE_LANES, dtype=np.float32), BF16)

    xs, pos = pl.pallas_call(
        functools.partial(_dispatch_body, tm=tm, tile_rows=te_rows, ne=ne, n_tiles=n_tiles),
        grid_spec=pltpu.PrefetchScalarGridSpec(
            num_scalar_prefetch=3, grid=(t // tm,),
            in_specs=[pl.BlockSpec((tm, d), lambda i, *_: (i, 0)),
                      pl.BlockSpec((1, d), lambda i, *_: (0, 0)),
                      pl.BlockSpec((tm, ROUTE_LANES), lambda i, *_: (i, 0)),
                      pl.BlockSpec((1, ROUTE_LANES), lambda i, *_: (0, 0)),
                      pl.BlockSpec((8, ROUTE_LANES), lambda i, *_: (0, 0))],
            out_specs=[pl.BlockSpec(memory_space=pl.ANY), pl.BlockSpec((8, tm), lambda i, *_: (0, i))],
            scratch_shapes=[pltpu.VMEM((tm * ROW_TILE, 128), F32), pltpu.VMEM((8, tm), jnp.int32),
                            pltpu.SMEM((8, tm), jnp.int32), pltpu.VMEM((te_rows // 2 * ROW_TILE, 128), F32),
                            pltpu.SemaphoreType.DMA((3,))]),
        out_shape=[jax.ShapeDtypeStruct((rows_total * ROW_TILE, 128), F32), jax.ShapeDtypeStruct((8, t), jnp.int32)],
        compiler_params=_cparams(("arbitrary",)), name="moe_dispatch",
    )(cnt_i, off_i, tot, h, nw2, info, off_row, sel)

    ys = pl.pallas_call(
        _expert_body,
        grid_spec=pltpu.PrefetchScalarGridSpec(
            num_scalar_prefetch=3, grid=(n_tiles,),
            in_specs=[pl.BlockSpec((te_rows * ROW_TILE, 128), lambda i, e, b, v: (b[i], 0)),
                      pl.BlockSpec((1, d, ff), lambda i, e, b, v: (e[i], 0, 0)),
                      pl.BlockSpec((1, d, ff), lambda i, e, b, v: (e[i], 0, 0)),
                      pl.BlockSpec((1, ff, d), lambda i, e, b, v: (e[i], 0, 0))],
            out_specs=pl.BlockSpec((te_rows * ROW_TILE, 128), lambda i, e, b, v: (i, 0))),
        out_shape=jax.ShapeDtypeStruct((rows_total * ROW_TILE, 128), F32),
        compiler_params=_cparams(("arbitrary",)), name="moe_expert",
    )(tile_e, ti_c, tot, xs, w_gate, w_up, w_down)

    final = final_w is not None
    fw = (final_w if final else nw).reshape(1, d)
    return pl.pallas_call(
        functools.partial(_combine_body, tm=tm, final=final), grid=(t // tm,),
        in_specs=[pl.BlockSpec((tm, d), lambda i: (i, 0)),
                  pl.BlockSpec((tm, ROUTE_LANES), lambda i: (i, 0)),
                  pl.BlockSpec((8, tm), lambda i: (0, i), memory_space=pltpu.SMEM),
                  pl.BlockSpec(memory_space=pl.ANY),
                  pl.BlockSpec((1, d), lambda i: (0, 0))],
        out_specs=pl.BlockSpec((tm, d), lambda i: (i, 0)),
        out_shape=jax.ShapeDtypeStruct((t, d), F32),
        scratch_shapes=[pltpu.VMEM((tm * ROW_TILE, 128), F32), pltpu.VMEM((tm * ROW_TILE, 128), F32),
                        pltpu.SemaphoreType.DMA((2,))],
        compiler_params=_cparams(("arbitrary",)), name="moe_combine",
    )(h, info, pos, ys, fw)


def _s5_weights(lam_re, lam_im, b_re, b_im, c_re, c_im, log_dt, nsteps):
    hp = lax.Precision.HIGHEST
    L = S5_CHUNK
    g, p = lam_re.shape
    ch = b_re.shape[-1]
    lam = lax.complex(lam_re.astype(F32), lam_im.astype(F32))
    dt = jnp.exp(log_dt.astype(F32))[:, None]
    lam_bar = jnp.exp(lam * dt)
    b_bar = ((lam_bar - 1.0) / lam)[..., None] * lax.complex(b_re.astype(F32), b_im.astype(F32))
    cmat = lax.complex(c_re.astype(F32), c_im.astype(F32))
    pows = [jnp.ones_like(lam_bar)]
    for _ in range(L):
        pows.append(pows[-1] * lam_bar)
    pw = jnp.stack(pows, axis=1)
    kern = jnp.real(jnp.einsum('gop,gtp,gpi->gtoi', cmat, pw[:, :L], b_bar, precision=hp))
    lag = np.arange(L)[None, :] - np.arange(L)[:, None]
    toep = jnp.where((lag >= 0)[None, :, :, None, None], kern[:, np.maximum(lag, 0)], 0.0)
    toep = toep.transpose(0, 1, 4, 2, 3).reshape(g, L * ch, L * ch)
    wst = pw[:, L - 1 - np.arange(L)][:, :, :, None] * b_bar[:, None, :, :]
    wst = wst.transpose(0, 1, 3, 2).reshape(g, L * ch, p)
    wst = jnp.concatenate([jnp.real(wst), jnp.imag(wst)], axis=-1)
    mo = cmat.transpose(0, 2, 1)[:, :, None, :] * pw[:, 1:L + 1].transpose(0, 2, 1)[:, :, :, None]
    mo = mo.reshape(g, p, L * ch)
    wout = jnp.concatenate([jnp.real(mo), -jnp.imag(mo)], axis=1)
    a = pw[:, L]
    ars, ais = [], []
    for _ in range(nsteps):
        ars.append(jnp.concatenate([jnp.real(a), jnp.real(a)], axis=-1))
        ais.append(jnp.concatenate([-jnp.imag(a), jnp.imag(a)], axis=-1))
        a = a * a
    wcat = jnp.concatenate([toep, wst], axis=-1).astype(BF16)
    return wcat, wout.astype(BF16), jnp.stack(ars, axis=1), jnp.stack(ais, axis=1)


def _s5_body(u_ref, wcat_ref, wout_ref, ar_ref, ai_ref, y_ref, *, nb, nsteps):
    ny = y_ref.shape[2]
    r = _dot(u_ref[0], wcat_ref[0])
    y1 = r[:, :ny]
    z = r[:, ny:]
    rows, lanes = z.shape
    ridx = lax.broadcasted_iota(jnp.int32, z.shape, 0)

    def shift(x, k):
        return jnp.where(ridx >= k, pltpu.roll(x, k, axis=0), 0.0)

    w = shift(z, nb)
    for k in range(nsteps):
        if (nb << k) >= rows:
            break
        s = shift(w, nb << k)
        w = w + s * ar_ref[0, k:k + 1, :] + pltpu.roll(s, lanes // 2, axis=1) * ai_ref[0, k:k + 1, :]
    y_ref[0] = y1 + _dot(w.astype(BF16), wout_ref[0])


def _s5(u, nb, seq, lam_re, lam_im, b_re, b_im, c_re, c_im, log_dt):
    t, width = u.shape
    L = S5_CHUNK
    ch = S5_GROUP_CH
    g = width // ch
    nc = seq // L
    nsteps = max(1, (nc - 1).bit_length())
    wcat, wout, ar, ai = _s5_weights(lam_re, lam_im, b_re, b_im, c_re, c_im, log_dt, nsteps)
    ug = u.reshape(nb, nc, L, g, ch).transpose(3, 1, 0, 2, 4).reshape(g, nc * nb, L * ch).astype(BF16)
    rows = nc * nb
    body = functools.partial(_s5_body, nb=nb, nsteps=nsteps)
    y = pl.pallas_call(
        body, grid=(g,),
        in_specs=[pl.BlockSpec((1, rows, L * ch), lambda i: (i, 0, 0)),
                  pl.BlockSpec((1,) + wcat.shape[1:], lambda i: (i, 0, 0)),
                  pl.BlockSpec((1,) + wout.shape[1:], lambda i: (i, 0, 0)),
                  pl.BlockSpec((1,) + ar.shape[1:], lambda i: (i, 0, 0)),
                  pl.BlockSpec((1,) + ai.shape[1:], lambda i: (i, 0, 0))],
        out_specs=pl.BlockSpec((1, rows, L * ch), lambda i: (i, 0, 0)),
        out_shape=jax.ShapeDtypeStruct((g, rows, L * ch), F32),
        compiler_params=_cparams(("parallel",)), name="s5",
    )(ug, wcat, wout, ar, ai)
    return y.reshape(g, nc, nb, L, ch).transpose(2, 1, 3, 0, 4).reshape(t, width)


def _hgrn_gmat():
    L = CHUNK
    blocks = 2 + int(np.log2(L))
    gm = np.zeros((blocks * L, L), np.float32)
    for j in range(L):
        gm[j, :j + 1] = 1.0
        gm[L + j, j + 1:] = 1.0
    li, m = 2, L
    while m >= 2:
        half = m // 2
        for j in range(L):
            pos = j % m
            r = j - pos + half - 1
            if pos >= half:
                gm[li * L + j, r + 1:j + 1] = 1.0
            else:
                gm[li * L + j, j + 1:r + 1] = 1.0
        li += 1
        m //= 2
    return gm


def _hgrn_body(main_ref, gm_ref, lb_ref, nw_ref, out_ref, st_ref, *, nb, nh, dh):
    L = CHUNK
    w = nh * dh

    @pl.when(pl.program_id(0) == 0)
    def _init():
        st_ref[...] = jnp.zeros_like(st_ref)

    row = lax.broadcasted_iota(jnp.int32, (L, L), 0)
    col = lax.broadcasted_iota(jnp.int32, (L, L), 1)
    rowd = lax.broadcasted_iota(jnp.int32, (L, dh), 0)
    eye = row == col
    gm = gm_ref[...]
    lb = lb_ref[...]
    for b in range(nb):
        fg = main_ref[b, :, w:2 * w]
        f = lb + (1.0 - lb) * jax.nn.sigmoid(fg)
        kk = (1.0 - lb) * jax.nn.sigmoid(-fg)
        lf = _split3(jnp.log(f))
        p_all = jnp.exp(_dot(gm, lf[0]) + _dot(gm, lf[1]) + _dot(gm, lf[2]))
        for h in range(nh):
            idx = b * nh + h
            hs = slice(h * dh, (h + 1) * dh)
            q = main_ref[b, :, h * dh:(h + 1) * dh]
            v = main_ref[b, :, 2 * w + h * dh:2 * w + (h + 1) * dh]
            og = main_ref[b, :, 3 * w + h * dh:3 * w + (h + 1) * dh]
            k = kk[:, hs]
            pb = p_all[0:L, hs]
            pe = p_all[L:2 * L, hs]
            st = st_ref[idx]
            o = _dot_nt((q * pb).astype(BF16), st.astype(BF16))
            attn = jnp.where(eye, _dot_nt(q.astype(BF16), k.astype(BF16)), 0.0)
            li, m = 2, L
            while m >= 2:
                pl_ = p_all[li * L:(li + 1) * L, hs]
                up = (rowd & (m - 1)) >= (m // 2)
                ql = jnp.where(up, q * pl_, 0.0).astype(BF16)
                kl = jnp.where(up, 0.0, k * pl_).astype(BF16)
                same = (row & ~(m - 1)) == (col & ~(m - 1))
                attn = attn + jnp.where(same, _dot_nt(ql, kl), 0.0)
                li += 1
                m //= 2
            vb = v.astype(BF16)
            o = o + _dot(attn.astype(BF16), vb)
            st_ref[idx] = st * pb[L - 1:L, :] + _dot_tn(vb, (k * pe).astype(BF16))
            on = o * lax.rsqrt(jnp.mean(o * o, axis=-1, keepdims=True) + RMS_EPS)
            out_ref[b, :, hs] = on * nw_ref[:, hs] * jax.nn.silu(og)


def _hgrn(main, lower_bound, out_norm, nb, seq):
    t, n = main.shape
    nh = HGRN_HEADS
    w = out_norm.shape[0]
    dh = w // nh
    nc = seq // CHUNK
    gm = jnp.asarray(_hgrn_gmat(), BF16)
    body = functools.partial(_hgrn_body, nb=nb, nh=nh, dh=dh)
    out = pl.pallas_call(
        body, grid=(nc,),
        in_specs=[pl.BlockSpec((nb, CHUNK, 4 * w), lambda c: (0, c, 0)),
                  pl.BlockSpec(gm.shape, lambda c: (0, 0)),
                  pl.BlockSpec((1, w), lambda c: (0, 0)),
                  pl.BlockSpec((1, w), lambda c: (0, 0))],
        out_specs=pl.BlockSpec((nb, CHUNK, w), lambda c: (0, c, 0)),
        out_shape=jax.ShapeDtypeStruct((nb, seq, w), F32),
        scratch_shapes=[pltpu.VMEM((nb * nh, dh, dh), F32)],
        compiler_params=_cparams(("arbitrary",)), name="hgrn",
    )(main.reshape(nb, seq, n), gm, lower_bound.reshape(1, w), out_norm.reshape(1, w))
    return out.reshape(t, w)


def _out_c_body(h_ref, ys_ref, u_ref, oh_ref, d_ref, wglu_ref, bglu_ref, w_ref, out_ref):
    ws = ys_ref.shape[1]
    z = jax.nn.gelu(ys_ref[...] + d_ref[...] * u_ref[...])
    gate = jax.nn.sigmoid(_dot(z.astype(BF16), wglu_ref[...]) + bglu_ref[...])
    out_ref[...] = (h_ref[...] + _dot((z * gate).astype(BF16), w_ref[:ws, :])
                    + _dot(oh_ref[...].astype(BF16), w_ref[ws:, :]))


def _out_c(h, ys, main, oh, d_skip, w_glu, b_glu, w_out):
    t, d = h.shape
    ws = ys.shape[1]
    wh = oh.shape[1]
    tm = TM_PROJ
    ub = (main.shape[1] - ws) // ws
    return pl.pallas_call(
        _out_c_body, grid=(t // tm,),
        in_specs=[pl.BlockSpec((tm, d), lambda i: (i, 0)),
                  pl.BlockSpec((tm, ws), lambda i: (i, 0)),
                  pl.BlockSpec((tm, ws), lambda i: (i, ub)),
                  pl.BlockSpec((tm, wh), lambda i: (i, 0)),
                  pl.BlockSpec((1, ws), lambda i: (0, 0)),
                  pl.BlockSpec(w_glu.shape, lambda i: (0, 0)),
                  pl.BlockSpec((1, ws), lambda i: (0, 0)),
                  pl.BlockSpec(w_out.shape, lambda i: (0, 0))],
        out_specs=pl.BlockSpec((tm, d), lambda i: (i, 0)),
        out_shape=jax.ShapeDtypeStruct((t, d), F32),
        compiler_params=_cparams(("parallel",)), name="out_c",
    )(h, ys, main, oh, d_skip.reshape(1, ws), w_glu, b_glu.reshape(1, ws), w_out)


def kernel(x, norm_mix, norm_ffn, norm_final, ab_w_in, ab_gate_bias, ab_head_norm, ab_conv_w, ab_w_out, cd_w_in, s5_lambda_re, s5_lambda_im, s5_b_re, s5_b_im, s5_c_re, s5_c_im, s5_d, s5_log_dt, s5_w_glu, s5_b_glu, hgrn_lb, hgrn_out_norm, cd_w_out, moe_w_group, moe_b_group, moe_w_router, moe_b_router, moe_w_gate, moe_w_up, moe_w_down):
    nb, seq, d = x.shape
    depth = norm_mix.shape[0]
    h = x.reshape(nb * seq, d)
    for layer in range(depth):
        j = layer // 2
        if layer % 2 == 0:
            wm = ab_head_norm.shape[1]
            ng = ab_gate_bias.shape[1]
            w_in = ab_w_in[j]
            w_main = jnp.concatenate([w_in[:, :4 * wm], w_in[:, 4 * wm + ng:]], axis=1).astype(BF16)
            w_gates = w_in[:, 4 * wm:4 * wm + ng].astype(BF16)
            main, g, gt = _proj(h, norm_mix[layer], w_main, w_gates)
            hm = _mlstm(main, g, gt, ab_gate_bias[j], ab_head_norm[j], nb, seq)
            h = _out_a(h, hm, main, ab_conv_w[j], ab_w_out[j].astype(BF16), seq)
        else:
            ws = s5_d.shape[1]
            w_in = cd_w_in[j]
            w_main = jnp.concatenate([w_in[:, ws:], w_in[:, :ws]], axis=1).astype(BF16)
            main = _proj(h, norm_mix[layer], w_main)
            sm = jax.nn.softmax(hgrn_lb.astype(F32), axis=0)
            lower_bound = jnp.cumsum(sm, axis=0)[layer] - sm[0]
            ys = _s5(main[:, main.shape[1] - ws:], nb, seq, s5_lambda_re[j], s5_lambda_im[j], s5_b_re[j], s5_b_im[j],
                     s5_c_re[j], s5_c_im[j], s5_log_dt[j])
            oh = _hgrn(main, lower_bound, hgrn_out_norm[j], nb, seq)
            h = _out_c(h, ys, main, oh, s5_d[j], s5_w_glu[j].astype(BF16), s5_b_glu[j], cd_w_out[j].astype(BF16))
        h = _moe(h, norm_ffn[layer], moe_w_group[layer], moe_b_group[layer], moe_w_router[layer], moe_b_router[layer],
                 moe_w_gate[layer], moe_w_up[layer], moe_w_down[layer],
                 norm_final if layer == depth - 1 else None)
    return h.reshape(nb, seq, d)
```

```python
import functools

import numpy as np
import jax
import jax.numpy as jnp
from jax import lax
from jax.experimental import pallas as pl
from jax.experimental.pallas import tpu as pltpu

F32 = jnp.float32
BF16 = jnp.bfloat16
RMS_EPS = 1e-6
CHUNK = 64
S5_CHUNK = 16
S5_GROUP_CH = 16
S5_STATE = 64
MLSTM_HEADS = 4
HGRN_HEADS = 4
MOE_GROUPS = 4
MOE_EPG = 8
ROUTE_LANES = 128
TM_PROJ = 512
TM_MOE = 512
TM_EXPERT = 512
MOE_TOP_K = 2
VMEM_LIMIT = 56 * 1024 * 1024

_NT = (((1,), (1,)), ((), ()))
_TN = (((0,), (0,)), ((), ()))


def _cparams(sem):
    return pltpu.CompilerParams(dimension_semantics=sem, vmem_limit_bytes=VMEM_LIMIT)


def _rms(x, w):
    return x * lax.rsqrt(jnp.mean(x * x, axis=-1, keepdims=True) + RMS_EPS) * w


def _split3(x):
    hi = x.astype(BF16)
    r = x - hi.astype(F32)
    mid = r.astype(BF16)
    lo = (r - mid.astype(F32)).astype(BF16)
    return hi, mid, lo


def _dot(a, b):
    return jnp.dot(a, b, preferred_element_type=F32)


def _dot_nt(a, b):
    return lax.dot_general(a, b, _NT, preferred_element_type=F32)


def _dot_tn(a, b):
    return lax.dot_general(a, b, _TN, preferred_element_type=F32)


def _proj_gates_body(x_ref, nw_ref, w_ref, wg_ref, wgt_ref, main_ref, g_ref, gt_ref):
    xn = _rms(x_ref[...], nw_ref[...]).astype(BF16)
    main_ref[...] = _dot(xn, w_ref[...])
    g_ref[...] = _dot(xn, wg_ref[...])[:, : g_ref.shape[1]]
    gt_ref[...] = _dot_nt(wgt_ref[...], xn)


def _proj_body(x_ref, nw_ref, w_ref, main_ref):
    xn = _rms(x_ref[...], nw_ref[...]).astype(BF16)
    main_ref[...] = _dot(xn, w_ref[...])


def _proj(h, nw, w_main, w_gates=None):
    t, d = h.shape
    n = w_main.shape[1]
    tm = TM_PROJ
    x_spec = pl.BlockSpec((tm, d), lambda i: (i, 0))
    nw_spec = pl.BlockSpec((1, d), lambda i: (0, 0))
    w_spec = pl.BlockSpec((d, n), lambda i: (0, 0))
    main_spec = pl.BlockSpec((tm, n), lambda i: (i, 0))
    main_shape = jax.ShapeDtypeStruct((t, n), F32)
    if w_gates is None:
        return pl.pallas_call(
            _proj_body, grid=(t // tm,), in_specs=[x_spec, nw_spec, w_spec], out_specs=main_spec,
            out_shape=main_shape, compiler_params=_cparams(("parallel",)), name="proj",
        )(h, nw.reshape(1, d), w_main)
    ng = w_gates.shape[1]
    wg_pad = jnp.zeros((d, 128), BF16).at[:, :ng].set(w_gates)
    return pl.pallas_call(
        _proj_gates_body, grid=(t // tm,),
        in_specs=[x_spec, nw_spec, w_spec, pl.BlockSpec((d, 128), lambda i: (0, 0)),
                  pl.BlockSpec((ng, d), lambda i: (0, 0))],
        out_specs=[main_spec, pl.BlockSpec((tm, ng), lambda i: (i, 0)), pl.BlockSpec((ng, tm), lambda i: (0, i))],
        out_shape=[main_shape, jax.ShapeDtypeStruct((t, ng), F32), jax.ShapeDtypeStruct((ng, t), F32)],
        compiler_params=_cparams(("parallel",)), name="proj_gates",
    )(h, nw.reshape(1, d), w_main, wg_pad, w_gates.T)


def _mlstm_body(main_ref, g_ref, gt_ref, br_ref, bc_ref, hn_ref, out_ref, c_ref, n_ref, m_ref, *, nb, nh, dh):
    L = CHUNK
    w = nh * dh

    @pl.when(pl.program_id(0) == 0)
    def _init():
        c_ref[...] = jnp.zeros_like(c_ref)
        n_ref[...] = jnp.zeros_like(n_ref)
        m_ref[...] = jnp.full_like(m_ref, -1e30)

    row = lax.broadcasted_iota(jnp.int32, (L, L), 0)
    col = lax.broadcasted_iota(jnp.int32, (L, L), 1)
    causal = col <= row
    tril = causal.astype(BF16)
    triu = (row <= col).astype(BF16)
    scale = dh ** -0.5
    for b in range(nb):
        g = g_ref[b] + br_ref[...]
        gt = gt_ref[b, 0] + bc_ref[...]
        i_c = g[:, :nh]
        i_r = gt[:nh, :]
        lfc = _split3(jax.nn.log_sigmoid(g[:, nh:]))
        lfr = _split3(jax.nn.log_sigmoid(gt[nh:, :]))
        bc_all = _dot(tril, lfc[0]) + _dot(tril, lfc[1]) + _dot(tril, lfc[2])
        br_all = _dot(lfr[0], triu) + _dot(lfr[1], triu) + _dot(lfr[2], triu)
        for h in range(nh):
            idx = b * nh + h
            q = main_ref[b, :, h * dh:(h + 1) * dh]
            k = main_ref[b, :, w + h * dh:w + (h + 1) * dh] * scale
            v = main_ref[b, :, 2 * w + h * dh:2 * w + (h + 1) * dh]
            o = main_ref[b, :, 3 * w + h * dh:3 * w + (h + 1) * dh]
            bc = bc_all[:, h:h + 1]
            br = br_all[h:h + 1, :]
            ir = i_r[h:h + 1, :]
            ic = i_c[:, h:h + 1]
            m_prev = m_ref[idx]
            c_prev = c_ref[idx]
            n_prev = n_ref[idx]
            logw = jnp.where(causal, bc - br + ir, -jnp.inf)
            inter = bc + m_prev
            m_row = jnp.maximum(jnp.max(logw, axis=-1, keepdims=True), inter)
            qb = q.astype(BF16)
            kb = k.astype(BF16)
            vb = v.astype(BF16)
            s = _dot_nt(qb, kb) * jnp.exp(logw - m_row)
            isc = jnp.exp(inter - m_row)
            num = _dot(s.astype(BF16), vb) + isc * _dot_nt(qb, c_prev.astype(BF16))
            den = jnp.sum(s, axis=-1, keepdims=True) + isc * jnp.sum(q * n_prev, axis=-1, keepdims=True)
            hout = num / jnp.maximum(jnp.abs(den), jnp.exp(-m_row))
            b_end = bc[L - 1:L, :]
            logg = b_end - bc + ic
            m_new = jnp.maximum(b_end + m_prev, jnp.max(logg, axis=0, keepdims=True))
            wk = jnp.exp(logg - m_new)
            decay = jnp.exp(b_end + m_prev - m_new)
            c_ref[idx] = decay * c_prev + _dot_tn((v * wk).astype(BF16), kb)
            n_ref[idx] = decay * n_prev + jnp.sum(wk * k, axis=0, keepdims=True)
            m_ref[idx] = m_new
            hn = hout * lax.rsqrt(jnp.mean(hout * hout, axis=-1, keepdims=True) + RMS_EPS)
            out_ref[b, :, h * dh:(h + 1) * dh] = hn * hn_ref[:, h * dh:(h + 1) * dh] * jax.nn.sigmoid(o)


def _mlstm(main, g, gt, gate_bias, head_norm, nb, seq):
    t, n = main.shape
    nh = MLSTM_HEADS
    w = head_norm.shape[0]
    dh = w // nh
    nc = seq // CHUNK
    main3 = main.reshape(nb, seq, n)
    g3 = g.reshape(nb, seq, 2 * nh)
    gt4 = gt.reshape(2 * nh, nb, nc, CHUNK).transpose(1, 2, 0, 3)
    body = functools.partial(_mlstm_body, nb=nb, nh=nh, dh=dh)
    out = pl.pallas_call(
        body, grid=(nc,),
        in_specs=[pl.BlockSpec((nb, CHUNK, 4 * w), lambda c: (0, c, 0)),
                  pl.BlockSpec((nb, CHUNK, 2 * nh), lambda c: (0, c, 0)),
                  pl.BlockSpec((nb, 1, 2 * nh, CHUNK), lambda c: (0, c, 0, 0)),
                  pl.BlockSpec((1, 2 * nh), lambda c: (0, 0)),
                  pl.BlockSpec((2 * nh, 1), lambda c: (0, 0)),
                  pl.BlockSpec((1, w), lambda c: (0, 0))],
        out_specs=pl.BlockSpec((nb, CHUNK, w), lambda c: (0, c, 0)),
        out_shape=jax.ShapeDtypeStruct((nb, seq, w), F32),
        scratch_shapes=[pltpu.VMEM((nb * nh, dh, dh), F32), pltpu.VMEM((nb * nh, 1, dh), F32),
                        pltpu.VMEM((nb * nh, 1, 1), F32)],
        compiler_params=_cparams(("arbitrary",)), name="mlstm",
    )(main3, g3, gt4, gate_bias.reshape(1, 2 * nh), gate_bias.reshape(2 * nh, 1), head_norm.reshape(1, w))
    return out.reshape(t, w)


def _out_a_body(h_ref, hm_ref, gb_ref, gc_ref, xin_ref, pgc_ref, pxin_ref, cw_ref, w_ref, out_ref, *, tm, seq):
    i = pl.program_id(0)
    wm = hm_ref.shape[1]
    p = gc_ref[...] * xin_ref[...]
    first = (i * tm) % seq == 0
    pp = jnp.where(first, 0.0, pgc_ref[...] * pxin_ref[...])
    rowi = lax.broadcasted_iota(jnp.int32, p.shape, 0)
    p1 = jnp.where(rowi == 0, pp[7:8, :], pltpu.roll(p, 1, axis=0))
    p2 = jnp.where(rowi == 0, pp[6:7, :], jnp.where(rowi == 1, pp[7:8, :], pltpu.roll(p, 2, axis=0)))
    yc = gb_ref[...] * (cw_ref[0:1, :] * p2 + cw_ref[1:2, :] * p1 + cw_ref[2:3, :] * p)
    out_ref[...] = (h_ref[...] + _dot(hm_ref[...].astype(BF16), w_ref[:wm, :])
                    + _dot(yc.astype(BF16), w_ref[wm:, :]))


def _out_a(h, hm, main, conv_w, w_out, seq):
    t, d = h.shape
    wm = hm.shape[1]
    wc = conv_w.shape[1]
    tm = TM_PROJ
    cb = (4 * wm) // wc
    rb = tm // 8
    prev = lambda i: jnp.maximum(i * rb - 1, 0)
    body = functools.partial(_out_a_body, tm=tm, seq=seq)
    return pl.pallas_call(
        body, grid=(t // tm,),
        in_specs=[pl.BlockSpec((tm, d), lambda i: (i, 0)),
                  pl.BlockSpec((tm, wm), lambda i: (i, 0)),
                  pl.BlockSpec((tm, wc), lambda i: (i, cb)),
                  pl.BlockSpec((tm, wc), lambda i: (i, cb + 1)),
                  pl.BlockSpec((tm, wc), lambda i: (i, cb + 2)),
                  pl.BlockSpec((8, wc), lambda i: (prev(i), cb + 1)),
                  pl.BlockSpec((8, wc), lambda i: (prev(i), cb + 2)),
                  pl.BlockSpec(conv_w.shape, lambda i: (0, 0)),
                  pl.BlockSpec(w_out.shape, lambda i: (0, 0))],
        out_specs=pl.BlockSpec((tm, d), lambda i: (i, 0)),
        out_shape=jax.ShapeDtypeStruct((t, d), F32),
        compiler_params=_cparams(("parallel",)), name="out_a",
    )(h, hm, main, main, main, main, main, conv_w, w_out)


def _route(logits, lane):
    ne = MOE_GROUPS * MOE_EPG
    big = 1e9
    gl = jnp.where((lane >= ne) & (lane < ne + MOE_GROUPS), logits, -jnp.inf)
    gmax = jnp.max(gl, axis=-1, keepdims=True)
    gidx = jnp.min(jnp.where(gl == gmax, lane - ne, big), axis=-1, keepdims=True)
    gval = 1.0 / jnp.sum(jnp.exp(gl - gmax), axis=-1, keepdims=True)
    lo = gidx * MOE_EPG
    sel = jnp.where((lane >= lo) & (lane < lo + MOE_EPG), logits, -jnp.inf)
    l1 = jnp.max(sel, axis=-1, keepdims=True)
    i1 = jnp.min(jnp.where(sel == l1, lane, big), axis=-1, keepdims=True)
    sel2 = jnp.where(lane == i1, -jnp.inf, sel)
    l2 = jnp.max(sel2, axis=-1, keepdims=True)
    i2 = jnp.min(jnp.where(sel2 == l2, lane, big), axis=-1, keepdims=True)
    r = jnp.exp(l2 - l1)
    w1 = gval / (1.0 + r)
    return i1, i2, w1, w1 * r


ROW_TILE = 8


def _rows_to_tiles(ref, x):
    n = x.shape[0]
    for c in range(ROW_TILE):
        ref[pl.ds(c, n, stride=ROW_TILE), :] = x[:, c * 128:(c + 1) * 128]


def _tiles_to_rows(ref, n):
    return jnp.concatenate([ref[pl.ds(c, n, stride=ROW_TILE), :] for c in range(ROW_TILE)], axis=1)


def _lane_put(lane, cols):
    out = jnp.where(lane == 0.0, cols[0], 0.0)
    for k in range(1, len(cols)):
        out = out + jnp.where(lane == float(k), cols[k], 0.0)
    return out


def _lane_get(lane, x, idx_col):
    return jnp.sum(jnp.where(lane == idx_col, x, 0.0), axis=-1, keepdims=True)


def _route_body(h_ref, nw_ref, wr_ref, br_ref, tri_ref, info_ref, cnt_ref):
    @pl.when(pl.program_id(0) == 0)
    def _init():
        cnt_ref[...] = jnp.zeros_like(cnt_ref)

    xn = _rms(h_ref[...], nw_ref[...])
    hi = xn.astype(BF16)
    lo = (xn - hi.astype(F32)).astype(BF16)
    logits = _dot(jnp.concatenate([hi, lo, hi], axis=1), wr_ref[...]) + br_ref[...]
    lane = lax.broadcasted_iota(jnp.int32, logits.shape, 1).astype(F32)
    i1, i2, w1, w2 = _route(logits, lane)
    ind = jnp.where((lane == i1) | (lane == i2), 1.0, 0.0)
    before = _dot(tri_ref[...], ind.astype(BF16)) + cnt_ref[0:1, :]
    info_ref[...] = _lane_put(lane, [w1, w2, i1, i2, _lane_get(lane, before, i1), _lane_get(lane, before, i2)])
    cnt_ref[0:1, :] = cnt_ref[0:1, :] + jnp.sum(ind, axis=0, keepdims=True)


def _dispatch_body(cnt_s, off_s, tot_s, h_ref, nw_ref, info_ref, offrow_ref, sel_ref, xs_ref, pos_ref,
                   xn_buf, pos_v, pos_s, zbuf, sem, *, tm, tile_rows, ne, n_tiles):
    i = pl.program_id(0)
    _rows_to_tiles(xn_buf, _rms(h_ref[...], nw_ref[...]))
    info = info_ref[...]
    lane = lax.broadcasted_iota(jnp.int32, info.shape, 1).astype(F32)
    offrow = offrow_ref[...]
    p1 = _lane_get(lane, offrow, info[:, 2:3]) + info[:, 4:5]
    p2 = _lane_get(lane, offrow, info[:, 3:4]) + info[:, 5:6]
    h1 = jnp.floor(p1 * (1.0 / 256.0))
    h2 = jnp.floor(p2 * (1.0 / 256.0))
    pieces = _lane_put(lane, [h1, p1 - 256.0 * h1, h2, p2 - 256.0 * h2]).astype(BF16)
    rows = _dot_nt(sel_ref[...], pieces)
    sub = lax.broadcasted_iota(jnp.int32, rows.shape, 0)
    r1 = rows[0:1, :] * 256.0 + rows[1:2, :]
    r2 = rows[2:3, :] * 256.0 + rows[3:4, :]
    posall = jnp.where(sub == 0, r1, jnp.where(sub == 1, r2, 0.0)).astype(jnp.int32)
    pos_ref[...] = posall
    pos_v[...] = posall
    cp = pltpu.make_async_copy(pos_v, pos_s, sem.at[2])
    cp.start()
    cp.wait()

    def issue(t, carry):
        src = xn_buf.at[pl.ds(pl.multiple_of(t * ROW_TILE, ROW_TILE), ROW_TILE)]
        for k in range(MOE_TOP_K):
            at = pl.multiple_of(pos_s[k, t] * ROW_TILE, ROW_TILE)
            pltpu.make_async_copy(src, xs_ref.at[pl.ds(at, ROW_TILE)], sem.at[k]).start(priority=k)
        return carry

    lax.fori_loop(0, tm, issue, 0, unroll=8)
    for k in range(MOE_TOP_K):
        pltpu.make_async_copy(xn_buf, xs_ref.at[pl.ds(0, tm * ROW_TILE)], sem.at[k]).wait()

    @pl.when(i == pl.num_programs(0) - 1)
    def _zero_unused_rows():
        zbuf[...] = jnp.zeros_like(zbuf)

        def fill(row, nrows):
            at = pl.multiple_of(row * ROW_TILE, ROW_TILE)
            c = pltpu.make_async_copy(zbuf.at[pl.ds(0, nrows * ROW_TILE)], xs_ref.at[pl.ds(at, nrows * ROW_TILE)],
                                      sem.at[2])
            c.start()
            c.wait()

        for e in range(ne):
            n_pad = (tile_rows - cnt_s[e] % tile_rows) % tile_rows
            start = off_s[e] + cnt_s[e]
            size = tile_rows // 2
            while size >= 1:
                @pl.when((n_pad & size) != 0)
                def _fill(size=size, n_pad=n_pad, start=start):
                    fill(start + (n_pad & ~(2 * size - 1)), size)
                size //= 2

        def zero_tile(j, carry):
            for k in range(2):
                fill(j * tile_rows + k * (tile_rows // 2), tile_rows // 2)
            return carry

        lax.fori_loop(tot_s[0], n_tiles, zero_tile, 0)


def _expert_body(te_s, blk_s, tot_s, xs_ref, wg_ref, wu_ref, wd_ref, ys_ref):
    valid = pl.program_id(0) < tot_s[0]
    rows = xs_ref.shape[0] // ROW_TILE

    @pl.when(valid)
    def _run():
        x = _tiles_to_rows(xs_ref, rows).astype(BF16)
        hid = jax.nn.silu(_dot(x, wg_ref[0, 0].astype(BF16))) * _dot(x, wu_ref[0, 0].astype(BF16))
        _rows_to_tiles(ys_ref, _dot(hid.astype(BF16), wd_ref[0, 0].astype(BF16)))

    @pl.when(jnp.logical_not(valid))
    def _unused_tile():
        ys_ref[...] = jnp.zeros_like(ys_ref)


def _combine_body(h_ref, info_ref, pos_s, ys_ref, fw_ref, out_ref, y1, y2, sem, *, tm, final):
    bufs = (y1, y2)

    def issue(t, carry):
        dst = pl.ds(pl.multiple_of(t * ROW_TILE, ROW_TILE), ROW_TILE)
        for k in range(MOE_TOP_K):
            at = pl.multiple_of(pos_s[k, t] * ROW_TILE, ROW_TILE)
            pltpu.make_async_copy(ys_ref.at[pl.ds(at, ROW_TILE)], bufs[k].at[dst], sem.at[k]).start(priority=k)
        return carry

    lax.fori_loop(0, tm, issue, 0, unroll=8)
    for k in range(MOE_TOP_K):
        pltpu.make_async_copy(ys_ref.at[pl.ds(0, tm * ROW_TILE)], bufs[k], sem.at[k]).wait()
    o = h_ref[...] + info_ref[:, 0:1] * _tiles_to_rows(y1, tm) + info_ref[:, 1:2] * _tiles_to_rows(y2, tm)
    if final:
        o = _rms(o, fw_ref[...])
    out_ref[...] = o


def _router_weights(w_group, b_group, w_router, b_router):
    d, ne = w_router.shape
    ng = w_group.shape[1]
    w = jnp.zeros((d, ROUTE_LANES), F32).at[:, :ne].set(w_router).at[:, ne:ne + ng].set(w_group)
    hi = w.astype(BF16)
    lo = (w - hi.astype(F32)).astype(BF16)
    bias = jnp.zeros((1, ROUTE_LANES), F32).at[0, :ne].set(b_router).at[0, ne:ne + ng].set(b_group)
    return jnp.concatenate([hi, hi, lo], axis=0), bias


def _moe(h, nw, w_group, b_group, w_router, b_router, w_gate, w_up, w_down, layer, final_w):
    t, d = h.shape
    assert d == ROW_TILE * 128, "row-as-tile layout needs d_model == 1024"
    _, ne, _, ff = w_gate.shape
    tm = TM_MOE
    te_rows = TM_EXPERT
    nw2 = nw.reshape(1, d)
    wr3, bias = _router_weights(w_group, b_group, w_router, b_router)
    tri = jnp.asarray(np.tril(np.ones((tm, tm), np.float32), -1), BF16)
    info, cnt = pl.pallas_call(
        _route_body, grid=(t // tm,),
        in_specs=[pl.BlockSpec((tm, d), lambda i: (i, 0)),
                  pl.BlockSpec((1, d), lambda i: (0, 0)),
                  pl.BlockSpec((3 * d, ROUTE_LANES), lambda i: (0, 0)),
                  pl.BlockSpec((1, ROUTE_LANES), lambda i: (0, 0)),
                  pl.BlockSpec((tm, tm), lambda i: (0, 0))],
        out_specs=[pl.BlockSpec((tm, ROUTE_LANES), lambda i: (i, 0)), pl.BlockSpec((8, ROUTE_LANES), lambda i: (0, 0))],
        out_shape=[jax.ShapeDtypeStruct((t, ROUTE_LANES), F32), jax.ShapeDtypeStruct((8, ROUTE_LANES), F32)],
        compiler_params=_cparams(("arbitrary",)), name="moe_route",
    )(h, nw2, wr3, bias, tri)

    cnt_i = cnt[0, :ne].astype(jnp.int32)
    ntile = (cnt_i + te_rows - 1) // te_rows
    tile_end = jnp.cumsum(ntile)
    off_i = (tile_end - ntile) * te_rows
    n_tiles = (MOE_TOP_K * t) // te_rows + ne
    rows_total = n_tiles * te_rows
    ti = jnp.arange(n_tiles, dtype=jnp.int32)
    tot = tile_end[-1:]
    ti_c = jnp.minimum(ti, tot[0] - 1)
    tile_e = jnp.sum((ti_c[:, None] >= tile_end[None, :]).astype(jnp.int32), axis=1)
    off_row = jnp.zeros((1, ROUTE_LANES), F32).at[0, :ne].set(off_i.astype(F32))
    sel = jnp.asarray(np.eye(8, ROUTE_LANES, dtype=np.float32), BF16)

    xs, pos = pl.pallas_call(
        functools.partial(_dispatch_body, tm=tm, tile_rows=te_rows, ne=ne, n_tiles=n_tiles),
        grid_spec=pltpu.PrefetchScalarGridSpec(
            num_scalar_prefetch=3, grid=(t // tm,),
            in_specs=[pl.BlockSpec((tm, d), lambda i, *_: (i, 0)),
                      pl.BlockSpec((1, d), lambda i, *_: (0, 0)),
                      pl.BlockSpec((tm, ROUTE_LANES), lambda i, *_: (i, 0)),
                      pl.BlockSpec((1, ROUTE_LANES), lambda i, *_: (0, 0)),
                      pl.BlockSpec((8, ROUTE_LANES), lambda i, *_: (0, 0))],
            out_specs=[pl.BlockSpec(memory_space=pl.ANY), pl.BlockSpec((8, tm), lambda i, *_: (0, i))],
            scratch_shapes=[pltpu.VMEM((tm * ROW_TILE, 128), F32), pltpu.VMEM((8, tm), jnp.int32),
                            pltpu.SMEM((8, tm), jnp.int32), pltpu.VMEM((te_rows // 2 * ROW_TILE, 128), F32),
                            pltpu.SemaphoreType.DMA((3,))]),
        out_shape=[jax.ShapeDtypeStruct((rows_total * ROW_TILE, 128), F32), jax.ShapeDtypeStruct((8, t), jnp.int32)],
        compiler_params=_cparams(("arbitrary",)), name="moe_dispatch",
    )(cnt_i, off_i, tot, h, nw2, info, off_row, sel)

    ys = pl.pallas_call(
        _expert_body,
        grid_spec=pltpu.PrefetchScalarGridSpec(
            num_scalar_prefetch=3, grid=(n_tiles,),
            in_specs=[pl.BlockSpec((te_rows * ROW_TILE, 128), lambda i, e, b, v: (b[i], 0)),
                      pl.BlockSpec((1, 1, d, ff), lambda i, e, b, v: (layer, e[i], 0, 0)),
                      pl.BlockSpec((1, 1, d, ff), lambda i, e, b, v: (layer, e[i], 0, 0)),
                      pl.BlockSpec((1, 1, ff, d), lambda i, e, b, v: (layer, e[i], 0, 0))],
            out_specs=pl.BlockSpec((te_rows * ROW_TILE, 128), lambda i, e, b, v: (i, 0))),
        out_shape=jax.ShapeDtypeStruct((rows_total * ROW_TILE, 128), F32),
        compiler_params=_cparams(("arbitrary",)), name="moe_expert",
    )(tile_e, ti_c, tot, xs, w_gate, w_up, w_down)

    final = final_w is not None
    fw = (final_w if final else nw).reshape(1, d)
    return pl.pallas_call(
        functools.partial(_combine_body, tm=tm, final=final), grid=(t // tm,),
        in_specs=[pl.BlockSpec((tm, d), lambda i: (i, 0)),
                  pl.BlockSpec((tm, ROUTE_LANES), lambda i: (i, 0)),
                  pl.BlockSpec((8, tm), lambda i: (0, i), memory_space=pltpu.SMEM),
                  pl.BlockSpec(memory_space=pl.ANY),
                  pl.BlockSpec((1, d), lambda i: (0, 0))],
        out_specs=pl.BlockSpec((tm, d), lambda i: (i, 0)),
        out_shape=jax.ShapeDtypeStruct((t, d), F32),
        scratch_shapes=[pltpu.VMEM((tm * ROW_TILE, 128), F32), pltpu.VMEM((tm * ROW_TILE, 128), F32),
                        pltpu.SemaphoreType.DMA((2,))],
        compiler_params=_cparams(("arbitrary",)), name="moe_combine",
    )(h, info, pos, ys, fw)


def _s5_weights(lam_re, lam_im, b_re, b_im, c_re, c_im, log_dt, nsteps):
    hp = lax.Precision.HIGHEST
    L = S5_CHUNK
    g, p = lam_re.shape
    ch = b_re.shape[-1]
    lam = lax.complex(lam_re.astype(F32), lam_im.astype(F32))
    dt = jnp.exp(log_dt.astype(F32))[:, None]
    lam_bar = jnp.exp(lam * dt)
    b_bar = ((lam_bar - 1.0) / lam)[..., None] * lax.complex(b_re.astype(F32), b_im.astype(F32))
    cmat = lax.complex(c_re.astype(F32), c_im.astype(F32))
    pows = [jnp.ones_like(lam_bar)]
    for _ in range(L):
        pows.append(pows[-1] * lam_bar)
    pw = jnp.stack(pows, axis=1)
    kern = jnp.real(jnp.einsum('gop,gtp,gpi->gtoi', cmat, pw[:, :L], b_bar, precision=hp))
    lag = np.arange(L)[None, :] - np.arange(L)[:, None]
    toep = jnp.where((lag >= 0)[None, :, :, None, None], kern[:, np.maximum(lag, 0)], 0.0)
    toep = toep.transpose(0, 1, 4, 2, 3).reshape(g, L * ch, L * ch)
    wst = pw[:, L - 1 - np.arange(L)][:, :, :, None] * b_bar[:, None, :, :]
    wst = wst.transpose(0, 1, 3, 2).reshape(g, L * ch, p)
    wst = jnp.concatenate([jnp.real(wst), jnp.imag(wst)], axis=-1)
    mo = cmat.transpose(0, 2, 1)[:, :, None, :] * pw[:, 1:L + 1].transpose(0, 2, 1)[:, :, :, None]
    mo = mo.reshape(g, p, L * ch)
    wout = jnp.concatenate([jnp.real(mo), -jnp.imag(mo)], axis=1)
    a = pw[:, L]
    ars, ais = [], []
    for _ in range(nsteps):
        ars.append(jnp.concatenate([jnp.real(a), jnp.real(a)], axis=-1))
        ais.append(jnp.concatenate([-jnp.imag(a), jnp.imag(a)], axis=-1))
        a = a * a
    wcat = jnp.concatenate([toep, wst], axis=-1).astype(BF16)
    return wcat, wout.astype(BF16), jnp.stack(ars, axis=1), jnp.stack(ais, axis=1)


def _s5_body(u_ref, wcat_ref, wout_ref, ar_ref, ai_ref, y_ref, us_ref, ys_ref, *, nsteps):
    L = S5_CHUNK
    ch = S5_GROUP_CH
    gpc = 128 // ch
    ny = L * ch
    nc = u_ref.shape[0] // L
    for s in range(L):
        us_ref[s] = u_ref[pl.ds(s, nc, stride=L), :]
    lane = lax.broadcasted_iota(jnp.int32, (nc, 128), 1)
    ridx = lax.broadcasted_iota(jnp.int32, (nc, 128), 0)

    def shift(x, k):
        return jnp.where(ridx >= k, pltpu.roll(x, k, axis=0), 0.0)

    for gi in range(gpc):
        halves = []
        for hh in range(ny // 128):
            acc = None
            for s8 in range(gpc):
                rot = ((s8 - gi) * ch) % 128
                src = us_ref[hh * gpc + s8]
                if rot:
                    src = pltpu.roll(src, rot, axis=1)
                slot = (lane >= s8 * ch) & (lane < (s8 + 1) * ch)
                acc = jnp.where(slot, src, 0.0) if acc is None else jnp.where(slot, src, acc)
            halves.append(acc)
        ug = jnp.concatenate(halves, axis=1).astype(BF16)
        r = _dot(ug, wcat_ref[gi])
        y1 = r[:, :ny]
        z = r[:, ny:]
        w = shift(z, 1)
        for k in range(nsteps):
            if (1 << k) >= nc:
                break
            sk = shift(w, 1 << k)
            w = w + sk * ar_ref[gi, k:k + 1, :] + pltpu.roll(sk, z.shape[1] // 2, axis=1) * ai_ref[gi, k:k + 1, :]
        yg = y1 + _dot(w.astype(BF16), wout_ref[gi])
        slot = (lane >= gi * ch) & (lane < (gi + 1) * ch)
        for t in range(L):
            src = yg[:, (t // gpc) * 128:(t // gpc + 1) * 128]
            rot = ((gi - t % gpc) * ch) % 128
            if rot:
                src = pltpu.roll(src, rot, axis=1)
            ys_ref[t] = jnp.where(slot, src, 0.0) if gi == 0 else jnp.where(slot, src, ys_ref[t])
    for t in range(L):
        y_ref[pl.ds(t, nc, stride=L), :] = ys_ref[t]


def _s5(main, col0, width, nb, seq, lam_re, lam_im, b_re, b_im, c_re, c_im, log_dt):
    t = main.shape[0]
    L = S5_CHUNK
    ch = S5_GROUP_CH
    g = width // ch
    nc = seq // L
    gpc = 128 // ch
    ncol = width // 128
    assert col0 % 128 == 0 and width % 128 == 0 and (L * ch) % 128 == 0
    nsteps = max(1, (nc - 1).bit_length())
    wcat, wout, ar, ai = _s5_weights(lam_re, lam_im, b_re, b_im, c_re, c_im, log_dt, nsteps)
    body = functools.partial(_s5_body, nsteps=nsteps)
    return pl.pallas_call(
        body, grid=(ncol, nb),
        in_specs=[pl.BlockSpec((seq, 128), lambda j, b: (b, col0 // 128 + j)),
                  pl.BlockSpec((gpc,) + wcat.shape[1:], lambda j, b: (j, 0, 0)),
                  pl.BlockSpec((gpc,) + wout.shape[1:], lambda j, b: (j, 0, 0)),
                  pl.BlockSpec((gpc,) + ar.shape[1:], lambda j, b: (j, 0, 0)),
                  pl.BlockSpec((gpc,) + ai.shape[1:], lambda j, b: (j, 0, 0))],
        out_specs=pl.BlockSpec((seq, 128), lambda j, b: (b, j)),
        out_shape=jax.ShapeDtypeStruct((t, width), F32),
        scratch_shapes=[pltpu.VMEM((L, nc, 128), F32), pltpu.VMEM((L, nc, 128), F32)],
        compiler_params=_cparams(("parallel", "parallel")), name="s5",
    )(main, wcat, wout, ar, ai)


def _hgrn_gmat():
    L = CHUNK
    blocks = 2 + int(np.log2(L))
    gm = np.zeros((blocks * L, L), np.float32)
    for j in range(L):
        gm[j, :j + 1] = 1.0
        gm[L + j, j + 1:] = 1.0
    li, m = 2, L
    while m >= 2:
        half = m // 2
        for j in range(L):
            pos = j % m
            r = j - pos + half - 1
            if pos >= half:
                gm[li * L + j, r + 1:j + 1] = 1.0
            else:
                gm[li * L + j, j + 1:r + 1] = 1.0
        li += 1
        m //= 2
    return gm


def _hgrn_body(main_ref, gm_ref, lb_ref, nw_ref, out_ref, st_ref, *, nb, nh, dh):
    L = CHUNK
    w = nh * dh

    @pl.when(pl.program_id(0) == 0)
    def _init():
        st_ref[...] = jnp.zeros_like(st_ref)

    row = lax.broadcasted_iota(jnp.int32, (L, L), 0)
    col = lax.broadcasted_iota(jnp.int32, (L, L), 1)
    rowd = lax.broadcasted_iota(jnp.int32, (L, dh), 0)
    eye = row == col
    gm = gm_ref[...]
    lb = lb_ref[...]
    for b in range(nb):
        fg = main_ref[b, :, w:2 * w]
        f = lb + (1.0 - lb) * jax.nn.sigmoid(fg)
        kk = (1.0 - lb) * jax.nn.sigmoid(-fg)
        lf = _split3(jnp.log(f))
        p_all = jnp.exp(_dot(gm, lf[0]) + _dot(gm, lf[1]) + _dot(gm, lf[2]))
        for h in range(nh):
            idx = b * nh + h
            hs = slice(h * dh, (h + 1) * dh)
            q = main_ref[b, :, h * dh:(h + 1) * dh]
            v = main_ref[b, :, 2 * w + h * dh:2 * w + (h + 1) * dh]
            og = main_ref[b, :, 3 * w + h * dh:3 * w + (h + 1) * dh]
            k = kk[:, hs]
            pb = p_all[0:L, hs]
            pe = p_all[L:2 * L, hs]
            st = st_ref[idx]
            o = _dot_nt((q * pb).astype(BF16), st.astype(BF16))
            attn = jnp.where(eye, _dot_nt(q.astype(BF16), k.astype(BF16)), 0.0)
            li, m = 2, L
            while m >= 2:
                pl_ = p_all[li * L:(li + 1) * L, hs]
                up = (rowd & (m - 1)) >= (m // 2)
                ql = jnp.where(up, q * pl_, 0.0).astype(BF16)
                kl = jnp.where(up, 0.0, k * pl_).astype(BF16)
                same = (row & ~(m - 1)) == (col & ~(m - 1))
                attn = attn + jnp.where(same, _dot_nt(ql, kl), 0.0)
                li += 1
                m //= 2
            vb = v.astype(BF16)
            o = o + _dot(attn.astype(BF16), vb)
            st_ref[idx] = st * pb[L - 1:L, :] + _dot_tn(vb, (k * pe).astype(BF16))
            on = o * lax.rsqrt(jnp.mean(o * o, axis=-1, keepdims=True) + RMS_EPS)
            out_ref[b, :, hs] = on * nw_ref[:, hs] * jax.nn.silu(og)


def _hgrn(main, lower_bound, out_norm, nb, seq):
    t, n = main.shape
    nh = HGRN_HEADS
    w = out_norm.shape[0]
    dh = w // nh
    nc = seq // CHUNK
    gm = jnp.asarray(_hgrn_gmat(), BF16)
    body = functools.partial(_hgrn_body, nb=nb, nh=nh, dh=dh)
    out = pl.pallas_call(
        body, grid=(nc,),
        in_specs=[pl.BlockSpec((nb, CHUNK, 4 * w), lambda c: (0, c, 0)),
                  pl.BlockSpec(gm.shape, lambda c: (0, 0)),
                  pl.BlockSpec((1, w), lambda c: (0, 0)),
                  pl.BlockSpec((1, w), lambda c: (0, 0))],
        out_specs=pl.BlockSpec((nb, CHUNK, w), lambda c: (0, c, 0)),
        out_shape=jax.ShapeDtypeStruct((nb, seq, w), F32),
        scratch_shapes=[pltpu.VMEM((nb * nh, dh, dh), F32)],
        compiler_params=_cparams(("arbitrary",)), name="hgrn",
    )(main.reshape(nb, seq, n), gm, lower_bound.reshape(1, w), out_norm.reshape(1, w))
    return out.reshape(t, w)


def _out_c_body(h_ref, ys_ref, u_ref, oh_ref, d_ref, wglu_ref, bglu_ref, w_ref, out_ref):
    ws = ys_ref.shape[1]
    z = jax.nn.gelu(ys_ref[...] + d_ref[...] * u_ref[...])
    gate = jax.nn.sigmoid(_dot(z.astype(BF16), wglu_ref[...]) + bglu_ref[...])
    out_ref[...] = (h_ref[...] + _dot((z * gate).astype(BF16), w_ref[:ws, :])
                    + _dot(oh_ref[...].astype(BF16), w_ref[ws:, :]))


def _out_c(h, ys, main, oh, d_skip, w_glu, b_glu, w_out):
    t, d = h.shape
    ws = ys.shape[1]
    wh = oh.shape[1]
    tm = TM_PROJ
    ub = (main.shape[1] - ws) // ws
    return pl.pallas_call(
        _out_c_body, grid=(t // tm,),
        in_specs=[pl.BlockSpec((tm, d), lambda i: (i, 0)),
                  pl.BlockSpec((tm, ws), lambda i: (i, 0)),
                  pl.BlockSpec((tm, ws), lambda i: (i, ub)),
                  pl.BlockSpec((tm, wh), lambda i: (i, 0)),
                  pl.BlockSpec((1, ws), lambda i: (0, 0)),
                  pl.BlockSpec(w_glu.shape, lambda i: (0, 0)),
                  pl.BlockSpec((1, ws), lambda i: (0, 0)),
                  pl.BlockSpec(w_out.shape, lambda i: (0, 0))],
        out_specs=pl.BlockSpec((tm, d), lambda i: (i, 0)),
        out_shape=jax.ShapeDtypeStruct((t, d), F32),
        compiler_params=_cparams(("parallel",)), name="out_c",
    )(h, ys, main, oh, d_skip.reshape(1, ws), w_glu, b_glu.reshape(1, ws), w_out)


def kernel(x, norm_mix, norm_ffn, norm_final, ab_w_in, ab_gate_bias, ab_head_norm, ab_conv_w, ab_w_out, cd_w_in, s5_lambda_re, s5_lambda_im, s5_b_re, s5_b_im, s5_c_re, s5_c_im, s5_d, s5_log_dt, s5_w_glu, s5_b_glu, hgrn_lb, hgrn_out_norm, cd_w_out, moe_w_group, moe_b_group, moe_w_router, moe_b_router, moe_w_gate, moe_w_up, moe_w_down):
    nb, seq, d = x.shape
    depth = norm_mix.shape[0]
    h = x.reshape(nb * seq, d)
    for layer in range(depth):
        j = layer // 2
        if layer % 2 == 0:
            wm = ab_head_norm.shape[1]
            ng = ab_gate_bias.shape[1]
            w_in = ab_w_in[j]
            w_main = jnp.concatenate([w_in[:, :4 * wm], w_in[:, 4 * wm + ng:]], axis=1).astype(BF16)
            w_gates = w_in[:, 4 * wm:4 * wm + ng].astype(BF16)
            main, g, gt = _proj(h, norm_mix[layer], w_main, w_gates)
            hm = _mlstm(main, g, gt, ab_gate_bias[j], ab_head_norm[j], nb, seq)
            h = _out_a(h, hm, main, ab_conv_w[j], ab_w_out[j].astype(BF16), seq)
        else:
            ws = s5_d.shape[1]
            w_in = cd_w_in[j]
            w_main = jnp.concatenate([w_in[:, ws:], w_in[:, :ws]], axis=1).astype(BF16)
            main = _proj(h, norm_mix[layer], w_main)
            sm = jax.nn.softmax(hgrn_lb.astype(F32), axis=0)
            lower_bound = jnp.cumsum(sm, axis=0)[layer] - sm[0]
            ys = _s5(main, main.shape[1] - ws, ws, nb, seq, s5_lambda_re[j], s5_lambda_im[j], s5_b_re[j], s5_b_im[j],
                     s5_c_re[j], s5_c_im[j], s5_log_dt[j])
            oh = _hgrn(main, lower_bound, hgrn_out_norm[j], nb, seq)
            h = _out_c(h, ys, main, oh, s5_d[j], s5_w_glu[j].astype(BF16), s5_b_glu[j], cd_w_out[j].astype(BF16))
        h = _moe(h, norm_ffn[layer], moe_w_group[layer], moe_b_group[layer], moe_w_router[layer], moe_b_router[layer],
                 moe_w_gate, moe_w_up, moe_w_down, layer,
                 norm_final if layer == depth - 1 else None)
    return h.reshape(nb, seq, d)
```

```python
import functools

import numpy as np
import jax
import jax.numpy as jnp
from jax import lax
from jax.experimental import pallas as pl
from jax.experimental.pallas import tpu as pltpu

F32 = jnp.float32
BF16 = jnp.bfloat16
RMS_EPS = 1e-6
MLSTM_CHUNK = 512
HGRN_CHUNK = 128
S5_CHUNK = 16
S5_GROUP_CH = 16
S5_STATE = 64
MLSTM_HEADS = 4
HGRN_HEADS = 4
MOE_GROUPS = 4
MOE_EPG = 8
ROUTE_LANES = 128
TM_PROJ = 512
TM_MOE = 512
TM_EXPERT = 512
MOE_TOP_K = 2
VMEM_LIMIT = 56 * 1024 * 1024

_NT = (((1,), (1,)), ((), ()))
_TN = (((0,), (0,)), ((), ()))


def _cparams(sem):
    return pltpu.CompilerParams(dimension_semantics=sem, vmem_limit_bytes=VMEM_LIMIT)


def _rms(x, w):
    return x * lax.rsqrt(jnp.mean(x * x, axis=-1, keepdims=True) + RMS_EPS) * w


def _split3(x):
    hi = x.astype(BF16)
    r = x - hi.astype(F32)
    mid = r.astype(BF16)
    lo = (r - mid.astype(F32)).astype(BF16)
    return hi, mid, lo


def _dot(a, b):
    return jnp.dot(a, b, preferred_element_type=F32)


def _dot_nt(a, b):
    return lax.dot_general(a, b, _NT, preferred_element_type=F32)


def _dot_tn(a, b):
    return lax.dot_general(a, b, _TN, preferred_element_type=F32)


def _proj_gates_body(x_ref, nw_ref, w_ref, wg_ref, wgt_ref, main_ref, g_ref, gt_ref):
    xn = _rms(x_ref[...], nw_ref[...]).astype(BF16)
    main_ref[...] = _dot(xn, w_ref[...])
    g_ref[...] = _dot(xn, wg_ref[...])[:, : g_ref.shape[1]]
    gt_ref[...] = _dot_nt(wgt_ref[...], xn)


def _proj_body(x_ref, nw_ref, w_ref, main_ref):
    xn = _rms(x_ref[...], nw_ref[...]).astype(BF16)
    main_ref[...] = _dot(xn, w_ref[...])


def _proj(h, nw, w_main, w_gates=None):
    t, d = h.shape
    n = w_main.shape[1]
    tm = TM_PROJ
    x_spec = pl.BlockSpec((tm, d), lambda i: (i, 0))
    nw_spec = pl.BlockSpec((1, d), lambda i: (0, 0))
    w_spec = pl.BlockSpec((d, n), lambda i: (0, 0))
    main_spec = pl.BlockSpec((tm, n), lambda i: (i, 0))
    main_shape = jax.ShapeDtypeStruct((t, n), F32)
    if w_gates is None:
        return pl.pallas_call(
            _proj_body, grid=(t // tm,), in_specs=[x_spec, nw_spec, w_spec], out_specs=main_spec,
            out_shape=main_shape, compiler_params=_cparams(("parallel",)), name="proj",
        )(h, nw.reshape(1, d), w_main)
    ng = w_gates.shape[1]
    wg_pad = jnp.zeros((d, 128), BF16).at[:, :ng].set(w_gates)
    return pl.pallas_call(
        _proj_gates_body, grid=(t // tm,),
        in_specs=[x_spec, nw_spec, w_spec, pl.BlockSpec((d, 128), lambda i: (0, 0)),
                  pl.BlockSpec((ng, d), lambda i: (0, 0))],
        out_specs=[main_spec, pl.BlockSpec((tm, ng), lambda i: (i, 0)), pl.BlockSpec((ng, tm), lambda i: (0, i))],
        out_shape=[main_shape, jax.ShapeDtypeStruct((t, ng), F32), jax.ShapeDtypeStruct((ng, t), F32)],
        compiler_params=_cparams(("parallel",)), name="proj_gates",
    )(h, nw.reshape(1, d), w_main, wg_pad, w_gates.T)


def _mlstm_body(main_ref, g_ref, gt_ref, br_ref, bc_ref, hn_ref, out_ref, c_ref, n_ref, m_ref, *, nb, nh, dh):
    L = MLSTM_CHUNK
    w = nh * dh

    @pl.when(pl.program_id(0) == 0)
    def _init():
        c_ref[...] = jnp.zeros_like(c_ref)
        n_ref[...] = jnp.zeros_like(n_ref)
        m_ref[...] = jnp.full_like(m_ref, -1e30)

    row = lax.broadcasted_iota(jnp.int32, (L, L), 0)
    col = lax.broadcasted_iota(jnp.int32, (L, L), 1)
    causal = col <= row
    tril = causal.astype(BF16)
    triu = (row <= col).astype(BF16)
    scale = dh ** -0.5
    for b in range(nb):
        g = g_ref[b] + br_ref[...]
        gt = gt_ref[b, 0] + bc_ref[...]
        i_c = g[:, :nh]
        i_r = gt[:nh, :]
        lfc = _split3(jax.nn.log_sigmoid(g[:, nh:]))
        lfr = _split3(jax.nn.log_sigmoid(gt[nh:, :]))
        bc_all = _dot(tril, lfc[0]) + _dot(tril, lfc[1]) + _dot(tril, lfc[2])
        br_all = _dot(lfr[0], triu) + _dot(lfr[1], triu) + _dot(lfr[2], triu)
        for h in range(nh):
            idx = b * nh + h
            q = main_ref[b, :, h * dh:(h + 1) * dh]
            k = main_ref[b, :, w + h * dh:w + (h + 1) * dh] * scale
            v = main_ref[b, :, 2 * w + h * dh:2 * w + (h + 1) * dh]
            o = main_ref[b, :, 3 * w + h * dh:3 * w + (h + 1) * dh]
            bc = bc_all[:, h:h + 1]
            br = br_all[h:h + 1, :]
            ir = i_r[h:h + 1, :]
            ic = i_c[:, h:h + 1]
            m_prev = m_ref[idx]
            c_prev = c_ref[idx]
            n_prev = n_ref[idx]
            logw = jnp.where(causal, bc - br + ir, -jnp.inf)
            inter = bc + m_prev
            m_row = jnp.maximum(jnp.max(logw, axis=-1, keepdims=True), inter)
            qb = q.astype(BF16)
            kb = k.astype(BF16)
            vb = v.astype(BF16)
            s = _dot_nt(qb, kb) * jnp.exp(logw - m_row)
            isc = jnp.exp(inter - m_row)
            num = _dot(s.astype(BF16), vb) + isc * _dot_nt(qb, c_prev.astype(BF16))
            den = jnp.sum(s, axis=-1, keepdims=True) + isc * jnp.sum(q * n_prev, axis=-1, keepdims=True)
            hout = num / jnp.maximum(jnp.abs(den), jnp.exp(-m_row))
            b_end = bc[L - 1:L, :]
            logg = b_end - bc + ic
            m_new = jnp.maximum(b_end + m_prev, jnp.max(logg, axis=0, keepdims=True))
            wk = jnp.exp(logg - m_new)
            decay = jnp.exp(b_end + m_prev - m_new)
            c_ref[idx] = decay * c_prev + _dot_tn((v * wk).astype(BF16), kb)
            n_ref[idx] = decay * n_prev + jnp.sum(wk * k, axis=0, keepdims=True)
            m_ref[idx] = m_new
            hn = hout * lax.rsqrt(jnp.mean(hout * hout, axis=-1, keepdims=True) + RMS_EPS)
            out_ref[b, :, h * dh:(h + 1) * dh] = hn * hn_ref[:, h * dh:(h + 1) * dh] * jax.nn.sigmoid(o)


def _mlstm(main, g, gt, gate_bias, head_norm, nb, seq):
    t, n = main.shape
    nh = MLSTM_HEADS
    w = head_norm.shape[0]
    dh = w // nh
    L = MLSTM_CHUNK
    nc = seq // L
    main3 = main.reshape(nb, seq, n)
    g3 = g.reshape(nb, seq, 2 * nh)
    gt4 = gt.reshape(2 * nh, nb, nc, L).transpose(1, 2, 0, 3)
    body = functools.partial(_mlstm_body, nb=nb, nh=nh, dh=dh)
    out = pl.pallas_call(
        body, grid=(nc,),
        in_specs=[pl.BlockSpec((nb, L, 4 * w), lambda c: (0, c, 0)),
                  pl.BlockSpec((nb, L, 2 * nh), lambda c: (0, c, 0)),
                  pl.BlockSpec((nb, 1, 2 * nh, L), lambda c: (0, c, 0, 0)),
                  pl.BlockSpec((1, 2 * nh), lambda c: (0, 0)),
                  pl.BlockSpec((2 * nh, 1), lambda c: (0, 0)),
                  pl.BlockSpec((1, w), lambda c: (0, 0))],
        out_specs=pl.BlockSpec((nb, L, w), lambda c: (0, c, 0)),
        out_shape=jax.ShapeDtypeStruct((nb, seq, w), F32),
        scratch_shapes=[pltpu.VMEM((nb * nh, dh, dh), F32), pltpu.VMEM((nb * nh, 1, dh), F32),
                        pltpu.VMEM((nb * nh, 1, 1), F32)],
        compiler_params=_cparams(("arbitrary",)), name="mlstm",
    )(main3, g3, gt4, gate_bias.reshape(1, 2 * nh), gate_bias.reshape(2 * nh, 1), head_norm.reshape(1, w))
    return out.reshape(t, w)


def _out_a_body(h_ref, hm_ref, gb_ref, gc_ref, xin_ref, pgc_ref, pxin_ref, cw_ref, w_ref, out_ref, *, tm, seq):
    i = pl.program_id(0)
    wm = hm_ref.shape[1]
    p = gc_ref[...] * xin_ref[...]
    first = (i * tm) % seq == 0
    pp = jnp.where(first, 0.0, pgc_ref[...] * pxin_ref[...])
    rowi = lax.broadcasted_iota(jnp.int32, p.shape, 0)
    p1 = jnp.where(rowi == 0, pp[7:8, :], pltpu.roll(p, 1, axis=0))
    p2 = jnp.where(rowi == 0, pp[6:7, :], jnp.where(rowi == 1, pp[7:8, :], pltpu.roll(p, 2, axis=0)))
    yc = gb_ref[...] * (cw_ref[0:1, :] * p2 + cw_ref[1:2, :] * p1 + cw_ref[2:3, :] * p)
    out_ref[...] = (h_ref[...] + _dot(hm_ref[...].astype(BF16), w_ref[:wm, :])
                    + _dot(yc.astype(BF16), w_ref[wm:, :]))


def _out_a(h, hm, main, conv_w, w_out, seq):
    t, d = h.shape
    wm = hm.shape[1]
    wc = conv_w.shape[1]
    tm = TM_PROJ
    cb = (4 * wm) // wc
    rb = tm // 8
    prev = lambda i: jnp.maximum(i * rb - 1, 0)
    body = functools.partial(_out_a_body, tm=tm, seq=seq)
    return pl.pallas_call(
        body, grid=(t // tm,),
        in_specs=[pl.BlockSpec((tm, d), lambda i: (i, 0)),
                  pl.BlockSpec((tm, wm), lambda i: (i, 0)),
                  pl.BlockSpec((tm, wc), lambda i: (i, cb)),
                  pl.BlockSpec((tm, wc), lambda i: (i, cb + 1)),
                  pl.BlockSpec((tm, wc), lambda i: (i, cb + 2)),
                  pl.BlockSpec((8, wc), lambda i: (prev(i), cb + 1)),
                  pl.BlockSpec((8, wc), lambda i: (prev(i), cb + 2)),
                  pl.BlockSpec(conv_w.shape, lambda i: (0, 0)),
                  pl.BlockSpec(w_out.shape, lambda i: (0, 0))],
        out_specs=pl.BlockSpec((tm, d), lambda i: (i, 0)),
        out_shape=jax.ShapeDtypeStruct((t, d), F32),
        compiler_params=_cparams(("parallel",)), name="out_a",
    )(h, hm, main, main, main, main, main, conv_w, w_out)


def _route(logits, lane):
    ne = MOE_GROUPS * MOE_EPG
    big = 1e9
    gl = jnp.where((lane >= ne) & (lane < ne + MOE_GROUPS), logits, -jnp.inf)
    gmax = jnp.max(gl, axis=-1, keepdims=True)
    gidx = jnp.min(jnp.where(gl == gmax, lane - ne, big), axis=-1, keepdims=True)
    gval = 1.0 / jnp.sum(jnp.exp(gl - gmax), axis=-1, keepdims=True)
    lo = gidx * MOE_EPG
    sel = jnp.where((lane >= lo) & (lane < lo + MOE_EPG), logits, -jnp.inf)
    l1 = jnp.max(sel, axis=-1, keepdims=True)
    i1 = jnp.min(jnp.where(sel == l1, lane, big), axis=-1, keepdims=True)
    sel2 = jnp.where(lane == i1, -jnp.inf, sel)
    l2 = jnp.max(sel2, axis=-1, keepdims=True)
    i2 = jnp.min(jnp.where(sel2 == l2, lane, big), axis=-1, keepdims=True)
    r = jnp.exp(l2 - l1)
    w1 = gval / (1.0 + r)
    return i1, i2, w1, w1 * r


ROW_TILE = 8


def _rows_to_tiles(ref, x):
    n = x.shape[0]
    for c in range(ROW_TILE):
        ref[pl.ds(c, n, stride=ROW_TILE), :] = x[:, c * 128:(c + 1) * 128]


def _tiles_to_rows(ref, n):
    return jnp.concatenate([ref[pl.ds(c, n, stride=ROW_TILE), :] for c in range(ROW_TILE)], axis=1)


def _lane_put(lane, cols):
    out = jnp.where(lane == 0.0, cols[0], 0.0)
    for k in range(1, len(cols)):
        out = out + jnp.where(lane == float(k), cols[k], 0.0)
    return out


def _lane_get(lane, x, idx_col):
    return jnp.sum(jnp.where(lane == idx_col, x, 0.0), axis=-1, keepdims=True)


def _route_body(h_ref, nw_ref, wr_ref, br_ref, tri_ref, info_ref, cnt_ref):
    @pl.when(pl.program_id(0) == 0)
    def _init():
        cnt_ref[...] = jnp.zeros_like(cnt_ref)

    xn = _rms(h_ref[...], nw_ref[...])
    hi = xn.astype(BF16)
    lo = (xn - hi.astype(F32)).astype(BF16)
    logits = _dot(jnp.concatenate([hi, lo, hi], axis=1), wr_ref[...]) + br_ref[...]
    lane = lax.broadcasted_iota(jnp.int32, logits.shape, 1).astype(F32)
    i1, i2, w1, w2 = _route(logits, lane)
    ind = jnp.where((lane == i1) | (lane == i2), 1.0, 0.0)
    before = _dot(tri_ref[...], ind.astype(BF16)) + cnt_ref[0:1, :]
    info_ref[...] = _lane_put(lane, [w1, w2, i1, i2, _lane_get(lane, before, i1), _lane_get(lane, before, i2)])
    cnt_ref[0:1, :] = cnt_ref[0:1, :] + jnp.sum(ind, axis=0, keepdims=True)


def _dispatch_body(cnt_s, off_s, tot_s, h_ref, nw_ref, info_ref, offrow_ref, sel_ref, xs_ref, pos_ref,
                   xn_buf, pos_v, pos_s, zbuf, sem, *, tm, tile_rows, ne, n_tiles):
    i = pl.program_id(0)
    _rows_to_tiles(xn_buf, _rms(h_ref[...], nw_ref[...]))
    info = info_ref[...]
    lane = lax.broadcasted_iota(jnp.int32, info.shape, 1).astype(F32)
    offrow = offrow_ref[...]
    p1 = _lane_get(lane, offrow, info[:, 2:3]) + info[:, 4:5]
    p2 = _lane_get(lane, offrow, info[:, 3:4]) + info[:, 5:6]
    h1 = jnp.floor(p1 * (1.0 / 256.0))
    h2 = jnp.floor(p2 * (1.0 / 256.0))
    pieces = _lane_put(lane, [h1, p1 - 256.0 * h1, h2, p2 - 256.0 * h2]).astype(BF16)
    rows = _dot_nt(sel_ref[...], pieces)
    sub = lax.broadcasted_iota(jnp.int32, rows.shape, 0)
    r1 = rows[0:1, :] * 256.0 + rows[1:2, :]
    r2 = rows[2:3, :] * 256.0 + rows[3:4, :]
    posall = jnp.where(sub == 0, r1, jnp.where(sub == 1, r2, 0.0)).astype(jnp.int32)
    pos_ref[...] = posall
    pos_v[...] = posall
    cp = pltpu.make_async_copy(pos_v, pos_s, sem.at[2])
    cp.start()
    cp.wait()

    def issue(t, carry):
        src = xn_buf.at[pl.ds(pl.multiple_of(t * ROW_TILE, ROW_TILE), ROW_TILE)]
        for k in range(MOE_TOP_K):
            at = pl.multiple_of(pos_s[k, t] * ROW_TILE, ROW_TILE)
            pltpu.make_async_copy(src, xs_ref.at[pl.ds(at, ROW_TILE)], sem.at[k]).start(priority=k)
        return carry

    lax.fori_loop(0, tm, issue, 0, unroll=8)
    for k in range(MOE_TOP_K):
        pltpu.make_async_copy(xn_buf, xs_ref.at[pl.ds(0, tm * ROW_TILE)], sem.at[k]).wait()

    @pl.when(i == pl.num_programs(0) - 1)
    def _zero_unused_rows():
        zbuf[...] = jnp.zeros_like(zbuf)

        def fill(row, nrows):
            at = pl.multiple_of(row * ROW_TILE, ROW_TILE)
            c = pltpu.make_async_copy(zbuf.at[pl.ds(0, nrows * ROW_TILE)], xs_ref.at[pl.ds(at, nrows * ROW_TILE)],
                                      sem.at[2])
            c.start()
            c.wait()

        for e in range(ne):
            n_pad = (tile_rows - cnt_s[e] % tile_rows) % tile_rows
            start = off_s[e] + cnt_s[e]
            size = tile_rows // 2
            while size >= 1:
                @pl.when((n_pad & size) != 0)
                def _fill(size=size, n_pad=n_pad, start=start):
                    fill(start + (n_pad & ~(2 * size - 1)), size)
                size //= 2

        def zero_tile(j, carry):
            for k in range(2):
                fill(j * tile_rows + k * (tile_rows // 2), tile_rows // 2)
            return carry

        lax.fori_loop(tot_s[0], n_tiles, zero_tile, 0)


def _expert_body(te_s, blk_s, tot_s, xs_ref, wg_ref, wu_ref, wd_ref, ys_ref):
    valid = pl.program_id(0) < tot_s[0]
    rows = xs_ref.shape[0] // ROW_TILE

    @pl.when(valid)
    def _run():
        x = _tiles_to_rows(xs_ref, rows).astype(BF16)
        hid = jax.nn.silu(_dot(x, wg_ref[0, 0].astype(BF16))) * _dot(x, wu_ref[0, 0].astype(BF16))
        _rows_to_tiles(ys_ref, _dot(hid.astype(BF16), wd_ref[0, 0].astype(BF16)))

    @pl.when(jnp.logical_not(valid))
    def _unused_tile():
        ys_ref[...] = jnp.zeros_like(ys_ref)


def _combine_body(h_ref, info_ref, pos_s, ys_ref, fw_ref, out_ref, y1, y2, sem, *, tm, final):
    bufs = (y1, y2)

    def issue(t, carry):
        dst = pl.ds(pl.multiple_of(t * ROW_TILE, ROW_TILE), ROW_TILE)
        for k in range(MOE_TOP_K):
            at = pl.multiple_of(pos_s[k, t] * ROW_TILE, ROW_TILE)
            pltpu.make_async_copy(ys_ref.at[pl.ds(at, ROW_TILE)], bufs[k].at[dst], sem.at[k]).start(priority=k)
        return carry

    lax.fori_loop(0, tm, issue, 0, unroll=8)
    for k in range(MOE_TOP_K):
        pltpu.make_async_copy(ys_ref.at[pl.ds(0, tm * ROW_TILE)], bufs[k], sem.at[k]).wait()
    o = h_ref[...] + info_ref[:, 0:1] * _tiles_to_rows(y1, tm) + info_ref[:, 1:2] * _tiles_to_rows(y2, tm)
    if final:
        o = _rms(o, fw_ref[...])
    out_ref[...] = o


def _router_weights(w_group, b_group, w_router, b_router):
    d, ne = w_router.shape
    ng = w_group.shape[1]
    w = jnp.zeros((d, ROUTE_LANES), F32).at[:, :ne].set(w_router).at[:, ne:ne + ng].set(w_group)
    hi = w.astype(BF16)
    lo = (w - hi.astype(F32)).astype(BF16)
    bias = jnp.zeros((1, ROUTE_LANES), F32).at[0, :ne].set(b_router).at[0, ne:ne + ng].set(b_group)
    return jnp.concatenate([hi, hi, lo], axis=0), bias


def _moe(h, nw, w_group, b_group, w_router, b_router, w_gate, w_up, w_down, layer, final_w):
    t, d = h.shape
    assert d == ROW_TILE * 128, "row-as-tile layout needs d_model == 1024"
    _, ne, _, ff = w_gate.shape
    tm = TM_MOE
    te_rows = TM_EXPERT
    nw2 = nw.reshape(1, d)
    wr3, bias = _router_weights(w_group, b_group, w_router, b_router)
    tri = jnp.asarray(np.tril(np.ones((tm, tm), np.float32), -1), BF16)
    info, cnt = pl.pallas_call(
        _route_body, grid=(t // tm,),
        in_specs=[pl.BlockSpec((tm, d), lambda i: (i, 0)),
                  pl.BlockSpec((1, d), lambda i: (0, 0)),
                  pl.BlockSpec((3 * d, ROUTE_LANES), lambda i: (0, 0)),
                  pl.BlockSpec((1, ROUTE_LANES), lambda i: (0, 0)),
                  pl.BlockSpec((tm, tm), lambda i: (0, 0))],
        out_specs=[pl.BlockSpec((tm, ROUTE_LANES), lambda i: (i, 0)), pl.BlockSpec((8, ROUTE_LANES), lambda i: (0, 0))],
        out_shape=[jax.ShapeDtypeStruct((t, ROUTE_LANES), F32), jax.ShapeDtypeStruct((8, ROUTE_LANES), F32)],
        compiler_params=_cparams(("arbitrary",)), name="moe_route",
    )(h, nw2, wr3, bias, tri)

    cnt_i = cnt[0, :ne].astype(jnp.int32)
    ntile = (cnt_i + te_rows - 1) // te_rows
    tile_end = jnp.cumsum(ntile)
    off_i = (tile_end - ntile) * te_rows
    n_tiles = (MOE_TOP_K * t) // te_rows + ne
    rows_total = n_tiles * te_rows
    ti = jnp.arange(n_tiles, dtype=jnp.int32)
    tot = tile_end[-1:]
    ti_c = jnp.minimum(ti, tot[0] - 1)
    tile_e = jnp.sum((ti_c[:, None] >= tile_end[None, :]).astype(jnp.int32), axis=1)
    off_row = jnp.zeros((1, ROUTE_LANES), F32).at[0, :ne].set(off_i.astype(F32))
    sel = jnp.asarray(np.eye(8, ROUTE_LANES, dtype=np.float32), BF16)

    xs, pos = pl.pallas_call(
        functools.partial(_dispatch_body, tm=tm, tile_rows=te_rows, ne=ne, n_tiles=n_tiles),
        grid_spec=pltpu.PrefetchScalarGridSpec(
            num_scalar_prefetch=3, grid=(t // tm,),
            in_specs=[pl.BlockSpec((tm, d), lambda i, *_: (i, 0)),
                      pl.BlockSpec((1, d), lambda i, *_: (0, 0)),
                      pl.BlockSpec((tm, ROUTE_LANES), lambda i, *_: (i, 0)),
                      pl.BlockSpec((1, ROUTE_LANES), lambda i, *_: (0, 0)),
                      pl.BlockSpec((8, ROUTE_LANES), lambda i, *_: (0, 0))],
            out_specs=[pl.BlockSpec(memory_space=pl.ANY), pl.BlockSpec((8, tm), lambda i, *_: (0, i))],
            scratch_shapes=[pltpu.VMEM((tm * ROW_TILE, 128), F32), pltpu.VMEM((8, tm), jnp.int32),
                            pltpu.SMEM((8, tm), jnp.int32), pltpu.VMEM((te_rows // 2 * ROW_TILE, 128), F32),
                            pltpu.SemaphoreType.DMA((3,))]),
        out_shape=[jax.ShapeDtypeStruct((rows_total * ROW_TILE, 128), F32), jax.ShapeDtypeStruct((8, t), jnp.int32)],
        compiler_params=_cparams(("arbitrary",)), name="moe_dispatch",
    )(cnt_i, off_i, tot, h, nw2, info, off_row, sel)

    ys = pl.pallas_call(
        _expert_body,
        grid_spec=pltpu.PrefetchScalarGridSpec(
            num_scalar_prefetch=3, grid=(n_tiles,),
            in_specs=[pl.BlockSpec((te_rows * ROW_TILE, 128), lambda i, e, b, v: (b[i], 0)),
                      pl.BlockSpec((1, 1, d, ff), lambda i, e, b, v: (layer, e[i], 0, 0)),
                      pl.BlockSpec((1, 1, d, ff), lambda i, e, b, v: (layer, e[i], 0, 0)),
                      pl.BlockSpec((1, 1, ff, d), lambda i, e, b, v: (layer, e[i], 0, 0))],
            out_specs=pl.BlockSpec((te_rows * ROW_TILE, 128), lambda i, e, b, v: (i, 0))),
        out_shape=jax.ShapeDtypeStruct((rows_total * ROW_TILE, 128), F32),
        compiler_params=_cparams(("arbitrary",)), name="moe_expert",
    )(tile_e, ti_c, tot, xs, w_gate, w_up, w_down)

    final = final_w is not None
    fw = (final_w if final else nw).reshape(1, d)
    return pl.pallas_call(
        functools.partial(_combine_body, tm=tm, final=final), grid=(t // tm,),
        in_specs=[pl.BlockSpec((tm, d), lambda i: (i, 0)),
                  pl.BlockSpec((tm, ROUTE_LANES), lambda i: (i, 0)),
                  pl.BlockSpec((8, tm), lambda i: (0, i), memory_space=pltpu.SMEM),
                  pl.BlockSpec(memory_space=pl.ANY),
                  pl.BlockSpec((1, d), lambda i: (0, 0))],
        out_specs=pl.BlockSpec((tm, d), lambda i: (i, 0)),
        out_shape=jax.ShapeDtypeStruct((t, d), F32),
        scratch_shapes=[pltpu.VMEM((tm * ROW_TILE, 128), F32), pltpu.VMEM((tm * ROW_TILE, 128), F32),
                        pltpu.SemaphoreType.DMA((2,))],
        compiler_params=_cparams(("arbitrary",)), name="moe_combine",
    )(h, info, pos, ys, fw)


def _s5_weights(lam_re, lam_im, b_re, b_im, c_re, c_im, log_dt, nsteps):
    hp = lax.Precision.HIGHEST
    L = S5_CHUNK
    g, p = lam_re.shape
    ch = b_re.shape[-1]
    lam = lax.complex(lam_re.astype(F32), lam_im.astype(F32))
    dt = jnp.exp(log_dt.astype(F32))[:, None]
    lam_bar = jnp.exp(lam * dt)
    b_bar = ((lam_bar - 1.0) / lam)[..., None] * lax.complex(b_re.astype(F32), b_im.astype(F32))
    cmat = lax.complex(c_re.astype(F32), c_im.astype(F32))
    pows = [jnp.ones_like(lam_bar)]
    for _ in range(L):
        pows.append(pows[-1] * lam_bar)
    pw = jnp.stack(pows, axis=1)
    kern = jnp.real(jnp.einsum('gop,gtp,gpi->gtoi', cmat, pw[:, :L], b_bar, precision=hp))
    lag = np.arange(L)[None, :] - np.arange(L)[:, None]
    toep = jnp.where((lag >= 0)[None, :, :, None, None], kern[:, np.maximum(lag, 0)], 0.0)
    toep = toep.transpose(0, 1, 4, 2, 3).reshape(g, L * ch, L * ch)
    wst = pw[:, L - 1 - np.arange(L)][:, :, :, None] * b_bar[:, None, :, :]
    wst = wst.transpose(0, 1, 3, 2).reshape(g, L * ch, p)
    wst = jnp.concatenate([jnp.real(wst), jnp.imag(wst)], axis=-1)
    mo = cmat.transpose(0, 2, 1)[:, :, None, :] * pw[:, 1:L + 1].transpose(0, 2, 1)[:, :, :, None]
    mo = mo.reshape(g, p, L * ch)
    wout = jnp.concatenate([jnp.real(mo), -jnp.imag(mo)], axis=1)
    a = pw[:, L]
    ars, ais = [], []
    for _ in range(nsteps):
        ars.append(jnp.concatenate([jnp.real(a), jnp.real(a)], axis=-1))
        ais.append(jnp.concatenate([-jnp.imag(a), jnp.imag(a)], axis=-1))
        a = a * a
    wcat = jnp.concatenate([toep, wst], axis=-1).astype(BF16)
    return wcat, wout.astype(BF16), jnp.stack(ars, axis=1), jnp.stack(ais, axis=1)


def _s5_body(u_ref, wcat_ref, wout_ref, ar_ref, ai_ref, y_ref, us_ref, ys_ref, *, nsteps):
    L = S5_CHUNK
    ch = S5_GROUP_CH
    gpc = 128 // ch
    ny = L * ch
    nc = u_ref.shape[0] // L
    for s in range(L):
        us_ref[s] = u_ref[pl.ds(s, nc, stride=L), :]
    lane = lax.broadcasted_iota(jnp.int32, (nc, 128), 1)
    ridx = lax.broadcasted_iota(jnp.int32, (nc, 128), 0)

    def shift(x, k):
        return jnp.where(ridx >= k, pltpu.roll(x, k, axis=0), 0.0)

    for gi in range(gpc):
        halves = []
        for hh in range(ny // 128):
            acc = None
            for s8 in range(gpc):
                rot = ((s8 - gi) * ch) % 128
                src = us_ref[hh * gpc + s8]
                if rot:
                    src = pltpu.roll(src, rot, axis=1)
                slot = (lane >= s8 * ch) & (lane < (s8 + 1) * ch)
                acc = jnp.where(slot, src, 0.0) if acc is None else jnp.where(slot, src, acc)
            halves.append(acc)
        ug = jnp.concatenate(halves, axis=1).astype(BF16)
        r = _dot(ug, wcat_ref[gi])
        y1 = r[:, :ny]
        z = r[:, ny:]
        w = shift(z, 1)
        for k in range(nsteps):
            if (1 << k) >= nc:
                break
            sk = shift(w, 1 << k)
            w = w + sk * ar_ref[gi, k:k + 1, :] + pltpu.roll(sk, z.shape[1] // 2, axis=1) * ai_ref[gi, k:k + 1, :]
        yg = y1 + _dot(w.astype(BF16), wout_ref[gi])
        slot = (lane >= gi * ch) & (lane < (gi + 1) * ch)
        for t in range(L):
            src = yg[:, (t // gpc) * 128:(t // gpc + 1) * 128]
            rot = ((gi - t % gpc) * ch) % 128
            if rot:
                src = pltpu.roll(src, rot, axis=1)
            ys_ref[t] = jnp.where(slot, src, 0.0) if gi == 0 else jnp.where(slot, src, ys_ref[t])
    for t in range(L):
        y_ref[pl.ds(t, nc, stride=L), :] = ys_ref[t]


def _s5(main, col0, width, nb, seq, lam_re, lam_im, b_re, b_im, c_re, c_im, log_dt):
    t = main.shape[0]
    L = S5_CHUNK
    ch = S5_GROUP_CH
    g = width // ch
    nc = seq // L
    gpc = 128 // ch
    ncol = width // 128
    assert col0 % 128 == 0 and width % 128 == 0 and (L * ch) % 128 == 0
    nsteps = max(1, (nc - 1).bit_length())
    wcat, wout, ar, ai = _s5_weights(lam_re, lam_im, b_re, b_im, c_re, c_im, log_dt, nsteps)
    body = functools.partial(_s5_body, nsteps=nsteps)
    return pl.pallas_call(
        body, grid=(ncol, nb),
        in_specs=[pl.BlockSpec((seq, 128), lambda j, b: (b, col0 // 128 + j)),
                  pl.BlockSpec((gpc,) + wcat.shape[1:], lambda j, b: (j, 0, 0)),
                  pl.BlockSpec((gpc,) + wout.shape[1:], lambda j, b: (j, 0, 0)),
                  pl.BlockSpec((gpc,) + ar.shape[1:], lambda j, b: (j, 0, 0)),
                  pl.BlockSpec((gpc,) + ai.shape[1:], lambda j, b: (j, 0, 0))],
        out_specs=pl.BlockSpec((seq, 128), lambda j, b: (b, j)),
        out_shape=jax.ShapeDtypeStruct((t, width), F32),
        scratch_shapes=[pltpu.VMEM((L, nc, 128), F32), pltpu.VMEM((L, nc, 128), F32)],
        compiler_params=_cparams(("parallel", "parallel")), name="s5",
    )(main, wcat, wout, ar, ai)


def _hgrn_gmat():
    L = HGRN_CHUNK
    blocks = 2 + int(np.log2(L))
    gm = np.zeros((blocks * L, L), np.float32)
    for j in range(L):
        gm[j, :j + 1] = 1.0
        gm[L + j, j + 1:] = 1.0
    li, m = 2, L
    while m >= 2:
        half = m // 2
        for j in range(L):
            pos = j % m
            r = j - pos + half - 1
            if pos >= half:
                gm[li * L + j, r + 1:j + 1] = 1.0
            else:
                gm[li * L + j, j + 1:r + 1] = 1.0
        li += 1
        m //= 2
    return gm


def _hgrn_body(main_ref, gm_ref, lb_ref, nw_ref, out_ref, st_ref, *, nb, nh, dh):
    L = HGRN_CHUNK
    w = nh * dh

    @pl.when(pl.program_id(0) == 0)
    def _init():
        st_ref[...] = jnp.zeros_like(st_ref)

    row = lax.broadcasted_iota(jnp.int32, (L, 2 * L), 0)
    col = lax.broadcasted_iota(jnp.int32, (L, 2 * L), 1) & (L - 1)
    rowd = lax.broadcasted_iota(jnp.int32, (L, 2 * dh), 0)
    laned = lax.broadcasted_iota(jnp.int32, (L, 2 * dh), 1)
    first = laned < dh
    eye = row == col

    def blockdiag(x):
        z = jnp.zeros_like(x)
        return jnp.concatenate([jnp.where(first, x, z), jnp.where(first, z, x)], axis=0)

    gm2 = gm_ref[...]
    lb = lb_ref[...]
    zst = jnp.zeros((dh, dh), BF16)
    for b in range(nb):
        fg = main_ref[b, :, w:2 * w]
        f = lb + (1.0 - lb) * jax.nn.sigmoid(fg)
        kk = (1.0 - lb) * jax.nn.sigmoid(-fg)
        lf = jnp.log(f)
        hi = lf.astype(BF16)
        mid = (lf - hi.astype(F32)).astype(BF16)
        p_all = jnp.exp(_dot(gm2, jnp.concatenate([hi, mid], axis=0)))
        for hp in range(nh // 2):
            i0 = b * nh + 2 * hp
            cs = slice(2 * hp * dh, (2 * hp + 2) * dh)
            q = main_ref[b, :, 2 * hp * dh:(2 * hp + 2) * dh]
            v = main_ref[b, :, 2 * w + 2 * hp * dh:2 * w + (2 * hp + 2) * dh]
            og = main_ref[b, :, 3 * w + 2 * hp * dh:3 * w + (2 * hp + 2) * dh]
            k = kk[:, cs]
            pb = p_all[0:L, cs]
            pe = p_all[L:2 * L, cs]
            st0 = st_ref[i0]
            st1 = st_ref[i0 + 1]
            stbd = jnp.concatenate([jnp.concatenate([st0.astype(BF16), zst], axis=1),
                                    jnp.concatenate([zst, st1.astype(BF16)], axis=1)], axis=0)
            o = _dot_nt((q * pb).astype(BF16), stbd)
            attn = jnp.where(eye, _dot_nt(q.astype(BF16), blockdiag(k.astype(BF16))), 0.0)
            li, m = 2, L
            while m >= 2:
                pl_ = p_all[li * L:(li + 1) * L, cs]
                up = (rowd & (m - 1)) >= (m // 2)
                ql = jnp.where(up, q * pl_, 0.0).astype(BF16)
                kl = jnp.where(up, 0.0, k * pl_).astype(BF16)
                same = (row & ~(m - 1)) == (col & ~(m - 1))
                attn = attn + jnp.where(same, _dot_nt(ql, blockdiag(kl)), 0.0)
                li += 1
                m //= 2
            vb = v.astype(BF16)
            o = o + _dot(attn.astype(BF16), blockdiag(vb))
            kh = (k * pe).astype(BF16)
            for j in range(2):
                hs = slice(j * dh, (j + 1) * dh)
                gs = slice((2 * hp + j) * dh, (2 * hp + j + 1) * dh)
                st = st0 if j == 0 else st1
                st_ref[i0 + j] = st * pb[L - 1:L, hs] + _dot_tn(vb[:, hs], kh[:, hs])
                oj = o[:, hs]
                on = oj * lax.rsqrt(jnp.mean(oj * oj, axis=-1, keepdims=True) + RMS_EPS)
                out_ref[b, :, gs] = on * nw_ref[:, gs] * jax.nn.silu(og[:, hs])


def _hgrn(main, lower_bound, out_norm, nb, seq):
    t, n = main.shape
    nh = HGRN_HEADS
    w = out_norm.shape[0]
    dh = w // nh
    L = HGRN_CHUNK
    nc = seq // L
    gm = _hgrn_gmat()
    gm = jnp.asarray(np.concatenate([gm, gm], axis=1), BF16)
    body = functools.partial(_hgrn_body, nb=nb, nh=nh, dh=dh)
    out = pl.pallas_call(
        body, grid=(nc,),
        in_specs=[pl.BlockSpec((nb, L, 4 * w), lambda c: (0, c, 0)),
                  pl.BlockSpec(gm.shape, lambda c: (0, 0)),
                  pl.BlockSpec((1, w), lambda c: (0, 0)),
                  pl.BlockSpec((1, w), lambda c: (0, 0))],
        out_specs=pl.BlockSpec((nb, L, w), lambda c: (0, c, 0)),
        out_shape=jax.ShapeDtypeStruct((nb, seq, w), F32),
        scratch_shapes=[pltpu.VMEM((nb * nh, dh, dh), F32)],
        compiler_params=_cparams(("arbitrary",)), name="hgrn",
    )(main.reshape(nb, seq, n), gm, lower_bound.reshape(1, w), out_norm.reshape(1, w))
    return out.reshape(t, w)


def _out_c_body(h_ref, ys_ref, u_ref, oh_ref, d_ref, wglu_ref, bglu_ref, w_ref, out_ref):
    ws = ys_ref.shape[1]
    z = jax.nn.gelu(ys_ref[...] + d_ref[...] * u_ref[...])
    gate = jax.nn.sigmoid(_dot(z.astype(BF16), wglu_ref[...]) + bglu_ref[...])
    out_ref[...] = (h_ref[...] + _dot((z * gate).astype(BF16), w_ref[:ws, :])
                    + _dot(oh_ref[...].astype(BF16), w_ref[ws:, :]))


def _out_c(h, ys, main, oh, d_skip, w_glu, b_glu, w_out):
    t, d = h.shape
    ws = ys.shape[1]
    wh = oh.shape[1]
    tm = TM_PROJ
    ub = (main.shape[1] - ws) // ws
    return pl.pallas_call(
        _out_c_body, grid=(t // tm,),
        in_specs=[pl.BlockSpec((tm, d), lambda i: (i, 0)),
                  pl.BlockSpec((tm, ws), lambda i: (i, 0)),
                  pl.BlockSpec((tm, ws), lambda i: (i, ub)),
                  pl.BlockSpec((tm, wh), lambda i: (i, 0)),
                  pl.BlockSpec((1, ws), lambda i: (0, 0)),
                  pl.BlockSpec(w_glu.shape, lambda i: (0, 0)),
                  pl.BlockSpec((1, ws), lambda i: (0, 0)),
                  pl.BlockSpec(w_out.shape, lambda i: (0, 0))],
        out_specs=pl.BlockSpec((tm, d), lambda i: (i, 0)),
        out_shape=jax.ShapeDtypeStruct((t, d), F32),
        compiler_params=_cparams(("parallel",)), name="out_c",
    )(h, ys, main, oh, d_skip.reshape(1, ws), w_glu, b_glu.reshape(1, ws), w_out)


def kernel(x, norm_mix, norm_ffn, norm_final, ab_w_in, ab_gate_bias, ab_head_norm, ab_conv_w, ab_w_out, cd_w_in, s5_lambda_re, s5_lambda_im, s5_b_re, s5_b_im, s5_c_re, s5_c_im, s5_d, s5_log_dt, s5_w_glu, s5_b_glu, hgrn_lb, hgrn_out_norm, cd_w_out, moe_w_group, moe_b_group, moe_w_router, moe_b_router, moe_w_gate, moe_w_up, moe_w_down):
    nb, seq, d = x.shape
    depth = norm_mix.shape[0]
    h = x.reshape(nb * seq, d)
    for layer in range(depth):
        j = layer // 2
        if layer % 2 == 0:
            wm = ab_head_norm.shape[1]
            ng = ab_gate_bias.shape[1]
            w_in = ab_w_in[j]
            w_main = jnp.concatenate([w_in[:, :4 * wm], w_in[:, 4 * wm + ng:]], axis=1).astype(BF16)
            w_gates = w_in[:, 4 * wm:4 * wm + ng].astype(BF16)
            main, g, gt = _proj(h, norm_mix[layer], w_main, w_gates)
            hm = _mlstm(main, g, gt, ab_gate_bias[j], ab_head_norm[j], nb, seq)
            h = _out_a(h, hm, main, ab_conv_w[j], ab_w_out[j].astype(BF16), seq)
        else:
            ws = s5_d.shape[1]
            w_in = cd_w_in[j]
            w_main = jnp.concatenate([w_in[:, ws:], w_in[:, :ws]], axis=1).astype(BF16)
            main = _proj(h, norm_mix[layer], w_main)
            sm = jax.nn.softmax(hgrn_lb.astype(F32), axis=0)
            lower_bound = jnp.cumsum(sm, axis=0)[layer] - sm[0]
            ys = _s5(main, main.shape[1] - ws, ws, nb, seq, s5_lambda_re[j], s5_lambda_im[j], s5_b_re[j], s5_b_im[j],
                     s5_c_re[j], s5_c_im[j], s5_log_dt[j])
            oh = _hgrn(main, lower_bound, hgrn_out_norm[j], nb, seq)
            h = _out_c(h, ys, main, oh, s5_d[j], s5_w_glu[j].astype(BF16), s5_b_glu[j], cd_w_out[j].astype(BF16))
        h = _moe(h, norm_ffn[layer], moe_w_group[layer], moe_b_group[layer], moe_w_router[layer], moe_b_router[layer],
                 moe_w_gate, moe_w_up, moe_w_down, layer,
                 norm_final if layer == depth - 1 else None)
    return h.reshape(nb, seq, d)
```

```python
import functools

import numpy as np
import jax
import jax.numpy as jnp
from jax import lax
from jax.experimental import pallas as pl
from jax.experimental.pallas import tpu as pltpu

F32 = jnp.float32
BF16 = jnp.bfloat16
RMS_EPS = 1e-6
MLSTM_CHUNK = 512
HGRN_CHUNK = 128
S5_CHUNK = 16
S5_GROUP_CH = 16
S5_STATE = 64
MLSTM_HEADS = 4
HGRN_HEADS = 4
MOE_GROUPS = 4
MOE_EPG = 8
ROUTE_LANES = 128
TM_PROJ = 512
TM_MOE = 512
TM_EXPERT = 512
MOE_TOP_K = 2
VMEM_LIMIT = 56 * 1024 * 1024

_NT = (((1,), (1,)), ((), ()))
_TN = (((0,), (0,)), ((), ()))


def _cparams(sem):
    return pltpu.CompilerParams(dimension_semantics=sem, vmem_limit_bytes=VMEM_LIMIT)


def _rms(x, w):
    return x * lax.rsqrt(jnp.mean(x * x, axis=-1, keepdims=True) + RMS_EPS) * w


def _split3(x):
    hi = x.astype(BF16)
    r = x - hi.astype(F32)
    mid = r.astype(BF16)
    lo = (r - mid.astype(F32)).astype(BF16)
    return hi, mid, lo


def _dot(a, b):
    return jnp.dot(a, b, preferred_element_type=F32)


def _dot_nt(a, b):
    return lax.dot_general(a, b, _NT, preferred_element_type=F32)


def _dot_tn(a, b):
    return lax.dot_general(a, b, _TN, preferred_element_type=F32)


def _proj_gates_body(x_ref, nw_ref, w_ref, wg_ref, wgt_ref, main_ref, g_ref, gt_ref):
    xn = _rms(x_ref[...], nw_ref[...]).astype(BF16)
    main_ref[...] = _dot(xn, w_ref[...])
    g_ref[...] = _dot(xn, wg_ref[...])[:, : g_ref.shape[1]]
    gt_ref[...] = _dot_nt(wgt_ref[...], xn)


def _proj_body(x_ref, nw_ref, w_ref, main_ref):
    xn = _rms(x_ref[...], nw_ref[...]).astype(BF16)
    main_ref[...] = _dot(xn, w_ref[...])


def _proj(h, nw, w_main, w_gates=None):
    t, d = h.shape
    n = w_main.shape[1]
    tm = TM_PROJ
    x_spec = pl.BlockSpec((tm, d), lambda i: (i, 0))
    nw_spec = pl.BlockSpec((1, d), lambda i: (0, 0))
    w_spec = pl.BlockSpec((d, n), lambda i: (0, 0))
    main_spec = pl.BlockSpec((tm, n), lambda i: (i, 0))
    main_shape = jax.ShapeDtypeStruct((t, n), F32)
    if w_gates is None:
        return pl.pallas_call(
            _proj_body, grid=(t // tm,), in_specs=[x_spec, nw_spec, w_spec], out_specs=main_spec,
            out_shape=main_shape, compiler_params=_cparams(("parallel",)), name="proj",
        )(h, nw.reshape(1, d), w_main)
    ng = w_gates.shape[1]
    wg_pad = jnp.zeros((d, 128), BF16).at[:, :ng].set(w_gates)
    return pl.pallas_call(
        _proj_gates_body, grid=(t // tm,),
        in_specs=[x_spec, nw_spec, w_spec, pl.BlockSpec((d, 128), lambda i: (0, 0)),
                  pl.BlockSpec((ng, d), lambda i: (0, 0))],
        out_specs=[main_spec, pl.BlockSpec((tm, ng), lambda i: (i, 0)), pl.BlockSpec((ng, tm), lambda i: (0, i))],
        out_shape=[main_shape, jax.ShapeDtypeStruct((t, ng), F32), jax.ShapeDtypeStruct((ng, t), F32)],
        compiler_params=_cparams(("parallel",)), name="proj_gates",
    )(h, nw.reshape(1, d), w_main, wg_pad, w_gates.T)


def _mlstm_body(main_ref, g_ref, gt_ref, br_ref, bc_ref, hn_ref, out_ref, c_ref, n_ref, m_ref, *, nb, nh, dh):
    L = MLSTM_CHUNK
    w = nh * dh

    @pl.when(pl.program_id(0) == 0)
    def _init():
        c_ref[...] = jnp.zeros_like(c_ref)
        n_ref[...] = jnp.zeros_like(n_ref)
        m_ref[...] = jnp.full_like(m_ref, -1e30)

    row = lax.broadcasted_iota(jnp.int32, (L, L), 0)
    col = lax.broadcasted_iota(jnp.int32, (L, L), 1)
    causal = col <= row
    tril = causal.astype(BF16)
    triu = (row <= col).astype(BF16)
    scale = dh ** -0.5
    for b in range(nb):
        g = g_ref[b] + br_ref[...]
        gt = gt_ref[b, 0] + bc_ref[...]
        i_c = g[:, :nh]
        i_r = gt[:nh, :]
        lfc = _split3(jax.nn.log_sigmoid(g[:, nh:]))
        lfr = _split3(jax.nn.log_sigmoid(gt[nh:, :]))
        bc_all = _dot(tril, lfc[0]) + _dot(tril, lfc[1]) + _dot(tril, lfc[2])
        br_all = _dot(lfr[0], triu) + _dot(lfr[1], triu) + _dot(lfr[2], triu)
        for h in range(nh):
            idx = b * nh + h
            q = main_ref[b, :, h * dh:(h + 1) * dh]
            k = main_ref[b, :, w + h * dh:w + (h + 1) * dh] * scale
            v = main_ref[b, :, 2 * w + h * dh:2 * w + (h + 1) * dh]
            o = main_ref[b, :, 3 * w + h * dh:3 * w + (h + 1) * dh]
            bc = bc_all[:, h:h + 1]
            br = br_all[h:h + 1, :]
            ir = i_r[h:h + 1, :]
            ic = i_c[:, h:h + 1]
            m_prev = m_ref[idx]
            c_prev = c_ref[idx]
            n_prev = n_ref[idx]
            logw = jnp.where(causal, bc - br + ir, -jnp.inf)
            inter = bc + m_prev
            m_row = jnp.maximum(jnp.max(logw, axis=-1, keepdims=True), inter)
            qb = q.astype(BF16)
            kb = k.astype(BF16)
            vb = v.astype(BF16)
            s = _dot_nt(qb, kb) * jnp.exp(logw - m_row)
            isc = jnp.exp(inter - m_row)
            num = _dot(s.astype(BF16), vb) + isc * _dot_nt(qb, c_prev.astype(BF16))
            den = jnp.sum(s, axis=-1, keepdims=True) + isc * jnp.sum(q * n_prev, axis=-1, keepdims=True)
            hout = num / jnp.maximum(jnp.abs(den), jnp.exp(-m_row))
            b_end = bc[L - 1:L, :]
            logg = b_end - bc + ic
            m_new = jnp.maximum(b_end + m_prev, jnp.max(logg, axis=0, keepdims=True))
            wk = jnp.exp(logg - m_new)
            decay = jnp.exp(b_end + m_prev - m_new)
            c_ref[idx] = decay * c_prev + _dot_tn((v * wk).astype(BF16), kb)
            n_ref[idx] = decay * n_prev + jnp.sum(wk * k, axis=0, keepdims=True)
            m_ref[idx] = m_new
            hn = hout * lax.rsqrt(jnp.mean(hout * hout, axis=-1, keepdims=True) + RMS_EPS)
            out_ref[b, :, h * dh:(h + 1) * dh] = hn * hn_ref[:, h * dh:(h + 1) * dh] * jax.nn.sigmoid(o)


def _mlstm(main, g, gt, gate_bias, head_norm, nb, seq):
    t, n = main.shape
    nh = MLSTM_HEADS
    w = head_norm.shape[0]
    dh = w // nh
    L = MLSTM_CHUNK
    nc = seq // L
    main3 = main.reshape(nb, seq, n)
    g3 = g.reshape(nb, seq, 2 * nh)
    gt4 = gt.reshape(2 * nh, nb, nc, L).transpose(1, 2, 0, 3)
    body = functools.partial(_mlstm_body, nb=nb, nh=nh, dh=dh)
    out = pl.pallas_call(
        body, grid=(nc,),
        in_specs=[pl.BlockSpec((nb, L, 4 * w), lambda c: (0, c, 0)),
                  pl.BlockSpec((nb, L, 2 * nh), lambda c: (0, c, 0)),
                  pl.BlockSpec((nb, 1, 2 * nh, L), lambda c: (0, c, 0, 0)),
                  pl.BlockSpec((1, 2 * nh), lambda c: (0, 0)),
                  pl.BlockSpec((2 * nh, 1), lambda c: (0, 0)),
                  pl.BlockSpec((1, w), lambda c: (0, 0))],
        out_specs=pl.BlockSpec((nb, L, w), lambda c: (0, c, 0)),
        out_shape=jax.ShapeDtypeStruct((nb, seq, w), F32),
        scratch_shapes=[pltpu.VMEM((nb * nh, dh, dh), F32), pltpu.VMEM((nb * nh, 1, dh), F32),
                        pltpu.VMEM((nb * nh, 1, 1), F32)],
        compiler_params=_cparams(("arbitrary",)), name="mlstm",
    )(main3, g3, gt4, gate_bias.reshape(1, 2 * nh), gate_bias.reshape(2 * nh, 1), head_norm.reshape(1, w))
    return out.reshape(t, w)


def _out_a_body(h_ref, hm_ref, gb_ref, gc_ref, xin_ref, pgc_ref, pxin_ref, cw_ref, w_ref, out_ref, *, tm, seq):
    i = pl.program_id(0)
    wm = hm_ref.shape[1]
    p = gc_ref[...] * xin_ref[...]
    first = (i * tm) % seq == 0
    pp = jnp.where(first, 0.0, pgc_ref[...] * pxin_ref[...])
    rowi = lax.broadcasted_iota(jnp.int32, p.shape, 0)
    p1 = jnp.where(rowi == 0, pp[7:8, :], pltpu.roll(p, 1, axis=0))
    p2 = jnp.where(rowi == 0, pp[6:7, :], jnp.where(rowi == 1, pp[7:8, :], pltpu.roll(p, 2, axis=0)))
    yc = gb_ref[...] * (cw_ref[0:1, :] * p2 + cw_ref[1:2, :] * p1 + cw_ref[2:3, :] * p)
    out_ref[...] = (h_ref[...] + _dot(hm_ref[...].astype(BF16), w_ref[:wm, :])
                    + _dot(yc.astype(BF16), w_ref[wm:, :]))


def _out_a(h, hm, main, conv_w, w_out, seq):
    t, d = h.shape
    wm = hm.shape[1]
    wc = conv_w.shape[1]
    tm = TM_PROJ
    cb = (4 * wm) // wc
    rb = tm // 8
    prev = lambda i: jnp.maximum(i * rb - 1, 0)
    body = functools.partial(_out_a_body, tm=tm, seq=seq)
    return pl.pallas_call(
        body, grid=(t // tm,),
        in_specs=[pl.BlockSpec((tm, d), lambda i: (i, 0)),
                  pl.BlockSpec((tm, wm), lambda i: (i, 0)),
                  pl.BlockSpec((tm, wc), lambda i: (i, cb)),
                  pl.BlockSpec((tm, wc), lambda i: (i, cb + 1)),
                  pl.BlockSpec((tm, wc), lambda i: (i, cb + 2)),
                  pl.BlockSpec((8, wc), lambda i: (prev(i), cb + 1)),
                  pl.BlockSpec((8, wc), lambda i: (prev(i), cb + 2)),
                  pl.BlockSpec(conv_w.shape, lambda i: (0, 0)),
                  pl.BlockSpec(w_out.shape, lambda i: (0, 0))],
        out_specs=pl.BlockSpec((tm, d), lambda i: (i, 0)),
        out_shape=jax.ShapeDtypeStruct((t, d), F32),
        compiler_params=_cparams(("parallel",)), name="out_a",
    )(h, hm, main, main, main, main, main, conv_w, w_out)


def _route(logits, lane):
    ne = MOE_GROUPS * MOE_EPG
    big = 1e9
    gl = jnp.where((lane >= ne) & (lane < ne + MOE_GROUPS), logits, -jnp.inf)
    gmax = jnp.max(gl, axis=-1, keepdims=True)
    gidx = jnp.min(jnp.where(gl == gmax, lane - ne, big), axis=-1, keepdims=True)
    gval = 1.0 / jnp.sum(jnp.exp(gl - gmax), axis=-1, keepdims=True)
    lo = gidx * MOE_EPG
    sel = jnp.where((lane >= lo) & (lane < lo + MOE_EPG), logits, -jnp.inf)
    l1 = jnp.max(sel, axis=-1, keepdims=True)
    i1 = jnp.min(jnp.where(sel == l1, lane, big), axis=-1, keepdims=True)
    sel2 = jnp.where(lane == i1, -jnp.inf, sel)
    l2 = jnp.max(sel2, axis=-1, keepdims=True)
    i2 = jnp.min(jnp.where(sel2 == l2, lane, big), axis=-1, keepdims=True)
    r = jnp.exp(l2 - l1)
    w1 = gval / (1.0 + r)
    return i1, i2, w1, w1 * r


ROW_TILE = 8


def _rows_to_tiles(ref, x):
    n = x.shape[0]
    for c in range(ROW_TILE):
        ref[pl.ds(c, n, stride=ROW_TILE), :] = x[:, c * 128:(c + 1) * 128]


def _tiles_to_rows(ref, n):
    return jnp.concatenate([ref[pl.ds(c, n, stride=ROW_TILE), :] for c in range(ROW_TILE)], axis=1)


def _lane_put(lane, cols):
    out = jnp.where(lane == 0.0, cols[0], 0.0)
    for k in range(1, len(cols)):
        out = out + jnp.where(lane == float(k), cols[k], 0.0)
    return out


def _lane_get(lane, x, idx_col):
    return jnp.sum(jnp.where(lane == idx_col, x, 0.0), axis=-1, keepdims=True)


def _route_body(h_ref, nw_ref, wr_ref, br_ref, tri_ref, info_ref, cnt_ref):
    @pl.when(pl.program_id(0) == 0)
    def _init():
        cnt_ref[...] = jnp.zeros_like(cnt_ref)

    xn = _rms(h_ref[...], nw_ref[...])
    hi = xn.astype(BF16)
    lo = (xn - hi.astype(F32)).astype(BF16)
    logits = _dot(jnp.concatenate([hi, lo, hi], axis=1), wr_ref[...]) + br_ref[...]
    lane = lax.broadcasted_iota(jnp.int32, logits.shape, 1).astype(F32)
    i1, i2, w1, w2 = _route(logits, lane)
    ind = jnp.where((lane == i1) | (lane == i2), 1.0, 0.0)
    before = _dot(tri_ref[...], ind.astype(BF16)) + cnt_ref[0:1, :]
    info_ref[...] = _lane_put(lane, [w1, w2, i1, i2, _lane_get(lane, before, i1), _lane_get(lane, before, i2)])
    cnt_ref[0:1, :] = cnt_ref[0:1, :] + jnp.sum(ind, axis=0, keepdims=True)


def _dispatch_body(cnt_s, off_s, tot_s, h_ref, nw_ref, info_ref, offrow_ref, sel_ref, xs_ref, pos_ref,
                   xn_buf, pos_v, pos_s, zbuf, sem, *, tm, tile_rows, ne, n_tiles):
    i = pl.program_id(0)
    _rows_to_tiles(xn_buf, _rms(h_ref[...], nw_ref[...]))
    info = info_ref[...]
    lane = lax.broadcasted_iota(jnp.int32, info.shape, 1).astype(F32)
    offrow = offrow_ref[...]
    p1 = _lane_get(lane, offrow, info[:, 2:3]) + info[:, 4:5]
    p2 = _lane_get(lane, offrow, info[:, 3:4]) + info[:, 5:6]
    h1 = jnp.floor(p1 * (1.0 / 256.0))
    h2 = jnp.floor(p2 * (1.0 / 256.0))
    pieces = _lane_put(lane, [h1, p1 - 256.0 * h1, h2, p2 - 256.0 * h2]).astype(BF16)
    rows = _dot_nt(sel_ref[...], pieces)
    sub = lax.broadcasted_iota(jnp.int32, rows.shape, 0)
    r1 = rows[0:1, :] * 256.0 + rows[1:2, :]
    r2 = rows[2:3, :] * 256.0 + rows[3:4, :]
    posall = jnp.where(sub == 0, r1, jnp.where(sub == 1, r2, 0.0)).astype(jnp.int32)
    pos_ref[...] = posall
    pos_v[...] = posall
    cp = pltpu.make_async_copy(pos_v, pos_s, sem.at[2])
    cp.start()
    cp.wait()

    def issue(t, carry):
        src = xn_buf.at[pl.ds(pl.multiple_of(t * ROW_TILE, ROW_TILE), ROW_TILE)]
        for k in range(MOE_TOP_K):
            at = pl.multiple_of(pos_s[k, t] * ROW_TILE, ROW_TILE)
            pltpu.make_async_copy(src, xs_ref.at[pl.ds(at, ROW_TILE)], sem.at[k]).start(priority=k)
        return carry

    lax.fori_loop(0, tm, issue, 0, unroll=8)
    for k in range(MOE_TOP_K):
        pltpu.make_async_copy(xn_buf, xs_ref.at[pl.ds(0, tm * ROW_TILE)], sem.at[k]).wait()

    @pl.when(i == pl.num_programs(0) - 1)
    def _zero_unused_rows():
        zbuf[...] = jnp.zeros_like(zbuf)

        def fill(row, nrows):
            at = pl.multiple_of(row * ROW_TILE, ROW_TILE)
            c = pltpu.make_async_copy(zbuf.at[pl.ds(0, nrows * ROW_TILE)], xs_ref.at[pl.ds(at, nrows * ROW_TILE)],
                                      sem.at[2])
            c.start()
            c.wait()

        for e in range(ne):
            n_pad = (tile_rows - cnt_s[e] % tile_rows) % tile_rows
            start = off_s[e] + cnt_s[e]
            size = tile_rows // 2
            while size >= 1:
                @pl.when((n_pad & size) != 0)
                def _fill(size=size, n_pad=n_pad, start=start):
                    fill(start + (n_pad & ~(2 * size - 1)), size)
                size //= 2

        def zero_tile(j, carry):
            for k in range(2):
                fill(j * tile_rows + k * (tile_rows // 2), tile_rows // 2)
            return carry

        lax.fori_loop(tot_s[0], n_tiles, zero_tile, 0)


def _expert_body(te_s, blk_s, tot_s, xs_ref, wg_ref, wu_ref, wd_ref, ys_ref):
    valid = pl.program_id(0) < tot_s[0]
    rows = xs_ref.shape[0] // ROW_TILE

    @pl.when(valid)
    def _run():
        x = _tiles_to_rows(xs_ref, rows).astype(BF16)
        hid = jax.nn.silu(_dot(x, wg_ref[0, 0].astype(BF16))) * _dot(x, wu_ref[0, 0].astype(BF16))
        _rows_to_tiles(ys_ref, _dot(hid.astype(BF16), wd_ref[0, 0].astype(BF16)))

    @pl.when(jnp.logical_not(valid))
    def _unused_tile():
        ys_ref[...] = jnp.zeros_like(ys_ref)


def _combine_body(h_ref, info_ref, pos_s, ys_ref, fw_ref, out_ref, y1, y2, sem, *, tm, final):
    bufs = (y1, y2)

    def issue(t, carry):
        dst = pl.ds(pl.multiple_of(t * ROW_TILE, ROW_TILE), ROW_TILE)
        for k in range(MOE_TOP_K):
            at = pl.multiple_of(pos_s[k, t] * ROW_TILE, ROW_TILE)
            pltpu.make_async_copy(ys_ref.at[pl.ds(at, ROW_TILE)], bufs[k].at[dst], sem.at[k]).start(priority=k)
        return carry

    lax.fori_loop(0, tm, issue, 0, unroll=8)
    for k in range(MOE_TOP_K):
        pltpu.make_async_copy(ys_ref.at[pl.ds(0, tm * ROW_TILE)], bufs[k], sem.at[k]).wait()
    o = h_ref[...] + info_ref[:, 0:1] * _tiles_to_rows(y1, tm) + info_ref[:, 1:2] * _tiles_to_rows(y2, tm)
    if final:
        o = _rms(o, fw_ref[...])
    out_ref[...] = o


PAIRS_PER_GROUP = MOE_EPG * (MOE_EPG - 1) // 2
N_BUCKETS = MOE_GROUPS * PAIRS_PER_GROUP
TB_BUCKET = 256
PACK_ROWS = 4


def _bucket_tables():
    ea, ec = [], []
    for g in range(MOE_GROUPS):
        for a in range(MOE_EPG):
            for c in range(a + 1, MOE_EPG):
                ea.append(g * MOE_EPG + a)
                ec.append(g * MOE_EPG + c)
    return np.asarray(ea, np.int32), np.asarray(ec, np.int32)


def _pb_route_body(h_ref, nw_ref, wr_ref, br_ref, tri_ref, sel_ref, rows_ref, brow_ref, cnt_ref):
    tm = h_ref.shape[0]
    half = h_ref.shape[1] // 2

    @pl.when(pl.program_id(0) == 0)
    def _init():
        cnt_ref[...] = jnp.zeros_like(cnt_ref)

    xn = _rms(h_ref[...], nw_ref[...])
    hi = xn.astype(BF16)
    lo = (xn - hi.astype(F32)).astype(BF16)
    logits = _dot(jnp.concatenate([hi, lo, hi], axis=1), wr_ref[...]) + br_ref[...]
    lane = lax.broadcasted_iota(jnp.int32, logits.shape, 1).astype(F32)
    i1, i2, w1, w2 = _route(logits, lane)
    swap = i2 < i1
    ea = jnp.where(swap, i2, i1)
    ec = jnp.where(swap, i1, i2)
    wa = jnp.where(swap, w2, w1)
    wc = jnp.where(swap, w1, w2)
    grp = jnp.floor(ea * (1.0 / MOE_EPG))
    a8 = ea - MOE_EPG * grp
    c8 = ec - MOE_EPG * grp
    bucket = grp * PAIRS_PER_GROUP + a8 * (2 * MOE_EPG - 1 - a8) * 0.5 + (c8 - a8 - 1.0)
    ind = jnp.where(lane == bucket, 1.0, 0.0)
    before = _dot(tri_ref[...], ind.astype(BF16)) + cnt_ref[0:1, :]
    rank = _lane_get(lane, before, bucket)
    cnt_ref[0:1, :] = cnt_ref[0:1, :] + jnp.sum(ind, axis=0, keepdims=True)
    rank_hi = jnp.floor(rank * (1.0 / 128.0))
    pieces = _lane_put(lane, [bucket, rank_hi, rank - 128.0 * rank_hi]).astype(BF16)
    tr = _dot_nt(sel_ref[...], pieces)
    sub = lax.broadcasted_iota(jnp.int32, tr.shape, 0)
    brow_ref[...] = jnp.where(sub == 0, tr[0:1, :], jnp.where(sub == 1, tr[1:2, :] * 128.0 + tr[2:3, :], 0.0)
                              ).astype(jnp.int32)
    bits = pltpu.bitcast(hi.astype(F32), jnp.uint32)
    words = (bits[:, :half] >> 16) | bits[:, half:]
    for c in range(PACK_ROWS):
        rows_ref[pl.ds(c, tm, stride=ROW_TILE), :] = words[:, c * 128:(c + 1) * 128]
    rows_ref[pl.ds(PACK_ROWS, tm, stride=ROW_TILE), :] = pltpu.bitcast(_lane_put(lane, [wa, wc]), jnp.uint32)
    for c in range(PACK_ROWS + 1, ROW_TILE):
        rows_ref[pl.ds(c, tm, stride=ROW_TILE), :] = jnp.zeros((tm, 128), jnp.uint32)


def _pb_invmap_body(off_s, brow_s, pos_s, src_hbm, src_s, sem, *, tm, rows_total):
    i = pl.program_id(0)

    @pl.when(i == 0)
    def _init():
        def zero(r, carry):
            src_s[r] = 0
            return carry
        lax.fori_loop(0, rows_total, zero, 0)

    def place(t, carry):
        p = off_s[brow_s[0, t]] + brow_s[1, t]
        src_s[p] = i * tm + t
        pos_s[0, t] = p
        return carry

    lax.fori_loop(0, tm, place, 0, unroll=8)

    @pl.when(i == pl.num_programs(0) - 1)
    def _flush():
        cp = pltpu.make_async_copy(src_s, src_hbm, sem)
        cp.start()
        cp.wait()


def _pb_expert_body(ea_s, ec_s, nv_s, tot_s, src_s, nxt_s, rows_hbm, wga, wua, wda, wgc, wuc, wdc, ys_ref,
                    xbuf, sem, *, tb):
    i = pl.program_id(0)
    tot = tot_s[0]

    def gather(idx_ref, count, slot):
        def issue(r2, carry):
            for par in range(2):
                r = 2 * r2 + par

                @pl.when(r < count)
                def _go(r=r, par=par):
                    at = pl.multiple_of(idx_ref[0, 0, r] * ROW_TILE, ROW_TILE)
                    to = pl.multiple_of(r * ROW_TILE, ROW_TILE)
                    pltpu.make_async_copy(rows_hbm.at[pl.ds(at, ROW_TILE)], xbuf.at[slot, pl.ds(to, ROW_TILE)],
                                          sem.at[slot]).start(priority=par)
            return carry

        lax.fori_loop(0, (count + 1) // 2, issue, 0)

    @pl.when(i == 0)
    def _first():
        xbuf[...] = jnp.zeros_like(xbuf)
        gather(src_s, nv_s[0], 0)

    def step(slot):
        @pl.when(i + 1 < tot)
        def _prefetch():
            gather(nxt_s, nv_s[i + 1], 1 - slot)

        n = nv_s[i] * ROW_TILE
        pltpu.make_async_copy(rows_hbm.at[pl.ds(0, n)], xbuf.at[slot, pl.ds(0, n)], sem.at[slot]).wait()
        w = jnp.concatenate([xbuf[slot, pl.ds(c, tb, stride=ROW_TILE), :] for c in range(PACK_ROWS)], axis=1)
        x = jnp.concatenate([pltpu.bitcast(w << 16, F32), pltpu.bitcast(w & jnp.uint32(0xFFFF0000), F32)],
                            axis=1).astype(BF16)
        wts = pltpu.bitcast(xbuf[slot, pl.ds(PACK_ROWS, tb, stride=ROW_TILE), :], F32)

        def expert(wg, wu, wd):
            hid = jax.nn.silu(_dot(x, wg[0, 0].astype(BF16))) * _dot(x, wu[0, 0].astype(BF16))
            return _dot(hid.astype(BF16), wd[0, 0].astype(BF16))

        _rows_to_tiles(ys_ref, wts[:, 0:1] * expert(wga, wua, wda) + wts[:, 1:2] * expert(wgc, wuc, wdc))

    for slot in range(2):
        @pl.when((i < tot) & (i % 2 == slot))
        def _run(slot=slot):
            step(slot)

    @pl.when(i >= tot)
    def _unused_tile():
        ys_ref[...] = jnp.zeros_like(ys_ref)


def _pb_combine_body(h_ref, pos_s, ys_hbm, fw_ref, out_ref, buf, sem, *, tm, final):
    def issue(t2, carry):
        for par in range(2):
            t = 2 * t2 + par
            at = pl.multiple_of(pos_s[0, t] * ROW_TILE, ROW_TILE)
            to = pl.multiple_of(t * ROW_TILE, ROW_TILE)
            pltpu.make_async_copy(ys_hbm.at[pl.ds(at, ROW_TILE)], buf.at[pl.ds(to, ROW_TILE)], sem.at[par]
                                  ).start(priority=par)
        return carry

    lax.fori_loop(0, tm // 2, issue, 0, unroll=8)
    for par in range(2):
        n = tm // 2 * ROW_TILE
        pltpu.make_async_copy(ys_hbm.at[pl.ds(0, n)], buf.at[pl.ds(0, n)], sem.at[par]).wait()
    o = h_ref[...] + _tiles_to_rows(buf, tm)
    if final:
        o = _rms(o, fw_ref[...])
    out_ref[...] = o


def _moe_pb(h, nw, w_group, b_group, w_router, b_router, w_gate, w_up, w_down, layer, final_w):
    t, d = h.shape
    assert d == ROW_TILE * 128 and MOE_TOP_K == 2 and N_BUCKETS <= ROUTE_LANES
    _, ne, _, ff = w_gate.shape
    tm = TM_MOE
    tb = TB_BUCKET
    nw2 = nw.reshape(1, d)
    wr3, bias = _router_weights(w_group, b_group, w_router, b_router)
    tri = jnp.asarray(np.tril(np.ones((tm, tm), np.float32), -1), BF16)
    sel = jnp.asarray(np.eye(8, ROUTE_LANES, dtype=np.float32), BF16)
    rows, brow, cnt = pl.pallas_call(
        _pb_route_body, grid=(t // tm,),
        in_specs=[pl.BlockSpec((tm, d), lambda i: (i, 0)),
                  pl.BlockSpec((1, d), lambda i: (0, 0)),
                  pl.BlockSpec((3 * d, ROUTE_LANES), lambda i: (0, 0)),
                  pl.BlockSpec((1, ROUTE_LANES), lambda i: (0, 0)),
                  pl.BlockSpec((tm, tm), lambda i: (0, 0)),
                  pl.BlockSpec((8, ROUTE_LANES), lambda i: (0, 0))],
        out_specs=[pl.BlockSpec((tm * ROW_TILE, 128), lambda i: (i, 0)),
                   pl.BlockSpec((8, tm), lambda i: (0, i)),
                   pl.BlockSpec((8, ROUTE_LANES), lambda i: (0, 0))],
        out_shape=[jax.ShapeDtypeStruct((t * ROW_TILE, 128), jnp.uint32),
                   jax.ShapeDtypeStruct((8, t), jnp.int32),
                   jax.ShapeDtypeStruct((8, ROUTE_LANES), F32)],
        compiler_params=_cparams(("arbitrary",)), name="moe_route",
    )(h, nw2, wr3, bias, tri, sel)

    cnt_i = cnt[0, :N_BUCKETS].astype(jnp.int32)
    ntile = (cnt_i + tb - 1) // tb
    tile_end = jnp.cumsum(ntile)
    tile_start = tile_end - ntile
    off_i = tile_start * tb
    n_tiles = t // tb + N_BUCKETS
    rows_total = n_tiles * tb
    tot = tile_end[-1:]
    ti = jnp.arange(n_tiles, dtype=jnp.int32)
    tile_b = jnp.minimum(jnp.sum((ti[:, None] >= tile_end[None, :]).astype(jnp.int32), axis=1), N_BUCKETS - 1)
    nvalid = jnp.clip(cnt_i[tile_b] - (ti - tile_start[tile_b]) * tb, 0, tb)
    tab_a, tab_c = _bucket_tables()
    tile_ea = jnp.asarray(tab_a)[tile_b]
    tile_ec = jnp.asarray(tab_c)[tile_b]

    pos, src = pl.pallas_call(
        functools.partial(_pb_invmap_body, tm=tm, rows_total=rows_total),
        grid_spec=pltpu.PrefetchScalarGridSpec(
            num_scalar_prefetch=1, grid=(t // tm,),
            in_specs=[pl.BlockSpec((8, tm), lambda i, o: (0, i), memory_space=pltpu.SMEM)],
            out_specs=[pl.BlockSpec((1, tm), lambda i, o: (0, i), memory_space=pltpu.SMEM),
                       pl.BlockSpec(memory_space=pl.ANY)],
            scratch_shapes=[pltpu.SMEM((rows_total,), jnp.int32), pltpu.SemaphoreType.DMA(())]),
        out_shape=[jax.ShapeDtypeStruct((1, t), jnp.int32), jax.ShapeDtypeStruct((rows_total,), jnp.int32)],
        compiler_params=_cparams(("arbitrary",)), name="moe_invmap",
    )(off_i, brow)

    src3 = src.reshape(n_tiles, 1, tb)
    wspec_a = lambda shape: pl.BlockSpec((1, 1) + shape, lambda i, ea, ec, nv, tt: (layer, ea[i], 0, 0))
    wspec_c = lambda shape: pl.BlockSpec((1, 1) + shape, lambda i, ea, ec, nv, tt: (layer, ec[i], 0, 0))
    ys = pl.pallas_call(
        functools.partial(_pb_expert_body, tb=tb),
        grid_spec=pltpu.PrefetchScalarGridSpec(
            num_scalar_prefetch=4, grid=(n_tiles,),
            in_specs=[pl.BlockSpec((1, 1, tb), lambda i, *_: (i, 0, 0), memory_space=pltpu.SMEM),
                      pl.BlockSpec((1, 1, tb), lambda i, *_: (jnp.minimum(i + 1, n_tiles - 1), 0, 0),
                                   memory_space=pltpu.SMEM),
                      pl.BlockSpec(memory_space=pl.ANY),
                      wspec_a((d, ff)), wspec_a((d, ff)), wspec_a((ff, d)),
                      wspec_c((d, ff)), wspec_c((d, ff)), wspec_c((ff, d))],
            out_specs=pl.BlockSpec((tb * ROW_TILE, 128), lambda i, *_: (i, 0)),
            scratch_shapes=[pltpu.VMEM((2, tb * ROW_TILE, 128), jnp.uint32), pltpu.SemaphoreType.DMA((2,))]),
        out_shape=jax.ShapeDtypeStruct((rows_total * ROW_TILE, 128), F32),
        compiler_params=_cparams(("arbitrary",)), name="moe_expert",
    )(tile_ea, tile_ec, nvalid, tot, src3, src3, rows, w_gate, w_up, w_down, w_gate, w_up, w_down)

    final = final_w is not None
    fw = (final_w if final else nw).reshape(1, d)
    return pl.pallas_call(
        functools.partial(_pb_combine_body, tm=tm, final=final), grid=(t // tm,),
        in_specs=[pl.BlockSpec((tm, d), lambda i: (i, 0)),
                  pl.BlockSpec((1, tm), lambda i: (0, i), memory_space=pltpu.SMEM),
                  pl.BlockSpec(memory_space=pl.ANY),
                  pl.BlockSpec((1, d), lambda i: (0, 0))],
        out_specs=pl.BlockSpec((tm, d), lambda i: (i, 0)),
        out_shape=jax.ShapeDtypeStruct((t, d), F32),
        scratch_shapes=[pltpu.VMEM((tm * ROW_TILE, 128), F32), pltpu.SemaphoreType.DMA((2,))],
        compiler_params=_cparams(("arbitrary",)), name="moe_combine",
    )(h, pos, ys, fw)


def _router_weights(w_group, b_group, w_router, b_router):
    d, ne = w_router.shape
    ng = w_group.shape[1]
    w = jnp.zeros((d, ROUTE_LANES), F32).at[:, :ne].set(w_router).at[:, ne:ne + ng].set(w_group)
    hi = w.astype(BF16)
    lo = (w - hi.astype(F32)).astype(BF16)
    bias = jnp.zeros((1, ROUTE_LANES), F32).at[0, :ne].set(b_router).at[0, ne:ne + ng].set(b_group)
    return jnp.concatenate([hi, hi, lo], axis=0), bias


def _moe(h, nw, w_group, b_group, w_router, b_router, w_gate, w_up, w_down, layer, final_w):
    t, d = h.shape
    assert d == ROW_TILE * 128, "row-as-tile layout needs d_model == 1024"
    _, ne, _, ff = w_gate.shape
    tm = TM_MOE
    te_rows = TM_EXPERT
    nw2 = nw.reshape(1, d)
    wr3, bias = _router_weights(w_group, b_group, w_router, b_router)
    tri = jnp.asarray(np.tril(np.ones((tm, tm), np.float32), -1), BF16)
    info, cnt = pl.pallas_call(
        _route_body, grid=(t // tm,),
        in_specs=[pl.BlockSpec((tm, d), lambda i: (i, 0)),
                  pl.BlockSpec((1, d), lambda i: (0, 0)),
                  pl.BlockSpec((3 * d, ROUTE_LANES), lambda i: (0, 0)),
                  pl.BlockSpec((1, ROUTE_LANES), lambda i: (0, 0)),
                  pl.BlockSpec((tm, tm), lambda i: (0, 0))],
        out_specs=[pl.BlockSpec((tm, ROUTE_LANES), lambda i: (i, 0)), pl.BlockSpec((8, ROUTE_LANES), lambda i: (0, 0))],
        out_shape=[jax.ShapeDtypeStruct((t, ROUTE_LANES), F32), jax.ShapeDtypeStruct((8, ROUTE_LANES), F32)],
        compiler_params=_cparams(("arbitrary",)), name="moe_route",
    )(h, nw2, wr3, bias, tri)

    cnt_i = cnt[0, :ne].astype(jnp.int32)
    ntile = (cnt_i + te_rows - 1) // te_rows
    tile_end = jnp.cumsum(ntile)
    off_i = (tile_end - ntile) * te_rows
    n_tiles = (MOE_TOP_K * t) // te_rows + ne
    rows_total = n_tiles * te_rows
    ti = jnp.arange(n_tiles, dtype=jnp.int32)
    tot = tile_end[-1:]
    ti_c = jnp.minimum(ti, tot[0] - 1)
    tile_e = jnp.sum((ti_c[:, None] >= tile_end[None, :]).astype(jnp.int32), axis=1)
    off_row = jnp.zeros((1, ROUTE_LANES), F32).at[0, :ne].set(off_i.astype(F32))
    sel = jnp.asarray(np.eye(8, ROUTE_LANES, dtype=np.float32), BF16)

    xs, pos = pl.pallas_call(
        functools.partial(_dispatch_body, tm=tm, tile_rows=te_rows, ne=ne, n_tiles=n_tiles),
        grid_spec=pltpu.PrefetchScalarGridSpec(
            num_scalar_prefetch=3, grid=(t // tm,),
            in_specs=[pl.BlockSpec((tm, d), lambda i, *_: (i, 0)),
                      pl.BlockSpec((1, d), lambda i, *_: (0, 0)),
                      pl.BlockSpec((tm, ROUTE_LANES), lambda i, *_: (i, 0)),
                      pl.BlockSpec((1, ROUTE_LANES), lambda i, *_: (0, 0)),
                      pl.BlockSpec((8, ROUTE_LANES), lambda i, *_: (0, 0))],
            out_specs=[pl.BlockSpec(memory_space=pl.ANY), pl.BlockSpec((8, tm), lambda i, *_: (0, i))],
            scratch_shapes=[pltpu.VMEM((tm * ROW_TILE, 128), F32), pltpu.VMEM((8, tm), jnp.int32),
                            pltpu.SMEM((8, tm), jnp.int32), pltpu.VMEM((te_rows // 2 * ROW_TILE, 128), F32),
                            pltpu.SemaphoreType.DMA((3,))]),
        out_shape=[jax.ShapeDtypeStruct((rows_total * ROW_TILE, 128), F32), jax.ShapeDtypeStruct((8, t), jnp.int32)],
        compiler_params=_cparams(("arbitrary",)), name="moe_dispatch",
    )(cnt_i, off_i, tot, h, nw2, info, off_row, sel)

    ys = pl.pallas_call(
        _expert_body,
        grid_spec=pltpu.PrefetchScalarGridSpec(
            num_scalar_prefetch=3, grid=(n_tiles,),
            in_specs=[pl.BlockSpec((te_rows * ROW_TILE, 128), lambda i, e, b, v: (b[i], 0)),
                      pl.BlockSpec((1, 1, d, ff), lambda i, e, b, v: (layer, e[i], 0, 0)),
                      pl.BlockSpec((1, 1, d, ff), lambda i, e, b, v: (layer, e[i], 0, 0)),
                      pl.BlockSpec((1, 1, ff, d), lambda i, e, b, v: (layer, e[i], 0, 0))],
            out_specs=pl.BlockSpec((te_rows * ROW_TILE, 128), lambda i, e, b, v: (i, 0))),
        out_shape=jax.ShapeDtypeStruct((rows_total * ROW_TILE, 128), F32),
        compiler_params=_cparams(("arbitrary",)), name="moe_expert",
    )(tile_e, ti_c, tot, xs, w_gate, w_up, w_down)

    final = final_w is not None
    fw = (final_w if final else nw).reshape(1, d)
    return pl.pallas_call(
        functools.partial(_combine_body, tm=tm, final=final), grid=(t // tm,),
        in_specs=[pl.BlockSpec((tm, d), lambda i: (i, 0)),
                  pl.BlockSpec((tm, ROUTE_LANES), lambda i: (i, 0)),
                  pl.BlockSpec((8, tm), lambda i: (0, i), memory_space=pltpu.SMEM),
                  pl.BlockSpec(memory_space=pl.ANY),
                  pl.BlockSpec((1, d), lambda i: (0, 0))],
        out_specs=pl.BlockSpec((tm, d), lambda i: (i, 0)),
        out_shape=jax.ShapeDtypeStruct((t, d), F32),
        scratch_shapes=[pltpu.VMEM((tm * ROW_TILE, 128), F32), pltpu.VMEM((tm * ROW_TILE, 128), F32),
                        pltpu.SemaphoreType.DMA((2,))],
        compiler_params=_cparams(("arbitrary",)), name="moe_combine",
    )(h, info, pos, ys, fw)


def _s5_weights(lam_re, lam_im, b_re, b_im, c_re, c_im, log_dt, nsteps):
    hp = lax.Precision.HIGHEST
    L = S5_CHUNK
    g, p = lam_re.shape
    ch = b_re.shape[-1]
    lam = lax.complex(lam_re.astype(F32), lam_im.astype(F32))
    dt = jnp.exp(log_dt.astype(F32))[:, None]
    lam_bar = jnp.exp(lam * dt)
    b_bar = ((lam_bar - 1.0) / lam)[..., None] * lax.complex(b_re.astype(F32), b_im.astype(F32))
    cmat = lax.complex(c_re.astype(F32), c_im.astype(F32))
    pows = [jnp.ones_like(lam_bar)]
    for _ in range(L):
        pows.append(pows[-1] * lam_bar)
    pw = jnp.stack(pows, axis=1)
    kern = jnp.real(jnp.einsum('gop,gtp,gpi->gtoi', cmat, pw[:, :L], b_bar, precision=hp))
    lag = np.arange(L)[None, :] - np.arange(L)[:, None]
    toep = jnp.where((lag >= 0)[None, :, :, None, None], kern[:, np.maximum(lag, 0)], 0.0)
    toep = toep.transpose(0, 1, 4, 2, 3).reshape(g, L * ch, L * ch)
    wst = pw[:, L - 1 - np.arange(L)][:, :, :, None] * b_bar[:, None, :, :]
    wst = wst.transpose(0, 1, 3, 2).reshape(g, L * ch, p)
    wst = jnp.concatenate([jnp.real(wst), jnp.imag(wst)], axis=-1)
    mo = cmat.transpose(0, 2, 1)[:, :, None, :] * pw[:, 1:L + 1].transpose(0, 2, 1)[:, :, :, None]
    mo = mo.reshape(g, p, L * ch)
    wout = jnp.concatenate([jnp.real(mo), -jnp.imag(mo)], axis=1)
    a = pw[:, L]
    ars, ais = [], []
    for _ in range(nsteps):
        ars.append(jnp.concatenate([jnp.real(a), jnp.real(a)], axis=-1))
        ais.append(jnp.concatenate([-jnp.imag(a), jnp.imag(a)], axis=-1))
        a = a * a
    wcat = jnp.concatenate([toep, wst], axis=-1).astype(BF16)
    return wcat, wout.astype(BF16), jnp.stack(ars, axis=1), jnp.stack(ais, axis=1)


def _s5_body(u_ref, wcat_ref, wout_ref, ar_ref, ai_ref, y_ref, us_ref, ys_ref, *, nsteps):
    L = S5_CHUNK
    ch = S5_GROUP_CH
    gpc = 128 // ch
    ny = L * ch
    nc = u_ref.shape[0] // L
    for s in range(L):
        us_ref[s] = u_ref[pl.ds(s, nc, stride=L), :]
    lane = lax.broadcasted_iota(jnp.int32, (nc, 128), 1)
    ridx = lax.broadcasted_iota(jnp.int32, (nc, 128), 0)

    def shift(x, k):
        return jnp.where(ridx >= k, pltpu.roll(x, k, axis=0), 0.0)

    for gi in range(gpc):
        halves = []
        for hh in range(ny // 128):
            acc = None
            for s8 in range(gpc):
                rot = ((s8 - gi) * ch) % 128
                src = us_ref[hh * gpc + s8]
                if rot:
                    src = pltpu.roll(src, rot, axis=1)
                slot = (lane >= s8 * ch) & (lane < (s8 + 1) * ch)
                acc = jnp.where(slot, src, 0.0) if acc is None else jnp.where(slot, src, acc)
            halves.append(acc)
        ug = jnp.concatenate(halves, axis=1).astype(BF16)
        r = _dot(ug, wcat_ref[gi])
        y1 = r[:, :ny]
        z = r[:, ny:]
        w = shift(z, 1)
        for k in range(nsteps):
            if (1 << k) >= nc:
                break
            sk = shift(w, 1 << k)
            w = w + sk * ar_ref[gi, k:k + 1, :] + pltpu.roll(sk, z.shape[1] // 2, axis=1) * ai_ref[gi, k:k + 1, :]
        yg = y1 + _dot(w.astype(BF16), wout_ref[gi])
        slot = (lane >= gi * ch) & (lane < (gi + 1) * ch)
        for t in range(L):
            src = yg[:, (t // gpc) * 128:(t // gpc + 1) * 128]
            rot = ((gi - t % gpc) * ch) % 128
            if rot:
                src = pltpu.roll(src, rot, axis=1)
            ys_ref[t] = jnp.where(slot, src, 0.0) if gi == 0 else jnp.where(slot, src, ys_ref[t])
    for t in range(L):
        y_ref[pl.ds(t, nc, stride=L), :] = ys_ref[t]


def _s5(main, col0, width, nb, seq, lam_re, lam_im, b_re, b_im, c_re, c_im, log_dt):
    t = main.shape[0]
    L = S5_CHUNK
    ch = S5_GROUP_CH
    g = width // ch
    nc = seq // L
    gpc = 128 // ch
    ncol = width // 128
    assert col0 % 128 == 0 and width % 128 == 0 and (L * ch) % 128 == 0
    nsteps = max(1, (nc - 1).bit_length())
    wcat, wout, ar, ai = _s5_weights(lam_re, lam_im, b_re, b_im, c_re, c_im, log_dt, nsteps)
    body = functools.partial(_s5_body, nsteps=nsteps)
    return pl.pallas_call(
        body, grid=(ncol, nb),
        in_specs=[pl.BlockSpec((seq, 128), lambda j, b: (b, col0 // 128 + j)),
                  pl.BlockSpec((gpc,) + wcat.shape[1:], lambda j, b: (j, 0, 0)),
                  pl.BlockSpec((gpc,) + wout.shape[1:], lambda j, b: (j, 0, 0)),
                  pl.BlockSpec((gpc,) + ar.shape[1:], lambda j, b: (j, 0, 0)),
                  pl.BlockSpec((gpc,) + ai.shape[1:], lambda j, b: (j, 0, 0))],
        out_specs=pl.BlockSpec((seq, 128), lambda j, b: (b, j)),
        out_shape=jax.ShapeDtypeStruct((t, width), F32),
        scratch_shapes=[pltpu.VMEM((L, nc, 128), F32), pltpu.VMEM((L, nc, 128), F32)],
        compiler_params=_cparams(("parallel", "parallel")), name="s5",
    )(main, wcat, wout, ar, ai)


def _hgrn_gmat():
    L = HGRN_CHUNK
    blocks = 2 + int(np.log2(L))
    gm = np.zeros((blocks * L, L), np.float32)
    for j in range(L):
        gm[j, :j + 1] = 1.0
        gm[L + j, j + 1:] = 1.0
    li, m = 2, L
    while m >= 2:
        half = m // 2
        for j in range(L):
            pos = j % m
            r = j - pos + half - 1
            if pos >= half:
                gm[li * L + j, r + 1:j + 1] = 1.0
            else:
                gm[li * L + j, j + 1:r + 1] = 1.0
        li += 1
        m //= 2
    return gm


def _hgrn_body(main_ref, gm_ref, lb_ref, nw_ref, out_ref, st_ref, *, nb, nh, dh):
    L = HGRN_CHUNK
    w = nh * dh

    @pl.when(pl.program_id(0) == 0)
    def _init():
        st_ref[...] = jnp.zeros_like(st_ref)

    row = lax.broadcasted_iota(jnp.int32, (L, 2 * L), 0)
    col = lax.broadcasted_iota(jnp.int32, (L, 2 * L), 1) & (L - 1)
    rowd = lax.broadcasted_iota(jnp.int32, (L, 2 * dh), 0)
    laned = lax.broadcasted_iota(jnp.int32, (L, 2 * dh), 1)
    first = laned < dh
    eye = row == col

    def blockdiag(x):
        z = jnp.zeros_like(x)
        return jnp.concatenate([jnp.where(first, x, z), jnp.where(first, z, x)], axis=0)

    gm2 = gm_ref[...]
    lb = lb_ref[...]
    zst = jnp.zeros((dh, dh), BF16)
    for b in range(nb):
        fg = main_ref[b, :, w:2 * w]
        f = lb + (1.0 - lb) * jax.nn.sigmoid(fg)
        kk = (1.0 - lb) * jax.nn.sigmoid(-fg)
        lf = jnp.log(f)
        hi = lf.astype(BF16)
        mid = (lf - hi.astype(F32)).astype(BF16)
        p_all = jnp.exp(_dot(gm2, jnp.concatenate([hi, mid], axis=0)))
        for hp in range(nh // 2):
            i0 = b * nh + 2 * hp
            cs = slice(2 * hp * dh, (2 * hp + 2) * dh)
            q = main_ref[b, :, 2 * hp * dh:(2 * hp + 2) * dh]
            v = main_ref[b, :, 2 * w + 2 * hp * dh:2 * w + (2 * hp + 2) * dh]
            og = main_ref[b, :, 3 * w + 2 * hp * dh:3 * w + (2 * hp + 2) * dh]
            k = kk[:, cs]
            pb = p_all[0:L, cs]
            pe = p_all[L:2 * L, cs]
            st0 = st_ref[i0]
            st1 = st_ref[i0 + 1]
            stbd = jnp.concatenate([jnp.concatenate([st0.astype(BF16), zst], axis=1),
                                    jnp.concatenate([zst, st1.astype(BF16)], axis=1)], axis=0)
            o = _dot_nt((q * pb).astype(BF16), stbd)
            attn = jnp.where(eye, _dot_nt(q.astype(BF16), blockdiag(k.astype(BF16))), 0.0)
            li, m = 2, L
            while m >= 2:
                pl_ = p_all[li * L:(li + 1) * L, cs]
                up = (rowd & (m - 1)) >= (m // 2)
                ql = jnp.where(up, q * pl_, 0.0).astype(BF16)
                kl = jnp.where(up, 0.0, k * pl_).astype(BF16)
                same = (row & ~(m - 1)) == (col & ~(m - 1))
                attn = attn + jnp.where(same, _dot_nt(ql, blockdiag(kl)), 0.0)
                li += 1
                m //= 2
            vb = v.astype(BF16)
            o = o + _dot(attn.astype(BF16), blockdiag(vb))
            kh = (k * pe).astype(BF16)
            for j in range(2):
                hs = slice(j * dh, (j + 1) * dh)
                gs = slice((2 * hp + j) * dh, (2 * hp + j + 1) * dh)
                st = st0 if j == 0 else st1
                st_ref[i0 + j] = st * pb[L - 1:L, hs] + _dot_tn(vb[:, hs], kh[:, hs])
                oj = o[:, hs]
                on = oj * lax.rsqrt(jnp.mean(oj * oj, axis=-1, keepdims=True) + RMS_EPS)
                out_ref[b, :, gs] = on * nw_ref[:, gs] * jax.nn.silu(og[:, hs])


def _hgrn(main, lower_bound, out_norm, nb, seq):
    t, n = main.shape
    nh = HGRN_HEADS
    w = out_norm.shape[0]
    dh = w // nh
    L = HGRN_CHUNK
    nc = seq // L
    gm = _hgrn_gmat()
    gm = jnp.asarray(np.concatenate([gm, gm], axis=1), BF16)
    body = functools.partial(_hgrn_body, nb=nb, nh=nh, dh=dh)
    out = pl.pallas_call(
        body, grid=(nc,),
        in_specs=[pl.BlockSpec((nb, L, 4 * w), lambda c: (0, c, 0)),
                  pl.BlockSpec(gm.shape, lambda c: (0, 0)),
                  pl.BlockSpec((1, w), lambda c: (0, 0)),
                  pl.BlockSpec((1, w), lambda c: (0, 0))],
        out_specs=pl.BlockSpec((nb, L, w), lambda c: (0, c, 0)),
        out_shape=jax.ShapeDtypeStruct((nb, seq, w), F32),
        scratch_shapes=[pltpu.VMEM((nb * nh, dh, dh), F32)],
        compiler_params=_cparams(("arbitrary",)), name="hgrn",
    )(main.reshape(nb, seq, n), gm, lower_bound.reshape(1, w), out_norm.reshape(1, w))
    return out.reshape(t, w)


def _out_c_body(h_ref, ys_ref, u_ref, oh_ref, d_ref, wglu_ref, bglu_ref, w_ref, out_ref):
    ws = ys_ref.shape[1]
    z = jax.nn.gelu(ys_ref[...] + d_ref[...] * u_ref[...])
    gate = jax.nn.sigmoid(_dot(z.astype(BF16), wglu_ref[...]) + bglu_ref[...])
    out_ref[...] = (h_ref[...] + _dot((z * gate).astype(BF16), w_ref[:ws, :])
                    + _dot(oh_ref[...].astype(BF16), w_ref[ws:, :]))


def _out_c(h, ys, main, oh, d_skip, w_glu, b_glu, w_out):
    t, d = h.shape
    ws = ys.shape[1]
    wh = oh.shape[1]
    tm = TM_PROJ
    ub = (main.shape[1] - ws) // ws
    return pl.pallas_call(
        _out_c_body, grid=(t // tm,),
        in_specs=[pl.BlockSpec((tm, d), lambda i: (i, 0)),
                  pl.BlockSpec((tm, ws), lambda i: (i, 0)),
                  pl.BlockSpec((tm, ws), lambda i: (i, ub)),
                  pl.BlockSpec((tm, wh), lambda i: (i, 0)),
                  pl.BlockSpec((1, ws), lambda i: (0, 0)),
                  pl.BlockSpec(w_glu.shape, lambda i: (0, 0)),
                  pl.BlockSpec((1, ws), lambda i: (0, 0)),
                  pl.BlockSpec(w_out.shape, lambda i: (0, 0))],
        out_specs=pl.BlockSpec((tm, d), lambda i: (i, 0)),
        out_shape=jax.ShapeDtypeStruct((t, d), F32),
        compiler_params=_cparams(("parallel",)), name="out_c",
    )(h, ys, main, oh, d_skip.reshape(1, ws), w_glu, b_glu.reshape(1, ws), w_out)


def kernel(x, norm_mix, norm_ffn, norm_final, ab_w_in, ab_gate_bias, ab_head_norm, ab_conv_w, ab_w_out, cd_w_in, s5_lambda_re, s5_lambda_im, s5_b_re, s5_b_im, s5_c_re, s5_c_im, s5_d, s5_log_dt, s5_w_glu, s5_b_glu, hgrn_lb, hgrn_out_norm, cd_w_out, moe_w_group, moe_b_group, moe_w_router, moe_b_router, moe_w_gate, moe_w_up, moe_w_down):
    nb, seq, d = x.shape
    depth = norm_mix.shape[0]
    h = x.reshape(nb * seq, d)
    for layer in range(depth):
        j = layer // 2
        if layer % 2 == 0:
            wm = ab_head_norm.shape[1]
            ng = ab_gate_bias.shape[1]
            w_in = ab_w_in[j]
            w_main = jnp.concatenate([w_in[:, :4 * wm], w_in[:, 4 * wm + ng:]], axis=1).astype(BF16)
            w_gates = w_in[:, 4 * wm:4 * wm + ng].astype(BF16)
            main, g, gt = _proj(h, norm_mix[layer], w_main, w_gates)
            hm = _mlstm(main, g, gt, ab_gate_bias[j], ab_head_norm[j], nb, seq)
            h = _out_a(h, hm, main, ab_conv_w[j], ab_w_out[j].astype(BF16), seq)
        else:
            ws = s5_d.shape[1]
            w_in = cd_w_in[j]
            w_main = jnp.concatenate([w_in[:, ws:], w_in[:, :ws]], axis=1).astype(BF16)
            main = _proj(h, norm_mix[layer], w_main)
            sm = jax.nn.softmax(hgrn_lb.astype(F32), axis=0)
            lower_bound = jnp.cumsum(sm, axis=0)[layer] - sm[0]
            ys = _s5(main, main.shape[1] - ws, ws, nb, seq, s5_lambda_re[j], s5_lambda_im[j], s5_b_re[j], s5_b_im[j],
                     s5_c_re[j], s5_c_im[j], s5_log_dt[j])
            oh = _hgrn(main, lower_bound, hgrn_out_norm[j], nb, seq)
            h = _out_c(h, ys, main, oh, s5_d[j], s5_w_glu[j].astype(BF16), s5_b_glu[j], cd_w_out[j].astype(BF16))
        h = _moe_pb(h, norm_ffn[layer], moe_w_group[layer], moe_b_group[layer], moe_w_router[layer], moe_b_router[layer],
                 moe_w_gate, moe_w_up, moe_w_down, layer,
                 norm_final if layer == depth - 1 else None)
    return h.reshape(nb, seq, d)
```

```python
import functools

import numpy as np
import jax
import jax.numpy as jnp
from jax import lax
from jax.experimental import pallas as pl
from jax.experimental.pallas import tpu as pltpu

F32 = jnp.float32
BF16 = jnp.bfloat16
RMS_EPS = 1e-6
MLSTM_CHUNK = 512
HGRN_CHUNK = 128
S5_CHUNK = 16
S5_GROUP_CH = 16
S5_STATE = 64
MLSTM_HEADS = 4
HGRN_HEADS = 4
MOE_GROUPS = 4
MOE_EPG = 8
ROUTE_LANES = 128
TM_PROJ = 512
TM_MOE = 512
TM_EXPERT = 512
MOE_TOP_K = 2
VMEM_LIMIT = 56 * 1024 * 1024

_NT = (((1,), (1,)), ((), ()))
_TN = (((0,), (0,)), ((), ()))


def _cparams(sem):
    return pltpu.CompilerParams(dimension_semantics=sem, vmem_limit_bytes=VMEM_LIMIT)


def _rms(x, w):
    return x * lax.rsqrt(jnp.mean(x * x, axis=-1, keepdims=True) + RMS_EPS) * w


def _split3(x):
    hi = x.astype(BF16)
    r = x - hi.astype(F32)
    mid = r.astype(BF16)
    lo = (r - mid.astype(F32)).astype(BF16)
    return hi, mid, lo


def _dot(a, b):
    return jnp.dot(a, b, preferred_element_type=F32)


def _dot_nt(a, b):
    return lax.dot_general(a, b, _NT, preferred_element_type=F32)


def _dot_tn(a, b):
    return lax.dot_general(a, b, _TN, preferred_element_type=F32)


def _proj_gates_body(x_ref, nw_ref, w_ref, wg_ref, wgt_ref, main_ref, g_ref, gt_ref):
    xn = _rms(x_ref[...], nw_ref[...]).astype(BF16)
    main_ref[...] = _dot(xn, w_ref[...])
    g_ref[...] = _dot(xn, wg_ref[...])[:, : g_ref.shape[1]]
    gt_ref[...] = _dot_nt(wgt_ref[...], xn)


def _proj_body(x_ref, nw_ref, w_ref, main_ref):
    xn = _rms(x_ref[...], nw_ref[...]).astype(BF16)
    main_ref[...] = _dot(xn, w_ref[...])


def _proj(h, nw, w_main, w_gates=None):
    t, d = h.shape
    n = w_main.shape[1]
    tm = TM_PROJ
    x_spec = pl.BlockSpec((tm, d), lambda i: (i, 0))
    nw_spec = pl.BlockSpec((1, d), lambda i: (0, 0))
    w_spec = pl.BlockSpec((d, n), lambda i: (0, 0))
    main_spec = pl.BlockSpec((tm, n), lambda i: (i, 0))
    main_shape = jax.ShapeDtypeStruct((t, n), F32)
    if w_gates is None:
        return pl.pallas_call(
            _proj_body, grid=(t // tm,), in_specs=[x_spec, nw_spec, w_spec], out_specs=main_spec,
            out_shape=main_shape, compiler_params=_cparams(("parallel",)), name="proj",
        )(h, nw.reshape(1, d), w_main)
    ng = w_gates.shape[1]
    wg_pad = jnp.zeros((d, 128), BF16).at[:, :ng].set(w_gates)
    return pl.pallas_call(
        _proj_gates_body, grid=(t // tm,),
        in_specs=[x_spec, nw_spec, w_spec, pl.BlockSpec((d, 128), lambda i: (0, 0)),
                  pl.BlockSpec((ng, d), lambda i: (0, 0))],
        out_specs=[main_spec, pl.BlockSpec((tm, ng), lambda i: (i, 0)), pl.BlockSpec((ng, tm), lambda i: (0, i))],
        out_shape=[main_shape, jax.ShapeDtypeStruct((t, ng), F32), jax.ShapeDtypeStruct((ng, t), F32)],
        compiler_params=_cparams(("parallel",)), name="proj_gates",
    )(h, nw.reshape(1, d), w_main, wg_pad, w_gates.T)


def _mlstm_body(main_ref, g_ref, gt_ref, br_ref, bc_ref, hn_ref, out_ref, c_ref, n_ref, m_ref, *, nb, nh, dh):
    L = MLSTM_CHUNK
    w = nh * dh

    @pl.when(pl.program_id(0) == 0)
    def _init():
        c_ref[...] = jnp.zeros_like(c_ref)
        n_ref[...] = jnp.zeros_like(n_ref)
        m_ref[...] = jnp.full_like(m_ref, -1e30)

    row = lax.broadcasted_iota(jnp.int32, (L, L), 0)
    col = lax.broadcasted_iota(jnp.int32, (L, L), 1)
    causal = col <= row
    tril = causal.astype(BF16)
    triu = (row <= col).astype(BF16)
    scale = dh ** -0.5
    for b in range(nb):
        g = g_ref[b] + br_ref[...]
        gt = gt_ref[b, 0] + bc_ref[...]
        i_c = g[:, :nh]
        i_r = gt[:nh, :]
        lfc = _split3(jax.nn.log_sigmoid(g[:, nh:]))
        lfr = _split3(jax.nn.log_sigmoid(gt[nh:, :]))
        bc_all = _dot(tril, lfc[0]) + _dot(tril, lfc[1]) + _dot(tril, lfc[2])
        br_all = _dot(lfr[0], triu) + _dot(lfr[1], triu) + _dot(lfr[2], triu)
        for h in range(nh):
            idx = b * nh + h
            q = main_ref[b, :, h * dh:(h + 1) * dh]
            k = main_ref[b, :, w + h * dh:w + (h + 1) * dh] * scale
            v = main_ref[b, :, 2 * w + h * dh:2 * w + (h + 1) * dh]
            o = main_ref[b, :, 3 * w + h * dh:3 * w + (h + 1) * dh]
            bc = bc_all[:, h:h + 1]
            br = br_all[h:h + 1, :]
            ir = i_r[h:h + 1, :]
            ic = i_c[:, h:h + 1]
            m_prev = m_ref[idx]
            c_prev = c_ref[idx]
            n_prev = n_ref[idx]
            logw = jnp.where(causal, bc - br + ir, -jnp.inf)
            inter = bc + m_prev
            m_row = jnp.maximum(jnp.max(logw, axis=-1, keepdims=True), inter)
            qb = q.astype(BF16)
            kb = k.astype(BF16)
            vb = v.astype(BF16)
            s = _dot_nt(qb, kb) * jnp.exp(logw - m_row)
            isc = jnp.exp(inter - m_row)
            num = _dot(s.astype(BF16), vb) + isc * _dot_nt(qb, c_prev.astype(BF16))
            den = jnp.sum(s, axis=-1, keepdims=True) + isc * jnp.sum(q * n_prev, axis=-1, keepdims=True)
            hout = num / jnp.maximum(jnp.abs(den), jnp.exp(-m_row))
            b_end = bc[L - 1:L, :]
            logg = b_end - bc + ic
            m_new = jnp.maximum(b_end + m_prev, jnp.max(logg, axis=0, keepdims=True))
            wk = jnp.exp(logg - m_new)
            decay = jnp.exp(b_end + m_prev - m_new)
            c_ref[idx] = decay * c_prev + _dot_tn((v * wk).astype(BF16), kb)
            n_ref[idx] = decay * n_prev + jnp.sum(wk * k, axis=0, keepdims=True)
            m_ref[idx] = m_new
            hn = hout * lax.rsqrt(jnp.mean(hout * hout, axis=-1, keepdims=True) + RMS_EPS)
            out_ref[b, :, h * dh:(h + 1) * dh] = hn * hn_ref[:, h * dh:(h + 1) * dh] * jax.nn.sigmoid(o)


def _mlstm(main, g, gt, gate_bias, head_norm, nb, seq):
    t, n = main.shape
    nh = MLSTM_HEADS
    w = head_norm.shape[0]
    dh = w // nh
    L = MLSTM_CHUNK
    nc = seq // L
    main3 = main.reshape(nb, seq, n)
    g3 = g.reshape(nb, seq, 2 * nh)
    gt4 = gt.reshape(2 * nh, nb, nc, L).transpose(1, 2, 0, 3)
    body = functools.partial(_mlstm_body, nb=nb, nh=nh, dh=dh)
    out = pl.pallas_call(
        body, grid=(nc,),
        in_specs=[pl.BlockSpec((nb, L, 4 * w), lambda c: (0, c, 0)),
                  pl.BlockSpec((nb, L, 2 * nh), lambda c: (0, c, 0)),
                  pl.BlockSpec((nb, 1, 2 * nh, L), lambda c: (0, c, 0, 0)),
                  pl.BlockSpec((1, 2 * nh), lambda c: (0, 0)),
                  pl.BlockSpec((2 * nh, 1), lambda c: (0, 0)),
                  pl.BlockSpec((1, w), lambda c: (0, 0))],
        out_specs=pl.BlockSpec((nb, L, w), lambda c: (0, c, 0)),
        out_shape=jax.ShapeDtypeStruct((nb, seq, w), F32),
        scratch_shapes=[pltpu.VMEM((nb * nh, dh, dh), F32), pltpu.VMEM((nb * nh, 1, dh), F32),
                        pltpu.VMEM((nb * nh, 1, 1), F32)],
        compiler_params=_cparams(("arbitrary",)), name="mlstm",
    )(main3, g3, gt4, gate_bias.reshape(1, 2 * nh), gate_bias.reshape(2 * nh, 1), head_norm.reshape(1, w))
    return out.reshape(t, w)


def _out_a_body(h_ref, hm_ref, gb_ref, gc_ref, xin_ref, pgc_ref, pxin_ref, cw_ref, w_ref, out_ref, *, tm, seq):
    i = pl.program_id(0)
    wm = hm_ref.shape[1]
    p = gc_ref[...] * xin_ref[...]
    first = (i * tm) % seq == 0
    pp = jnp.where(first, 0.0, pgc_ref[...] * pxin_ref[...])
    rowi = lax.broadcasted_iota(jnp.int32, p.shape, 0)
    p1 = jnp.where(rowi == 0, pp[7:8, :], pltpu.roll(p, 1, axis=0))
    p2 = jnp.where(rowi == 0, pp[6:7, :], jnp.where(rowi == 1, pp[7:8, :], pltpu.roll(p, 2, axis=0)))
    yc = gb_ref[...] * (cw_ref[0:1, :] * p2 + cw_ref[1:2, :] * p1 + cw_ref[2:3, :] * p)
    out_ref[...] = (h_ref[...] + _dot(hm_ref[...].astype(BF16), w_ref[:wm, :])
                    + _dot(yc.astype(BF16), w_ref[wm:, :]))


def _out_a(h, hm, main, conv_w, w_out, seq):
    t, d = h.shape
    wm = hm.shape[1]
    wc = conv_w.shape[1]
    tm = TM_PROJ
    cb = (4 * wm) // wc
    rb = tm // 8
    prev = lambda i: jnp.maximum(i * rb - 1, 0)
    body = functools.partial(_out_a_body, tm=tm, seq=seq)
    return pl.pallas_call(
        body, grid=(t // tm,),
        in_specs=[pl.BlockSpec((tm, d), lambda i: (i, 0)),
                  pl.BlockSpec((tm, wm), lambda i: (i, 0)),
                  pl.BlockSpec((tm, wc), lambda i: (i, cb)),
                  pl.BlockSpec((tm, wc), lambda i: (i, cb + 1)),
                  pl.BlockSpec((tm, wc), lambda i: (i, cb + 2)),
                  pl.BlockSpec((8, wc), lambda i: (prev(i), cb + 1)),
                  pl.BlockSpec((8, wc), lambda i: (prev(i), cb + 2)),
                  pl.BlockSpec(conv_w.shape, lambda i: (0, 0)),
                  pl.BlockSpec(w_out.shape, lambda i: (0, 0))],
        out_specs=pl.BlockSpec((tm, d), lambda i: (i, 0)),
        out_shape=jax.ShapeDtypeStruct((t, d), F32),
        compiler_params=_cparams(("parallel",)), name="out_a",
    )(h, hm, main, main, main, main, main, conv_w, w_out)


def _route(logits, lane):
    ne = MOE_GROUPS * MOE_EPG
    big = 1e9
    gl = jnp.where((lane >= ne) & (lane < ne + MOE_GROUPS), logits, -jnp.inf)
    gmax = jnp.max(gl, axis=-1, keepdims=True)
    gidx = jnp.min(jnp.where(gl == gmax, lane - ne, big), axis=-1, keepdims=True)
    gval = 1.0 / jnp.sum(jnp.exp(gl - gmax), axis=-1, keepdims=True)
    lo = gidx * MOE_EPG
    sel = jnp.where((lane >= lo) & (lane < lo + MOE_EPG), logits, -jnp.inf)
    l1 = jnp.max(sel, axis=-1, keepdims=True)
    i1 = jnp.min(jnp.where(sel == l1, lane, big), axis=-1, keepdims=True)
    sel2 = jnp.where(lane == i1, -jnp.inf, sel)
    l2 = jnp.max(sel2, axis=-1, keepdims=True)
    i2 = jnp.min(jnp.where(sel2 == l2, lane, big), axis=-1, keepdims=True)
    r = jnp.exp(l2 - l1)
    w1 = gval / (1.0 + r)
    return i1, i2, w1, w1 * r


ROW_TILE = 8


def _rows_to_tiles(ref, x):
    n = x.shape[0]
    for c in range(ROW_TILE):
        ref[pl.ds(c, n, stride=ROW_TILE), :] = x[:, c * 128:(c + 1) * 128]


def _tiles_to_rows(ref, n):
    return jnp.concatenate([ref[pl.ds(c, n, stride=ROW_TILE), :] for c in range(ROW_TILE)], axis=1)


def _lane_put(lane, cols):
    out = jnp.where(lane == 0.0, cols[0], 0.0)
    for k in range(1, len(cols)):
        out = out + jnp.where(lane == float(k), cols[k], 0.0)
    return out


def _lane_get(lane, x, idx_col):
    return jnp.sum(jnp.where(lane == idx_col, x, 0.0), axis=-1, keepdims=True)


def _route_body(h_ref, nw_ref, wr_ref, br_ref, tri_ref, info_ref, cnt_ref, tile_ref):
    @pl.when(pl.program_id(0) == 0)
    def _init():
        cnt_ref[...] = jnp.zeros_like(cnt_ref)

    xn = _rms(h_ref[...], nw_ref[...])
    hi = xn.astype(BF16)
    lo = (xn - hi.astype(F32)).astype(BF16)
    logits = _dot(jnp.concatenate([hi, lo, hi], axis=1), wr_ref[...]) + br_ref[...]
    lane = lax.broadcasted_iota(jnp.int32, logits.shape, 1).astype(F32)
    i1, i2, w1, w2 = _route(logits, lane)
    ind = jnp.where((lane == i1) | (lane == i2), 1.0, 0.0)
    ahead = _dot(tri_ref[...], ind.astype(BF16))
    before = ahead + cnt_ref[0:1, :]
    tcnt = jnp.broadcast_to(jnp.sum(ind, axis=0, keepdims=True), cnt_ref.shape)
    lane8 = lax.broadcasted_iota(jnp.int32, cnt_ref.shape, 1)
    sub8 = lax.broadcasted_iota(jnp.int32, cnt_ref.shape, 0)
    incl = tcnt
    sh = 1
    while sh < cnt_ref.shape[1]:
        incl = incl + jnp.where(lane8 >= sh, pltpu.roll(incl, sh, axis=1), 0.0)
        sh *= 2
    toff = incl - tcnt
    local = ahead + toff[0:1, :]
    info_ref[...] = _lane_put(lane, [w1, w2, i1, i2, _lane_get(lane, before, i1), _lane_get(lane, before, i2),
                                     _lane_get(lane, local, i1), _lane_get(lane, local, i2)])
    tile_ref[...] = jnp.where(sub8 == 0, tcnt, jnp.where(sub8 == 1, toff, 0.0))
    cnt_ref[0:1, :] = cnt_ref[0:1, :] + tcnt[0:1, :]


def _dispatch_body(cnt_s, off_s, tot_s, h_ref, nw_ref, info_ref, offrow_ref, sel_ref, runs_s, xs_ref, pos_ref,
                   xn_buf, blk, pos_v, pos_s, zbuf, sem, *, tm, tile_rows, ne, n_tiles):
    i = pl.program_id(0)
    _rows_to_tiles(xn_buf, _rms(h_ref[...], nw_ref[...]))
    info = info_ref[...]
    lane = lax.broadcasted_iota(jnp.int32, info.shape, 1).astype(F32)
    offrow = offrow_ref[...]
    vals = [_lane_get(lane, offrow, info[:, 2:3]) + info[:, 4:5],
            _lane_get(lane, offrow, info[:, 3:4]) + info[:, 5:6],
            info[:, 6:7], info[:, 7:8]]
    cols = []
    for v in vals:
        hi = jnp.floor(v * (1.0 / 256.0))
        cols += [hi, v - 256.0 * hi]
    rows = _dot_nt(sel_ref[...], _lane_put(lane, cols).astype(BF16))
    sub = lax.broadcasted_iota(jnp.int32, rows.shape, 0)
    posall = jnp.zeros(rows.shape, F32)
    for k in range(len(vals)):
        posall = jnp.where(sub == k, rows[2 * k:2 * k + 1, :] * 256.0 + rows[2 * k + 1:2 * k + 2, :], posall)
    posall = posall.astype(jnp.int32)
    pos_ref[...] = posall
    pos_v[...] = posall
    cp = pltpu.make_async_copy(pos_v, pos_s, sem.at[1])
    cp.start()
    cp.wait()

    def place(t, carry):
        row = xn_buf[pl.ds(pl.multiple_of(t * ROW_TILE, ROW_TILE), ROW_TILE), :]
        for k in range(MOE_TOP_K):
            blk[pl.ds(pl.multiple_of(pos_s[MOE_TOP_K + k, t] * ROW_TILE, ROW_TILE), ROW_TILE), :] = row
        return carry

    lax.fori_loop(0, tm, place, 0, unroll=8)
    for e in range(ne):
        n = runs_s[0, 2, e] * ROW_TILE

        @pl.when(n > 0)
        def _send(e=e, n=n):
            src = pl.multiple_of(runs_s[0, 0, e] * ROW_TILE, ROW_TILE)
            dst = pl.multiple_of(runs_s[0, 1, e] * ROW_TILE, ROW_TILE)
            pltpu.make_async_copy(blk.at[pl.ds(src, n)], xs_ref.at[pl.ds(dst, n)], sem.at[0]).start(priority=e % 2)
    pltpu.make_async_copy(blk, xs_ref.at[pl.ds(0, MOE_TOP_K * tm * ROW_TILE)], sem.at[0]).wait()

    @pl.when(i == pl.num_programs(0) - 1)
    def _zero_unused_rows():
        zbuf[...] = jnp.zeros_like(zbuf)

        def fill(row, nrows):
            at = pl.multiple_of(row * ROW_TILE, ROW_TILE)
            n = nrows * ROW_TILE
            c = pltpu.make_async_copy(zbuf.at[pl.ds(0, n)], xs_ref.at[pl.ds(at, n)], sem.at[1])
            c.start()
            c.wait()

        for e in range(ne):
            n_pad = (tile_rows - cnt_s[e] % tile_rows) % tile_rows

            @pl.when(n_pad > 0)
            def _fill(e=e, n_pad=n_pad):
                fill(off_s[e] + cnt_s[e], n_pad)

        def zero_tile(j, carry):
            fill(j * tile_rows, tile_rows)
            return carry

        lax.fori_loop(tot_s[0], n_tiles, zero_tile, 0)


def _expert_body(te_s, blk_s, tot_s, xs_ref, wg_ref, wu_ref, wd_ref, ys_ref):
    valid = pl.program_id(0) < tot_s[0]
    rows = xs_ref.shape[0] // ROW_TILE

    @pl.when(valid)
    def _run():
        x = _tiles_to_rows(xs_ref, rows).astype(BF16)
        hid = jax.nn.silu(_dot(x, wg_ref[0, 0].astype(BF16))) * _dot(x, wu_ref[0, 0].astype(BF16))
        _rows_to_tiles(ys_ref, _dot(hid.astype(BF16), wd_ref[0, 0].astype(BF16)))

    @pl.when(jnp.logical_not(valid))
    def _unused_tile():
        ys_ref[...] = jnp.zeros_like(ys_ref)


def _combine_body(h_ref, info_ref, pos_s, runs_s, ys_ref, fw_ref, out_ref, blk, y1, y2, sem, *, tm, ne, final):
    for e in range(ne):
        n = runs_s[0, 2, e] * ROW_TILE

        @pl.when(n > 0)
        def _fetch(e=e, n=n):
            dst = pl.multiple_of(runs_s[0, 0, e] * ROW_TILE, ROW_TILE)
            src = pl.multiple_of(runs_s[0, 1, e] * ROW_TILE, ROW_TILE)
            pltpu.make_async_copy(ys_ref.at[pl.ds(src, n)], blk.at[pl.ds(dst, n)], sem.at[0]).start(priority=e % 2)
    pltpu.make_async_copy(ys_ref.at[pl.ds(0, MOE_TOP_K * tm * ROW_TILE)], blk, sem.at[0]).wait()
    bufs = (y1, y2)

    def pick(t, carry):
        dst = pl.ds(pl.multiple_of(t * ROW_TILE, ROW_TILE), ROW_TILE)
        for k in range(MOE_TOP_K):
            at = pl.multiple_of(pos_s[MOE_TOP_K + k, t] * ROW_TILE, ROW_TILE)
            bufs[k][dst, :] = blk[pl.ds(at, ROW_TILE), :]
        return carry

    lax.fori_loop(0, tm, pick, 0, unroll=8)
    o = h_ref[...] + info_ref[:, 0:1] * _tiles_to_rows(y1, tm) + info_ref[:, 1:2] * _tiles_to_rows(y2, tm)
    if final:
        o = _rms(o, fw_ref[...])
    out_ref[...] = o


PAIRS_PER_GROUP = MOE_EPG * (MOE_EPG - 1) // 2
N_BUCKETS = MOE_GROUPS * PAIRS_PER_GROUP
TB_BUCKET = 256
PACK_ROWS = 4


def _bucket_tables():
    ea, ec = [], []
    for g in range(MOE_GROUPS):
        for a in range(MOE_EPG):
            for c in range(a + 1, MOE_EPG):
                ea.append(g * MOE_EPG + a)
                ec.append(g * MOE_EPG + c)
    return np.asarray(ea, np.int32), np.asarray(ec, np.int32)


def _pb_route_body(h_ref, nw_ref, wr_ref, br_ref, tri_ref, sel_ref, rows_ref, brow_ref, cnt_ref):
    tm = h_ref.shape[0]
    half = h_ref.shape[1] // 2

    @pl.when(pl.program_id(0) == 0)
    def _init():
        cnt_ref[...] = jnp.zeros_like(cnt_ref)

    xn = _rms(h_ref[...], nw_ref[...])
    hi = xn.astype(BF16)
    lo = (xn - hi.astype(F32)).astype(BF16)
    logits = _dot(jnp.concatenate([hi, lo, hi], axis=1), wr_ref[...]) + br_ref[...]
    lane = lax.broadcasted_iota(jnp.int32, logits.shape, 1).astype(F32)
    i1, i2, w1, w2 = _route(logits, lane)
    swap = i2 < i1
    ea = jnp.where(swap, i2, i1)
    ec = jnp.where(swap, i1, i2)
    wa = jnp.where(swap, w2, w1)
    wc = jnp.where(swap, w1, w2)
    grp = jnp.floor(ea * (1.0 / MOE_EPG))
    a8 = ea - MOE_EPG * grp
    c8 = ec - MOE_EPG * grp
    bucket = grp * PAIRS_PER_GROUP + a8 * (2 * MOE_EPG - 1 - a8) * 0.5 + (c8 - a8 - 1.0)
    ind = jnp.where(lane == bucket, 1.0, 0.0)
    before = _dot(tri_ref[...], ind.astype(BF16)) + cnt_ref[0:1, :]
    rank = _lane_get(lane, before, bucket)
    cnt_ref[0:1, :] = cnt_ref[0:1, :] + jnp.sum(ind, axis=0, keepdims=True)
    rank_hi = jnp.floor(rank * (1.0 / 128.0))
    pieces = _lane_put(lane, [bucket, rank_hi, rank - 128.0 * rank_hi]).astype(BF16)
    tr = _dot_nt(sel_ref[...], pieces)
    sub = lax.broadcasted_iota(jnp.int32, tr.shape, 0)
    brow_ref[...] = jnp.where(sub == 0, tr[0:1, :], jnp.where(sub == 1, tr[1:2, :] * 128.0 + tr[2:3, :], 0.0)
                              ).astype(jnp.int32)
    bits = pltpu.bitcast(hi.astype(F32), jnp.uint32)
    words = (bits[:, :half] >> 16) | bits[:, half:]
    for c in range(PACK_ROWS):
        rows_ref[pl.ds(c, tm, stride=ROW_TILE), :] = words[:, c * 128:(c + 1) * 128]
    rows_ref[pl.ds(PACK_ROWS, tm, stride=ROW_TILE), :] = pltpu.bitcast(_lane_put(lane, [wa, wc]), jnp.uint32)
    for c in range(PACK_ROWS + 1, ROW_TILE):
        rows_ref[pl.ds(c, tm, stride=ROW_TILE), :] = jnp.zeros((tm, 128), jnp.uint32)


def _pb_invmap_body(off_s, brow_s, pos_s, src_hbm, src_s, sem, *, tm, rows_total):
    i = pl.program_id(0)

    @pl.when(i == 0)
    def _init():
        def zero(r, carry):
            src_s[r] = 0
            return carry
        lax.fori_loop(0, rows_total, zero, 0)

    def place(t, carry):
        p = off_s[brow_s[0, t]] + brow_s[1, t]
        src_s[p] = i * tm + t
        pos_s[0, t] = p
        return carry

    lax.fori_loop(0, tm, place, 0, unroll=8)

    @pl.when(i == pl.num_programs(0) - 1)
    def _flush():
        cp = pltpu.make_async_copy(src_s, src_hbm, sem)
        cp.start()
        cp.wait()


def _pb_expert_body(ea_s, ec_s, nv_s, tot_s, src_s, nxt_s, rows_hbm, wga, wua, wda, wgc, wuc, wdc, ys_ref,
                    xbuf, sem, *, tb):
    i = pl.program_id(0)
    tot = tot_s[0]

    def gather(idx_ref, count, slot):
        def issue(r2, carry):
            for par in range(2):
                r = 2 * r2 + par

                @pl.when(r < count)
                def _go(r=r, par=par):
                    at = pl.multiple_of(idx_ref[0, 0, r] * ROW_TILE, ROW_TILE)
                    to = pl.multiple_of(r * ROW_TILE, ROW_TILE)
                    pltpu.make_async_copy(rows_hbm.at[pl.ds(at, ROW_TILE)], xbuf.at[slot, pl.ds(to, ROW_TILE)],
                                          sem.at[slot]).start(priority=par)
            return carry

        lax.fori_loop(0, (count + 1) // 2, issue, 0)

    @pl.when(i == 0)
    def _first():
        xbuf[...] = jnp.zeros_like(xbuf)
        gather(src_s, nv_s[0], 0)

    def step(slot):
        @pl.when(i + 1 < tot)
        def _prefetch():
            gather(nxt_s, nv_s[i + 1], 1 - slot)

        n = nv_s[i] * ROW_TILE
        pltpu.make_async_copy(rows_hbm.at[pl.ds(0, n)], xbuf.at[slot, pl.ds(0, n)], sem.at[slot]).wait()
        w = jnp.concatenate([xbuf[slot, pl.ds(c, tb, stride=ROW_TILE), :] for c in range(PACK_ROWS)], axis=1)
        x = jnp.concatenate([pltpu.bitcast(w << 16, F32), pltpu.bitcast(w & jnp.uint32(0xFFFF0000), F32)],
                            axis=1).astype(BF16)
        wts = pltpu.bitcast(xbuf[slot, pl.ds(PACK_ROWS, tb, stride=ROW_TILE), :], F32)

        def expert(wg, wu, wd):
            hid = jax.nn.silu(_dot(x, wg[0, 0].astype(BF16))) * _dot(x, wu[0, 0].astype(BF16))
            return _dot(hid.astype(BF16), wd[0, 0].astype(BF16))

        _rows_to_tiles(ys_ref, wts[:, 0:1] * expert(wga, wua, wda) + wts[:, 1:2] * expert(wgc, wuc, wdc))

    for slot in range(2):
        @pl.when((i < tot) & (i % 2 == slot))
        def _run(slot=slot):
            step(slot)

    @pl.when(i >= tot)
    def _unused_tile():
        ys_ref[...] = jnp.zeros_like(ys_ref)


def _pb_combine_body(h_ref, pos_s, ys_hbm, fw_ref, out_ref, buf, sem, *, tm, final):
    def issue(t2, carry):
        for par in range(2):
            t = 2 * t2 + par
            at = pl.multiple_of(pos_s[0, t] * ROW_TILE, ROW_TILE)
            to = pl.multiple_of(t * ROW_TILE, ROW_TILE)
            pltpu.make_async_copy(ys_hbm.at[pl.ds(at, ROW_TILE)], buf.at[pl.ds(to, ROW_TILE)], sem.at[par]
                                  ).start(priority=par)
        return carry

    lax.fori_loop(0, tm // 2, issue, 0, unroll=8)
    for par in range(2):
        n = tm // 2 * ROW_TILE
        pltpu.make_async_copy(ys_hbm.at[pl.ds(0, n)], buf.at[pl.ds(0, n)], sem.at[par]).wait()
    o = h_ref[...] + _tiles_to_rows(buf, tm)
    if final:
        o = _rms(o, fw_ref[...])
    out_ref[...] = o


def _moe_pb(h, nw, w_group, b_group, w_router, b_router, w_gate, w_up, w_down, layer, final_w):
    t, d = h.shape
    assert d == ROW_TILE * 128 and MOE_TOP_K == 2 and N_BUCKETS <= ROUTE_LANES
    _, ne, _, ff = w_gate.shape
    tm = TM_MOE
    tb = TB_BUCKET
    nw2 = nw.reshape(1, d)
    wr3, bias = _router_weights(w_group, b_group, w_router, b_router)
    tri = jnp.asarray(np.tril(np.ones((tm, tm), np.float32), -1), BF16)
    sel = jnp.asarray(np.eye(8, ROUTE_LANES, dtype=np.float32), BF16)
    rows, brow, cnt = pl.pallas_call(
        _pb_route_body, grid=(t // tm,),
        in_specs=[pl.BlockSpec((tm, d), lambda i: (i, 0)),
                  pl.BlockSpec((1, d), lambda i: (0, 0)),
                  pl.BlockSpec((3 * d, ROUTE_LANES), lambda i: (0, 0)),
                  pl.BlockSpec((1, ROUTE_LANES), lambda i: (0, 0)),
                  pl.BlockSpec((tm, tm), lambda i: (0, 0)),
                  pl.BlockSpec((8, ROUTE_LANES), lambda i: (0, 0))],
        out_specs=[pl.BlockSpec((tm * ROW_TILE, 128), lambda i: (i, 0)),
                   pl.BlockSpec((8, tm), lambda i: (0, i)),
                   pl.BlockSpec((8, ROUTE_LANES), lambda i: (0, 0))],
        out_shape=[jax.ShapeDtypeStruct((t * ROW_TILE, 128), jnp.uint32),
                   jax.ShapeDtypeStruct((8, t), jnp.int32),
                   jax.ShapeDtypeStruct((8, ROUTE_LANES), F32)],
        compiler_params=_cparams(("arbitrary",)), name="moe_route",
    )(h, nw2, wr3, bias, tri, sel)

    cnt_i = cnt[0, :N_BUCKETS].astype(jnp.int32)
    ntile = (cnt_i + tb - 1) // tb
    tile_end = jnp.cumsum(ntile)
    tile_start = tile_end - ntile
    off_i = tile_start * tb
    n_tiles = t // tb + N_BUCKETS
    rows_total = n_tiles * tb
    tot = tile_end[-1:]
    ti = jnp.arange(n_tiles, dtype=jnp.int32)
    tile_b = jnp.minimum(jnp.sum((ti[:, None] >= tile_end[None, :]).astype(jnp.int32), axis=1), N_BUCKETS - 1)
    nvalid = jnp.clip(cnt_i[tile_b] - (ti - tile_start[tile_b]) * tb, 0, tb)
    tab_a, tab_c = _bucket_tables()
    tile_ea = jnp.asarray(tab_a)[tile_b]
    tile_ec = jnp.asarray(tab_c)[tile_b]

    pos, src = pl.pallas_call(
        functools.partial(_pb_invmap_body, tm=tm, rows_total=rows_total),
        grid_spec=pltpu.PrefetchScalarGridSpec(
            num_scalar_prefetch=1, grid=(t // tm,),
            in_specs=[pl.BlockSpec((8, tm), lambda i, o: (0, i), memory_space=pltpu.SMEM)],
            out_specs=[pl.BlockSpec((1, tm), lambda i, o: (0, i), memory_space=pltpu.SMEM),
                       pl.BlockSpec(memory_space=pl.ANY)],
            scratch_shapes=[pltpu.SMEM((rows_total,), jnp.int32), pltpu.SemaphoreType.DMA(())]),
        out_shape=[jax.ShapeDtypeStruct((1, t), jnp.int32), jax.ShapeDtypeStruct((rows_total,), jnp.int32)],
        compiler_params=_cparams(("arbitrary",)), name="moe_invmap",
    )(off_i, brow)

    src3 = src.reshape(n_tiles, 1, tb)
    wspec_a = lambda shape: pl.BlockSpec((1, 1) + shape, lambda i, ea, ec, nv, tt: (layer, ea[i], 0, 0))
    wspec_c = lambda shape: pl.BlockSpec((1, 1) + shape, lambda i, ea, ec, nv, tt: (layer, ec[i], 0, 0))
    ys = pl.pallas_call(
        functools.partial(_pb_expert_body, tb=tb),
        grid_spec=pltpu.PrefetchScalarGridSpec(
            num_scalar_prefetch=4, grid=(n_tiles,),
            in_specs=[pl.BlockSpec((1, 1, tb), lambda i, *_: (i, 0, 0), memory_space=pltpu.SMEM),
                      pl.BlockSpec((1, 1, tb), lambda i, *_: (jnp.minimum(i + 1, n_tiles - 1), 0, 0),
                                   memory_space=pltpu.SMEM),
                      pl.BlockSpec(memory_space=pl.ANY),
                      wspec_a((d, ff)), wspec_a((d, ff)), wspec_a((ff, d)),
                      wspec_c((d, ff)), wspec_c((d, ff)), wspec_c((ff, d))],
            out_specs=pl.BlockSpec((tb * ROW_TILE, 128), lambda i, *_: (i, 0)),
            scratch_shapes=[pltpu.VMEM((2, tb * ROW_TILE, 128), jnp.uint32), pltpu.SemaphoreType.DMA((2,))]),
        out_shape=jax.ShapeDtypeStruct((rows_total * ROW_TILE, 128), F32),
        compiler_params=_cparams(("arbitrary",)), name="moe_expert",
    )(tile_ea, tile_ec, nvalid, tot, src3, src3, rows, w_gate, w_up, w_down, w_gate, w_up, w_down)

    final = final_w is not None
    fw = (final_w if final else nw).reshape(1, d)
    return pl.pallas_call(
        functools.partial(_pb_combine_body, tm=tm, final=final), grid=(t // tm,),
        in_specs=[pl.BlockSpec((tm, d), lambda i: (i, 0)),
                  pl.BlockSpec((1, tm), lambda i: (0, i), memory_space=pltpu.SMEM),
                  pl.BlockSpec(memory_space=pl.ANY),
                  pl.BlockSpec((1, d), lambda i: (0, 0))],
        out_specs=pl.BlockSpec((tm, d), lambda i: (i, 0)),
        out_shape=jax.ShapeDtypeStruct((t, d), F32),
        scratch_shapes=[pltpu.VMEM((tm * ROW_TILE, 128), F32), pltpu.SemaphoreType.DMA((2,))],
        compiler_params=_cparams(("arbitrary",)), name="moe_combine",
    )(h, pos, ys, fw)


def _router_weights(w_group, b_group, w_router, b_router):
    d, ne = w_router.shape
    ng = w_group.shape[1]
    w = jnp.zeros((d, ROUTE_LANES), F32).at[:, :ne].set(w_router).at[:, ne:ne + ng].set(w_group)
    hi = w.astype(BF16)
    lo = (w - hi.astype(F32)).astype(BF16)
    bias = jnp.zeros((1, ROUTE_LANES), F32).at[0, :ne].set(b_router).at[0, ne:ne + ng].set(b_group)
    return jnp.concatenate([hi, hi, lo], axis=0), bias


def _moe(h, nw, w_group, b_group, w_router, b_router, w_gate, w_up, w_down, layer, final_w):
    t, d = h.shape
    assert d == ROW_TILE * 128, "row-as-tile layout needs d_model == 1024"
    _, ne, _, ff = w_gate.shape
    tm = TM_MOE
    te_rows = TM_EXPERT
    nw2 = nw.reshape(1, d)
    wr3, bias = _router_weights(w_group, b_group, w_router, b_router)
    tri = jnp.asarray(np.tril(np.ones((tm, tm), np.float32), -1), BF16)
    nt = t // tm
    info, cnt, tile_info = pl.pallas_call(
        _route_body, grid=(nt,),
        in_specs=[pl.BlockSpec((tm, d), lambda i: (i, 0)),
                  pl.BlockSpec((1, d), lambda i: (0, 0)),
                  pl.BlockSpec((3 * d, ROUTE_LANES), lambda i: (0, 0)),
                  pl.BlockSpec((1, ROUTE_LANES), lambda i: (0, 0)),
                  pl.BlockSpec((tm, tm), lambda i: (0, 0))],
        out_specs=[pl.BlockSpec((tm, ROUTE_LANES), lambda i: (i, 0)), pl.BlockSpec((8, ROUTE_LANES), lambda i: (0, 0)),
                   pl.BlockSpec((8, ROUTE_LANES), lambda i: (i, 0))],
        out_shape=[jax.ShapeDtypeStruct((t, ROUTE_LANES), F32), jax.ShapeDtypeStruct((8, ROUTE_LANES), F32),
                   jax.ShapeDtypeStruct((nt * 8, ROUTE_LANES), F32)],
        compiler_params=_cparams(("arbitrary",)), name="moe_route",
    )(h, nw2, wr3, bias, tri)

    cnt_i = cnt[0, :ne].astype(jnp.int32)
    ntile = (cnt_i + te_rows - 1) // te_rows
    tile_end = jnp.cumsum(ntile)
    off_i = (tile_end - ntile) * te_rows
    n_tiles = (MOE_TOP_K * t) // te_rows + ne
    rows_total = n_tiles * te_rows
    ti = jnp.arange(n_tiles, dtype=jnp.int32)
    tot = tile_end[-1:]
    ti_c = jnp.minimum(ti, tot[0] - 1)
    tile_e = jnp.sum((ti_c[:, None] >= tile_end[None, :]).astype(jnp.int32), axis=1)
    off_row = jnp.zeros((1, ROUTE_LANES), F32).at[0, :ne].set(off_i.astype(F32))
    sel = jnp.asarray(np.eye(8, ROUTE_LANES, dtype=np.float32), BF16)
    tinfo = tile_info.reshape(nt, 8, ROUTE_LANES).astype(jnp.int32)
    tcnt = tinfo[:, 0, :]
    gstart = off_row.astype(jnp.int32) + jnp.cumsum(tcnt, axis=0) - tcnt
    runs = jnp.zeros((nt, 8, ROUTE_LANES), jnp.int32).at[:, 0].set(tinfo[:, 1, :]).at[:, 1].set(gstart).at[:, 2].set(tcnt)

    xs, pos = pl.pallas_call(
        functools.partial(_dispatch_body, tm=tm, tile_rows=te_rows, ne=ne, n_tiles=n_tiles),
        grid_spec=pltpu.PrefetchScalarGridSpec(
            num_scalar_prefetch=3, grid=(nt,),
            in_specs=[pl.BlockSpec((tm, d), lambda i, *_: (i, 0)),
                      pl.BlockSpec((1, d), lambda i, *_: (0, 0)),
                      pl.BlockSpec((tm, ROUTE_LANES), lambda i, *_: (i, 0)),
                      pl.BlockSpec((1, ROUTE_LANES), lambda i, *_: (0, 0)),
                      pl.BlockSpec((8, ROUTE_LANES), lambda i, *_: (0, 0)),
                      pl.BlockSpec((1, 8, ROUTE_LANES), lambda i, *_: (i, 0, 0), memory_space=pltpu.SMEM)],
            out_specs=[pl.BlockSpec(memory_space=pl.ANY), pl.BlockSpec((8, tm), lambda i, *_: (0, i))],
            scratch_shapes=[pltpu.VMEM((tm * ROW_TILE, 128), F32), pltpu.VMEM((MOE_TOP_K * tm * ROW_TILE, 128), F32),
                            pltpu.VMEM((8, tm), jnp.int32), pltpu.SMEM((8, tm), jnp.int32),
                            pltpu.VMEM((te_rows * ROW_TILE, 128), F32), pltpu.SemaphoreType.DMA((2,))]),
        out_shape=[jax.ShapeDtypeStruct((rows_total * ROW_TILE, 128), F32), jax.ShapeDtypeStruct((8, t), jnp.int32)],
        compiler_params=_cparams(("arbitrary",)), name="moe_dispatch",
    )(cnt_i, off_i, tot, h, nw2, info, off_row, sel, runs)

    ys = pl.pallas_call(
        _expert_body,
        grid_spec=pltpu.PrefetchScalarGridSpec(
            num_scalar_prefetch=3, grid=(n_tiles,),
            in_specs=[pl.BlockSpec((te_rows * ROW_TILE, 128), lambda i, e, b, v: (b[i], 0)),
                      pl.BlockSpec((1, 1, d, ff), lambda i, e, b, v: (layer, e[i], 0, 0)),
                      pl.BlockSpec((1, 1, d, ff), lambda i, e, b, v: (layer, e[i], 0, 0)),
                      pl.BlockSpec((1, 1, ff, d), lambda i, e, b, v: (layer, e[i], 0, 0))],
            out_specs=pl.BlockSpec((te_rows * ROW_TILE, 128), lambda i, e, b, v: (i, 0))),
        out_shape=jax.ShapeDtypeStruct((rows_total * ROW_TILE, 128), F32),
        compiler_params=_cparams(("arbitrary",)), name="moe_expert",
    )(tile_e, ti_c, tot, xs, w_gate, w_up, w_down)

    final = final_w is not None
    fw = (final_w if final else nw).reshape(1, d)
    return pl.pallas_call(
        functools.partial(_combine_body, tm=tm, ne=ne, final=final), grid=(nt,),
        in_specs=[pl.BlockSpec((tm, d), lambda i: (i, 0)),
                  pl.BlockSpec((tm, ROUTE_LANES), lambda i: (i, 0)),
                  pl.BlockSpec((8, tm), lambda i: (0, i), memory_space=pltpu.SMEM),
                  pl.BlockSpec((1, 8, ROUTE_LANES), lambda i: (i, 0, 0), memory_space=pltpu.SMEM),
                  pl.BlockSpec(memory_space=pl.ANY),
                  pl.BlockSpec((1, d), lambda i: (0, 0))],
        out_specs=pl.BlockSpec((tm, d), lambda i: (i, 0)),
        out_shape=jax.ShapeDtypeStruct((t, d), F32),
        scratch_shapes=[pltpu.VMEM((MOE_TOP_K * tm * ROW_TILE, 128), F32), pltpu.VMEM((tm * ROW_TILE, 128), F32),
                        pltpu.VMEM((tm * ROW_TILE, 128), F32), pltpu.SemaphoreType.DMA((1,))],
        compiler_params=_cparams(("arbitrary",)), name="moe_combine",
    )(h, info, pos, runs, ys, fw)


def _s5_weights(lam_re, lam_im, b_re, b_im, c_re, c_im, log_dt, nsteps):
    hp = lax.Precision.HIGHEST
    L = S5_CHUNK
    g, p = lam_re.shape
    ch = b_re.shape[-1]
    lam = lax.complex(lam_re.astype(F32), lam_im.astype(F32))
    dt = jnp.exp(log_dt.astype(F32))[:, None]
    lam_bar = jnp.exp(lam * dt)
    b_bar = ((lam_bar - 1.0) / lam)[..., None] * lax.complex(b_re.astype(F32), b_im.astype(F32))
    cmat = lax.complex(c_re.astype(F32), c_im.astype(F32))
    pows = [jnp.ones_like(lam_bar)]
    for _ in range(L):
        pows.append(pows[-1] * lam_bar)
    pw = jnp.stack(pows, axis=1)
    kern = jnp.real(jnp.einsum('gop,gtp,gpi->gtoi', cmat, pw[:, :L], b_bar, precision=hp))
    lag = np.arange(L)[None, :] - np.arange(L)[:, None]
    toep = jnp.where((lag >= 0)[None, :, :, None, None], kern[:, np.maximum(lag, 0)], 0.0)
    toep = toep.transpose(0, 1, 4, 2, 3).reshape(g, L * ch, L * ch)
    wst = pw[:, L - 1 - np.arange(L)][:, :, :, None] * b_bar[:, None, :, :]
    wst = wst.transpose(0, 1, 3, 2).reshape(g, L * ch, p)
    wst = jnp.concatenate([jnp.real(wst), jnp.imag(wst)], axis=-1)
    mo = cmat.transpose(0, 2, 1)[:, :, None, :] * pw[:, 1:L + 1].transpose(0, 2, 1)[:, :, :, None]
    mo = mo.reshape(g, p, L * ch)
    wout = jnp.concatenate([jnp.real(mo), -jnp.imag(mo)], axis=1)
    a = pw[:, L]
    ars, ais = [], []
    for _ in range(nsteps):
        ars.append(jnp.concatenate([jnp.real(a), jnp.real(a)], axis=-1))
        ais.append(jnp.concatenate([-jnp.imag(a), jnp.imag(a)], axis=-1))
        a = a * a
    wcat = jnp.concatenate([toep, wst], axis=-1).astype(BF16)
    return wcat, wout.astype(BF16), jnp.stack(ars, axis=1), jnp.stack(ais, axis=1)


def _s5_body(u_ref, wcat_ref, wout_ref, ar_ref, ai_ref, y_ref, us_ref, ys_ref, *, nsteps):
    L = S5_CHUNK
    ch = S5_GROUP_CH
    gpc = 128 // ch
    ny = L * ch
    nc = u_ref.shape[0] // L
    for s in range(L):
        us_ref[s] = u_ref[pl.ds(s, nc, stride=L), :]
    lane = lax.broadcasted_iota(jnp.int32, (nc, 128), 1)
    ridx = lax.broadcasted_iota(jnp.int32, (nc, 128), 0)

    def shift(x, k):
        return jnp.where(ridx >= k, pltpu.roll(x, k, axis=0), 0.0)

    for gi in range(gpc):
        halves = []
        for hh in range(ny // 128):
            acc = None
            for s8 in range(gpc):
                rot = ((s8 - gi) * ch) % 128
                src = us_ref[hh * gpc + s8]
                if rot:
                    src = pltpu.roll(src, rot, axis=1)
                slot = (lane >= s8 * ch) & (lane < (s8 + 1) * ch)
                acc = jnp.where(slot, src, 0.0) if acc is None else jnp.where(slot, src, acc)
            halves.append(acc)
        ug = jnp.concatenate(halves, axis=1).astype(BF16)
        r = _dot(ug, wcat_ref[gi])
        y1 = r[:, :ny]
        z = r[:, ny:]
        w = shift(z, 1)
        for k in range(nsteps):
            if (1 << k) >= nc:
                break
            sk = shift(w, 1 << k)
            w = w + sk * ar_ref[gi, k:k + 1, :] + pltpu.roll(sk, z.shape[1] // 2, axis=1) * ai_ref[gi, k:k + 1, :]
        yg = y1 + _dot(w.astype(BF16), wout_ref[gi])
        slot = (lane >= gi * ch) & (lane < (gi + 1) * ch)
        for t in range(L):
            src = yg[:, (t // gpc) * 128:(t // gpc + 1) * 128]
            rot = ((gi - t % gpc) * ch) % 128
            if rot:
                src = pltpu.roll(src, rot, axis=1)
            ys_ref[t] = jnp.where(slot, src, 0.0) if gi == 0 else jnp.where(slot, src, ys_ref[t])
    for t in range(L):
        y_ref[pl.ds(t, nc, stride=L), :] = ys_ref[t]


def _s5(main, col0, width, nb, seq, lam_re, lam_im, b_re, b_im, c_re, c_im, log_dt):
    t = main.shape[0]
    L = S5_CHUNK
    ch = S5_GROUP_CH
    g = width // ch
    nc = seq // L
    gpc = 128 // ch
    ncol = width // 128
    assert col0 % 128 == 0 and width % 128 == 0 and (L * ch) % 128 == 0
    nsteps = max(1, (nc - 1).bit_length())
    wcat, wout, ar, ai = _s5_weights(lam_re, lam_im, b_re, b_im, c_re, c_im, log_dt, nsteps)
    body = functools.partial(_s5_body, nsteps=nsteps)
    return pl.pallas_call(
        body, grid=(ncol, nb),
        in_specs=[pl.BlockSpec((seq, 128), lambda j, b: (b, col0 // 128 + j)),
                  pl.BlockSpec((gpc,) + wcat.shape[1:], lambda j, b: (j, 0, 0)),
                  pl.BlockSpec((gpc,) + wout.shape[1:], lambda j, b: (j, 0, 0)),
                  pl.BlockSpec((gpc,) + ar.shape[1:], lambda j, b: (j, 0, 0)),
                  pl.BlockSpec((gpc,) + ai.shape[1:], lambda j, b: (j, 0, 0))],
        out_specs=pl.BlockSpec((seq, 128), lambda j, b: (b, j)),
        out_shape=jax.ShapeDtypeStruct((t, width), F32),
        scratch_shapes=[pltpu.VMEM((L, nc, 128), F32), pltpu.VMEM((L, nc, 128), F32)],
        compiler_params=_cparams(("parallel", "parallel")), name="s5",
    )(main, wcat, wout, ar, ai)


def _hgrn_gmat():
    L = HGRN_CHUNK
    blocks = 2 + int(np.log2(L))
    gm = np.zeros((blocks * L, L), np.float32)
    for j in range(L):
        gm[j, :j + 1] = 1.0
        gm[L + j, j + 1:] = 1.0
    li, m = 2, L
    while m >= 2:
        half = m // 2
        for j in range(L):
            pos = j % m
            r = j - pos + half - 1
            if pos >= half:
                gm[li * L + j, r + 1:j + 1] = 1.0
            else:
                gm[li * L + j, j + 1:r + 1] = 1.0
        li += 1
        m //= 2
    return gm


def _hgrn_body(main_ref, gm_ref, lb_ref, nw_ref, out_ref, st_ref, *, nb, nh, dh):
    L = HGRN_CHUNK
    w = nh * dh

    @pl.when(pl.program_id(0) == 0)
    def _init():
        st_ref[...] = jnp.zeros_like(st_ref)

    row = lax.broadcasted_iota(jnp.int32, (L, 2 * L), 0)
    col = lax.broadcasted_iota(jnp.int32, (L, 2 * L), 1) & (L - 1)
    rowd = lax.broadcasted_iota(jnp.int32, (L, 2 * dh), 0)
    laned = lax.broadcasted_iota(jnp.int32, (L, 2 * dh), 1)
    first = laned < dh
    eye = row == col

    def blockdiag(x):
        z = jnp.zeros_like(x)
        return jnp.concatenate([jnp.where(first, x, z), jnp.where(first, z, x)], axis=0)

    gm2 = gm_ref[...]
    lb = lb_ref[...]
    zst = jnp.zeros((dh, dh), BF16)
    for b in range(nb):
        fg = main_ref[b, :, w:2 * w]
        f = lb + (1.0 - lb) * jax.nn.sigmoid(fg)
        kk = (1.0 - lb) * jax.nn.sigmoid(-fg)
        lf = jnp.log(f)
        hi = lf.astype(BF16)
        mid = (lf - hi.astype(F32)).astype(BF16)
        p_all = jnp.exp(_dot(gm2, jnp.concatenate([hi, mid], axis=0)))
        for hp in range(nh // 2):
            i0 = b * nh + 2 * hp
            cs = slice(2 * hp * dh, (2 * hp + 2) * dh)
            q = main_ref[b, :, 2 * hp * dh:(2 * hp + 2) * dh]
            v = main_ref[b, :, 2 * w + 2 * hp * dh:2 * w + (2 * hp + 2) * dh]
            og = main_ref[b, :, 3 * w + 2 * hp * dh:3 * w + (2 * hp + 2) * dh]
            k = kk[:, cs]
            pb = p_all[0:L, cs]
            pe = p_all[L:2 * L, cs]
            st0 = st_ref[i0]
            st1 = st_ref[i0 + 1]
            stbd = jnp.concatenate([jnp.concatenate([st0.astype(BF16), zst], axis=1),
                                    jnp.concatenate([zst, st1.astype(BF16)], axis=1)], axis=0)
            o = _dot_nt((q * pb).astype(BF16), stbd)
            attn = jnp.where(eye, _dot_nt(q.astype(BF16), blockdiag(k.astype(BF16))), 0.0)
            li, m = 2, L
            while m >= 2:
                pl_ = p_all[li * L:(li + 1) * L, cs]
                up = (rowd & (m - 1)) >= (m // 2)
                ql = jnp.where(up, q * pl_, 0.0).astype(BF16)
                kl = jnp.where(up, 0.0, k * pl_).astype(BF16)
                same = (row & ~(m - 1)) == (col & ~(m - 1))
                attn = attn + jnp.where(same, _dot_nt(ql, blockdiag(kl)), 0.0)
                li += 1
                m //= 2
            vb = v.astype(BF16)
            o = o + _dot(attn.astype(BF16), blockdiag(vb))
            kh = (k * pe).astype(BF16)
            for j in range(2):
                hs = slice(j * dh, (j + 1) * dh)
                gs = slice((2 * hp + j) * dh, (2 * hp + j + 1) * dh)
                st = st0 if j == 0 else st1
                st_ref[i0 + j] = st * pb[L - 1:L, hs] + _dot_tn(vb[:, hs], kh[:, hs])
                oj = o[:, hs]
                on = oj * lax.rsqrt(jnp.mean(oj * oj, axis=-1, keepdims=True) + RMS_EPS)
                out_ref[b, :, gs] = on * nw_ref[:, gs] * jax.nn.silu(og[:, hs])


def _hgrn(main, lower_bound, out_norm, nb, seq):
    t, n = main.shape
    nh = HGRN_HEADS
    w = out_norm.shape[0]
    dh = w // nh
    L = HGRN_CHUNK
    nc = seq // L
    gm = _hgrn_gmat()
    gm = jnp.asarray(np.concatenate([gm, gm], axis=1), BF16)
    body = functools.partial(_hgrn_body, nb=nb, nh=nh, dh=dh)
    out = pl.pallas_call(
        body, grid=(nc,),
        in_specs=[pl.BlockSpec((nb, L, 4 * w), lambda c: (0, c, 0)),
                  pl.BlockSpec(gm.shape, lambda c: (0, 0)),
                  pl.BlockSpec((1, w), lambda c: (0, 0)),
                  pl.BlockSpec((1, w), lambda c: (0, 0))],
        out_specs=pl.BlockSpec((nb, L, w), lambda c: (0, c, 0)),
        out_shape=jax.ShapeDtypeStruct((nb, seq, w), F32),
        scratch_shapes=[pltpu.VMEM((nb * nh, dh, dh), F32)],
        compiler_params=_cparams(("arbitrary",)), name="hgrn",
    )(main.reshape(nb, seq, n), gm, lower_bound.reshape(1, w), out_norm.reshape(1, w))
    return out.reshape(t, w)


def _out_c_body(h_ref, ys_ref, u_ref, oh_ref, d_ref, wglu_ref, bglu_ref, w_ref, out_ref):
    ws = ys_ref.shape[1]
    z = jax.nn.gelu(ys_ref[...] + d_ref[...] * u_ref[...])
    gate = jax.nn.sigmoid(_dot(z.astype(BF16), wglu_ref[...]) + bglu_ref[...])
    out_ref[...] = (h_ref[...] + _dot((z * gate).astype(BF16), w_ref[:ws, :])
                    + _dot(oh_ref[...].astype(BF16), w_ref[ws:, :]))


def _out_c(h, ys, main, oh, d_skip, w_glu, b_glu, w_out):
    t, d = h.shape
    ws = ys.shape[1]
    wh = oh.shape[1]
    tm = TM_PROJ
    ub = (main.shape[1] - ws) // ws
    return pl.pallas_call(
        _out_c_body, grid=(t // tm,),
        in_specs=[pl.BlockSpec((tm, d), lambda i: (i, 0)),
                  pl.BlockSpec((tm, ws), lambda i: (i, 0)),
                  pl.BlockSpec((tm, ws), lambda i: (i, ub)),
                  pl.BlockSpec((tm, wh), lambda i: (i, 0)),
                  pl.BlockSpec((1, ws), lambda i: (0, 0)),
                  pl.BlockSpec(w_glu.shape, lambda i: (0, 0)),
                  pl.BlockSpec((1, ws), lambda i: (0, 0)),
                  pl.BlockSpec(w_out.shape, lambda i: (0, 0))],
        out_specs=pl.BlockSpec((tm, d), lambda i: (i, 0)),
        out_shape=jax.ShapeDtypeStruct((t, d), F32),
        compiler_params=_cparams(("parallel",)), name="out_c",
    )(h, ys, main, oh, d_skip.reshape(1, ws), w_glu, b_glu.reshape(1, ws), w_out)


def kernel(x, norm_mix, norm_ffn, norm_final, ab_w_in, ab_gate_bias, ab_head_norm, ab_conv_w, ab_w_out, cd_w_in, s5_lambda_re, s5_lambda_im, s5_b_re, s5_b_im, s5_c_re, s5_c_im, s5_d, s5_log_dt, s5_w_glu, s5_b_glu, hgrn_lb, hgrn_out_norm, cd_w_out, moe_w_group, moe_b_group, moe_w_router, moe_b_router, moe_w_gate, moe_w_up, moe_w_down):
    nb, seq, d = x.shape
    depth = norm_mix.shape[0]
    h = x.reshape(nb * seq, d)
    for layer in range(depth):
        j = layer // 2
        if layer % 2 == 0:
            wm = ab_head_norm.shape[1]
            ng = ab_gate_bias.shape[1]
            w_in = ab_w_in[j]
            w_main = jnp.concatenate([w_in[:, :4 * wm], w_in[:, 4 * wm + ng:]], axis=1).astype(BF16)
            w_gates = w_in[:, 4 * wm:4 * wm + ng].astype(BF16)
            main, g, gt = _proj(h, norm_mix[layer], w_main, w_gates)
            hm = _mlstm(main, g, gt, ab_gate_bias[j], ab_head_norm[j], nb, seq)
            h = _out_a(h, hm, main, ab_conv_w[j], ab_w_out[j].astype(BF16), seq)
        else:
            ws = s5_d.shape[1]
            w_in = cd_w_in[j]
            w_main = jnp.concatenate([w_in[:, ws:], w_in[:, :ws]], axis=1).astype(BF16)
            main = _proj(h, norm_mix[layer], w_main)
            sm = jax.nn.softmax(hgrn_lb.astype(F32), axis=0)
            lower_bound = jnp.cumsum(sm, axis=0)[layer] - sm[0]
            ys = _s5(main, main.shape[1] - ws, ws, nb, seq, s5_lambda_re[j], s5_lambda_im[j], s5_b_re[j], s5_b_im[j],
                     s5_c_re[j], s5_c_im[j], s5_log_dt[j])
            oh = _hgrn(main, lower_bound, hgrn_out_norm[j], nb, seq)
            h = _out_c(h, ys, main, oh, s5_d[j], s5_w_glu[j].astype(BF16), s5_b_glu[j], cd_w_out[j].astype(BF16))
        h = _moe(h, norm_ffn[layer], moe_w_group[layer], moe_b_group[layer], moe_w_router[layer], moe_b_router[layer],
                 moe_w_gate, moe_w_up, moe_w_down, layer,
                 norm_final if layer == depth - 1 else None)
    return h.reshape(nb, seq, d)
```

```python
import functools

import numpy as np
import jax
import jax.numpy as jnp
from jax import lax
from jax.experimental import pallas as pl
from jax.experimental.pallas import tpu as pltpu

F32 = jnp.float32
BF16 = jnp.bfloat16
RMS_EPS = 1e-6
MLSTM_CHUNK = 512
HGRN_CHUNK = 128
S5_CHUNK = 16
S5_GROUP_CH = 16
S5_STATE = 64
MLSTM_HEADS = 4
HGRN_HEADS = 4
MOE_GROUPS = 4
MOE_EPG = 8
ROUTE_LANES = 128
TM_PROJ = 512
TM_MOE = 512
TM_EXPERT = 512
MOE_TOP_K = 2
VMEM_LIMIT = 56 * 1024 * 1024

_NT = (((1,), (1,)), ((), ()))
_TN = (((0,), (0,)), ((), ()))


def _cparams(sem):
    return pltpu.CompilerParams(dimension_semantics=sem, vmem_limit_bytes=VMEM_LIMIT)


def _rms(x, w):
    return x * lax.rsqrt(jnp.mean(x * x, axis=-1, keepdims=True) + RMS_EPS) * w


def _split3(x):
    hi = x.astype(BF16)
    r = x - hi.astype(F32)
    mid = r.astype(BF16)
    lo = (r - mid.astype(F32)).astype(BF16)
    return hi, mid, lo


def _dot(a, b):
    return jnp.dot(a, b, preferred_element_type=F32)


def _dot_nt(a, b):
    return lax.dot_general(a, b, _NT, preferred_element_type=F32)


def _dot_tn(a, b):
    return lax.dot_general(a, b, _TN, preferred_element_type=F32)


def _proj_gates_body(x_ref, nw_ref, w_ref, wg_ref, wgt_ref, main_ref, g_ref, gt_ref):
    xn = _rms(x_ref[...], nw_ref[...]).astype(BF16)
    main_ref[...] = _dot(xn, w_ref[...])
    g_ref[...] = _dot(xn, wg_ref[...])[:, : g_ref.shape[1]]
    gt_ref[...] = _dot_nt(wgt_ref[...], xn)


def _proj_body(x_ref, nw_ref, w_ref, main_ref):
    xn = _rms(x_ref[...], nw_ref[...]).astype(BF16)
    main_ref[...] = _dot(xn, w_ref[...])


def _proj(h, nw, w_main, w_gates=None):
    t, d = h.shape
    n = w_main.shape[1]
    tm = TM_PROJ
    x_spec = pl.BlockSpec((tm, d), lambda i: (i, 0))
    nw_spec = pl.BlockSpec((1, d), lambda i: (0, 0))
    w_spec = pl.BlockSpec((d, n), lambda i: (0, 0))
    main_spec = pl.BlockSpec((tm, n), lambda i: (i, 0))
    main_shape = jax.ShapeDtypeStruct((t, n), F32)
    if w_gates is None:
        return pl.pallas_call(
            _proj_body, grid=(t // tm,), in_specs=[x_spec, nw_spec, w_spec], out_specs=main_spec,
            out_shape=main_shape, compiler_params=_cparams(("parallel",)), name="proj",
        )(h, nw.reshape(1, d), w_main)
    ng = w_gates.shape[1]
    wg_pad = jnp.zeros((d, 128), BF16).at[:, :ng].set(w_gates)
    return pl.pallas_call(
        _proj_gates_body, grid=(t // tm,),
        in_specs=[x_spec, nw_spec, w_spec, pl.BlockSpec((d, 128), lambda i: (0, 0)),
                  pl.BlockSpec((ng, d), lambda i: (0, 0))],
        out_specs=[main_spec, pl.BlockSpec((tm, ng), lambda i: (i, 0)), pl.BlockSpec((ng, tm), lambda i: (0, i))],
        out_shape=[main_shape, jax.ShapeDtypeStruct((t, ng), F32), jax.ShapeDtypeStruct((ng, t), F32)],
        compiler_params=_cparams(("parallel",)), name="proj_gates",
    )(h, nw.reshape(1, d), w_main, wg_pad, w_gates.T)


def _mlstm_body(main_ref, g_ref, gt_ref, br_ref, bc_ref, hn_ref, out_ref, c_ref, n_ref, m_ref, *, nb, nh, dh):
    L = MLSTM_CHUNK
    w = nh * dh

    @pl.when(pl.program_id(0) == 0)
    def _init():
        c_ref[...] = jnp.zeros_like(c_ref)
        n_ref[...] = jnp.zeros_like(n_ref)
        m_ref[...] = jnp.full_like(m_ref, -1e30)

    row = lax.broadcasted_iota(jnp.int32, (L, L), 0)
    col = lax.broadcasted_iota(jnp.int32, (L, L), 1)
    causal = col <= row
    tril = causal.astype(BF16)
    triu = (row <= col).astype(BF16)
    scale = dh ** -0.5
    for b in range(nb):
        g = g_ref[b] + br_ref[...]
        gt = gt_ref[b, 0] + bc_ref[...]
        i_c = g[:, :nh]
        i_r = gt[:nh, :]
        lfc = _split3(jax.nn.log_sigmoid(g[:, nh:]))
        lfr = _split3(jax.nn.log_sigmoid(gt[nh:, :]))
        bc_all = _dot(tril, lfc[0]) + _dot(tril, lfc[1]) + _dot(tril, lfc[2])
        br_all = _dot(lfr[0], triu) + _dot(lfr[1], triu) + _dot(lfr[2], triu)
        for h in range(nh):
            idx = b * nh + h
            q = main_ref[b, :, h * dh:(h + 1) * dh]
            k = main_ref[b, :, w + h * dh:w + (h + 1) * dh] * scale
            v = main_ref[b, :, 2 * w + h * dh:2 * w + (h + 1) * dh]
            o = main_ref[b, :, 3 * w + h * dh:3 * w + (h + 1) * dh]
            bc = bc_all[:, h:h + 1]
            br = br_all[h:h + 1, :]
            ir = i_r[h:h + 1, :]
            ic = i_c[:, h:h + 1]
            m_prev = m_ref[idx]
            c_prev = c_ref[idx]
            n_prev = n_ref[idx]
            logw = jnp.where(causal, bc - br + ir, -jnp.inf)
            inter = bc + m_prev
            m_row = jnp.maximum(jnp.max(logw, axis=-1, keepdims=True), inter)
            qb = q.astype(BF16)
            kb = k.astype(BF16)
            vb = v.astype(BF16)
            s = _dot_nt(qb, kb) * jnp.exp(logw - m_row)
            isc = jnp.exp(inter - m_row)
            num = _dot(s.astype(BF16), vb) + isc * _dot_nt(qb, c_prev.astype(BF16))
            den = jnp.sum(s, axis=-1, keepdims=True) + isc * jnp.sum(q * n_prev, axis=-1, keepdims=True)
            hout = num / jnp.maximum(jnp.abs(den), jnp.exp(-m_row))
            b_end = bc[L - 1:L, :]
            logg = b_end - bc + ic
            m_new = jnp.maximum(b_end + m_prev, jnp.max(logg, axis=0, keepdims=True))
            wk = jnp.exp(logg - m_new)
            decay = jnp.exp(b_end + m_prev - m_new)
            c_ref[idx] = decay * c_prev + _dot_tn((v * wk).astype(BF16), kb)
            n_ref[idx] = decay * n_prev + jnp.sum(wk * k, axis=0, keepdims=True)
            m_ref[idx] = m_new
            hn = hout * lax.rsqrt(jnp.mean(hout * hout, axis=-1, keepdims=True) + RMS_EPS)
            out_ref[b, :, h * dh:(h + 1) * dh] = hn * hn_ref[:, h * dh:(h + 1) * dh] * jax.nn.sigmoid(o)


def _mlstm(main, g, gt, gate_bias, head_norm, nb, seq):
    t, n = main.shape
    nh = MLSTM_HEADS
    w = head_norm.shape[0]
    dh = w // nh
    L = MLSTM_CHUNK
    nc = seq // L
    main3 = main.reshape(nb, seq, n)
    g3 = g.reshape(nb, seq, 2 * nh)
    gt4 = gt.reshape(2 * nh, nb, nc, L).transpose(1, 2, 0, 3)
    body = functools.partial(_mlstm_body, nb=nb, nh=nh, dh=dh)
    out = pl.pallas_call(
        body, grid=(nc,),
        in_specs=[pl.BlockSpec((nb, L, 4 * w), lambda c: (0, c, 0)),
                  pl.BlockSpec((nb, L, 2 * nh), lambda c: (0, c, 0)),
                  pl.BlockSpec((nb, 1, 2 * nh, L), lambda c: (0, c, 0, 0)),
                  pl.BlockSpec((1, 2 * nh), lambda c: (0, 0)),
                  pl.BlockSpec((2 * nh, 1), lambda c: (0, 0)),
                  pl.BlockSpec((1, w), lambda c: (0, 0))],
        out_specs=pl.BlockSpec((nb, L, w), lambda c: (0, c, 0)),
        out_shape=jax.ShapeDtypeStruct((nb, seq, w), F32),
        scratch_shapes=[pltpu.VMEM((nb * nh, dh, dh), F32), pltpu.VMEM((nb * nh, 1, dh), F32),
                        pltpu.VMEM((nb * nh, 1, 1), F32)],
        compiler_params=_cparams(("arbitrary",)), name="mlstm",
    )(main3, g3, gt4, gate_bias.reshape(1, 2 * nh), gate_bias.reshape(2 * nh, 1), head_norm.reshape(1, w))
    return out.reshape(t, w)


def _out_a_body(h_ref, hm_ref, gb_ref, gc_ref, xin_ref, pgc_ref, pxin_ref, cw_ref, w_ref, out_ref, *, tm, seq):
    i = pl.program_id(0)
    wm = hm_ref.shape[1]
    p = gc_ref[...] * xin_ref[...]
    first = (i * tm) % seq == 0
    pp = jnp.where(first, 0.0, pgc_ref[...] * pxin_ref[...])
    rowi = lax.broadcasted_iota(jnp.int32, p.shape, 0)
    p1 = jnp.where(rowi == 0, pp[7:8, :], pltpu.roll(p, 1, axis=0))
    p2 = jnp.where(rowi == 0, pp[6:7, :], jnp.where(rowi == 1, pp[7:8, :], pltpu.roll(p, 2, axis=0)))
    yc = gb_ref[...] * (cw_ref[0:1, :] * p2 + cw_ref[1:2, :] * p1 + cw_ref[2:3, :] * p)
    out_ref[...] = (h_ref[...] + _dot(hm_ref[...].astype(BF16), w_ref[:wm, :])
                    + _dot(yc.astype(BF16), w_ref[wm:, :]))


def _out_a(h, hm, main, conv_w, w_out, seq):
    t, d = h.shape
    wm = hm.shape[1]
    wc = conv_w.shape[1]
    tm = TM_PROJ
    cb = (4 * wm) // wc
    rb = tm // 8
    prev = lambda i: jnp.maximum(i * rb - 1, 0)
    body = functools.partial(_out_a_body, tm=tm, seq=seq)
    return pl.pallas_call(
        body, grid=(t // tm,),
        in_specs=[pl.BlockSpec((tm, d), lambda i: (i, 0)),
                  pl.BlockSpec((tm, wm), lambda i: (i, 0)),
                  pl.BlockSpec((tm, wc), lambda i: (i, cb)),
                  pl.BlockSpec((tm, wc), lambda i: (i, cb + 1)),
                  pl.BlockSpec((tm, wc), lambda i: (i, cb + 2)),
                  pl.BlockSpec((8, wc), lambda i: (prev(i), cb + 1)),
                  pl.BlockSpec((8, wc), lambda i: (prev(i), cb + 2)),
                  pl.BlockSpec(conv_w.shape, lambda i: (0, 0)),
                  pl.BlockSpec(w_out.shape, lambda i: (0, 0))],
        out_specs=pl.BlockSpec((tm, d), lambda i: (i, 0)),
        out_shape=jax.ShapeDtypeStruct((t, d), F32),
        compiler_params=_cparams(("parallel",)), name="out_a",
    )(h, hm, main, main, main, main, main, conv_w, w_out)


def _route(logits, lane):
    ne = MOE_GROUPS * MOE_EPG
    big = 1e9
    gl = jnp.where((lane >= ne) & (lane < ne + MOE_GROUPS), logits, -jnp.inf)
    gmax = jnp.max(gl, axis=-1, keepdims=True)
    gidx = jnp.min(jnp.where(gl == gmax, lane - ne, big), axis=-1, keepdims=True)
    gval = 1.0 / jnp.sum(jnp.exp(gl - gmax), axis=-1, keepdims=True)
    lo = gidx * MOE_EPG
    sel = jnp.where((lane >= lo) & (lane < lo + MOE_EPG), logits, -jnp.inf)
    l1 = jnp.max(sel, axis=-1, keepdims=True)
    i1 = jnp.min(jnp.where(sel == l1, lane, big), axis=-1, keepdims=True)
    sel2 = jnp.where(lane == i1, -jnp.inf, sel)
    l2 = jnp.max(sel2, axis=-1, keepdims=True)
    i2 = jnp.min(jnp.where(sel2 == l2, lane, big), axis=-1, keepdims=True)
    r = jnp.exp(l2 - l1)
    w1 = gval / (1.0 + r)
    return i1, i2, w1, w1 * r


ROW_TILE = 8


def _rows_to_tiles(ref, x):
    n = x.shape[0]
    for c in range(ROW_TILE):
        ref[pl.ds(c, n, stride=ROW_TILE), :] = x[:, c * 128:(c + 1) * 128]


def _tiles_to_rows(ref, n):
    return jnp.concatenate([ref[pl.ds(c, n, stride=ROW_TILE), :] for c in range(ROW_TILE)], axis=1)


def _lane_put(lane, cols):
    out = jnp.where(lane == 0.0, cols[0], 0.0)
    for k in range(1, len(cols)):
        out = out + jnp.where(lane == float(k), cols[k], 0.0)
    return out


def _lane_get(lane, x, idx_col):
    return jnp.sum(jnp.where(lane == idx_col, x, 0.0), axis=-1, keepdims=True)


def _route_body(h_ref, nw_ref, wr_ref, br_ref, tri_ref, info_ref, cnt_ref, tile_ref):
    @pl.when(pl.program_id(0) == 0)
    def _init():
        cnt_ref[...] = jnp.zeros_like(cnt_ref)

    xn = _rms(h_ref[...], nw_ref[...])
    hi = xn.astype(BF16)
    lo = (xn - hi.astype(F32)).astype(BF16)
    logits = _dot(jnp.concatenate([hi, lo, hi], axis=1), wr_ref[...]) + br_ref[...]
    lane = lax.broadcasted_iota(jnp.int32, logits.shape, 1).astype(F32)
    i1, i2, w1, w2 = _route(logits, lane)
    ind = jnp.where((lane == i1) | (lane == i2), 1.0, 0.0)
    ahead = _dot(tri_ref[...], ind.astype(BF16))
    before = ahead + cnt_ref[0:1, :]
    tcnt = jnp.broadcast_to(jnp.sum(ind, axis=0, keepdims=True), cnt_ref.shape)
    lane8 = lax.broadcasted_iota(jnp.int32, cnt_ref.shape, 1)
    sub8 = lax.broadcasted_iota(jnp.int32, cnt_ref.shape, 0)
    incl = tcnt
    sh = 1
    while sh < cnt_ref.shape[1]:
        incl = incl + jnp.where(lane8 >= sh, pltpu.roll(incl, sh, axis=1), 0.0)
        sh *= 2
    toff = incl - tcnt
    local = ahead + toff[0:1, :]
    info_ref[...] = _lane_put(lane, [w1, w2, i1, i2, _lane_get(lane, before, i1), _lane_get(lane, before, i2),
                                     _lane_get(lane, local, i1), _lane_get(lane, local, i2)])
    tile_ref[...] = jnp.where(sub8 == 0, tcnt, jnp.where(sub8 == 1, toff, 0.0))
    cnt_ref[0:1, :] = cnt_ref[0:1, :] + tcnt[0:1, :]


def _dispatch_body(cnt_s, off_s, tot_s, h_ref, nw_ref, info_ref, offrow_ref, sel_ref, runs_s, xs_ref, pos_ref,
                   xn_buf, blk, pos_v, pos_s, zbuf, sem, *, tm, tile_rows, ne, n_tiles):
    i = pl.program_id(0)
    last = pl.num_programs(0) - 1
    slots = MOE_TOP_K * tm * ROW_TILE

    def wait_runs(half):
        pltpu.make_async_copy(blk.at[half], xs_ref.at[pl.ds(0, slots)], sem.at[half]).wait()

    for half in range(2):
        rs = slice(half * tm, (half + 1) * tm)

        @pl.when(i > 0)
        def _drain_previous(half=half):
            wait_runs(half)

        _rows_to_tiles(xn_buf, _rms(h_ref[rs, :], nw_ref[...]))
        info = info_ref[rs, :]
        lane = lax.broadcasted_iota(jnp.int32, info.shape, 1).astype(F32)
        offrow = offrow_ref[...]
        vals = [_lane_get(lane, offrow, info[:, 2:3]) + info[:, 4:5],
                _lane_get(lane, offrow, info[:, 3:4]) + info[:, 5:6],
                info[:, 6:7], info[:, 7:8]]
        cols = []
        for v in vals:
            hi = jnp.floor(v * (1.0 / 256.0))
            cols += [hi, v - 256.0 * hi]
        rows = _dot_nt(sel_ref[...], _lane_put(lane, cols).astype(BF16))
        sub = lax.broadcasted_iota(jnp.int32, rows.shape, 0)
        posall = jnp.zeros(rows.shape, F32)
        for k in range(len(vals)):
            posall = jnp.where(sub == k, rows[2 * k:2 * k + 1, :] * 256.0 + rows[2 * k + 1:2 * k + 2, :], posall)
        posall = posall.astype(jnp.int32)
        pos_ref[:, rs] = posall
        pos_v[...] = posall
        cp = pltpu.make_async_copy(pos_v, pos_s, sem.at[2])
        cp.start()
        cp.wait()

        def place(t, carry, half=half):
            row = xn_buf[pl.ds(pl.multiple_of(t * ROW_TILE, ROW_TILE), ROW_TILE), :]
            for k in range(MOE_TOP_K):
                blk[half, pl.ds(pl.multiple_of(pos_s[MOE_TOP_K + k, t] * ROW_TILE, ROW_TILE), ROW_TILE), :] = row
            return carry

        lax.fori_loop(0, tm, place, 0, unroll=8)
        for e in range(ne):
            n = runs_s[half, 2, e] * ROW_TILE

            @pl.when(n > 0)
            def _send(e=e, n=n, half=half):
                src = pl.multiple_of(runs_s[half, 0, e] * ROW_TILE, ROW_TILE)
                dst = pl.multiple_of(runs_s[half, 1, e] * ROW_TILE, ROW_TILE)
                pltpu.make_async_copy(blk.at[half, pl.ds(src, n)], xs_ref.at[pl.ds(dst, n)], sem.at[half]
                                      ).start(priority=e % 2)

    @pl.when(i == last)
    def _zero_unused_rows():
        wait_runs(0)
        wait_runs(1)
        zbuf[...] = jnp.zeros_like(zbuf)

        def fill(row, nrows):
            at = pl.multiple_of(row * ROW_TILE, ROW_TILE)
            n = nrows * ROW_TILE
            c = pltpu.make_async_copy(zbuf.at[pl.ds(0, n)], xs_ref.at[pl.ds(at, n)], sem.at[2])
            c.start()
            c.wait()

        for e in range(ne):
            n_pad = (tile_rows - cnt_s[e] % tile_rows) % tile_rows

            @pl.when(n_pad > 0)
            def _fill(e=e, n_pad=n_pad):
                fill(off_s[e] + cnt_s[e], n_pad)

        def zero_tile(j, carry):
            fill(j * tile_rows, tile_rows)
            return carry

        lax.fori_loop(tot_s[0], n_tiles, zero_tile, 0)


def _expert_body(te_s, blk_s, tot_s, xs_ref, wg_ref, wu_ref, wd_ref, ys_ref):
    valid = pl.program_id(0) < tot_s[0]
    rows = xs_ref.shape[0] // ROW_TILE

    @pl.when(valid)
    def _run():
        x = _tiles_to_rows(xs_ref, rows).astype(BF16)
        hid = jax.nn.silu(_dot(x, wg_ref[0, 0].astype(BF16))) * _dot(x, wu_ref[0, 0].astype(BF16))
        _rows_to_tiles(ys_ref, _dot(hid.astype(BF16), wd_ref[0, 0].astype(BF16)))

    @pl.when(jnp.logical_not(valid))
    def _unused_tile():
        ys_ref[...] = jnp.zeros_like(ys_ref)


def _combine_body(h_ref, info_ref, pos_s, runs_s, next_s, ys_ref, fw_ref, out_ref, blk, y1, y2, sem, *,
                  tm, ne, final):
    i = pl.program_id(0)
    slots = MOE_TOP_K * tm * ROW_TILE
    bufs = (y1, y2)

    def fetch(runs, half):
        for e in range(ne):
            n = runs[half, 2, e] * ROW_TILE

            @pl.when(n > 0)
            def _fetch(e=e, n=n):
                dst = pl.multiple_of(runs[half, 0, e] * ROW_TILE, ROW_TILE)
                src = pl.multiple_of(runs[half, 1, e] * ROW_TILE, ROW_TILE)
                pltpu.make_async_copy(ys_ref.at[pl.ds(src, n)], blk.at[half, pl.ds(dst, n)], sem.at[half]
                                      ).start(priority=e % 2)

    def finish(half):
        pltpu.make_async_copy(ys_ref.at[pl.ds(0, slots)], blk.at[half], sem.at[half]).wait()
        rs = slice(half * tm, (half + 1) * tm)

        def pick(t, carry):
            dst = pl.ds(pl.multiple_of(t * ROW_TILE, ROW_TILE), ROW_TILE)
            for k in range(MOE_TOP_K):
                at = pl.multiple_of(pos_s[MOE_TOP_K + k, half * tm + t] * ROW_TILE, ROW_TILE)
                bufs[k][dst, :] = blk[half, pl.ds(at, ROW_TILE), :]
            return carry

        lax.fori_loop(0, tm, pick, 0, unroll=8)
        o = (h_ref[rs, :] + info_ref[rs, 0:1] * _tiles_to_rows(y1, tm) + info_ref[rs, 1:2] * _tiles_to_rows(y2, tm))
        if final:
            o = _rms(o, fw_ref[...])
        out_ref[rs, :] = o

    @pl.when(i == 0)
    def _first():
        fetch(runs_s, 0)

    fetch(runs_s, 1)
    finish(0)

    @pl.when(i < pl.num_programs(0) - 1)
    def _prefetch():
        fetch(next_s, 0)

    finish(1)


PAIRS_PER_GROUP = MOE_EPG * (MOE_EPG - 1) // 2
N_BUCKETS = MOE_GROUPS * PAIRS_PER_GROUP
TB_BUCKET = 256
PACK_ROWS = 4


def _bucket_tables():
    ea, ec = [], []
    for g in range(MOE_GROUPS):
        for a in range(MOE_EPG):
            for c in range(a + 1, MOE_EPG):
                ea.append(g * MOE_EPG + a)
                ec.append(g * MOE_EPG + c)
    return np.asarray(ea, np.int32), np.asarray(ec, np.int32)


def _pb_route_body(h_ref, nw_ref, wr_ref, br_ref, tri_ref, sel_ref, rows_ref, brow_ref, cnt_ref):
    tm = h_ref.shape[0]
    half = h_ref.shape[1] // 2

    @pl.when(pl.program_id(0) == 0)
    def _init():
        cnt_ref[...] = jnp.zeros_like(cnt_ref)

    xn = _rms(h_ref[...], nw_ref[...])
    hi = xn.astype(BF16)
    lo = (xn - hi.astype(F32)).astype(BF16)
    logits = _dot(jnp.concatenate([hi, lo, hi], axis=1), wr_ref[...]) + br_ref[...]
    lane = lax.broadcasted_iota(jnp.int32, logits.shape, 1).astype(F32)
    i1, i2, w1, w2 = _route(logits, lane)
    swap = i2 < i1
    ea = jnp.where(swap, i2, i1)
    ec = jnp.where(swap, i1, i2)
    wa = jnp.where(swap, w2, w1)
    wc = jnp.where(swap, w1, w2)
    grp = jnp.floor(ea * (1.0 / MOE_EPG))
    a8 = ea - MOE_EPG * grp
    c8 = ec - MOE_EPG * grp
    bucket = grp * PAIRS_PER_GROUP + a8 * (2 * MOE_EPG - 1 - a8) * 0.5 + (c8 - a8 - 1.0)
    ind = jnp.where(lane == bucket, 1.0, 0.0)
    before = _dot(tri_ref[...], ind.astype(BF16)) + cnt_ref[0:1, :]
    rank = _lane_get(lane, before, bucket)
    cnt_ref[0:1, :] = cnt_ref[0:1, :] + jnp.sum(ind, axis=0, keepdims=True)
    rank_hi = jnp.floor(rank * (1.0 / 128.0))
    pieces = _lane_put(lane, [bucket, rank_hi, rank - 128.0 * rank_hi]).astype(BF16)
    tr = _dot_nt(sel_ref[...], pieces)
    sub = lax.broadcasted_iota(jnp.int32, tr.shape, 0)
    brow_ref[...] = jnp.where(sub == 0, tr[0:1, :], jnp.where(sub == 1, tr[1:2, :] * 128.0 + tr[2:3, :], 0.0)
                              ).astype(jnp.int32)
    bits = pltpu.bitcast(hi.astype(F32), jnp.uint32)
    words = (bits[:, :half] >> 16) | bits[:, half:]
    for c in range(PACK_ROWS):
        rows_ref[pl.ds(c, tm, stride=ROW_TILE), :] = words[:, c * 128:(c + 1) * 128]
    rows_ref[pl.ds(PACK_ROWS, tm, stride=ROW_TILE), :] = pltpu.bitcast(_lane_put(lane, [wa, wc]), jnp.uint32)
    for c in range(PACK_ROWS + 1, ROW_TILE):
        rows_ref[pl.ds(c, tm, stride=ROW_TILE), :] = jnp.zeros((tm, 128), jnp.uint32)


def _pb_invmap_body(off_s, brow_s, pos_s, src_hbm, src_s, sem, *, tm, rows_total):
    i = pl.program_id(0)

    @pl.when(i == 0)
    def _init():
        def zero(r, carry):
            src_s[r] = 0
            return carry
        lax.fori_loop(0, rows_total, zero, 0)

    def place(t, carry):
        p = off_s[brow_s[0, t]] + brow_s[1, t]
        src_s[p] = i * tm + t
        pos_s[0, t] = p
        return carry

    lax.fori_loop(0, tm, place, 0, unroll=8)

    @pl.when(i == pl.num_programs(0) - 1)
    def _flush():
        cp = pltpu.make_async_copy(src_s, src_hbm, sem)
        cp.start()
        cp.wait()


def _pb_expert_body(ea_s, ec_s, nv_s, tot_s, src_s, nxt_s, rows_hbm, wga, wua, wda, wgc, wuc, wdc, ys_ref,
                    xbuf, sem, *, tb):
    i = pl.program_id(0)
    tot = tot_s[0]

    def gather(idx_ref, count, slot):
        def issue(r2, carry):
            for par in range(2):
                r = 2 * r2 + par

                @pl.when(r < count)
                def _go(r=r, par=par):
                    at = pl.multiple_of(idx_ref[0, 0, r] * ROW_TILE, ROW_TILE)
                    to = pl.multiple_of(r * ROW_TILE, ROW_TILE)
                    pltpu.make_async_copy(rows_hbm.at[pl.ds(at, ROW_TILE)], xbuf.at[slot, pl.ds(to, ROW_TILE)],
                                          sem.at[slot]).start(priority=par)
            return carry

        lax.fori_loop(0, (count + 1) // 2, issue, 0)

    @pl.when(i == 0)
    def _first():
        xbuf[...] = jnp.zeros_like(xbuf)
        gather(src_s, nv_s[0], 0)

    def step(slot):
        @pl.when(i + 1 < tot)
        def _prefetch():
            gather(nxt_s, nv_s[i + 1], 1 - slot)

        n = nv_s[i] * ROW_TILE
        pltpu.make_async_copy(rows_hbm.at[pl.ds(0, n)], xbuf.at[slot, pl.ds(0, n)], sem.at[slot]).wait()
        w = jnp.concatenate([xbuf[slot, pl.ds(c, tb, stride=ROW_TILE), :] for c in range(PACK_ROWS)], axis=1)
        x = jnp.concatenate([pltpu.bitcast(w << 16, F32), pltpu.bitcast(w & jnp.uint32(0xFFFF0000), F32)],
                            axis=1).astype(BF16)
        wts = pltpu.bitcast(xbuf[slot, pl.ds(PACK_ROWS, tb, stride=ROW_TILE), :], F32)

        def expert(wg, wu, wd):
            hid = jax.nn.silu(_dot(x, wg[0, 0].astype(BF16))) * _dot(x, wu[0, 0].astype(BF16))
            return _dot(hid.astype(BF16), wd[0, 0].astype(BF16))

        _rows_to_tiles(ys_ref, wts[:, 0:1] * expert(wga, wua, wda) + wts[:, 1:2] * expert(wgc, wuc, wdc))

    for slot in range(2):
        @pl.when((i < tot) & (i % 2 == slot))
        def _run(slot=slot):
            step(slot)

    @pl.when(i >= tot)
    def _unused_tile():
        ys_ref[...] = jnp.zeros_like(ys_ref)


def _pb_combine_body(h_ref, pos_s, ys_hbm, fw_ref, out_ref, buf, sem, *, tm, final):
    def issue(t2, carry):
        for par in range(2):
            t = 2 * t2 + par
            at = pl.multiple_of(pos_s[0, t] * ROW_TILE, ROW_TILE)
            to = pl.multiple_of(t * ROW_TILE, ROW_TILE)
            pltpu.make_async_copy(ys_hbm.at[pl.ds(at, ROW_TILE)], buf.at[pl.ds(to, ROW_TILE)], sem.at[par]
                                  ).start(priority=par)
        return carry

    lax.fori_loop(0, tm // 2, issue, 0, unroll=8)
    for par in range(2):
        n = tm // 2 * ROW_TILE
        pltpu.make_async_copy(ys_hbm.at[pl.ds(0, n)], buf.at[pl.ds(0, n)], sem.at[par]).wait()
    o = h_ref[...] + _tiles_to_rows(buf, tm)
    if final:
        o = _rms(o, fw_ref[...])
    out_ref[...] = o


def _moe_pb(h, nw, w_group, b_group, w_router, b_router, w_gate, w_up, w_down, layer, final_w):
    t, d = h.shape
    assert d == ROW_TILE * 128 and MOE_TOP_K == 2 and N_BUCKETS <= ROUTE_LANES
    _, ne, _, ff = w_gate.shape
    tm = TM_MOE
    tb = TB_BUCKET
    nw2 = nw.reshape(1, d)
    wr3, bias = _router_weights(w_group, b_group, w_router, b_router)
    tri = jnp.asarray(np.tril(np.ones((tm, tm), np.float32), -1), BF16)
    sel = jnp.asarray(np.eye(8, ROUTE_LANES, dtype=np.float32), BF16)
    rows, brow, cnt = pl.pallas_call(
        _pb_route_body, grid=(t // tm,),
        in_specs=[pl.BlockSpec((tm, d), lambda i: (i, 0)),
                  pl.BlockSpec((1, d), lambda i: (0, 0)),
                  pl.BlockSpec((3 * d, ROUTE_LANES), lambda i: (0, 0)),
                  pl.BlockSpec((1, ROUTE_LANES), lambda i: (0, 0)),
                  pl.BlockSpec((tm, tm), lambda i: (0, 0)),
                  pl.BlockSpec((8, ROUTE_LANES), lambda i: (0, 0))],
        out_specs=[pl.BlockSpec((tm * ROW_TILE, 128), lambda i: (i, 0)),
                   pl.BlockSpec((8, tm), lambda i: (0, i)),
                   pl.BlockSpec((8, ROUTE_LANES), lambda i: (0, 0))],
        out_shape=[jax.ShapeDtypeStruct((t * ROW_TILE, 128), jnp.uint32),
                   jax.ShapeDtypeStruct((8, t), jnp.int32),
                   jax.ShapeDtypeStruct((8, ROUTE_LANES), F32)],
        compiler_params=_cparams(("arbitrary",)), name="moe_route",
    )(h, nw2, wr3, bias, tri, sel)

    cnt_i = cnt[0, :N_BUCKETS].astype(jnp.int32)
    ntile = (cnt_i + tb - 1) // tb
    tile_end = jnp.cumsum(ntile)
    tile_start = tile_end - ntile
    off_i = tile_start * tb
    n_tiles = t // tb + N_BUCKETS
    rows_total = n_tiles * tb
    tot = tile_end[-1:]
    ti = jnp.arange(n_tiles, dtype=jnp.int32)
    tile_b = jnp.minimum(jnp.sum((ti[:, None] >= tile_end[None, :]).astype(jnp.int32), axis=1), N_BUCKETS - 1)
    nvalid = jnp.clip(cnt_i[tile_b] - (ti - tile_start[tile_b]) * tb, 0, tb)
    tab_a, tab_c = _bucket_tables()
    tile_ea = jnp.asarray(tab_a)[tile_b]
    tile_ec = jnp.asarray(tab_c)[tile_b]

    pos, src = pl.pallas_call(
        functools.partial(_pb_invmap_body, tm=tm, rows_total=rows_total),
        grid_spec=pltpu.PrefetchScalarGridSpec(
            num_scalar_prefetch=1, grid=(t // tm,),
            in_specs=[pl.BlockSpec((8, tm), lambda i, o: (0, i), memory_space=pltpu.SMEM)],
            out_specs=[pl.BlockSpec((1, tm), lambda i, o: (0, i), memory_space=pltpu.SMEM),
                       pl.BlockSpec(memory_space=pl.ANY)],
            scratch_shapes=[pltpu.SMEM((rows_total,), jnp.int32), pltpu.SemaphoreType.DMA(())]),
        out_shape=[jax.ShapeDtypeStruct((1, t), jnp.int32), jax.ShapeDtypeStruct((rows_total,), jnp.int32)],
        compiler_params=_cparams(("arbitrary",)), name="moe_invmap",
    )(off_i, brow)

    src3 = src.reshape(n_tiles, 1, tb)
    wspec_a = lambda shape: pl.BlockSpec((1, 1) + shape, lambda i, ea, ec, nv, tt: (layer, ea[i], 0, 0))
    wspec_c = lambda shape: pl.BlockSpec((1, 1) + shape, lambda i, ea, ec, nv, tt: (layer, ec[i], 0, 0))
    ys = pl.pallas_call(
        functools.partial(_pb_expert_body, tb=tb),
        grid_spec=pltpu.PrefetchScalarGridSpec(
            num_scalar_prefetch=4, grid=(n_tiles,),
            in_specs=[pl.BlockSpec((1, 1, tb), lambda i, *_: (i, 0, 0), memory_space=pltpu.SMEM),
                      pl.BlockSpec((1, 1, tb), lambda i, *_: (jnp.minimum(i + 1, n_tiles - 1), 0, 0),
                                   memory_space=pltpu.SMEM),
                      pl.BlockSpec(memory_space=pl.ANY),
                      wspec_a((d, ff)), wspec_a((d, ff)), wspec_a((ff, d)),
                      wspec_c((d, ff)), wspec_c((d, ff)), wspec_c((ff, d))],
            out_specs=pl.BlockSpec((tb * ROW_TILE, 128), lambda i, *_: (i, 0)),
            scratch_shapes=[pltpu.VMEM((2, tb * ROW_TILE, 128), jnp.uint32), pltpu.SemaphoreType.DMA((2,))]),
        out_shape=jax.ShapeDtypeStruct((rows_total * ROW_TILE, 128), F32),
        compiler_params=_cparams(("arbitrary",)), name="moe_expert",
    )(tile_ea, tile_ec, nvalid, tot, src3, src3, rows, w_gate, w_up, w_down, w_gate, w_up, w_down)

    final = final_w is not None
    fw = (final_w if final else nw).reshape(1, d)
    return pl.pallas_call(
        functools.partial(_pb_combine_body, tm=tm, final=final), grid=(t // tm,),
        in_specs=[pl.BlockSpec((tm, d), lambda i: (i, 0)),
                  pl.BlockSpec((1, tm), lambda i: (0, i), memory_space=pltpu.SMEM),
                  pl.BlockSpec(memory_space=pl.ANY),
                  pl.BlockSpec((1, d), lambda i: (0, 0))],
        out_specs=pl.BlockSpec((tm, d), lambda i: (i, 0)),
        out_shape=jax.ShapeDtypeStruct((t, d), F32),
        scratch_shapes=[pltpu.VMEM((tm * ROW_TILE, 128), F32), pltpu.SemaphoreType.DMA((2,))],
        compiler_params=_cparams(("arbitrary",)), name="moe_combine",
    )(h, pos, ys, fw)


def _router_weights(w_group, b_group, w_router, b_router):
    d, ne = w_router.shape
    ng = w_group.shape[1]
    w = jnp.zeros((d, ROUTE_LANES), F32).at[:, :ne].set(w_router).at[:, ne:ne + ng].set(w_group)
    hi = w.astype(BF16)
    lo = (w - hi.astype(F32)).astype(BF16)
    bias = jnp.zeros((1, ROUTE_LANES), F32).at[0, :ne].set(b_router).at[0, ne:ne + ng].set(b_group)
    return jnp.concatenate([hi, hi, lo], axis=0), bias


def _moe(h, nw, w_group, b_group, w_router, b_router, w_gate, w_up, w_down, layer, final_w):
    t, d = h.shape
    assert d == ROW_TILE * 128, "row-as-tile layout needs d_model == 1024"
    _, ne, _, ff = w_gate.shape
    tm = TM_MOE
    te_rows = TM_EXPERT
    nw2 = nw.reshape(1, d)
    wr3, bias = _router_weights(w_group, b_group, w_router, b_router)
    tri = jnp.asarray(np.tril(np.ones((tm, tm), np.float32), -1), BF16)
    nt = t // tm
    info, cnt, tile_info = pl.pallas_call(
        _route_body, grid=(nt,),
        in_specs=[pl.BlockSpec((tm, d), lambda i: (i, 0)),
                  pl.BlockSpec((1, d), lambda i: (0, 0)),
                  pl.BlockSpec((3 * d, ROUTE_LANES), lambda i: (0, 0)),
                  pl.BlockSpec((1, ROUTE_LANES), lambda i: (0, 0)),
                  pl.BlockSpec((tm, tm), lambda i: (0, 0))],
        out_specs=[pl.BlockSpec((tm, ROUTE_LANES), lambda i: (i, 0)), pl.BlockSpec((8, ROUTE_LANES), lambda i: (0, 0)),
                   pl.BlockSpec((8, ROUTE_LANES), lambda i: (i, 0))],
        out_shape=[jax.ShapeDtypeStruct((t, ROUTE_LANES), F32), jax.ShapeDtypeStruct((8, ROUTE_LANES), F32),
                   jax.ShapeDtypeStruct((nt * 8, ROUTE_LANES), F32)],
        compiler_params=_cparams(("arbitrary",)), name="moe_route",
    )(h, nw2, wr3, bias, tri)

    cnt_i = cnt[0, :ne].astype(jnp.int32)
    ntile = (cnt_i + te_rows - 1) // te_rows
    tile_end = jnp.cumsum(ntile)
    off_i = (tile_end - ntile) * te_rows
    n_tiles = (MOE_TOP_K * t) // te_rows + ne
    rows_total = n_tiles * te_rows
    ti = jnp.arange(n_tiles, dtype=jnp.int32)
    tot = tile_end[-1:]
    ti_c = jnp.minimum(ti, tot[0] - 1)
    tile_e = jnp.sum((ti_c[:, None] >= tile_end[None, :]).astype(jnp.int32), axis=1)
    off_row = jnp.zeros((1, ROUTE_LANES), F32).at[0, :ne].set(off_i.astype(F32))
    sel = jnp.asarray(np.eye(8, ROUTE_LANES, dtype=np.float32), BF16)
    tinfo = tile_info.reshape(nt, 8, ROUTE_LANES).astype(jnp.int32)
    tcnt = tinfo[:, 0, :]
    gstart = off_row.astype(jnp.int32) + jnp.cumsum(tcnt, axis=0) - tcnt
    runs = jnp.zeros((nt, 8, ROUTE_LANES), jnp.int32).at[:, 0].set(tinfo[:, 1, :]).at[:, 1].set(gstart).at[:, 2].set(tcnt)

    xs, pos = pl.pallas_call(
        functools.partial(_dispatch_body, tm=tm, tile_rows=te_rows, ne=ne, n_tiles=n_tiles),
        grid_spec=pltpu.PrefetchScalarGridSpec(
            num_scalar_prefetch=3, grid=(nt // 2,),
            in_specs=[pl.BlockSpec((2 * tm, d), lambda i, *_: (i, 0)),
                      pl.BlockSpec((1, d), lambda i, *_: (0, 0)),
                      pl.BlockSpec((2 * tm, ROUTE_LANES), lambda i, *_: (i, 0)),
                      pl.BlockSpec((1, ROUTE_LANES), lambda i, *_: (0, 0)),
                      pl.BlockSpec((8, ROUTE_LANES), lambda i, *_: (0, 0)),
                      pl.BlockSpec((2, 8, ROUTE_LANES), lambda i, *_: (i, 0, 0), memory_space=pltpu.SMEM)],
            out_specs=[pl.BlockSpec(memory_space=pl.ANY), pl.BlockSpec((8, 2 * tm), lambda i, *_: (0, i))],
            scratch_shapes=[pltpu.VMEM((tm * ROW_TILE, 128), F32),
                            pltpu.VMEM((2, MOE_TOP_K * tm * ROW_TILE, 128), F32),
                            pltpu.VMEM((8, tm), jnp.int32), pltpu.SMEM((8, tm), jnp.int32),
                            pltpu.VMEM((te_rows * ROW_TILE, 128), F32), pltpu.SemaphoreType.DMA((3,))]),
        out_shape=[jax.ShapeDtypeStruct((rows_total * ROW_TILE, 128), F32), jax.ShapeDtypeStruct((8, t), jnp.int32)],
        compiler_params=_cparams(("arbitrary",)), name="moe_dispatch",
    )(cnt_i, off_i, tot, h, nw2, info, off_row, sel, runs)

    ys = pl.pallas_call(
        _expert_body,
        grid_spec=pltpu.PrefetchScalarGridSpec(
            num_scalar_prefetch=3, grid=(n_tiles,),
            in_specs=[pl.BlockSpec((te_rows * ROW_TILE, 128), lambda i, e, b, v: (b[i], 0)),
                      pl.BlockSpec((1, 1, d, ff), lambda i, e, b, v: (layer, e[i], 0, 0)),
                      pl.BlockSpec((1, 1, d, ff), lambda i, e, b, v: (layer, e[i], 0, 0)),
                      pl.BlockSpec((1, 1, ff, d), lambda i, e, b, v: (layer, e[i], 0, 0))],
            out_specs=pl.BlockSpec((te_rows * ROW_TILE, 128), lambda i, e, b, v: (i, 0))),
        out_shape=jax.ShapeDtypeStruct((rows_total * ROW_TILE, 128), F32),
        compiler_params=_cparams(("arbitrary",)), name="moe_expert",
    )(tile_e, ti_c, tot, xs, w_gate, w_up, w_down)

    final = final_w is not None
    fw = (final_w if final else nw).reshape(1, d)
    return pl.pallas_call(
        functools.partial(_combine_body, tm=tm, ne=ne, final=final), grid=(nt // 2,),
        in_specs=[pl.BlockSpec((2 * tm, d), lambda i: (i, 0)),
                  pl.BlockSpec((2 * tm, ROUTE_LANES), lambda i: (i, 0)),
                  pl.BlockSpec((8, 2 * tm), lambda i: (0, i), memory_space=pltpu.SMEM),
                  pl.BlockSpec((2, 8, ROUTE_LANES), lambda i: (i, 0, 0), memory_space=pltpu.SMEM),
                  pl.BlockSpec((2, 8, ROUTE_LANES), lambda i: (jnp.minimum(i + 1, nt // 2 - 1), 0, 0),
                               memory_space=pltpu.SMEM),
                  pl.BlockSpec(memory_space=pl.ANY),
                  pl.BlockSpec((1, d), lambda i: (0, 0))],
        out_specs=pl.BlockSpec((2 * tm, d), lambda i: (i, 0)),
        out_shape=jax.ShapeDtypeStruct((t, d), F32),
        scratch_shapes=[pltpu.VMEM((2, MOE_TOP_K * tm * ROW_TILE, 128), F32), pltpu.VMEM((tm * ROW_TILE, 128), F32),
                        pltpu.VMEM((tm * ROW_TILE, 128), F32), pltpu.SemaphoreType.DMA((2,))],
        compiler_params=_cparams(("arbitrary",)), name="moe_combine",
    )(h, info, pos, runs, runs, ys, fw)


def _s5_weights(lam_re, lam_im, b_re, b_im, c_re, c_im, log_dt, nsteps):
    hp = lax.Precision.HIGHEST
    L = S5_CHUNK
    g, p = lam_re.shape
    ch = b_re.shape[-1]
    lam = lax.complex(lam_re.astype(F32), lam_im.astype(F32))
    dt = jnp.exp(log_dt.astype(F32))[:, None]
    lam_bar = jnp.exp(lam * dt)
    b_bar = ((lam_bar - 1.0) / lam)[..., None] * lax.complex(b_re.astype(F32), b_im.astype(F32))
    cmat = lax.complex(c_re.astype(F32), c_im.astype(F32))
    pows = [jnp.ones_like(lam_bar)]
    for _ in range(L):
        pows.append(pows[-1] * lam_bar)
    pw = jnp.stack(pows, axis=1)
    kern = jnp.real(jnp.einsum('gop,gtp,gpi->gtoi', cmat, pw[:, :L], b_bar, precision=hp))
    lag = np.arange(L)[None, :] - np.arange(L)[:, None]
    toep = jnp.where((lag >= 0)[None, :, :, None, None], kern[:, np.maximum(lag, 0)], 0.0)
    toep = toep.transpose(0, 1, 4, 2, 3).reshape(g, L * ch, L * ch)
    wst = pw[:, L - 1 - np.arange(L)][:, :, :, None] * b_bar[:, None, :, :]
    wst = wst.transpose(0, 1, 3, 2).reshape(g, L * ch, p)
    wst = jnp.concatenate([jnp.real(wst), jnp.imag(wst)], axis=-1)
    mo = cmat.transpose(0, 2, 1)[:, :, None, :] * pw[:, 1:L + 1].transpose(0, 2, 1)[:, :, :, None]
    mo = mo.reshape(g, p, L * ch)
    wout = jnp.concatenate([jnp.real(mo), -jnp.imag(mo)], axis=1)
    a = pw[:, L]
    ars, ais = [], []
    for _ in range(nsteps):
        ars.append(jnp.concatenate([jnp.real(a), jnp.real(a)], axis=-1))
        ais.append(jnp.concatenate([-jnp.imag(a), jnp.imag(a)], axis=-1))
        a = a * a
    wcat = jnp.concatenate([toep, wst], axis=-1).astype(BF16)
    return wcat, wout.astype(BF16), jnp.stack(ars, axis=1), jnp.stack(ais, axis=1)


def _s5_body(u_ref, wcat_ref, wout_ref, ar_ref, ai_ref, y_ref, us_ref, ys_ref, *, nsteps):
    L = S5_CHUNK
    ch = S5_GROUP_CH
    gpc = 128 // ch
    ny = L * ch
    nc = u_ref.shape[0] // L
    for s in range(L):
        us_ref[s] = u_ref[pl.ds(s, nc, stride=L), :]
    lane = lax.broadcasted_iota(jnp.int32, (nc, 128), 1)
    ridx = lax.broadcasted_iota(jnp.int32, (nc, 128), 0)

    def shift(x, k):
        return jnp.where(ridx >= k, pltpu.roll(x, k, axis=0), 0.0)

    for gi in range(gpc):
        halves = []
        for hh in range(ny // 128):
            acc = None
            for s8 in range(gpc):
                rot = ((s8 - gi) * ch) % 128
                src = us_ref[hh * gpc + s8]
                if rot:
                    src = pltpu.roll(src, rot, axis=1)
                slot = (lane >= s8 * ch) & (lane < (s8 + 1) * ch)
                acc = jnp.where(slot, src, 0.0) if acc is None else jnp.where(slot, src, acc)
            halves.append(acc)
        ug = jnp.concatenate(halves, axis=1).astype(BF16)
        r = _dot(ug, wcat_ref[gi])
        y1 = r[:, :ny]
        z = r[:, ny:]
        w = shift(z, 1)
        for k in range(nsteps):
            if (1 << k) >= nc:
                break
            sk = shift(w, 1 << k)
            w = w + sk * ar_ref[gi, k:k + 1, :] + pltpu.roll(sk, z.shape[1] // 2, axis=1) * ai_ref[gi, k:k + 1, :]
        yg = y1 + _dot(w.astype(BF16), wout_ref[gi])
        slot = (lane >= gi * ch) & (lane < (gi + 1) * ch)
        for t in range(L):
            src = yg[:, (t // gpc) * 128:(t // gpc + 1) * 128]
            rot = ((gi - t % gpc) * ch) % 128
            if rot:
                src = pltpu.roll(src, rot, axis=1)
            ys_ref[t] = jnp.where(slot, src, 0.0) if gi == 0 else jnp.where(slot, src, ys_ref[t])
    for t in range(L):
        y_ref[pl.ds(t, nc, stride=L), :] = ys_ref[t]


def _s5(main, col0, width, nb, seq, lam_re, lam_im, b_re, b_im, c_re, c_im, log_dt):
    t = main.shape[0]
    L = S5_CHUNK
    ch = S5_GROUP_CH
    g = width // ch
    nc = seq // L
    gpc = 128 // ch
    ncol = width // 128
    assert col0 % 128 == 0 and width % 128 == 0 and (L * ch) % 128 == 0
    nsteps = max(1, (nc - 1).bit_length())
    wcat, wout, ar, ai = _s5_weights(lam_re, lam_im, b_re, b_im, c_re, c_im, log_dt, nsteps)
    body = functools.partial(_s5_body, nsteps=nsteps)
    return pl.pallas_call(
        body, grid=(ncol, nb),
        in_specs=[pl.BlockSpec((seq, 128), lambda j, b: (b, col0 // 128 + j)),
                  pl.BlockSpec((gpc,) + wcat.shape[1:], lambda j, b: (j, 0, 0)),
                  pl.BlockSpec((gpc,) + wout.shape[1:], lambda j, b: (j, 0, 0)),
                  pl.BlockSpec((gpc,) + ar.shape[1:], lambda j, b: (j, 0, 0)),
                  pl.BlockSpec((gpc,) + ai.shape[1:], lambda j, b: (j, 0, 0))],
        out_specs=pl.BlockSpec((seq, 128), lambda j, b: (b, j)),
        out_shape=jax.ShapeDtypeStruct((t, width), F32),
        scratch_shapes=[pltpu.VMEM((L, nc, 128), F32), pltpu.VMEM((L, nc, 128), F32)],
        compiler_params=_cparams(("parallel", "parallel")), name="s5",
    )(main, wcat, wout, ar, ai)


def _hgrn_gmat():
    L = HGRN_CHUNK
    blocks = 2 + int(np.log2(L))
    gm = np.zeros((blocks * L, L), np.float32)
    for j in range(L):
        gm[j, :j + 1] = 1.0
        gm[L + j, j + 1:] = 1.0
    li, m = 2, L
    while m >= 2:
        half = m // 2
        for j in range(L):
            pos = j % m
            r = j - pos + half - 1
            if pos >= half:
                gm[li * L + j, r + 1:j + 1] = 1.0
            else:
                gm[li * L + j, j + 1:r + 1] = 1.0
        li += 1
        m //= 2
    return gm


def _hgrn_body(main_ref, gm_ref, lb_ref, nw_ref, out_ref, st_ref, *, nb, nh, dh):
    L = HGRN_CHUNK
    w = nh * dh

    @pl.when(pl.program_id(0) == 0)
    def _init():
        st_ref[...] = jnp.zeros_like(st_ref)

    row = lax.broadcasted_iota(jnp.int32, (L, 2 * L), 0)
    col = lax.broadcasted_iota(jnp.int32, (L, 2 * L), 1) & (L - 1)
    rowd = lax.broadcasted_iota(jnp.int32, (L, 2 * dh), 0)
    laned = lax.broadcasted_iota(jnp.int32, (L, 2 * dh), 1)
    first = laned < dh
    eye = row == col

    def blockdiag(x):
        z = jnp.zeros_like(x)
        return jnp.concatenate([jnp.where(first, x, z), jnp.where(first, z, x)], axis=0)

    gm2 = gm_ref[...]
    lb = lb_ref[...]
    zst = jnp.zeros((dh, dh), BF16)
    for b in range(nb):
        fg = main_ref[b, :, w:2 * w]
        f = lb + (1.0 - lb) * jax.nn.sigmoid(fg)
        kk = (1.0 - lb) * jax.nn.sigmoid(-fg)
        lf = jnp.log(f)
        hi = lf.astype(BF16)
        mid = (lf - hi.astype(F32)).astype(BF16)
        p_all = jnp.exp(_dot(gm2, jnp.concatenate([hi, mid], axis=0)))
        for hp in range(nh // 2):
            i0 = b * nh + 2 * hp
            cs = slice(2 * hp * dh, (2 * hp + 2) * dh)
            q = main_ref[b, :, 2 * hp * dh:(2 * hp + 2) * dh]
            v = main_ref[b, :, 2 * w + 2 * hp * dh:2 * w + (2 * hp + 2) * dh]
            og = main_ref[b, :, 3 * w + 2 * hp * dh:3 * w + (2 * hp + 2) * dh]
            k = kk[:, cs]
            pb = p_all[0:L, cs]
            pe = p_all[L:2 * L, cs]
            st0 = st_ref[i0]
            st1 = st_ref[i0 + 1]
            stbd = jnp.concatenate([jnp.concatenate([st0.astype(BF16), zst], axis=1),
                                    jnp.concatenate([zst, st1.astype(BF16)], axis=1)], axis=0)
            o = _dot_nt((q * pb).astype(BF16), stbd)
            attn = jnp.where(eye, _dot_nt(q.astype(BF16), blockdiag(k.astype(BF16))), 0.0)
            li, m = 2, L
            while m >= 2:
                pl_ = p_all[li * L:(li + 1) * L, cs]
                up = (rowd & (m - 1)) >= (m // 2)
                ql = jnp.where(up, q * pl_, 0.0).astype(BF16)
                kl = jnp.where(up, 0.0, k * pl_).astype(BF16)
                same = (row & ~(m - 1)) == (col & ~(m - 1))
                attn = attn + jnp.where(same, _dot_nt(ql, blockdiag(kl)), 0.0)
                li += 1
                m //= 2
            vb = v.astype(BF16)
            o = o + _dot(attn.astype(BF16), blockdiag(vb))
            kh = (k * pe).astype(BF16)
            for j in range(2):
                hs = slice(j * dh, (j + 1) * dh)
                gs = slice((2 * hp + j) * dh, (2 * hp + j + 1) * dh)
                st = st0 if j == 0 else st1
                st_ref[i0 + j] = st * pb[L - 1:L, hs] + _dot_tn(vb[:, hs], kh[:, hs])
                oj = o[:, hs]
                on = oj * lax.rsqrt(jnp.mean(oj * oj, axis=-1, keepdims=True) + RMS_EPS)
                out_ref[b, :, gs] = on * nw_ref[:, gs] * jax.nn.silu(og[:, hs])


def _hgrn(main, lower_bound, out_norm, nb, seq):
    t, n = main.shape
    nh = HGRN_HEADS
    w = out_norm.shape[0]
    dh = w // nh
    L = HGRN_CHUNK
    nc = seq // L
    gm = _hgrn_gmat()
    gm = jnp.asarray(np.concatenate([gm, gm], axis=1), BF16)
    body = functools.partial(_hgrn_body, nb=nb, nh=nh, dh=dh)
    out = pl.pallas_call(
        body, grid=(nc,),
        in_specs=[pl.BlockSpec((nb, L, 4 * w), lambda c: (0, c, 0)),
                  pl.BlockSpec(gm.shape, lambda c: (0, 0)),
                  pl.BlockSpec((1, w), lambda c: (0, 0)),
                  pl.BlockSpec((1, w), lambda c: (0, 0))],
        out_specs=pl.BlockSpec((nb, L, w), lambda c: (0, c, 0)),
        out_shape=jax.ShapeDtypeStruct((nb, seq, w), F32),
        scratch_shapes=[pltpu.VMEM((nb * nh, dh, dh), F32)],
        compiler_params=_cparams(("arbitrary",)), name="hgrn",
    )(main.reshape(nb, seq, n), gm, lower_bound.reshape(1, w), out_norm.reshape(1, w))
    return out.reshape(t, w)


def _out_c_body(h_ref, ys_ref, u_ref, oh_ref, d_ref, wglu_ref, bglu_ref, w_ref, out_ref):
    ws = ys_ref.shape[1]
    z = jax.nn.gelu(ys_ref[...] + d_ref[...] * u_ref[...])
    gate = jax.nn.sigmoid(_dot(z.astype(BF16), wglu_ref[...]) + bglu_ref[...])
    out_ref[...] = (h_ref[...] + _dot((z * gate).astype(BF16), w_ref[:ws, :])
                    + _dot(oh_ref[...].astype(BF16), w_ref[ws:, :]))


def _out_c(h, ys, main, oh, d_skip, w_glu, b_glu, w_out):
    t, d = h.shape
    ws = ys.shape[1]
    wh = oh.shape[1]
    tm = TM_PROJ
    ub = (main.shape[1] - ws) // ws
    return pl.pallas_call(
        _out_c_body, grid=(t // tm,),
        in_specs=[pl.BlockSpec((tm, d), lambda i: (i, 0)),
                  pl.BlockSpec((tm, ws), lambda i: (i, 0)),
                  pl.BlockSpec((tm, ws), lambda i: (i, ub)),
                  pl.BlockSpec((tm, wh), lambda i: (i, 0)),
                  pl.BlockSpec((1, ws), lambda i: (0, 0)),
                  pl.BlockSpec(w_glu.shape, lambda i: (0, 0)),
                  pl.BlockSpec((1, ws), lambda i: (0, 0)),
                  pl.BlockSpec(w_out.shape, lambda i: (0, 0))],
        out_specs=pl.BlockSpec((tm, d), lambda i: (i, 0)),
        out_shape=jax.ShapeDtypeStruct((t, d), F32),
        compiler_params=_cparams(("parallel",)), name="out_c",
    )(h, ys, main, oh, d_skip.reshape(1, ws), w_glu, b_glu.reshape(1, ws), w_out)


def kernel(x, norm_mix, norm_ffn, norm_final, ab_w_in, ab_gate_bias, ab_head_norm, ab_conv_w, ab_w_out, cd_w_in, s5_lambda_re, s5_lambda_im, s5_b_re, s5_b_im, s5_c_re, s5_c_im, s5_d, s5_log_dt, s5_w_glu, s5_b_glu, hgrn_lb, hgrn_out_norm, cd_w_out, moe_w_group, moe_b_group, moe_w_router, moe_b_router, moe_w_gate, moe_w_up, moe_w_down):
    nb, seq, d = x.shape
    depth = norm_mix.shape[0]
    h = x.reshape(nb * seq, d)
    for layer in range(depth):
        j = layer // 2
        if layer % 2 == 0:
            wm = ab_head_norm.shape[1]
            ng = ab_gate_bias.shape[1]
            w_in = ab_w_in[j]
            w_main = jnp.concatenate([w_in[:, :4 * wm], w_in[:, 4 * wm + ng:]], axis=1).astype(BF16)
            w_gates = w_in[:, 4 * wm:4 * wm + ng].astype(BF16)
            main, g, gt = _proj(h, norm_mix[layer], w_main, w_gates)
            hm = _mlstm(main, g, gt, ab_gate_bias[j], ab_head_norm[j], nb, seq)
            h = _out_a(h, hm, main, ab_conv_w[j], ab_w_out[j].astype(BF16), seq)
        else:
            ws = s5_d.shape[1]
            w_in = cd_w_in[j]
            w_main = jnp.concatenate([w_in[:, ws:], w_in[:, :ws]], axis=1).astype(BF16)
            main = _proj(h, norm_mix[layer], w_main)
            sm = jax.nn.softmax(hgrn_lb.astype(F32), axis=0)
            lower_bound = jnp.cumsum(sm, axis=0)[layer] - sm[0]
            ys = _s5(main, main.shape[1] - ws, ws, nb, seq, s5_lambda_re[j], s5_lambda_im[j], s5_b_re[j], s5_b_im[j],
                     s5_c_re[j], s5_c_im[j], s5_log_dt[j])
            oh = _hgrn(main, lower_bound, hgrn_out_norm[j], nb, seq)
            h = _out_c(h, ys, main, oh, s5_d[j], s5_w_glu[j].astype(BF16), s5_b_glu[j], cd_w_out[j].astype(BF16))
        h = _moe(h, norm_ffn[layer], moe_w_group[layer], moe_b_group[layer], moe_w_router[layer], moe_b_router[layer],
                 moe_w_gate, moe_w_up, moe_w_down, layer,
                 norm_final if layer == depth - 1 else None)
    return h.reshape(nb, seq, d)
```

```python
import functools

import numpy as np
import jax
import jax.numpy as jnp
from jax import lax
from jax.experimental import pallas as pl
from jax.experimental.pallas import tpu as pltpu

F32 = jnp.float32
BF16 = jnp.bfloat16
RMS_EPS = 1e-6
MLSTM_CHUNK = 512
HGRN_CHUNK = 128
S5_CHUNK = 16
S5_GROUP_CH = 16
S5_STATE = 64
MLSTM_HEADS = 4
HGRN_HEADS = 4
MOE_GROUPS = 4
MOE_EPG = 8
ROUTE_LANES = 128
TM_PROJ = 512
TM_MOE = 512
TM_EXPERT = 512
MOE_TOP_K = 2
VMEM_LIMIT = 56 * 1024 * 1024

_NT = (((1,), (1,)), ((), ()))
_TN = (((0,), (0,)), ((), ()))


def _cparams(sem):
    return pltpu.CompilerParams(dimension_semantics=sem, vmem_limit_bytes=VMEM_LIMIT)


def _rms(x, w):
    return x * lax.rsqrt(jnp.mean(x * x, axis=-1, keepdims=True) + RMS_EPS) * w


def _split3(x):
    hi = x.astype(BF16)
    r = x - hi.astype(F32)
    mid = r.astype(BF16)
    lo = (r - mid.astype(F32)).astype(BF16)
    return hi, mid, lo


def _dot(a, b):
    return jnp.dot(a, b, preferred_element_type=F32)


def _dot_nt(a, b):
    return lax.dot_general(a, b, _NT, preferred_element_type=F32)


def _dot_tn(a, b):
    return lax.dot_general(a, b, _TN, preferred_element_type=F32)


def _proj_gates_body(x_ref, nw_ref, w_ref, wg_ref, wgt_ref, main_ref, g_ref, gt_ref):
    xn = _rms(x_ref[...], nw_ref[...]).astype(BF16)
    main_ref[...] = _dot(xn, w_ref[...])
    g_ref[...] = _dot(xn, wg_ref[...])[:, : g_ref.shape[1]]
    gt_ref[...] = _dot_nt(wgt_ref[...], xn)


def _proj_body(x_ref, nw_ref, w_ref, main_ref):
    xn = _rms(x_ref[...], nw_ref[...]).astype(BF16)
    main_ref[...] = _dot(xn, w_ref[...])


def _proj(h, nw, w_main, w_gates=None):
    t, d = h.shape
    n = w_main.shape[1]
    tm = TM_PROJ
    x_spec = pl.BlockSpec((tm, d), lambda i: (i, 0))
    nw_spec = pl.BlockSpec((1, d), lambda i: (0, 0))
    w_spec = pl.BlockSpec((d, n), lambda i: (0, 0))
    main_spec = pl.BlockSpec((tm, n), lambda i: (i, 0))
    main_shape = jax.ShapeDtypeStruct((t, n), F32)
    if w_gates is None:
        return pl.pallas_call(
            _proj_body, grid=(t // tm,), in_specs=[x_spec, nw_spec, w_spec], out_specs=main_spec,
            out_shape=main_shape, compiler_params=_cparams(("parallel",)), name="proj",
        )(h, nw.reshape(1, d), w_main)
    ng = w_gates.shape[1]
    wg_pad = jnp.zeros((d, 128), BF16).at[:, :ng].set(w_gates)
    return pl.pallas_call(
        _proj_gates_body, grid=(t // tm,),
        in_specs=[x_spec, nw_spec, w_spec, pl.BlockSpec((d, 128), lambda i: (0, 0)),
                  pl.BlockSpec((ng, d), lambda i: (0, 0))],
        out_specs=[main_spec, pl.BlockSpec((tm, ng), lambda i: (i, 0)), pl.BlockSpec((ng, tm), lambda i: (0, i))],
        out_shape=[main_shape, jax.ShapeDtypeStruct((t, ng), F32), jax.ShapeDtypeStruct((ng, t), F32)],
        compiler_params=_cparams(("parallel",)), name="proj_gates",
    )(h, nw.reshape(1, d), w_main, wg_pad, w_gates.T)


def _mlstm_body(main_ref, g_ref, gt_ref, br_ref, bc_ref, hn_ref, out_ref, c_ref, n_ref, m_ref, *, nb, nh, dh):
    L = MLSTM_CHUNK
    w = nh * dh

    @pl.when(pl.program_id(0) == 0)
    def _init():
        c_ref[...] = jnp.zeros_like(c_ref)
        n_ref[...] = jnp.zeros_like(n_ref)
        m_ref[...] = jnp.full_like(m_ref, -1e30)

    row = lax.broadcasted_iota(jnp.int32, (L, L), 0)
    col = lax.broadcasted_iota(jnp.int32, (L, L), 1)
    causal = col <= row
    tril = causal.astype(BF16)
    triu = (row <= col).astype(BF16)
    scale = dh ** -0.5
    for b in range(nb):
        g = g_ref[b] + br_ref[...]
        gt = gt_ref[b, 0] + bc_ref[...]
        i_c = g[:, :nh]
        i_r = gt[:nh, :]
        lfc = _split3(jax.nn.log_sigmoid(g[:, nh:]))
        lfr = _split3(jax.nn.log_sigmoid(gt[nh:, :]))
        bc_all = _dot(tril, lfc[0]) + _dot(tril, lfc[1]) + _dot(tril, lfc[2])
        br_all = _dot(lfr[0], triu) + _dot(lfr[1], triu) + _dot(lfr[2], triu)
        for h in range(nh):
            idx = b * nh + h
            q = main_ref[b, :, h * dh:(h + 1) * dh]
            k = main_ref[b, :, w + h * dh:w + (h + 1) * dh] * scale
            v = main_ref[b, :, 2 * w + h * dh:2 * w + (h + 1) * dh]
            o = main_ref[b, :, 3 * w + h * dh:3 * w + (h + 1) * dh]
            bc = bc_all[:, h:h + 1]
            br = br_all[h:h + 1, :]
            ir = i_r[h:h + 1, :]
            ic = i_c[:, h:h + 1]
            m_prev = m_ref[idx]
            c_prev = c_ref[idx]
            n_prev = n_ref[idx]
            logw = jnp.where(causal, bc - br + ir, -jnp.inf)
            inter = bc + m_prev
            m_row = jnp.maximum(jnp.max(logw, axis=-1, keepdims=True), inter)
            qb = q.astype(BF16)
            kb = k.astype(BF16)
            vb = v.astype(BF16)
            s = _dot_nt(qb, kb) * jnp.exp(logw - m_row)
            isc = jnp.exp(inter - m_row)
            num = _dot(s.astype(BF16), vb) + isc * _dot_nt(qb, c_prev.astype(BF16))
            den = jnp.sum(s, axis=-1, keepdims=True) + isc * jnp.sum(q * n_prev, axis=-1, keepdims=True)
            hout = num / jnp.maximum(jnp.abs(den), jnp.exp(-m_row))
            b_end = bc[L - 1:L, :]
            logg = b_end - bc + ic
            m_new = jnp.maximum(b_end + m_prev, jnp.max(logg, axis=0, keepdims=True))
            wk = jnp.exp(logg - m_new)
            decay = jnp.exp(b_end + m_prev - m_new)
            c_ref[idx] = decay * c_prev + _dot_tn((v * wk).astype(BF16), kb)
            n_ref[idx] = decay * n_prev + jnp.sum(wk * k, axis=0, keepdims=True)
            m_ref[idx] = m_new
            hn = hout * lax.rsqrt(jnp.mean(hout * hout, axis=-1, keepdims=True) + RMS_EPS)
            out_ref[b, :, h * dh:(h + 1) * dh] = hn * hn_ref[:, h * dh:(h + 1) * dh] * jax.nn.sigmoid(o)


def _mlstm(main, g, gt, gate_bias, head_norm, nb, seq):
    t, n = main.shape
    nh = MLSTM_HEADS
    w = head_norm.shape[0]
    dh = w // nh
    L = MLSTM_CHUNK
    nc = seq // L
    main3 = main.reshape(nb, seq, n)
    g3 = g.reshape(nb, seq, 2 * nh)
    gt4 = gt.reshape(2 * nh, nb, nc, L).transpose(1, 2, 0, 3)
    body = functools.partial(_mlstm_body, nb=nb, nh=nh, dh=dh)
    out = pl.pallas_call(
        body, grid=(nc,),
        in_specs=[pl.BlockSpec((nb, L, 4 * w), lambda c: (0, c, 0)),
                  pl.BlockSpec((nb, L, 2 * nh), lambda c: (0, c, 0)),
                  pl.BlockSpec((nb, 1, 2 * nh, L), lambda c: (0, c, 0, 0)),
                  pl.BlockSpec((1, 2 * nh), lambda c: (0, 0)),
                  pl.BlockSpec((2 * nh, 1), lambda c: (0, 0)),
                  pl.BlockSpec((1, w), lambda c: (0, 0))],
        out_specs=pl.BlockSpec((nb, L, w), lambda c: (0, c, 0)),
        out_shape=jax.ShapeDtypeStruct((nb, seq, w), F32),
        scratch_shapes=[pltpu.VMEM((nb * nh, dh, dh), F32), pltpu.VMEM((nb * nh, 1, dh), F32),
                        pltpu.VMEM((nb * nh, 1, 1), F32)],
        compiler_params=_cparams(("arbitrary",)), name="mlstm",
    )(main3, g3, gt4, gate_bias.reshape(1, 2 * nh), gate_bias.reshape(2 * nh, 1), head_norm.reshape(1, w))
    return out.reshape(t, w)


def _out_a_body(h_ref, hm_ref, gb_ref, gc_ref, xin_ref, pgc_ref, pxin_ref, cw_ref, w_ref, out_ref, *, tm, seq):
    i = pl.program_id(0)
    wm = hm_ref.shape[1]
    p = gc_ref[...] * xin_ref[...]
    first = (i * tm) % seq == 0
    pp = jnp.where(first, 0.0, pgc_ref[...] * pxin_ref[...])
    rowi = lax.broadcasted_iota(jnp.int32, p.shape, 0)
    p1 = jnp.where(rowi == 0, pp[7:8, :], pltpu.roll(p, 1, axis=0))
    p2 = jnp.where(rowi == 0, pp[6:7, :], jnp.where(rowi == 1, pp[7:8, :], pltpu.roll(p, 2, axis=0)))
    yc = gb_ref[...] * (cw_ref[0:1, :] * p2 + cw_ref[1:2, :] * p1 + cw_ref[2:3, :] * p)
    out_ref[...] = (h_ref[...] + _dot(hm_ref[...].astype(BF16), w_ref[:wm, :])
                    + _dot(yc.astype(BF16), w_ref[wm:, :]))


def _out_a(h, hm, main, conv_w, w_out, seq):
    t, d = h.shape
    wm = hm.shape[1]
    wc = conv_w.shape[1]
    tm = TM_PROJ
    cb = (4 * wm) // wc
    rb = tm // 8
    prev = lambda i: jnp.maximum(i * rb - 1, 0)
    body = functools.partial(_out_a_body, tm=tm, seq=seq)
    return pl.pallas_call(
        body, grid=(t // tm,),
        in_specs=[pl.BlockSpec((tm, d), lambda i: (i, 0)),
                  pl.BlockSpec((tm, wm), lambda i: (i, 0)),
                  pl.BlockSpec((tm, wc), lambda i: (i, cb)),
                  pl.BlockSpec((tm, wc), lambda i: (i, cb + 1)),
                  pl.BlockSpec((tm, wc), lambda i: (i, cb + 2)),
                  pl.BlockSpec((8, wc), lambda i: (prev(i), cb + 1)),
                  pl.BlockSpec((8, wc), lambda i: (prev(i), cb + 2)),
                  pl.BlockSpec(conv_w.shape, lambda i: (0, 0)),
                  pl.BlockSpec(w_out.shape, lambda i: (0, 0))],
        out_specs=pl.BlockSpec((tm, d), lambda i: (i, 0)),
        out_shape=jax.ShapeDtypeStruct((t, d), F32),
        compiler_params=_cparams(("parallel",)), name="out_a",
    )(h, hm, main, main, main, main, main, conv_w, w_out)


def _route(logits, lane):
    ne = MOE_GROUPS * MOE_EPG
    big = 1e9
    gl = jnp.where((lane >= ne) & (lane < ne + MOE_GROUPS), logits, -jnp.inf)
    gmax = jnp.max(gl, axis=-1, keepdims=True)
    gidx = jnp.min(jnp.where(gl == gmax, lane - ne, big), axis=-1, keepdims=True)
    gval = 1.0 / jnp.sum(jnp.exp(gl - gmax), axis=-1, keepdims=True)
    lo = gidx * MOE_EPG
    sel = jnp.where((lane >= lo) & (lane < lo + MOE_EPG), logits, -jnp.inf)
    l1 = jnp.max(sel, axis=-1, keepdims=True)
    i1 = jnp.min(jnp.where(sel == l1, lane, big), axis=-1, keepdims=True)
    sel2 = jnp.where(lane == i1, -jnp.inf, sel)
    l2 = jnp.max(sel2, axis=-1, keepdims=True)
    i2 = jnp.min(jnp.where(sel2 == l2, lane, big), axis=-1, keepdims=True)
    r = jnp.exp(l2 - l1)
    w1 = gval / (1.0 + r)
    return i1, i2, w1, w1 * r


ROW_TILE = 8


def _rows_to_tiles(ref, x):
    n = x.shape[0]
    for c in range(ROW_TILE):
        ref[pl.ds(c, n, stride=ROW_TILE), :] = x[:, c * 128:(c + 1) * 128]


def _tiles_to_rows(ref, n):
    return jnp.concatenate([ref[pl.ds(c, n, stride=ROW_TILE), :] for c in range(ROW_TILE)], axis=1)


def _lane_put(lane, cols):
    out = jnp.where(lane == 0.0, cols[0], 0.0)
    for k in range(1, len(cols)):
        out = out + jnp.where(lane == float(k), cols[k], 0.0)
    return out


def _lane_get(lane, x, idx_col):
    return jnp.sum(jnp.where(lane == idx_col, x, 0.0), axis=-1, keepdims=True)


def _route_body(h_ref, nw_ref, wr_ref, br_ref, tri_ref, info_ref, cnt_ref, tile_ref):
    @pl.when(pl.program_id(0) == 0)
    def _init():
        cnt_ref[...] = jnp.zeros_like(cnt_ref)

    xn = _rms(h_ref[...], nw_ref[...])
    hi = xn.astype(BF16)
    lo = (xn - hi.astype(F32)).astype(BF16)
    logits = _dot(jnp.concatenate([hi, lo, hi], axis=1), wr_ref[...]) + br_ref[...]
    lane = lax.broadcasted_iota(jnp.int32, logits.shape, 1).astype(F32)
    i1, i2, w1, w2 = _route(logits, lane)
    ind = jnp.where((lane == i1) | (lane == i2), 1.0, 0.0)
    ahead = _dot(tri_ref[...], ind.astype(BF16))
    before = ahead + cnt_ref[0:1, :]
    tcnt = jnp.broadcast_to(jnp.sum(ind, axis=0, keepdims=True), cnt_ref.shape)
    lane8 = lax.broadcasted_iota(jnp.int32, cnt_ref.shape, 1)
    sub8 = lax.broadcasted_iota(jnp.int32, cnt_ref.shape, 0)
    incl = tcnt
    sh = 1
    while sh < cnt_ref.shape[1]:
        incl = incl + jnp.where(lane8 >= sh, pltpu.roll(incl, sh, axis=1), 0.0)
        sh *= 2
    toff = incl - tcnt
    local = ahead + toff[0:1, :]
    info_ref[...] = _lane_put(lane, [w1, w2, i1, i2, _lane_get(lane, before, i1), _lane_get(lane, before, i2),
                                     _lane_get(lane, local, i1), _lane_get(lane, local, i2)])
    tile_ref[...] = jnp.where(sub8 == 0, tcnt, jnp.where(sub8 == 1, toff, 0.0))
    cnt_ref[0:1, :] = cnt_ref[0:1, :] + tcnt[0:1, :]


def _dispatch_body(cnt_s, off_s, tot_s, h_ref, nw_ref, info_ref, offrow_ref, sel_ref, runs_s, xs_ref, pos_ref,
                   xn_buf, blk, pos_v, pos_s, zbuf, sem, *, tm, tile_rows, ne, n_tiles):
    i = pl.program_id(0)
    last = pl.num_programs(0) - 1
    slots = MOE_TOP_K * tm * ROW_TILE

    def wait_runs(half):
        pltpu.make_async_copy(blk.at[half], xs_ref.at[pl.ds(0, slots)], sem.at[half]).wait()

    for half in range(2):
        rs = slice(half * tm, (half + 1) * tm)

        @pl.when(i > 0)
        def _drain_previous(half=half):
            wait_runs(half)

        info = info_ref[rs, :]
        lane = lax.broadcasted_iota(jnp.int32, info.shape, 1).astype(F32)
        offrow = offrow_ref[...]
        vals = [_lane_get(lane, offrow, info[:, 2:3]) + info[:, 4:5],
                _lane_get(lane, offrow, info[:, 3:4]) + info[:, 5:6],
                info[:, 6:7], info[:, 7:8]]
        cols = []
        for v in vals:
            hi = jnp.floor(v * (1.0 / 256.0))
            cols += [hi, v - 256.0 * hi]
        rows = _dot_nt(sel_ref[...], _lane_put(lane, cols).astype(BF16))
        sub = lax.broadcasted_iota(jnp.int32, rows.shape, 0)
        posall = jnp.zeros(rows.shape, F32)
        for k in range(len(vals)):
            posall = jnp.where(sub == k, rows[2 * k:2 * k + 1, :] * 256.0 + rows[2 * k + 1:2 * k + 2, :], posall)
        posall = posall.astype(jnp.int32)
        pos_ref[:, rs] = posall
        pos_v[...] = posall
        cp = pltpu.make_async_copy(pos_v, pos_s, sem.at[2])
        cp.start()
        _rows_to_tiles(xn_buf, _rms(h_ref[rs, :], nw_ref[...]))
        cp.wait()

        def place(t, carry, half=half):
            row = xn_buf[pl.ds(pl.multiple_of(t * ROW_TILE, ROW_TILE), ROW_TILE), :]
            for k in range(MOE_TOP_K):
                blk[half, pl.ds(pl.multiple_of(pos_s[MOE_TOP_K + k, t] * ROW_TILE, ROW_TILE), ROW_TILE), :] = row
            return carry

        lax.fori_loop(0, tm, place, 0, unroll=8)
        for e in range(ne):
            n = runs_s[half, 2, e] * ROW_TILE

            @pl.when(n > 0)
            def _send(e=e, n=n, half=half):
                src = pl.multiple_of(runs_s[half, 0, e] * ROW_TILE, ROW_TILE)
                dst = pl.multiple_of(runs_s[half, 1, e] * ROW_TILE, ROW_TILE)
                pltpu.make_async_copy(blk.at[half, pl.ds(src, n)], xs_ref.at[pl.ds(dst, n)], sem.at[half]
                                      ).start(priority=e % 2)

    @pl.when(i == last)
    def _zero_unused_rows():
        wait_runs(0)
        wait_runs(1)
        zbuf[...] = jnp.zeros_like(zbuf)

        def fill(row, nrows):
            at = pl.multiple_of(row * ROW_TILE, ROW_TILE)
            n = nrows * ROW_TILE
            c = pltpu.make_async_copy(zbuf.at[pl.ds(0, n)], xs_ref.at[pl.ds(at, n)], sem.at[2])
            c.start()
            c.wait()

        for e in range(ne):
            n_pad = (tile_rows - cnt_s[e] % tile_rows) % tile_rows

            @pl.when(n_pad > 0)
            def _fill(e=e, n_pad=n_pad):
                fill(off_s[e] + cnt_s[e], n_pad)

        def zero_tile(j, carry):
            fill(j * tile_rows, tile_rows)
            return carry

        lax.fori_loop(tot_s[0], n_tiles, zero_tile, 0)


def _expert_body(te_s, blk_s, tot_s, xs_ref, wg_ref, wu_ref, wd_ref, ys_ref):
    valid = pl.program_id(0) < tot_s[0]
    rows = xs_ref.shape[0] // ROW_TILE

    @pl.when(valid)
    def _run():
        x = _tiles_to_rows(xs_ref, rows).astype(BF16)
        hid = jax.nn.silu(_dot(x, wg_ref[0, 0].astype(BF16))) * _dot(x, wu_ref[0, 0].astype(BF16))
        _rows_to_tiles(ys_ref, _dot(hid.astype(BF16), wd_ref[0, 0].astype(BF16)))

    @pl.when(jnp.logical_not(valid))
    def _unused_tile():
        ys_ref[...] = jnp.zeros_like(ys_ref)


def _combine_body(h_ref, info_ref, pos_s, runs_s, next_s, ys_ref, fw_ref, out_ref, blk, y1, y2, sem, *,
                  tm, ne, final):
    i = pl.program_id(0)
    slots = MOE_TOP_K * tm * ROW_TILE
    bufs = (y1, y2)

    def fetch(runs, half):
        for e in range(ne):
            n = runs[half, 2, e] * ROW_TILE

            @pl.when(n > 0)
            def _fetch(e=e, n=n):
                dst = pl.multiple_of(runs[half, 0, e] * ROW_TILE, ROW_TILE)
                src = pl.multiple_of(runs[half, 1, e] * ROW_TILE, ROW_TILE)
                pltpu.make_async_copy(ys_ref.at[pl.ds(src, n)], blk.at[half, pl.ds(dst, n)], sem.at[half]
                                      ).start(priority=e % 2)

    def finish(half):
        pltpu.make_async_copy(ys_ref.at[pl.ds(0, slots)], blk.at[half], sem.at[half]).wait()
        rs = slice(half * tm, (half + 1) * tm)

        def pick(t, carry):
            dst = pl.ds(pl.multiple_of(t * ROW_TILE, ROW_TILE), ROW_TILE)
            for k in range(MOE_TOP_K):
                at = pl.multiple_of(pos_s[MOE_TOP_K + k, half * tm + t] * ROW_TILE, ROW_TILE)
                bufs[k][dst, :] = blk[half, pl.ds(at, ROW_TILE), :]
            return carry

        lax.fori_loop(0, tm, pick, 0, unroll=8)
        o = (h_ref[rs, :] + info_ref[rs, 0:1] * _tiles_to_rows(y1, tm) + info_ref[rs, 1:2] * _tiles_to_rows(y2, tm))
        if final:
            o = _rms(o, fw_ref[...])
        out_ref[rs, :] = o

    @pl.when(i == 0)
    def _first():
        fetch(runs_s, 0)

    fetch(runs_s, 1)
    finish(0)

    @pl.when(i < pl.num_programs(0) - 1)
    def _prefetch():
        fetch(next_s, 0)

    finish(1)


def _router_weights(w_group, b_group, w_router, b_router):
    d, ne = w_router.shape
    ng = w_group.shape[1]
    w = jnp.zeros((d, ROUTE_LANES), F32).at[:, :ne].set(w_router).at[:, ne:ne + ng].set(w_group)
    hi = w.astype(BF16)
    lo = (w - hi.astype(F32)).astype(BF16)
    bias = jnp.zeros((1, ROUTE_LANES), F32).at[0, :ne].set(b_router).at[0, ne:ne + ng].set(b_group)
    return jnp.concatenate([hi, hi, lo], axis=0), bias


def _moe(h, nw, w_group, b_group, w_router, b_router, w_gate, w_up, w_down, layer, final_w):
    t, d = h.shape
    assert d == ROW_TILE * 128, "row-as-tile layout needs d_model == 1024"
    _, ne, _, ff = w_gate.shape
    tm = TM_MOE
    te_rows = TM_EXPERT
    nw2 = nw.reshape(1, d)
    wr3, bias = _router_weights(w_group, b_group, w_router, b_router)
    tri = jnp.asarray(np.tril(np.ones((tm, tm), np.float32), -1), BF16)
    nt = t // tm
    info, cnt, tile_info = pl.pallas_call(
        _route_body, grid=(nt,),
        in_specs=[pl.BlockSpec((tm, d), lambda i: (i, 0)),
                  pl.BlockSpec((1, d), lambda i: (0, 0)),
                  pl.BlockSpec((3 * d, ROUTE_LANES), lambda i: (0, 0)),
                  pl.BlockSpec((1, ROUTE_LANES), lambda i: (0, 0)),
                  pl.BlockSpec((tm, tm), lambda i: (0, 0))],
        out_specs=[pl.BlockSpec((tm, ROUTE_LANES), lambda i: (i, 0)), pl.BlockSpec((8, ROUTE_LANES), lambda i: (0, 0)),
                   pl.BlockSpec((8, ROUTE_LANES), lambda i: (i, 0))],
        out_shape=[jax.ShapeDtypeStruct((t, ROUTE_LANES), F32), jax.ShapeDtypeStruct((8, ROUTE_LANES), F32),
                   jax.ShapeDtypeStruct((nt * 8, ROUTE_LANES), F32)],
        compiler_params=_cparams(("arbitrary",)), name="moe_route",
    )(h, nw2, wr3, bias, tri)

    cnt_i = cnt[0, :ne].astype(jnp.int32)
    ntile = (cnt_i + te_rows - 1) // te_rows
    tile_end = jnp.cumsum(ntile)
    off_i = (tile_end - ntile) * te_rows
    n_tiles = (MOE_TOP_K * t) // te_rows + ne
    rows_total = n_tiles * te_rows
    ti = jnp.arange(n_tiles, dtype=jnp.int32)
    tot = tile_end[-1:]
    ti_c = jnp.minimum(ti, tot[0] - 1)
    tile_e = jnp.sum((ti_c[:, None] >= tile_end[None, :]).astype(jnp.int32), axis=1)
    off_row = jnp.zeros((1, ROUTE_LANES), F32).at[0, :ne].set(off_i.astype(F32))
    sel = jnp.asarray(np.eye(8, ROUTE_LANES, dtype=np.float32), BF16)
    tinfo = tile_info.reshape(nt, 8, ROUTE_LANES).astype(jnp.int32)
    tcnt = tinfo[:, 0, :]
    gstart = off_row.astype(jnp.int32) + jnp.cumsum(tcnt, axis=0) - tcnt
    runs = jnp.zeros((nt, 8, ROUTE_LANES), jnp.int32).at[:, 0].set(tinfo[:, 1, :]).at[:, 1].set(gstart).at[:, 2].set(tcnt)

    xs, pos = pl.pallas_call(
        functools.partial(_dispatch_body, tm=tm, tile_rows=te_rows, ne=ne, n_tiles=n_tiles),
        grid_spec=pltpu.PrefetchScalarGridSpec(
            num_scalar_prefetch=3, grid=(nt // 2,),
            in_specs=[pl.BlockSpec((2 * tm, d), lambda i, *_: (i, 0)),
                      pl.BlockSpec((1, d), lambda i, *_: (0, 0)),
                      pl.BlockSpec((2 * tm, ROUTE_LANES), lambda i, *_: (i, 0)),
                      pl.BlockSpec((1, ROUTE_LANES), lambda i, *_: (0, 0)),
                      pl.BlockSpec((8, ROUTE_LANES), lambda i, *_: (0, 0)),
                      pl.BlockSpec((2, 8, ROUTE_LANES), lambda i, *_: (i, 0, 0), memory_space=pltpu.SMEM)],
            out_specs=[pl.BlockSpec(memory_space=pl.ANY), pl.BlockSpec((8, 2 * tm), lambda i, *_: (0, i))],
            scratch_shapes=[pltpu.VMEM((tm * ROW_TILE, 128), F32),
                            pltpu.VMEM((2, MOE_TOP_K * tm * ROW_TILE, 128), F32),
                            pltpu.VMEM((8, tm), jnp.int32), pltpu.SMEM((8, tm), jnp.int32),
                            pltpu.VMEM((te_rows * ROW_TILE, 128), F32), pltpu.SemaphoreType.DMA((3,))]),
        out_shape=[jax.ShapeDtypeStruct((rows_total * ROW_TILE, 128), F32), jax.ShapeDtypeStruct((8, t), jnp.int32)],
        compiler_params=_cparams(("arbitrary",)), name="moe_dispatch",
    )(cnt_i, off_i, tot, h, nw2, info, off_row, sel, runs)

    ys = pl.pallas_call(
        _expert_body,
        grid_spec=pltpu.PrefetchScalarGridSpec(
            num_scalar_prefetch=3, grid=(n_tiles,),
            in_specs=[pl.BlockSpec((te_rows * ROW_TILE, 128), lambda i, e, b, v: (b[i], 0)),
                      pl.BlockSpec((1, 1, d, ff), lambda i, e, b, v: (layer, e[i], 0, 0)),
                      pl.BlockSpec((1, 1, d, ff), lambda i, e, b, v: (layer, e[i], 0, 0)),
                      pl.BlockSpec((1, 1, ff, d), lambda i, e, b, v: (layer, e[i], 0, 0))],
            out_specs=pl.BlockSpec((te_rows * ROW_TILE, 128), lambda i, e, b, v: (i, 0))),
        out_shape=jax.ShapeDtypeStruct((rows_total * ROW_TILE, 128), F32),
        compiler_params=_cparams(("arbitrary",)), name="moe_expert",
    )(tile_e, ti_c, tot, xs, w_gate, w_up, w_down)

    final = final_w is not None
    fw = (final_w if final else nw).reshape(1, d)
    return pl.pallas_call(
        functools.partial(_combine_body, tm=tm, ne=ne, final=final), grid=(nt // 2,),
        in_specs=[pl.BlockSpec((2 * tm, d), lambda i: (i, 0)),
                  pl.BlockSpec((2 * tm, ROUTE_LANES), lambda i: (i, 0)),
                  pl.BlockSpec((8, 2 * tm), lambda i: (0, i), memory_space=pltpu.SMEM),
                  pl.BlockSpec((2, 8, ROUTE_LANES), lambda i: (i, 0, 0), memory_space=pltpu.SMEM),
                  pl.BlockSpec((2, 8, ROUTE_LANES), lambda i: (jnp.minimum(i + 1, nt // 2 - 1), 0, 0),
                               memory_space=pltpu.SMEM),
                  pl.BlockSpec(memory_space=pl.ANY),
                  pl.BlockSpec((1, d), lambda i: (0, 0))],
        out_specs=pl.BlockSpec((2 * tm, d), lambda i: (i, 0)),
        out_shape=jax.ShapeDtypeStruct((t, d), F32),
        scratch_shapes=[pltpu.VMEM((2, MOE_TOP_K * tm * ROW_TILE, 128), F32), pltpu.VMEM((tm * ROW_TILE, 128), F32),
                        pltpu.VMEM((tm * ROW_TILE, 128), F32), pltpu.SemaphoreType.DMA((2,))],
        compiler_params=_cparams(("arbitrary",)), name="moe_combine",
    )(h, info, pos, runs, runs, ys, fw)


def _s5_weights(lam_re, lam_im, b_re, b_im, c_re, c_im, log_dt, nsteps):
    hp = lax.Precision.HIGHEST
    L = S5_CHUNK
    g, p = lam_re.shape
    ch = b_re.shape[-1]
    lam = lax.complex(lam_re.astype(F32), lam_im.astype(F32))
    dt = jnp.exp(log_dt.astype(F32))[:, None]
    lam_bar = jnp.exp(lam * dt)
    b_bar = ((lam_bar - 1.0) / lam)[..., None] * lax.complex(b_re.astype(F32), b_im.astype(F32))
    cmat = lax.complex(c_re.astype(F32), c_im.astype(F32))
    pows = [jnp.ones_like(lam_bar)]
    for _ in range(L):
        pows.append(pows[-1] * lam_bar)
    pw = jnp.stack(pows, axis=1)
    kern = jnp.real(jnp.einsum('gop,gtp,gpi->gito', cmat, pw[:, :L], b_bar, precision=hp))
    kpad = jnp.pad(kern.reshape(g, ch, L * ch), ((0, 0), (0, 0), ((L - 1) * ch, 0)))
    toep = jnp.stack([kpad[:, :, (L - 1 - s) * ch:(2 * L - 1 - s) * ch] for s in range(L)], axis=1)
    toep = toep.reshape(g, L * ch, L * ch)
    wst = pw[:, L - 1 - np.arange(L)][:, :, :, None] * b_bar[:, None, :, :]
    wst = wst.transpose(0, 1, 3, 2).reshape(g, L * ch, p)
    wst = jnp.concatenate([jnp.real(wst), jnp.imag(wst)], axis=-1)
    mo = cmat.transpose(0, 2, 1)[:, :, None, :] * pw[:, 1:L + 1].transpose(0, 2, 1)[:, :, :, None]
    mo = mo.reshape(g, p, L * ch)
    wout = jnp.concatenate([jnp.real(mo), -jnp.imag(mo)], axis=1)
    a = pw[:, L]
    ars, ais = [], []
    for _ in range(nsteps):
        ars.append(jnp.concatenate([jnp.real(a), jnp.real(a)], axis=-1))
        ais.append(jnp.concatenate([-jnp.imag(a), jnp.imag(a)], axis=-1))
        a = a * a
    wcat = jnp.concatenate([toep, wst], axis=-1).astype(BF16)
    return wcat, wout.astype(BF16), jnp.stack(ars, axis=1), jnp.stack(ais, axis=1)


def _s5_body(u_ref, wcat_ref, wout_ref, ar_ref, ai_ref, y_ref, us_ref, ys_ref, *, nsteps):
    L = S5_CHUNK
    ch = S5_GROUP_CH
    gpc = 128 // ch
    ny = L * ch
    nc = u_ref.shape[0] // L
    for s in range(L):
        us_ref[s] = u_ref[pl.ds(s, nc, stride=L), :]
    lane = lax.broadcasted_iota(jnp.int32, (nc, 128), 1)
    ridx = lax.broadcasted_iota(jnp.int32, (nc, 128), 0)

    def shift(x, k):
        return jnp.where(ridx >= k, pltpu.roll(x, k, axis=0), 0.0)

    for gi in range(gpc):
        halves = []
        for hh in range(ny // 128):
            acc = None
            for s8 in range(gpc):
                rot = ((s8 - gi) * ch) % 128
                src = us_ref[hh * gpc + s8]
                if rot:
                    src = pltpu.roll(src, rot, axis=1)
                slot = (lane >= s8 * ch) & (lane < (s8 + 1) * ch)
                acc = jnp.where(slot, src, 0.0) if acc is None else jnp.where(slot, src, acc)
            halves.append(acc)
        ug = jnp.concatenate(halves, axis=1).astype(BF16)
        r = _dot(ug, wcat_ref[gi])
        y1 = r[:, :ny]
        z = r[:, ny:]
        w = shift(z, 1)
        wx = pltpu.roll(w, z.shape[1] // 2, axis=1)
        for k in range(nsteps):
            if (1 << k) >= nc:
                break
            sk = shift(w, 1 << k)
            sx = shift(wx, 1 << k)
            ar = ar_ref[gi, k:k + 1, :]
            ai = ai_ref[gi, k:k + 1, :]
            w, wx = w + sk * ar + sx * ai, wx + sx * ar - sk * ai
        yg = y1 + _dot(w.astype(BF16), wout_ref[gi])
        slot = (lane >= gi * ch) & (lane < (gi + 1) * ch)
        for t in range(L):
            src = yg[:, (t // gpc) * 128:(t // gpc + 1) * 128]
            rot = ((gi - t % gpc) * ch) % 128
            if rot:
                src = pltpu.roll(src, rot, axis=1)
            ys_ref[t] = jnp.where(slot, src, 0.0) if gi == 0 else jnp.where(slot, src, ys_ref[t])
    for t in range(L):
        y_ref[pl.ds(t, nc, stride=L), :] = ys_ref[t]


def _s5(main, col0, width, nb, seq, lam_re, lam_im, b_re, b_im, c_re, c_im, log_dt):
    t = main.shape[0]
    L = S5_CHUNK
    ch = S5_GROUP_CH
    g = width // ch
    nc = seq // L
    gpc = 128 // ch
    ncol = width // 128
    assert col0 % 128 == 0 and width % 128 == 0 and (L * ch) % 128 == 0
    nsteps = max(1, (nc - 1).bit_length())
    wcat, wout, ar, ai = _s5_weights(lam_re, lam_im, b_re, b_im, c_re, c_im, log_dt, nsteps)
    body = functools.partial(_s5_body, nsteps=nsteps)
    return pl.pallas_call(
        body, grid=(ncol, nb),
        in_specs=[pl.BlockSpec((seq, 128), lambda j, b: (b, col0 // 128 + j)),
                  pl.BlockSpec((gpc,) + wcat.shape[1:], lambda j, b: (j, 0, 0)),
                  pl.BlockSpec((gpc,) + wout.shape[1:], lambda j, b: (j, 0, 0)),
                  pl.BlockSpec((gpc,) + ar.shape[1:], lambda j, b: (j, 0, 0)),
                  pl.BlockSpec((gpc,) + ai.shape[1:], lambda j, b: (j, 0, 0))],
        out_specs=pl.BlockSpec((seq, 128), lambda j, b: (b, j)),
        out_shape=jax.ShapeDtypeStruct((t, width), F32),
        scratch_shapes=[pltpu.VMEM((L, nc, 128), F32), pltpu.VMEM((L, nc, 128), F32)],
        compiler_params=_cparams(("parallel", "parallel")), name="s5",
    )(main, wcat, wout, ar, ai)


def _hgrn_gmat():
    L = HGRN_CHUNK
    blocks = 2 + int(np.log2(L))
    gm = np.zeros((blocks * L, L), np.float32)
    for j in range(L):
        gm[j, :j + 1] = 1.0
        gm[L + j, j + 1:] = 1.0
    li, m = 2, L
    while m >= 2:
        half = m // 2
        for j in range(L):
            pos = j % m
            r = j - pos + half - 1
            if pos >= half:
                gm[li * L + j, r + 1:j + 1] = 1.0
            else:
                gm[li * L + j, j + 1:r + 1] = 1.0
        li += 1
        m //= 2
    return gm


def _hgrn_body(main_ref, gm_ref, lb_ref, nw_ref, out_ref, st_ref, *, nb, nh, dh):
    L = HGRN_CHUNK
    w = nh * dh

    @pl.when(pl.program_id(0) == 0)
    def _init():
        st_ref[...] = jnp.zeros_like(st_ref)

    row = lax.broadcasted_iota(jnp.int32, (L, 2 * L), 0)
    col = lax.broadcasted_iota(jnp.int32, (L, 2 * L), 1) & (L - 1)
    rowd = lax.broadcasted_iota(jnp.int32, (L, 2 * dh), 0)
    laned = lax.broadcasted_iota(jnp.int32, (L, 2 * dh), 1)
    first = laned < dh
    eye = row == col

    def blockdiag(x):
        z = jnp.zeros_like(x)
        return jnp.concatenate([jnp.where(first, x, z), jnp.where(first, z, x)], axis=0)

    gm2 = gm_ref[...]
    lb = lb_ref[...]
    zst = jnp.zeros((dh, dh), BF16)
    for b in range(nb):
        fg = main_ref[b, :, w:2 * w]
        f = lb + (1.0 - lb) * jax.nn.sigmoid(fg)
        kk = (1.0 - lb) * jax.nn.sigmoid(-fg)
        lf = jnp.log(f)
        hi = lf.astype(BF16)
        mid = (lf - hi.astype(F32)).astype(BF16)
        p_all = jnp.exp(_dot(gm2, jnp.concatenate([hi, mid], axis=0)))
        for hp in range(nh // 2):
            i0 = b * nh + 2 * hp
            cs = slice(2 * hp * dh, (2 * hp + 2) * dh)
            q = main_ref[b, :, 2 * hp * dh:(2 * hp + 2) * dh]
            v = main_ref[b, :, 2 * w + 2 * hp * dh:2 * w + (2 * hp + 2) * dh]
            og = main_ref[b, :, 3 * w + 2 * hp * dh:3 * w + (2 * hp + 2) * dh]
            k = kk[:, cs]
            pb = p_all[0:L, cs]
            pe = p_all[L:2 * L, cs]
            st0 = st_ref[i0]
            st1 = st_ref[i0 + 1]
            stbd = jnp.concatenate([jnp.concatenate([st0.astype(BF16), zst], axis=1),
                                    jnp.concatenate([zst, st1.astype(BF16)], axis=1)], axis=0)
            o = _dot_nt((q * pb).astype(BF16), stbd)
            attn = jnp.where(eye, _dot_nt(q.astype(BF16), blockdiag(k.astype(BF16))), 0.0)
            li, m = 2, L
            while m >= 2:
                pl_ = p_all[li * L:(li + 1) * L, cs]
                up = (rowd & (m - 1)) >= (m // 2)
                ql = jnp.where(up, q * pl_, 0.0).astype(BF16)
                kl = jnp.where(up, 0.0, k * pl_).astype(BF16)
                same = (row & ~(m - 1)) == (col & ~(m - 1))
                attn = attn + jnp.where(same, _dot_nt(ql, blockdiag(kl)), 0.0)
                li += 1
                m //= 2
            vb = v.astype(BF16)
            o = o + _dot(attn.astype(BF16), blockdiag(vb))
            kh = (k * pe).astype(BF16)
            for j in range(2):
                hs = slice(j * dh, (j + 1) * dh)
                gs = slice((2 * hp + j) * dh, (2 * hp + j + 1) * dh)
                st = st0 if j == 0 else st1
                st_ref[i0 + j] = st * pb[L - 1:L, hs] + _dot_tn(vb[:, hs], kh[:, hs])
                oj = o[:, hs]
                on = oj * lax.rsqrt(jnp.mean(oj * oj, axis=-1, keepdims=True) + RMS_EPS)
                out_ref[b, :, gs] = on * nw_ref[:, gs] * jax.nn.silu(og[:, hs])


def _hgrn(main, lower_bound, out_norm, nb, seq):
    t, n = main.shape
    nh = HGRN_HEADS
    w = out_norm.shape[0]
    dh = w // nh
    L = HGRN_CHUNK
    nc = seq // L
    gm = _hgrn_gmat()
    gm = jnp.asarray(np.concatenate([gm, gm], axis=1), BF16)
    body = functools.partial(_hgrn_body, nb=nb, nh=nh, dh=dh)
    out = pl.pallas_call(
        body, grid=(nc,),
        in_specs=[pl.BlockSpec((nb, L, 4 * w), lambda c: (0, c, 0)),
                  pl.BlockSpec(gm.shape, lambda c: (0, 0)),
                  pl.BlockSpec((1, w), lambda c: (0, 0)),
                  pl.BlockSpec((1, w), lambda c: (0, 0))],
        out_specs=pl.BlockSpec((nb, L, w), lambda c: (0, c, 0)),
        out_shape=jax.ShapeDtypeStruct((nb, seq, w), F32),
        scratch_shapes=[pltpu.VMEM((nb * nh, dh, dh), F32)],
        compiler_params=_cparams(("arbitrary",)), name="hgrn",
    )(main.reshape(nb, seq, n), gm, lower_bound.reshape(1, w), out_norm.reshape(1, w))
    return out.reshape(t, w)


def _out_c_body(h_ref, ys_ref, u_ref, oh_ref, d_ref, wglu_ref, bglu_ref, w_ref, out_ref):
    ws = ys_ref.shape[1]
    z = jax.nn.gelu(ys_ref[...] + d_ref[...] * u_ref[...])
    gate = jax.nn.sigmoid(_dot(z.astype(BF16), wglu_ref[...]) + bglu_ref[...])
    out_ref[...] = (h_ref[...] + _dot((z * gate).astype(BF16), w_ref[:ws, :])
                    + _dot(oh_ref[...].astype(BF16), w_ref[ws:, :]))


def _out_c(h, ys, main, oh, d_skip, w_glu, b_glu, w_out):
    t, d = h.shape
    ws = ys.shape[1]
    wh = oh.shape[1]
    tm = TM_PROJ
    ub = (main.shape[1] - ws) // ws
    return pl.pallas_call(
        _out_c_body, grid=(t // tm,),
        in_specs=[pl.BlockSpec((tm, d), lambda i: (i, 0)),
                  pl.BlockSpec((tm, ws), lambda i: (i, 0)),
                  pl.BlockSpec((tm, ws), lambda i: (i, ub)),
                  pl.BlockSpec((tm, wh), lambda i: (i, 0)),
                  pl.BlockSpec((1, ws), lambda i: (0, 0)),
                  pl.BlockSpec(w_glu.shape, lambda i: (0, 0)),
                  pl.BlockSpec((1, ws), lambda i: (0, 0)),
                  pl.BlockSpec(w_out.shape, lambda i: (0, 0))],
        out_specs=pl.BlockSpec((tm, d), lambda i: (i, 0)),
        out_shape=jax.ShapeDtypeStruct((t, d), F32),
        compiler_params=_cparams(("parallel",)), name="out_c",
    )(h, ys, main, oh, d_skip.reshape(1, ws), w_glu, b_glu.reshape(1, ws), w_out)


def kernel(x, norm_mix, norm_ffn, norm_final, ab_w_in, ab_gate_bias, ab_head_norm, ab_conv_w, ab_w_out, cd_w_in, s5_lambda_re, s5_lambda_im, s5_b_re, s5_b_im, s5_c_re, s5_c_im, s5_d, s5_log_dt, s5_w_glu, s5_b_glu, hgrn_lb, hgrn_out_norm, cd_w_out, moe_w_group, moe_b_group, moe_w_router, moe_b_router, moe_w_gate, moe_w_up, moe_w_down):
    nb, seq, d = x.shape
    depth = norm_mix.shape[0]
    h = x.reshape(nb * seq, d)
    for layer in range(depth):
        j = layer // 2
        if layer % 2 == 0:
            wm = ab_head_norm.shape[1]
            ng = ab_gate_bias.shape[1]
            w_in = ab_w_in[j]
            w_main = jnp.concatenate([w_in[:, :4 * wm], w_in[:, 4 * wm + ng:]], axis=1).astype(BF16)
            w_gates = w_in[:, 4 * wm:4 * wm + ng].astype(BF16)
            main, g, gt = _proj(h, norm_mix[layer], w_main, w_gates)
            hm = _mlstm(main, g, gt, ab_gate_bias[j], ab_head_norm[j], nb, seq)
            h = _out_a(h, hm, main, ab_conv_w[j], ab_w_out[j].astype(BF16), seq)
        else:
            ws = s5_d.shape[1]
            w_in = cd_w_in[j]
            w_main = jnp.concatenate([w_in[:, ws:], w_in[:, :ws]], axis=1).astype(BF16)
            main = _proj(h, norm_mix[layer], w_main)
            sm = jax.nn.softmax(hgrn_lb.astype(F32), axis=0)
            lower_bound = jnp.cumsum(sm, axis=0)[layer] - sm[0]
            ys = _s5(main, main.shape[1] - ws, ws, nb, seq, s5_lambda_re[j], s5_lambda_im[j], s5_b_re[j], s5_b_im[j],
                     s5_c_re[j], s5_c_im[j], s5_log_dt[j])
            oh = _hgrn(main, lower_bound, hgrn_out_norm[j], nb, seq)
            h = _out_c(h, ys, main, oh, s5_d[j], s5_w_glu[j].astype(BF16), s5_b_glu[j], cd_w_out[j].astype(BF16))
        h = _moe(h, norm_ffn[layer], moe_w_group[layer], moe_b_group[layer], moe_w_router[layer], moe_b_router[layer],
                 moe_w_gate, moe_w_up, moe_w_down, layer,
                 norm_final if layer == depth - 1 else None)
    return h.reshape(nb, seq, d)
```

```python
import functools

import numpy as np
import jax
import jax.numpy as jnp
from jax import lax
from jax.experimental import pallas as pl
from jax.experimental.pallas import tpu as pltpu

F32 = jnp.float32
BF16 = jnp.bfloat16
RMS_EPS = 1e-6
MLSTM_CHUNK = 512
HGRN_CHUNK = 128
S5_CHUNK = 16
S5_GROUP_CH = 16
S5_STATE = 64
MLSTM_HEADS = 4
HGRN_HEADS = 4
MOE_GROUPS = 4
MOE_EPG = 8
ROUTE_LANES = 128
TM_PROJ = 512
TM_MOE = 512
TM_EXPERT = 512
MOE_TOP_K = 2
VMEM_LIMIT = 56 * 1024 * 1024

_NT = (((1,), (1,)), ((), ()))
_TN = (((0,), (0,)), ((), ()))


def _cparams(sem):
    return pltpu.CompilerParams(dimension_semantics=sem, vmem_limit_bytes=VMEM_LIMIT)


def _rms(x, w):
    return x * lax.rsqrt(jnp.mean(x * x, axis=-1, keepdims=True) + RMS_EPS) * w


def _split3(x):
    hi = x.astype(BF16)
    r = x - hi.astype(F32)
    mid = r.astype(BF16)
    lo = (r - mid.astype(F32)).astype(BF16)
    return hi, mid, lo


def _dot(a, b):
    return jnp.dot(a, b, preferred_element_type=F32)


def _dot_nt(a, b):
    return lax.dot_general(a, b, _NT, preferred_element_type=F32)


def _dot_tn(a, b):
    return lax.dot_general(a, b, _TN, preferred_element_type=F32)


def _proj_gates_body(x_ref, nw_ref, w_ref, wg_ref, wgt_ref, main_ref, g_ref, gt_ref):
    xn = _rms(x_ref[...], nw_ref[...]).astype(BF16)
    main_ref[...] = _dot(xn, w_ref[...]).astype(main_ref.dtype)
    g_ref[...] = _dot(xn, wg_ref[...])[:, : g_ref.shape[1]]
    gt_ref[...] = _dot_nt(wgt_ref[...], xn)


def _proj_split_body(x_ref, nw_ref, w_ref, lo_ref, hi_ref):
    xn = _rms(x_ref[...], nw_ref[...]).astype(BF16)
    r = _dot(xn, w_ref[...])
    n_lo = lo_ref.shape[1]
    lo_ref[...] = r[:, :n_lo].astype(lo_ref.dtype)
    hi_ref[...] = r[:, n_lo:]


def _proj(h, nw, w_main, w_gates=None, n_lo=None):
    t, d = h.shape
    n = w_main.shape[1]
    tm = TM_PROJ
    x_spec = pl.BlockSpec((tm, d), lambda i: (i, 0))
    nw_spec = pl.BlockSpec((1, d), lambda i: (0, 0))
    w_spec = pl.BlockSpec((d, n), lambda i: (0, 0))
    if w_gates is None:
        return pl.pallas_call(
            _proj_split_body, grid=(t // tm,), in_specs=[x_spec, nw_spec, w_spec],
            out_specs=[pl.BlockSpec((tm, n_lo), lambda i: (i, 0)), pl.BlockSpec((tm, n - n_lo), lambda i: (i, 0))],
            out_shape=[jax.ShapeDtypeStruct((t, n_lo), BF16), jax.ShapeDtypeStruct((t, n - n_lo), F32)],
            compiler_params=_cparams(("parallel",)), name="proj",
        )(h, nw.reshape(1, d), w_main)
    ng = w_gates.shape[1]
    wg_pad = jnp.zeros((d, 128), BF16).at[:, :ng].set(w_gates)
    return pl.pallas_call(
        _proj_gates_body, grid=(t // tm,),
        in_specs=[x_spec, nw_spec, w_spec, pl.BlockSpec((d, 128), lambda i: (0, 0)),
                  pl.BlockSpec((ng, d), lambda i: (0, 0))],
        out_specs=[pl.BlockSpec((tm, n), lambda i: (i, 0)), pl.BlockSpec((tm, ng), lambda i: (i, 0)),
                   pl.BlockSpec((ng, tm), lambda i: (0, i))],
        out_shape=[jax.ShapeDtypeStruct((t, n), BF16), jax.ShapeDtypeStruct((t, ng), F32),
                   jax.ShapeDtypeStruct((ng, t), F32)],
        compiler_params=_cparams(("parallel",)), name="proj_gates",
    )(h, nw.reshape(1, d), w_main, wg_pad, w_gates.T)


def _mlstm_body(main_ref, g_ref, gt_ref, br_ref, bc_ref, hn_ref, out_ref, c_ref, n_ref, m_ref, *, nb, nh, dh):
    L = MLSTM_CHUNK
    w = nh * dh

    @pl.when(pl.program_id(0) == 0)
    def _init():
        c_ref[...] = jnp.zeros_like(c_ref)
        n_ref[...] = jnp.zeros_like(n_ref)
        m_ref[...] = jnp.full_like(m_ref, -1e30)

    row = lax.broadcasted_iota(jnp.int32, (L, L), 0)
    col = lax.broadcasted_iota(jnp.int32, (L, L), 1)
    causal = col <= row
    tril = causal.astype(BF16)
    triu = (row <= col).astype(BF16)
    scale = dh ** -0.5
    for b in range(nb):
        g = g_ref[b] + br_ref[...]
        gt = gt_ref[b, 0] + bc_ref[...]
        i_c = g[:, :nh]
        i_r = gt[:nh, :]
        lfc = _split3(jax.nn.log_sigmoid(g[:, nh:]))
        lfr = _split3(jax.nn.log_sigmoid(gt[nh:, :]))
        bc_all = _dot(tril, lfc[0]) + _dot(tril, lfc[1]) + _dot(tril, lfc[2])
        br_all = _dot(lfr[0], triu) + _dot(lfr[1], triu) + _dot(lfr[2], triu)
        for h in range(nh):
            idx = b * nh + h
            qb = main_ref[b, :, h * dh:(h + 1) * dh]
            vb = main_ref[b, :, 2 * w + h * dh:2 * w + (h + 1) * dh]
            q = qb.astype(F32)
            k = main_ref[b, :, w + h * dh:w + (h + 1) * dh].astype(F32) * scale
            v = vb.astype(F32)
            o = main_ref[b, :, 3 * w + h * dh:3 * w + (h + 1) * dh].astype(F32)
            bc = bc_all[:, h:h + 1]
            br = br_all[h:h + 1, :]
            ir = i_r[h:h + 1, :]
            ic = i_c[:, h:h + 1]
            m_prev = m_ref[idx]
            c_prev = c_ref[idx]
            n_prev = n_ref[idx]
            logw = jnp.where(causal, bc - br + ir, -jnp.inf)
            inter = bc + m_prev
            m_row = jnp.maximum(jnp.max(logw, axis=-1, keepdims=True), inter)
            kb = k.astype(BF16)
            s = _dot_nt(qb, kb) * jnp.exp(logw - m_row)
            isc = jnp.exp(inter - m_row)
            num = _dot(s.astype(BF16), vb) + isc * _dot_nt(qb, c_prev.astype(BF16))
            den = jnp.sum(s, axis=-1, keepdims=True) + isc * jnp.sum(q * n_prev, axis=-1, keepdims=True)
            hout = num / jnp.maximum(jnp.abs(den), jnp.exp(-m_row))
            b_end = bc[L - 1:L, :]
            logg = b_end - bc + ic
            m_new = jnp.maximum(b_end + m_prev, jnp.max(logg, axis=0, keepdims=True))
            wk = jnp.exp(logg - m_new)
            decay = jnp.exp(b_end + m_prev - m_new)
            c_ref[idx] = decay * c_prev + _dot_tn((v * wk).astype(BF16), kb)
            n_ref[idx] = decay * n_prev + jnp.sum(wk * k, axis=0, keepdims=True)
            m_ref[idx] = m_new
            hn = hout * lax.rsqrt(jnp.mean(hout * hout, axis=-1, keepdims=True) + RMS_EPS)
            out_ref[b, :, h * dh:(h + 1) * dh] = (hn * hn_ref[:, h * dh:(h + 1) * dh] * jax.nn.sigmoid(o)
                                                  ).astype(out_ref.dtype)


def _mlstm(main, g, gt, gate_bias, head_norm, nb, seq):
    t, n = main.shape
    nh = MLSTM_HEADS
    w = head_norm.shape[0]
    dh = w // nh
    L = MLSTM_CHUNK
    nc = seq // L
    main3 = main.reshape(nb, seq, n)
    g3 = g.reshape(nb, seq, 2 * nh)
    gt4 = gt.reshape(2 * nh, nb, nc, L).transpose(1, 2, 0, 3)
    body = functools.partial(_mlstm_body, nb=nb, nh=nh, dh=dh)
    out = pl.pallas_call(
        body, grid=(nc,),
        in_specs=[pl.BlockSpec((nb, L, 4 * w), lambda c: (0, c, 0)),
                  pl.BlockSpec((nb, L, 2 * nh), lambda c: (0, c, 0)),
                  pl.BlockSpec((nb, 1, 2 * nh, L), lambda c: (0, c, 0, 0)),
                  pl.BlockSpec((1, 2 * nh), lambda c: (0, 0)),
                  pl.BlockSpec((2 * nh, 1), lambda c: (0, 0)),
                  pl.BlockSpec((1, w), lambda c: (0, 0))],
        out_specs=pl.BlockSpec((nb, L, w), lambda c: (0, c, 0)),
        out_shape=jax.ShapeDtypeStruct((nb, seq, w), BF16),
        scratch_shapes=[pltpu.VMEM((nb * nh, dh, dh), F32), pltpu.VMEM((nb * nh, 1, dh), F32),
                        pltpu.VMEM((nb * nh, 1, 1), F32)],
        compiler_params=_cparams(("arbitrary",)), name="mlstm",
    )(main3, g3, gt4, gate_bias.reshape(1, 2 * nh), gate_bias.reshape(2 * nh, 1), head_norm.reshape(1, w))
    return out.reshape(t, w)


def _out_a_body(h_ref, hm_ref, gb_ref, gc_ref, xin_ref, pgc_ref, pxin_ref, cw_ref, w_ref, out_ref, *, tm, seq):
    i = pl.program_id(0)
    wm = hm_ref.shape[1]
    p = gc_ref[...].astype(F32) * xin_ref[...].astype(F32)
    first = (i * tm) % seq == 0
    hr = pgc_ref.shape[0]
    pp = jnp.where(first, 0.0, pgc_ref[...].astype(F32) * pxin_ref[...].astype(F32))
    rowi = lax.broadcasted_iota(jnp.int32, p.shape, 0)
    p1 = jnp.where(rowi == 0, pp[hr - 1:hr, :], pltpu.roll(p, 1, axis=0))
    p2 = jnp.where(rowi == 0, pp[hr - 2:hr - 1, :], jnp.where(rowi == 1, pp[hr - 1:hr, :], pltpu.roll(p, 2, axis=0)))
    yc = gb_ref[...].astype(F32) * (cw_ref[0:1, :] * p2 + cw_ref[1:2, :] * p1 + cw_ref[2:3, :] * p)
    out_ref[...] = (h_ref[...] + _dot(hm_ref[...], w_ref[:wm, :]) + _dot(yc.astype(BF16), w_ref[wm:, :]))


def _out_a(h, hm, main, conv_w, w_out, seq):
    t, d = h.shape
    wm = hm.shape[1]
    wc = conv_w.shape[1]
    tm = TM_PROJ
    cb = (4 * wm) // wc
    halo = 16
    rb = tm // halo
    prev = lambda i: jnp.maximum(i * rb - 1, 0)
    body = functools.partial(_out_a_body, tm=tm, seq=seq)
    return pl.pallas_call(
        body, grid=(t // tm,),
        in_specs=[pl.BlockSpec((tm, d), lambda i: (i, 0)),
                  pl.BlockSpec((tm, wm), lambda i: (i, 0)),
                  pl.BlockSpec((tm, wc), lambda i: (i, cb)),
                  pl.BlockSpec((tm, wc), lambda i: (i, cb + 1)),
                  pl.BlockSpec((tm, wc), lambda i: (i, cb + 2)),
                  pl.BlockSpec((halo, wc), lambda i: (prev(i), cb + 1)),
                  pl.BlockSpec((halo, wc), lambda i: (prev(i), cb + 2)),
                  pl.BlockSpec(conv_w.shape, lambda i: (0, 0)),
                  pl.BlockSpec(w_out.shape, lambda i: (0, 0))],
        out_specs=pl.BlockSpec((tm, d), lambda i: (i, 0)),
        out_shape=jax.ShapeDtypeStruct((t, d), F32),
        compiler_params=_cparams(("parallel",)), name="out_a",
    )(h, hm, main, main, main, main, main, conv_w, w_out)


def _route(logits, lane):
    ne = MOE_GROUPS * MOE_EPG
    big = 1e9
    gl = jnp.where((lane >= ne) & (lane < ne + MOE_GROUPS), logits, -jnp.inf)
    gmax = jnp.max(gl, axis=-1, keepdims=True)
    gidx = jnp.min(jnp.where(gl == gmax, lane - ne, big), axis=-1, keepdims=True)
    gval = 1.0 / jnp.sum(jnp.exp(gl - gmax), axis=-1, keepdims=True)
    lo = gidx * MOE_EPG
    sel = jnp.where((lane >= lo) & (lane < lo + MOE_EPG), logits, -jnp.inf)
    l1 = jnp.max(sel, axis=-1, keepdims=True)
    i1 = jnp.min(jnp.where(sel == l1, lane, big), axis=-1, keepdims=True)
    sel2 = jnp.where(lane == i1, -jnp.inf, sel)
    l2 = jnp.max(sel2, axis=-1, keepdims=True)
    i2 = jnp.min(jnp.where(sel2 == l2, lane, big), axis=-1, keepdims=True)
    r = jnp.exp(l2 - l1)
    w1 = gval / (1.0 + r)
    return i1, i2, w1, w1 * r


ROW_TILE = 8


def _rows_to_tiles(ref, x):
    n = x.shape[0]
    for c in range(ROW_TILE):
        ref[pl.ds(c, n, stride=ROW_TILE), :] = x[:, c * 128:(c + 1) * 128]


def _tiles_to_rows(ref, n):
    return jnp.concatenate([ref[pl.ds(c, n, stride=ROW_TILE), :] for c in range(ROW_TILE)], axis=1)


def _lane_put(lane, cols):
    out = jnp.where(lane == 0.0, cols[0], 0.0)
    for k in range(1, len(cols)):
        out = out + jnp.where(lane == float(k), cols[k], 0.0)
    return out


def _lane_get(lane, x, idx_col):
    return jnp.sum(jnp.where(lane == idx_col, x, 0.0), axis=-1, keepdims=True)


def _route_body(h_ref, nw_ref, wr_ref, br_ref, tri_ref, info_ref, cnt_ref, tile_ref):
    @pl.when(pl.program_id(0) == 0)
    def _init():
        cnt_ref[...] = jnp.zeros_like(cnt_ref)

    xn = _rms(h_ref[...], nw_ref[...])
    hi = xn.astype(BF16)
    lo = (xn - hi.astype(F32)).astype(BF16)
    logits = _dot(jnp.concatenate([hi, lo, hi], axis=1), wr_ref[...]) + br_ref[...]
    lane = lax.broadcasted_iota(jnp.int32, logits.shape, 1).astype(F32)
    i1, i2, w1, w2 = _route(logits, lane)
    ind = jnp.where((lane == i1) | (lane == i2), 1.0, 0.0)
    ahead = _dot(tri_ref[...], ind.astype(BF16))
    before = ahead + cnt_ref[0:1, :]
    tcnt = jnp.broadcast_to(jnp.sum(ind, axis=0, keepdims=True), cnt_ref.shape)
    lane8 = lax.broadcasted_iota(jnp.int32, cnt_ref.shape, 1)
    sub8 = lax.broadcasted_iota(jnp.int32, cnt_ref.shape, 0)
    incl = tcnt
    sh = 1
    while sh < cnt_ref.shape[1]:
        incl = incl + jnp.where(lane8 >= sh, pltpu.roll(incl, sh, axis=1), 0.0)
        sh *= 2
    toff = incl - tcnt
    local = ahead + toff[0:1, :]
    info_ref[...] = _lane_put(lane, [w1, w2, i1, i2, _lane_get(lane, before, i1), _lane_get(lane, before, i2),
                                     _lane_get(lane, local, i1), _lane_get(lane, local, i2)])
    tile_ref[...] = jnp.where(sub8 == 0, tcnt, jnp.where(sub8 == 1, toff, 0.0))
    cnt_ref[0:1, :] = cnt_ref[0:1, :] + tcnt[0:1, :]


def _dispatch_body(cnt_s, off_s, tot_s, h_ref, nw_ref, info_ref, offrow_ref, sel_ref, runs_s, xs_ref, pos_ref,
                   xn_buf, blk, pos_v, pos_s, zbuf, sem, *, tm, tile_rows, ne, n_tiles):
    i = pl.program_id(0)
    last = pl.num_programs(0) - 1
    slots = MOE_TOP_K * tm * ROW_TILE

    def wait_runs(half):
        pltpu.make_async_copy(blk.at[half], xs_ref.at[pl.ds(0, slots)], sem.at[half]).wait()

    for half in range(2):
        rs = slice(half * tm, (half + 1) * tm)

        @pl.when(i > 0)
        def _drain_previous(half=half):
            wait_runs(half)

        info = info_ref[rs, :]
        lane = lax.broadcasted_iota(jnp.int32, info.shape, 1).astype(F32)
        offrow = offrow_ref[...]
        vals = [_lane_get(lane, offrow, info[:, 2:3]) + info[:, 4:5],
                _lane_get(lane, offrow, info[:, 3:4]) + info[:, 5:6],
                info[:, 6:7] * ROW_TILE, info[:, 7:8] * ROW_TILE]
        cols = []
        for v in vals:
            hi = jnp.floor(v * (1.0 / 256.0))
            cols += [hi, v - 256.0 * hi]
        rows = _dot_nt(sel_ref[...], _lane_put(lane, cols).astype(BF16))
        sub = lax.broadcasted_iota(jnp.int32, rows.shape, 0)
        posall = jnp.zeros(rows.shape, F32)
        for k in range(len(vals)):
            posall = jnp.where(sub == k, rows[2 * k:2 * k + 1, :] * 256.0 + rows[2 * k + 1:2 * k + 2, :], posall)
        posall = posall.astype(jnp.int32)
        pos_ref[:, rs] = posall
        pos_v[...] = posall
        cp = pltpu.make_async_copy(pos_v, pos_s, sem.at[2])
        cp.start()
        _rows_to_tiles(xn_buf, _rms(h_ref[rs, :], nw_ref[...]))
        cp.wait()

        def place(t, carry, half=half):
            row = xn_buf[pl.ds(pl.multiple_of(t * ROW_TILE, ROW_TILE), ROW_TILE), :]
            for k in range(MOE_TOP_K):
                blk[half, pl.ds(pl.multiple_of(pos_s[MOE_TOP_K + k, t], ROW_TILE), ROW_TILE), :] = row
            return carry

        lax.fori_loop(0, tm, place, 0, unroll=16)
        for e in range(ne):
            n = runs_s[half, 2, e] * ROW_TILE

            @pl.when(n > 0)
            def _send(e=e, n=n, half=half):
                src = pl.multiple_of(runs_s[half, 0, e] * ROW_TILE, ROW_TILE)
                dst = pl.multiple_of(runs_s[half, 1, e] * ROW_TILE, ROW_TILE)
                pltpu.make_async_copy(blk.at[half, pl.ds(src, n)], xs_ref.at[pl.ds(dst, n)], sem.at[half]
                                      ).start(priority=e % 2)

    @pl.when(i == last)
    def _zero_unused_rows():
        wait_runs(0)
        wait_runs(1)
        zbuf[...] = jnp.zeros_like(zbuf)

        def fill(row, nrows):
            at = pl.multiple_of(row * ROW_TILE, ROW_TILE)
            n = nrows * ROW_TILE
            c = pltpu.make_async_copy(zbuf.at[pl.ds(0, n)], xs_ref.at[pl.ds(at, n)], sem.at[2])
            c.start()
            c.wait()

        for e in range(ne):
            n_pad = (tile_rows - cnt_s[e] % tile_rows) % tile_rows

            @pl.when(n_pad > 0)
            def _fill(e=e, n_pad=n_pad):
                fill(off_s[e] + cnt_s[e], n_pad)

        def zero_tile(j, carry):
            fill(j * tile_rows, tile_rows)
            return carry

        lax.fori_loop(tot_s[0], n_tiles, zero_tile, 0)


def _expert_body(te_s, blk_s, tot_s, xs_ref, wg_ref, wu_ref, wd_ref, ys_ref):
    valid = pl.program_id(0) < tot_s[0]
    rows = xs_ref.shape[0] // ROW_TILE

    @pl.when(valid)
    def _run():
        x = _tiles_to_rows(xs_ref, rows).astype(BF16)
        hid = jax.nn.silu(_dot(x, wg_ref[0, 0].astype(BF16))) * _dot(x, wu_ref[0, 0].astype(BF16))
        _rows_to_tiles(ys_ref, _dot(hid.astype(BF16), wd_ref[0, 0].astype(BF16)))

    @pl.when(jnp.logical_not(valid))
    def _unused_tile():
        ys_ref[...] = jnp.zeros_like(ys_ref)


def _combine_body(h_ref, info_ref, pos_s, runs_s, next_s, ys_ref, fw_ref, out_ref, blk, y1, y2, sem, *,
                  tm, ne, final):
    i = pl.program_id(0)
    slots = MOE_TOP_K * tm * ROW_TILE
    bufs = (y1, y2)

    def fetch(runs, half):
        for e in range(ne):
            n = runs[half, 2, e] * ROW_TILE

            @pl.when(n > 0)
            def _fetch(e=e, n=n):
                dst = pl.multiple_of(runs[half, 0, e] * ROW_TILE, ROW_TILE)
                src = pl.multiple_of(runs[half, 1, e] * ROW_TILE, ROW_TILE)
                pltpu.make_async_copy(ys_ref.at[pl.ds(src, n)], blk.at[half, pl.ds(dst, n)], sem.at[half]
                                      ).start(priority=e % 2)

    def finish(half):
        pltpu.make_async_copy(ys_ref.at[pl.ds(0, slots)], blk.at[half], sem.at[half]).wait()
        rs = slice(half * tm, (half + 1) * tm)

        def pick(t, carry):
            dst = pl.ds(pl.multiple_of(t * ROW_TILE, ROW_TILE), ROW_TILE)
            for k in range(MOE_TOP_K):
                at = pl.multiple_of(pos_s[MOE_TOP_K + k, half * tm + t], ROW_TILE)
                bufs[k][dst, :] = blk[half, pl.ds(at, ROW_TILE), :]
            return carry

        lax.fori_loop(0, tm, pick, 0, unroll=8)
        o = (h_ref[rs, :] + info_ref[rs, 0:1] * _tiles_to_rows(y1, tm) + info_ref[rs, 1:2] * _tiles_to_rows(y2, tm))
        if final:
            o = _rms(o, fw_ref[...])
        out_ref[rs, :] = o

    @pl.when(i == 0)
    def _first():
        fetch(runs_s, 0)

    fetch(runs_s, 1)
    finish(0)

    @pl.when(i < pl.num_programs(0) - 1)
    def _prefetch():
        fetch(next_s, 0)

    finish(1)


def _router_weights(w_group, b_group, w_router, b_router):
    d, ne = w_router.shape
    ng = w_group.shape[1]
    w = jnp.zeros((d, ROUTE_LANES), F32).at[:, :ne].set(w_router).at[:, ne:ne + ng].set(w_group)
    hi = w.astype(BF16)
    lo = (w - hi.astype(F32)).astype(BF16)
    bias = jnp.zeros((1, ROUTE_LANES), F32).at[0, :ne].set(b_router).at[0, ne:ne + ng].set(b_group)
    return jnp.concatenate([hi, hi, lo], axis=0), bias


def _moe(h, nw, w_group, b_group, w_router, b_router, w_gate, w_up, w_down, layer, final_w):
    t, d = h.shape
    assert d == ROW_TILE * 128, "row-as-tile layout needs d_model == 1024"
    _, ne, _, ff = w_gate.shape
    tm = TM_MOE
    te_rows = TM_EXPERT
    nw2 = nw.reshape(1, d)
    wr3, bias = _router_weights(w_group, b_group, w_router, b_router)
    tri = jnp.asarray(np.tril(np.ones((tm, tm), np.float32), -1), BF16)
    nt = t // tm
    info, cnt, tile_info = pl.pallas_call(
        _route_body, grid=(nt,),
        in_specs=[pl.BlockSpec((tm, d), lambda i: (i, 0)),
                  pl.BlockSpec((1, d), lambda i: (0, 0)),
                  pl.BlockSpec((3 * d, ROUTE_LANES), lambda i: (0, 0)),
                  pl.BlockSpec((1, ROUTE_LANES), lambda i: (0, 0)),
                  pl.BlockSpec((tm, tm), lambda i: (0, 0))],
        out_specs=[pl.BlockSpec((tm, ROUTE_LANES), lambda i: (i, 0)), pl.BlockSpec((8, ROUTE_LANES), lambda i: (0, 0)),
                   pl.BlockSpec((8, ROUTE_LANES), lambda i: (i, 0))],
        out_shape=[jax.ShapeDtypeStruct((t, ROUTE_LANES), F32), jax.ShapeDtypeStruct((8, ROUTE_LANES), F32),
                   jax.ShapeDtypeStruct((nt * 8, ROUTE_LANES), F32)],
        compiler_params=_cparams(("arbitrary",)), name="moe_route",
    )(h, nw2, wr3, bias, tri)

    cnt_i = cnt[0, :ne].astype(jnp.int32)
    ntile = (cnt_i + te_rows - 1) // te_rows
    tile_end = jnp.cumsum(ntile)
    off_i = (tile_end - ntile) * te_rows
    n_tiles = (MOE_TOP_K * t) // te_rows + ne
    rows_total = n_tiles * te_rows
    ti = jnp.arange(n_tiles, dtype=jnp.int32)
    tot = tile_end[-1:]
    ti_c = jnp.minimum(ti, tot[0] - 1)
    tile_e = jnp.sum((ti_c[:, None] >= tile_end[None, :]).astype(jnp.int32), axis=1)
    off_row = jnp.zeros((1, ROUTE_LANES), F32).at[0, :ne].set(off_i.astype(F32))
    sel = jnp.asarray(np.eye(8, ROUTE_LANES, dtype=np.float32), BF16)
    tinfo = tile_info.reshape(nt, 8, ROUTE_LANES).astype(jnp.int32)
    tcnt = tinfo[:, 0, :]
    gstart = off_row.astype(jnp.int32) + jnp.cumsum(tcnt, axis=0) - tcnt
    runs = jnp.zeros((nt, 8, ROUTE_LANES), jnp.int32).at[:, 0].set(tinfo[:, 1, :]).at[:, 1].set(gstart).at[:, 2].set(tcnt)

    xs, pos = pl.pallas_call(
        functools.partial(_dispatch_body, tm=tm, tile_rows=te_rows, ne=ne, n_tiles=n_tiles),
        grid_spec=pltpu.PrefetchScalarGridSpec(
            num_scalar_prefetch=3, grid=(nt // 2,),
            in_specs=[pl.BlockSpec((2 * tm, d), lambda i, *_: (i, 0)),
                      pl.BlockSpec((1, d), lambda i, *_: (0, 0)),
                      pl.BlockSpec((2 * tm, ROUTE_LANES), lambda i, *_: (i, 0)),
                      pl.BlockSpec((1, ROUTE_LANES), lambda i, *_: (0, 0)),
                      pl.BlockSpec((8, ROUTE_LANES), lambda i, *_: (0, 0)),
                      pl.BlockSpec((2, 8, ROUTE_LANES), lambda i, *_: (i, 0, 0), memory_space=pltpu.SMEM)],
            out_specs=[pl.BlockSpec(memory_space=pl.ANY), pl.BlockSpec((8, 2 * tm), lambda i, *_: (0, i))],
            scratch_shapes=[pltpu.VMEM((tm * ROW_TILE, 128), F32),
                            pltpu.VMEM((2, MOE_TOP_K * tm * ROW_TILE, 128), F32),
                            pltpu.VMEM((8, tm), jnp.int32), pltpu.SMEM((8, tm), jnp.int32),
                            pltpu.VMEM((te_rows * ROW_TILE, 128), F32), pltpu.SemaphoreType.DMA((3,))]),
        out_shape=[jax.ShapeDtypeStruct((rows_total * ROW_TILE, 128), F32), jax.ShapeDtypeStruct((8, t), jnp.int32)],
        compiler_params=_cparams(("arbitrary",)), name="moe_dispatch",
    )(cnt_i, off_i, tot, h, nw2, info, off_row, sel, runs)

    ys = pl.pallas_call(
        _expert_body,
        grid_spec=pltpu.PrefetchScalarGridSpec(
            num_scalar_prefetch=3, grid=(n_tiles,),
            in_specs=[pl.BlockSpec((te_rows * ROW_TILE, 128), lambda i, e, b, v: (b[i], 0)),
                      pl.BlockSpec((1, 1, d, ff), lambda i, e, b, v: (layer, e[i], 0, 0)),
                      pl.BlockSpec((1, 1, d, ff), lambda i, e, b, v: (layer, e[i], 0, 0)),
                      pl.BlockSpec((1, 1, ff, d), lambda i, e, b, v: (layer, e[i], 0, 0))],
            out_specs=pl.BlockSpec((te_rows * ROW_TILE, 128), lambda i, e, b, v: (i, 0))),
        out_shape=jax.ShapeDtypeStruct((rows_total * ROW_TILE, 128), F32),
        compiler_params=_cparams(("arbitrary",)), name="moe_expert",
    )(tile_e, ti_c, tot, xs, w_gate, w_up, w_down)

    final = final_w is not None
    fw = (final_w if final else nw).reshape(1, d)
    return pl.pallas_call(
        functools.partial(_combine_body, tm=tm, ne=ne, final=final), grid=(nt // 2,),
        in_specs=[pl.BlockSpec((2 * tm, d), lambda i: (i, 0)),
                  pl.BlockSpec((2 * tm, ROUTE_LANES), lambda i: (i, 0)),
                  pl.BlockSpec((8, 2 * tm), lambda i: (0, i), memory_space=pltpu.SMEM),
                  pl.BlockSpec((2, 8, ROUTE_LANES), lambda i: (i, 0, 0), memory_space=pltpu.SMEM),
                  pl.BlockSpec((2, 8, ROUTE_LANES), lambda i: (jnp.minimum(i + 1, nt // 2 - 1), 0, 0),
                               memory_space=pltpu.SMEM),
                  pl.BlockSpec(memory_space=pl.ANY),
                  pl.BlockSpec((1, d), lambda i: (0, 0))],
        out_specs=pl.BlockSpec((2 * tm, d), lambda i: (i, 0)),
        out_shape=jax.ShapeDtypeStruct((t, d), F32),
        scratch_shapes=[pltpu.VMEM((2, MOE_TOP_K * tm * ROW_TILE, 128), F32), pltpu.VMEM((tm * ROW_TILE, 128), F32),
                        pltpu.VMEM((tm * ROW_TILE, 128), F32), pltpu.SemaphoreType.DMA((2,))],
        compiler_params=_cparams(("arbitrary",)), name="moe_combine",
    )(h, info, pos, runs, runs, ys, fw)


def _s5_weights(lam_re, lam_im, b_re, b_im, c_re, c_im, log_dt, nsteps):
    hp = lax.Precision.HIGHEST
    L = S5_CHUNK
    g, p = lam_re.shape
    ch = b_re.shape[-1]
    lam = lax.complex(lam_re.astype(F32), lam_im.astype(F32))
    dt = jnp.exp(log_dt.astype(F32))[:, None]
    lam_bar = jnp.exp(lam * dt)
    b_bar = ((lam_bar - 1.0) / lam)[..., None] * lax.complex(b_re.astype(F32), b_im.astype(F32))
    cmat = lax.complex(c_re.astype(F32), c_im.astype(F32))
    pows = [jnp.ones_like(lam_bar)]
    for _ in range(L):
        pows.append(pows[-1] * lam_bar)
    pw = jnp.stack(pows, axis=1)
    kern = jnp.real(jnp.einsum('gop,gtp,gpi->gito', cmat, pw[:, :L], b_bar, precision=hp))
    kpad = jnp.pad(kern.reshape(g, ch, L * ch), ((0, 0), (0, 0), ((L - 1) * ch, 0)))
    toep = jnp.stack([kpad[:, :, (L - 1 - s) * ch:(2 * L - 1 - s) * ch] for s in range(L)], axis=1)
    toep = toep.reshape(g, L * ch, L * ch)
    wst = pw[:, L - 1 - np.arange(L)][:, :, :, None] * b_bar[:, None, :, :]
    wst = wst.transpose(0, 1, 3, 2).reshape(g, L * ch, p)
    wst = jnp.concatenate([jnp.real(wst), jnp.imag(wst)], axis=-1)
    mo = cmat.transpose(0, 2, 1)[:, :, None, :] * pw[:, 1:L + 1].transpose(0, 2, 1)[:, :, :, None]
    mo = mo.reshape(g, p, L * ch)
    wout = jnp.concatenate([jnp.real(mo), -jnp.imag(mo)], axis=1)
    a = pw[:, L]
    ars, ais = [], []
    for _ in range(nsteps):
        ars.append(jnp.concatenate([jnp.real(a), jnp.real(a)], axis=-1))
        ais.append(jnp.concatenate([-jnp.imag(a), jnp.imag(a)], axis=-1))
        a = a * a
    wcat = jnp.concatenate([toep, wst], axis=-1).astype(BF16)
    return wcat, wout.astype(BF16), jnp.stack(ars, axis=1), jnp.stack(ais, axis=1)


def _s5_body(u_ref, wcat_ref, wout_ref, ar_ref, ai_ref, y_ref, us_ref, ys_ref, *, nsteps):
    L = S5_CHUNK
    ch = S5_GROUP_CH
    gpc = 128 // ch
    ny = L * ch
    nc = u_ref.shape[0] // L
    for s in range(L):
        us_ref[s] = u_ref[pl.ds(s, nc, stride=L), :]
    lane = lax.broadcasted_iota(jnp.int32, (nc, 128), 1)
    ridx = lax.broadcasted_iota(jnp.int32, (nc, 128), 0)

    def shift(x, k):
        return jnp.where(ridx >= k, pltpu.roll(x, k, axis=0), 0.0)

    for gi in range(gpc):
        halves = []
        for hh in range(ny // 128):
            acc = None
            for s8 in range(gpc):
                rot = ((s8 - gi) * ch) % 128
                src = us_ref[hh * gpc + s8]
                if rot:
                    src = pltpu.roll(src, rot, axis=1)
                slot = (lane >= s8 * ch) & (lane < (s8 + 1) * ch)
                acc = jnp.where(slot, src, 0.0) if acc is None else jnp.where(slot, src, acc)
            halves.append(acc)
        ug = jnp.concatenate(halves, axis=1).astype(BF16)
        r = _dot(ug, wcat_ref[gi])
        y1 = r[:, :ny]
        z = r[:, ny:]
        w = shift(z, 1)
        wx = pltpu.roll(w, z.shape[1] // 2, axis=1)
        for k in range(nsteps):
            if (1 << k) >= nc:
                break
            sk = shift(w, 1 << k)
            sx = shift(wx, 1 << k)
            ar = ar_ref[gi, k:k + 1, :]
            ai = ai_ref[gi, k:k + 1, :]
            w, wx = w + sk * ar + sx * ai, wx + sx * ar - sk * ai
        yg = y1 + _dot(w.astype(BF16), wout_ref[gi])
        slot = (lane >= gi * ch) & (lane < (gi + 1) * ch)
        for t in range(L):
            src = yg[:, (t // gpc) * 128:(t // gpc + 1) * 128]
            rot = ((gi - t % gpc) * ch) % 128
            if rot:
                src = pltpu.roll(src, rot, axis=1)
            ys_ref[t] = jnp.where(slot, src, 0.0) if gi == 0 else jnp.where(slot, src, ys_ref[t])
    for t in range(L):
        y_ref[pl.ds(t, nc, stride=L), :] = ys_ref[t]


def _s5(main, col0, width, nb, seq, lam_re, lam_im, b_re, b_im, c_re, c_im, log_dt):
    t = main.shape[0]
    L = S5_CHUNK
    ch = S5_GROUP_CH
    g = width // ch
    nc = seq // L
    gpc = 128 // ch
    ncol = width // 128
    assert col0 % 128 == 0 and width % 128 == 0 and (L * ch) % 128 == 0
    nsteps = max(1, (nc - 1).bit_length())
    wcat, wout, ar, ai = _s5_weights(lam_re, lam_im, b_re, b_im, c_re, c_im, log_dt, nsteps)
    body = functools.partial(_s5_body, nsteps=nsteps)
    return pl.pallas_call(
        body, grid=(ncol, nb),
        in_specs=[pl.BlockSpec((seq, 128), lambda j, b: (b, col0 // 128 + j)),
                  pl.BlockSpec((gpc,) + wcat.shape[1:], lambda j, b: (j, 0, 0)),
                  pl.BlockSpec((gpc,) + wout.shape[1:], lambda j, b: (j, 0, 0)),
                  pl.BlockSpec((gpc,) + ar.shape[1:], lambda j, b: (j, 0, 0)),
                  pl.BlockSpec((gpc,) + ai.shape[1:], lambda j, b: (j, 0, 0))],
        out_specs=pl.BlockSpec((seq, 128), lambda j, b: (b, j)),
        out_shape=jax.ShapeDtypeStruct((t, width), F32),
        scratch_shapes=[pltpu.VMEM((L, nc, 128), F32), pltpu.VMEM((L, nc, 128), F32)],
        compiler_params=_cparams(("parallel", "parallel")), name="s5",
    )(main, wcat, wout, ar, ai)


def _hgrn_gmat():
    L = HGRN_CHUNK
    blocks = 2 + int(np.log2(L))
    gm = np.zeros((blocks * L, L), np.float32)
    for j in range(L):
        gm[j, :j + 1] = 1.0
        gm[L + j, j + 1:] = 1.0
    li, m = 2, L
    while m >= 2:
        half = m // 2
        for j in range(L):
            pos = j % m
            r = j - pos + half - 1
            if pos >= half:
                gm[li * L + j, r + 1:j + 1] = 1.0
            else:
                gm[li * L + j, j + 1:r + 1] = 1.0
        li += 1
        m //= 2
    return gm


def _hgrn_body(mh_ref, fg_ref, gm_ref, lb_ref, nw_ref, out_ref, st_ref, *, nb, nh, dh):
    L = HGRN_CHUNK
    w = nh * dh

    @pl.when(pl.program_id(0) == 0)
    def _init():
        st_ref[...] = jnp.zeros_like(st_ref)

    row = lax.broadcasted_iota(jnp.int32, (L, 2 * L), 0)
    col = lax.broadcasted_iota(jnp.int32, (L, 2 * L), 1) & (L - 1)
    rowd = lax.broadcasted_iota(jnp.int32, (L, 2 * dh), 0)
    laned = lax.broadcasted_iota(jnp.int32, (L, 2 * dh), 1)
    first = laned < dh
    eye = row == col

    def blockdiag(x):
        z = jnp.zeros_like(x)
        return jnp.concatenate([jnp.where(first, x, z), jnp.where(first, z, x)], axis=0)

    gm2 = gm_ref[...]
    lb = lb_ref[...]
    zst = jnp.zeros((dh, dh), BF16)
    for b in range(nb):
        fg = fg_ref[b]
        f = lb + (1.0 - lb) * jax.nn.sigmoid(fg)
        kk = (1.0 - lb) * jax.nn.sigmoid(-fg)
        lf = jnp.log(f)
        hi = lf.astype(BF16)
        mid = (lf - hi.astype(F32)).astype(BF16)
        p_all = jnp.exp(_dot(gm2, jnp.concatenate([hi, mid], axis=0)))
        for hp in range(nh // 2):
            i0 = b * nh + 2 * hp
            cs = slice(2 * hp * dh, (2 * hp + 2) * dh)
            qb = mh_ref[b, :, 2 * hp * dh:(2 * hp + 2) * dh]
            vb = mh_ref[b, :, w + 2 * hp * dh:w + (2 * hp + 2) * dh]
            og = mh_ref[b, :, 2 * w + 2 * hp * dh:2 * w + (2 * hp + 2) * dh].astype(F32)
            q = qb.astype(F32)
            k = kk[:, cs]
            pb = p_all[0:L, cs]
            pe = p_all[L:2 * L, cs]
            st0 = st_ref[i0]
            st1 = st_ref[i0 + 1]
            stbd = jnp.concatenate([jnp.concatenate([st0.astype(BF16), zst], axis=1),
                                    jnp.concatenate([zst, st1.astype(BF16)], axis=1)], axis=0)
            o = _dot_nt((q * pb).astype(BF16), stbd)
            attn = jnp.where(eye, _dot_nt(qb, blockdiag(k.astype(BF16))), 0.0)
            li, m = 2, L
            while m >= 2:
                pl_ = p_all[li * L:(li + 1) * L, cs]
                up = (rowd & (m - 1)) >= (m // 2)
                ql = jnp.where(up, q * pl_, 0.0).astype(BF16)
                kl = jnp.where(up, 0.0, k * pl_).astype(BF16)
                same = (row & ~(m - 1)) == (col & ~(m - 1))
                attn = attn + jnp.where(same, _dot_nt(ql, blockdiag(kl)), 0.0)
                li += 1
                m //= 2
            o = o + _dot(attn.astype(BF16), blockdiag(vb))
            kh = (k * pe).astype(BF16)
            for j in range(2):
                hs = slice(j * dh, (j + 1) * dh)
                gs = slice((2 * hp + j) * dh, (2 * hp + j + 1) * dh)
                st = st0 if j == 0 else st1
                st_ref[i0 + j] = st * pb[L - 1:L, hs] + _dot_tn(vb[:, hs], kh[:, hs])
                oj = o[:, hs]
                on = oj * lax.rsqrt(jnp.mean(oj * oj, axis=-1, keepdims=True) + RMS_EPS)
                out_ref[b, :, gs] = (on * nw_ref[:, gs] * jax.nn.silu(og[:, hs])).astype(out_ref.dtype)


def _hgrn(mh, mf, lower_bound, out_norm, nb, seq):
    t = mh.shape[0]
    nh = HGRN_HEADS
    w = out_norm.shape[0]
    dh = w // nh
    L = HGRN_CHUNK
    nc = seq // L
    gm = _hgrn_gmat()
    gm = jnp.asarray(np.concatenate([gm, gm], axis=1), BF16)
    body = functools.partial(_hgrn_body, nb=nb, nh=nh, dh=dh)
    out = pl.pallas_call(
        body, grid=(nc,),
        in_specs=[pl.BlockSpec((nb, L, 3 * w), lambda c: (0, c, 0)),
                  pl.BlockSpec((nb, L, w), lambda c: (0, c, 0)),
                  pl.BlockSpec(gm.shape, lambda c: (0, 0)),
                  pl.BlockSpec((1, w), lambda c: (0, 0)),
                  pl.BlockSpec((1, w), lambda c: (0, 0))],
        out_specs=pl.BlockSpec((nb, L, w), lambda c: (0, c, 0)),
        out_shape=jax.ShapeDtypeStruct((nb, seq, w), BF16),
        scratch_shapes=[pltpu.VMEM((nb * nh, dh, dh), F32)],
        compiler_params=_cparams(("arbitrary",)), name="hgrn",
    )(mh.reshape(nb, seq, mh.shape[1]), mf.reshape(nb, seq, mf.shape[1]), gm, lower_bound.reshape(1, w),
      out_norm.reshape(1, w))
    return out.reshape(t, w)


def _out_c_body(h_ref, ys_ref, u_ref, oh_ref, d_ref, wglu_ref, bglu_ref, w_ref, out_ref):
    ws = ys_ref.shape[1]
    z = jax.nn.gelu(ys_ref[...] + d_ref[...] * u_ref[...])
    gate = jax.nn.sigmoid(_dot(z.astype(BF16), wglu_ref[...]) + bglu_ref[...])
    out_ref[...] = (h_ref[...] + _dot((z * gate).astype(BF16), w_ref[:ws, :]) + _dot(oh_ref[...], w_ref[ws:, :]))


def _out_c(h, ys, main, oh, d_skip, w_glu, b_glu, w_out):
    t, d = h.shape
    ws = ys.shape[1]
    wh = oh.shape[1]
    tm = TM_PROJ
    ub = (main.shape[1] - ws) // ws
    return pl.pallas_call(
        _out_c_body, grid=(t // tm,),
        in_specs=[pl.BlockSpec((tm, d), lambda i: (i, 0)),
                  pl.BlockSpec((tm, ws), lambda i: (i, 0)),
                  pl.BlockSpec((tm, ws), lambda i: (i, ub)),
                  pl.BlockSpec((tm, wh), lambda i: (i, 0)),
                  pl.BlockSpec((1, ws), lambda i: (0, 0)),
                  pl.BlockSpec(w_glu.shape, lambda i: (0, 0)),
                  pl.BlockSpec((1, ws), lambda i: (0, 0)),
                  pl.BlockSpec(w_out.shape, lambda i: (0, 0))],
        out_specs=pl.BlockSpec((tm, d), lambda i: (i, 0)),
        out_shape=jax.ShapeDtypeStruct((t, d), F32),
        compiler_params=_cparams(("parallel",)), name="out_c",
    )(h, ys, main, oh, d_skip.reshape(1, ws), w_glu, b_glu.reshape(1, ws), w_out)


def kernel(x, norm_mix, norm_ffn, norm_final, ab_w_in, ab_gate_bias, ab_head_norm, ab_conv_w, ab_w_out, cd_w_in, s5_lambda_re, s5_lambda_im, s5_b_re, s5_b_im, s5_c_re, s5_c_im, s5_d, s5_log_dt, s5_w_glu, s5_b_glu, hgrn_lb, hgrn_out_norm, cd_w_out, moe_w_group, moe_b_group, moe_w_router, moe_b_router, moe_w_gate, moe_w_up, moe_w_down):
    nb, seq, d = x.shape
    depth = norm_mix.shape[0]
    h = x.reshape(nb * seq, d)
    for layer in range(depth):
        j = layer // 2
        if layer % 2 == 0:
            wm = ab_head_norm.shape[1]
            ng = ab_gate_bias.shape[1]
            w_in = ab_w_in[j]
            w_main = jnp.concatenate([w_in[:, :4 * wm], w_in[:, 4 * wm + ng:]], axis=1).astype(BF16)
            w_gates = w_in[:, 4 * wm:4 * wm + ng].astype(BF16)
            main, g, gt = _proj(h, norm_mix[layer], w_main, w_gates)
            hm = _mlstm(main, g, gt, ab_gate_bias[j], ab_head_norm[j], nb, seq)
            h = _out_a(h, hm, main, ab_conv_w[j], ab_w_out[j].astype(BF16), seq)
        else:
            ws = s5_d.shape[1]
            w_in = cd_w_in[j]
            wh = hgrn_out_norm.shape[1]
            cu, cq, cf, ci, co = (w_in[:, :ws], w_in[:, ws:ws + wh], w_in[:, ws + wh:ws + 2 * wh],
                                  w_in[:, ws + 2 * wh:ws + 3 * wh], w_in[:, ws + 3 * wh:])
            w_main = jnp.concatenate([cq, ci, co, cf, cu], axis=1).astype(BF16)
            mh, mf = _proj(h, norm_mix[layer], w_main, n_lo=3 * wh)
            sm = jax.nn.softmax(hgrn_lb.astype(F32), axis=0)
            lower_bound = jnp.cumsum(sm, axis=0)[layer] - sm[0]
            ys = _s5(mf, wh, ws, nb, seq, s5_lambda_re[j], s5_lambda_im[j], s5_b_re[j], s5_b_im[j],
                     s5_c_re[j], s5_c_im[j], s5_log_dt[j])
            oh = _hgrn(mh, mf, lower_bound, hgrn_out_norm[j], nb, seq)
            h = _out_c(h, ys, mf, oh, s5_d[j], s5_w_glu[j].astype(BF16), s5_b_glu[j], cd_w_out[j].astype(BF16))
        h = _moe(h, norm_ffn[layer], moe_w_group[layer], moe_b_group[layer], moe_w_router[layer], moe_b_router[layer],
                 moe_w_gate, moe_w_up, moe_w_down, layer,
                 norm_final if layer == depth - 1 else None)
    return h.reshape(nb, seq, d)
```

```python
import functools

import numpy as np
import jax
import jax.numpy as jnp
from jax import lax
from jax.experimental import pallas as pl
from jax.experimental.pallas import tpu as pltpu

F32 = jnp.float32
BF16 = jnp.bfloat16
RMS_EPS = 1e-6
MLSTM_CHUNK = 512
HGRN_CHUNK = 128
S5_CHUNK = 16
S5_GROUP_CH = 16
S5_STATE = 64
MLSTM_HEADS = 4
HGRN_HEADS = 4
MOE_GROUPS = 4
MOE_EPG = 8
ROUTE_LANES = 128
TM_PROJ = 512
TM_MOE = 512
TM_EXPERT = 512
MOE_TOP_K = 2
VMEM_LIMIT = 56 * 1024 * 1024

_NT = (((1,), (1,)), ((), ()))
_TN = (((0,), (0,)), ((), ()))


def _cparams(sem):
    return pltpu.CompilerParams(dimension_semantics=sem, vmem_limit_bytes=VMEM_LIMIT)


def _rms(x, w):
    return x * lax.rsqrt(jnp.mean(x * x, axis=-1, keepdims=True) + RMS_EPS) * w


def _split3(x):
    hi = x.astype(BF16)
    r = x - hi.astype(F32)
    mid = r.astype(BF16)
    lo = (r - mid.astype(F32)).astype(BF16)
    return hi, mid, lo


def _dot(a, b):
    return jnp.dot(a, b, preferred_element_type=F32)


def _dot_nt(a, b):
    return lax.dot_general(a, b, _NT, preferred_element_type=F32)


def _dot_tn(a, b):
    return lax.dot_general(a, b, _TN, preferred_element_type=F32)


def _proj_gates_body(x_ref, nw_ref, w_ref, wg_ref, wgt_ref, main_ref, g_ref, gt_ref):
    xn = _rms(x_ref[...], nw_ref[...]).astype(BF16)
    main_ref[...] = _dot(xn, w_ref[...]).astype(main_ref.dtype)
    g_ref[...] = _dot(xn, wg_ref[...])[:, : g_ref.shape[1]]
    gt_ref[...] = _dot_nt(wgt_ref[...], xn)


def _proj_split_body(x_ref, nw_ref, w_ref, lo_ref, hi_ref):
    xn = _rms(x_ref[...], nw_ref[...]).astype(BF16)
    r = _dot(xn, w_ref[...])
    n_lo = lo_ref.shape[1]
    lo_ref[...] = r[:, :n_lo].astype(lo_ref.dtype)
    hi_ref[...] = r[:, n_lo:]


def _proj(h, nw, w_main, w_gates=None, n_lo=None):
    t, d = h.shape
    n = w_main.shape[1]
    tm = TM_PROJ
    x_spec = pl.BlockSpec((tm, d), lambda i: (i, 0))
    nw_spec = pl.BlockSpec((1, d), lambda i: (0, 0))
    w_spec = pl.BlockSpec((d, n), lambda i: (0, 0))
    if w_gates is None:
        return pl.pallas_call(
            _proj_split_body, grid=(t // tm,), in_specs=[x_spec, nw_spec, w_spec],
            out_specs=[pl.BlockSpec((tm, n_lo), lambda i: (i, 0)), pl.BlockSpec((tm, n - n_lo), lambda i: (i, 0))],
            out_shape=[jax.ShapeDtypeStruct((t, n_lo), BF16), jax.ShapeDtypeStruct((t, n - n_lo), F32)],
            compiler_params=_cparams(("parallel",)), name="proj",
        )(h, nw.reshape(1, d), w_main)
    ng = w_gates.shape[1]
    wg_pad = jnp.zeros((d, 128), BF16).at[:, :ng].set(w_gates)
    return pl.pallas_call(
        _proj_gates_body, grid=(t // tm,),
        in_specs=[x_spec, nw_spec, w_spec, pl.BlockSpec((d, 128), lambda i: (0, 0)),
                  pl.BlockSpec((ng, d), lambda i: (0, 0))],
        out_specs=[pl.BlockSpec((tm, n), lambda i: (i, 0)), pl.BlockSpec((tm, ng), lambda i: (i, 0)),
                   pl.BlockSpec((ng, tm), lambda i: (0, i))],
        out_shape=[jax.ShapeDtypeStruct((t, n), BF16), jax.ShapeDtypeStruct((t, ng), F32),
                   jax.ShapeDtypeStruct((ng, t), F32)],
        compiler_params=_cparams(("parallel",)), name="proj_gates",
    )(h, nw.reshape(1, d), w_main, wg_pad, w_gates.T)


def _mlstm_body(main_ref, g_ref, gt_ref, br_ref, bc_ref, hn_ref, out_ref, c_ref, n_ref, m_ref, *, nb, nh, dh):
    L = MLSTM_CHUNK
    w = nh * dh

    @pl.when(pl.program_id(0) == 0)
    def _init():
        c_ref[...] = jnp.zeros_like(c_ref)
        n_ref[...] = jnp.zeros_like(n_ref)
        m_ref[...] = jnp.full_like(m_ref, -1e30)

    row = lax.broadcasted_iota(jnp.int32, (L, L), 0)
    col = lax.broadcasted_iota(jnp.int32, (L, L), 1)
    causal = col <= row
    tril = causal.astype(BF16)
    triu = (row <= col).astype(BF16)
    scale = dh ** -0.5
    for b in range(nb):
        g = g_ref[b] + br_ref[...]
        gt = gt_ref[b, 0] + bc_ref[...]
        i_c = g[:, :nh]
        i_r = gt[:nh, :]
        lfc = _split3(jax.nn.log_sigmoid(g[:, nh:]))
        lfr = _split3(jax.nn.log_sigmoid(gt[nh:, :]))
        bc_all = _dot(tril, lfc[0]) + _dot(tril, lfc[1]) + _dot(tril, lfc[2])
        br_all = _dot(lfr[0], triu) + _dot(lfr[1], triu) + _dot(lfr[2], triu)
        for h in range(nh):
            idx = b * nh + h
            qb = main_ref[b, :, h * dh:(h + 1) * dh]
            vb = main_ref[b, :, 2 * w + h * dh:2 * w + (h + 1) * dh]
            q = qb.astype(F32)
            k = main_ref[b, :, w + h * dh:w + (h + 1) * dh].astype(F32) * scale
            v = vb.astype(F32)
            o = main_ref[b, :, 3 * w + h * dh:3 * w + (h + 1) * dh].astype(F32)
            bc = bc_all[:, h:h + 1]
            br = br_all[h:h + 1, :]
            ir = i_r[h:h + 1, :]
            ic = i_c[:, h:h + 1]
            m_prev = m_ref[idx]
            c_prev = c_ref[idx]
            n_prev = n_ref[idx]
            logw = jnp.where(causal, bc - br + ir, -jnp.inf)
            inter = bc + m_prev
            m_row = jnp.maximum(jnp.max(logw, axis=-1, keepdims=True), inter)
            kb = k.astype(BF16)
            s = _dot_nt(qb, kb) * jnp.exp(logw - m_row)
            isc = jnp.exp(inter - m_row)
            num = _dot(s.astype(BF16), vb) + isc * _dot_nt(qb, c_prev.astype(BF16))
            den = jnp.sum(s, axis=-1, keepdims=True) + isc * jnp.sum(q * n_prev, axis=-1, keepdims=True)
            hout = num / jnp.maximum(jnp.abs(den), jnp.exp(-m_row))
            b_end = bc[L - 1:L, :]
            logg = b_end - bc + ic
            m_new = jnp.maximum(b_end + m_prev, jnp.max(logg, axis=0, keepdims=True))
            wk = jnp.exp(logg - m_new)
            decay = jnp.exp(b_end + m_prev - m_new)
            c_ref[idx] = decay * c_prev + _dot_tn((v * wk).astype(BF16), kb)
            n_ref[idx] = decay * n_prev + jnp.sum(wk * k, axis=0, keepdims=True)
            m_ref[idx] = m_new
            hn = hout * lax.rsqrt(jnp.mean(hout * hout, axis=-1, keepdims=True) + RMS_EPS)
            out_ref[b, :, h * dh:(h + 1) * dh] = (hn * hn_ref[:, h * dh:(h + 1) * dh] * jax.nn.sigmoid(o)
                                                  ).astype(out_ref.dtype)


def _mlstm(main, g, gt, gate_bias, head_norm, nb, seq):
    t, n = main.shape
    nh = MLSTM_HEADS
    w = head_norm.shape[0]
    dh = w // nh
    L = MLSTM_CHUNK
    nc = seq // L
    main3 = main.reshape(nb, seq, n)
    g3 = g.reshape(nb, seq, 2 * nh)
    gt4 = gt.reshape(2 * nh, nb, nc, L).transpose(1, 2, 0, 3)
    body = functools.partial(_mlstm_body, nb=nb, nh=nh, dh=dh)
    out = pl.pallas_call(
        body, grid=(nc,),
        in_specs=[pl.BlockSpec((nb, L, 4 * w), lambda c: (0, c, 0)),
                  pl.BlockSpec((nb, L, 2 * nh), lambda c: (0, c, 0)),
                  pl.BlockSpec((nb, 1, 2 * nh, L), lambda c: (0, c, 0, 0)),
                  pl.BlockSpec((1, 2 * nh), lambda c: (0, 0)),
                  pl.BlockSpec((2 * nh, 1), lambda c: (0, 0)),
                  pl.BlockSpec((1, w), lambda c: (0, 0))],
        out_specs=pl.BlockSpec((nb, L, w), lambda c: (0, c, 0)),
        out_shape=jax.ShapeDtypeStruct((nb, seq, w), BF16),
        scratch_shapes=[pltpu.VMEM((nb * nh, dh, dh), F32), pltpu.VMEM((nb * nh, 1, dh), F32),
                        pltpu.VMEM((nb * nh, 1, 1), F32)],
        compiler_params=_cparams(("arbitrary",)), name="mlstm",
    )(main3, g3, gt4, gate_bias.reshape(1, 2 * nh), gate_bias.reshape(2 * nh, 1), head_norm.reshape(1, w))
    return out.reshape(t, w)


def _out_a_body(h_ref, hm_ref, gb_ref, gc_ref, xin_ref, pgc_ref, pxin_ref, cw_ref, w_ref, out_ref, *, tm, seq):
    i = pl.program_id(0)
    wm = hm_ref.shape[1]
    p = gc_ref[...].astype(F32) * xin_ref[...].astype(F32)
    first = (i * tm) % seq == 0
    hr = pgc_ref.shape[0]
    pp = jnp.where(first, 0.0, pgc_ref[...].astype(F32) * pxin_ref[...].astype(F32))
    rowi = lax.broadcasted_iota(jnp.int32, p.shape, 0)
    p1 = jnp.where(rowi == 0, pp[hr - 1:hr, :], pltpu.roll(p, 1, axis=0))
    p2 = jnp.where(rowi == 0, pp[hr - 2:hr - 1, :], jnp.where(rowi == 1, pp[hr - 1:hr, :], pltpu.roll(p, 2, axis=0)))
    yc = gb_ref[...].astype(F32) * (cw_ref[0:1, :] * p2 + cw_ref[1:2, :] * p1 + cw_ref[2:3, :] * p)
    out_ref[...] = (h_ref[...] + _dot(hm_ref[...], w_ref[:wm, :]) + _dot(yc.astype(BF16), w_ref[wm:, :]))


def _out_a(h, hm, main, conv_w, w_out, seq):
    t, d = h.shape
    wm = hm.shape[1]
    wc = conv_w.shape[1]
    tm = TM_PROJ
    cb = (4 * wm) // wc
    halo = 16
    rb = tm // halo
    prev = lambda i: jnp.maximum(i * rb - 1, 0)
    body = functools.partial(_out_a_body, tm=tm, seq=seq)
    return pl.pallas_call(
        body, grid=(t // tm,),
        in_specs=[pl.BlockSpec((tm, d), lambda i: (i, 0)),
                  pl.BlockSpec((tm, wm), lambda i: (i, 0)),
                  pl.BlockSpec((tm, wc), lambda i: (i, cb)),
                  pl.BlockSpec((tm, wc), lambda i: (i, cb + 1)),
                  pl.BlockSpec((tm, wc), lambda i: (i, cb + 2)),
                  pl.BlockSpec((halo, wc), lambda i: (prev(i), cb + 1)),
                  pl.BlockSpec((halo, wc), lambda i: (prev(i), cb + 2)),
                  pl.BlockSpec(conv_w.shape, lambda i: (0, 0)),
                  pl.BlockSpec(w_out.shape, lambda i: (0, 0))],
        out_specs=pl.BlockSpec((tm, d), lambda i: (i, 0)),
        out_shape=jax.ShapeDtypeStruct((t, d), F32),
        compiler_params=_cparams(("parallel",)), name="out_a",
    )(h, hm, main, main, main, main, main, conv_w, w_out)


def _route(logits, lane):
    ne = MOE_GROUPS * MOE_EPG
    big = 1e9
    gl = jnp.where((lane >= ne) & (lane < ne + MOE_GROUPS), logits, -jnp.inf)
    gmax = jnp.max(gl, axis=-1, keepdims=True)
    gidx = jnp.min(jnp.where(gl == gmax, lane - ne, big), axis=-1, keepdims=True)
    gval = 1.0 / jnp.sum(jnp.exp(gl - gmax), axis=-1, keepdims=True)
    lo = gidx * MOE_EPG
    sel = jnp.where((lane >= lo) & (lane < lo + MOE_EPG), logits, -jnp.inf)
    l1 = jnp.max(sel, axis=-1, keepdims=True)
    i1 = jnp.min(jnp.where(sel == l1, lane, big), axis=-1, keepdims=True)
    sel2 = jnp.where(lane == i1, -jnp.inf, sel)
    l2 = jnp.max(sel2, axis=-1, keepdims=True)
    i2 = jnp.min(jnp.where(sel2 == l2, lane, big), axis=-1, keepdims=True)
    r = jnp.exp(l2 - l1)
    w1 = gval / (1.0 + r)
    return i1, i2, w1, w1 * r


ROW_TILE = 8


def _rows_to_tiles(ref, x):
    n = x.shape[0]
    for c in range(ROW_TILE):
        ref[pl.ds(c, n, stride=ROW_TILE), :] = x[:, c * 128:(c + 1) * 128]


def _tiles_to_rows(ref, n):
    return jnp.concatenate([ref[pl.ds(c, n, stride=ROW_TILE), :] for c in range(ROW_TILE)], axis=1)


def _lane_put(lane, cols):
    out = jnp.where(lane == 0.0, cols[0], 0.0)
    for k in range(1, len(cols)):
        out = out + jnp.where(lane == float(k), cols[k], 0.0)
    return out


def _lane_get(lane, x, idx_col):
    return jnp.sum(jnp.where(lane == idx_col, x, 0.0), axis=-1, keepdims=True)


def _route_body(h_ref, nw_ref, wr_ref, br_ref, tri_ref, sel_ref, info_ref, pos_ref, tile_ref):
    xn = _rms(h_ref[...], nw_ref[...])
    hi = xn.astype(BF16)
    lo = (xn - hi.astype(F32)).astype(BF16)
    tm = hi.shape[0]
    r = _dot(jnp.concatenate([hi, lo], axis=0), wr_ref[...])
    logits = r[:tm, :ROUTE_LANES] + r[tm:, :ROUTE_LANES] + r[:tm, ROUTE_LANES:] + br_ref[...]
    lane = lax.broadcasted_iota(jnp.int32, logits.shape, 1).astype(F32)
    i1, i2, w1, w2 = _route(logits, lane)
    ind = jnp.where((lane == i1) | (lane == i2), 1.0, 0.0)
    ahead = _dot(tri_ref[...], ind.astype(BF16))
    tcnt = jnp.broadcast_to(jnp.sum(ind, axis=0, keepdims=True), tile_ref.shape)
    lane8 = lax.broadcasted_iota(jnp.int32, tile_ref.shape, 1)
    sub8 = lax.broadcasted_iota(jnp.int32, tile_ref.shape, 0)
    incl = tcnt
    sh = 1
    while sh < tile_ref.shape[1]:
        incl = incl + jnp.where(lane8 >= sh, pltpu.roll(incl, sh, axis=1), 0.0)
        sh *= 2
    toff = incl - tcnt
    local = (ahead + toff[0:1, :]) * ROW_TILE
    info_ref[...] = _lane_put(lane, [w1, w2])
    tile_ref[...] = jnp.where(sub8 == 0, tcnt, jnp.where(sub8 == 1, toff, 0.0))
    cols = []
    for v in (_lane_get(lane, local, i1), _lane_get(lane, local, i2)):
        vh = jnp.floor(v * (1.0 / 256.0))
        cols += [vh, v - 256.0 * vh]
    tr = _dot_nt(sel_ref[...], _lane_put(lane, cols).astype(BF16))
    subt = lax.broadcasted_iota(jnp.int32, tr.shape, 0)
    pos_ref[...] = jnp.where(subt == 0, tr[0:1, :] * 256.0 + tr[1:2, :],
                             jnp.where(subt == 1, tr[2:3, :] * 256.0 + tr[3:4, :], 0.0)).astype(jnp.int32)


def _dispatch_body(cnt_s, off_s, tot_s, h_ref, nw_ref, pos_s, runs_s, xs_ref, xn_buf, blk, zbuf, sem, *,
                   tm, tile_rows, ne, n_tiles):
    i = pl.program_id(0)
    last = pl.num_programs(0) - 1
    slots = MOE_TOP_K * tm * ROW_TILE

    def wait_runs(half):
        pltpu.make_async_copy(blk.at[half], xs_ref.at[pl.ds(0, slots)], sem.at[half]).wait()

    for half in range(2):
        rs = slice(half * tm, (half + 1) * tm)

        @pl.when(i > 0)
        def _drain_previous(half=half):
            wait_runs(half)

        _rows_to_tiles(xn_buf, _rms(h_ref[rs, :], nw_ref[...]))

        def place(t, carry, half=half):
            row = xn_buf[pl.ds(pl.multiple_of(t * ROW_TILE, ROW_TILE), ROW_TILE), :]
            for k in range(MOE_TOP_K):
                blk[half, pl.ds(pl.multiple_of(pos_s[k, half * tm + t], ROW_TILE), ROW_TILE), :] = row
            return carry

        lax.fori_loop(0, tm, place, 0, unroll=16)
        for e in range(ne):
            n = runs_s[half, 2, e] * ROW_TILE

            @pl.when(n > 0)
            def _send(e=e, n=n, half=half):
                src = pl.multiple_of(runs_s[half, 0, e] * ROW_TILE, ROW_TILE)
                dst = pl.multiple_of(runs_s[half, 1, e] * ROW_TILE, ROW_TILE)
                pltpu.make_async_copy(blk.at[half, pl.ds(src, n)], xs_ref.at[pl.ds(dst, n)], sem.at[half]
                                      ).start(priority=e % 2)

    @pl.when(i == last)
    def _zero_unused_rows():
        wait_runs(0)
        wait_runs(1)
        zbuf[...] = jnp.zeros_like(zbuf)

        def fill(row, nrows):
            at = pl.multiple_of(row * ROW_TILE, ROW_TILE)
            n = nrows * ROW_TILE
            c = pltpu.make_async_copy(zbuf.at[pl.ds(0, n)], xs_ref.at[pl.ds(at, n)], sem.at[2])
            c.start()
            c.wait()

        for e in range(ne):
            n_pad = (tile_rows - cnt_s[e] % tile_rows) % tile_rows

            @pl.when(n_pad > 0)
            def _fill(e=e, n_pad=n_pad):
                fill(off_s[e] + cnt_s[e], n_pad)

        def zero_tile(j, carry):
            fill(j * tile_rows, tile_rows)
            return carry

        lax.fori_loop(tot_s[0], n_tiles, zero_tile, 0)


def _expert_body(te_s, blk_s, tot_s, xs_ref, wg_ref, wu_ref, wd_ref, ys_ref):
    valid = pl.program_id(0) < tot_s[0]
    rows = xs_ref.shape[0] // ROW_TILE

    @pl.when(valid)
    def _run():
        x = _tiles_to_rows(xs_ref, rows).astype(BF16)
        hid = jax.nn.silu(_dot(x, wg_ref[0, 0].astype(BF16))) * _dot(x, wu_ref[0, 0].astype(BF16))
        _rows_to_tiles(ys_ref, _dot(hid.astype(BF16), wd_ref[0, 0].astype(BF16)))

    @pl.when(jnp.logical_not(valid))
    def _unused_tile():
        ys_ref[...] = jnp.zeros_like(ys_ref)


def _combine_body(h_ref, info_ref, pos_s, runs_s, next_s, ys_ref, fw_ref, out_ref, blk, y1, y2, sem, *,
                  tm, ne, final):
    i = pl.program_id(0)
    slots = MOE_TOP_K * tm * ROW_TILE
    bufs = (y1, y2)

    def fetch(runs, half):
        for e in range(ne):
            n = runs[half, 2, e] * ROW_TILE

            @pl.when(n > 0)
            def _fetch(e=e, n=n):
                dst = pl.multiple_of(runs[half, 0, e] * ROW_TILE, ROW_TILE)
                src = pl.multiple_of(runs[half, 1, e] * ROW_TILE, ROW_TILE)
                pltpu.make_async_copy(ys_ref.at[pl.ds(src, n)], blk.at[half, pl.ds(dst, n)], sem.at[half]
                                      ).start(priority=e % 2)

    def finish(half):
        pltpu.make_async_copy(ys_ref.at[pl.ds(0, slots)], blk.at[half], sem.at[half]).wait()
        rs = slice(half * tm, (half + 1) * tm)

        def pick(t, carry):
            dst = pl.ds(pl.multiple_of(t * ROW_TILE, ROW_TILE), ROW_TILE)
            for k in range(MOE_TOP_K):
                at = pl.multiple_of(pos_s[k, half * tm + t], ROW_TILE)
                bufs[k][dst, :] = blk[half, pl.ds(at, ROW_TILE), :]
            return carry

        lax.fori_loop(0, tm, pick, 0, unroll=8)
        o = (h_ref[rs, :] + info_ref[rs, 0:1] * _tiles_to_rows(y1, tm) + info_ref[rs, 1:2] * _tiles_to_rows(y2, tm))
        if final:
            o = _rms(o, fw_ref[...])
        out_ref[rs, :] = o

    @pl.when(i == 0)
    def _first():
        fetch(runs_s, 0)

    fetch(runs_s, 1)
    finish(0)

    @pl.when(i < pl.num_programs(0) - 1)
    def _prefetch():
        fetch(next_s, 0)

    finish(1)


def _router_weights(w_group, b_group, w_router, b_router):
    d, ne = w_router.shape
    ng = w_group.shape[1]
    w = jnp.zeros((d, ROUTE_LANES), F32).at[:, :ne].set(w_router).at[:, ne:ne + ng].set(w_group)
    hi = w.astype(BF16)
    lo = (w - hi.astype(F32)).astype(BF16)
    bias = jnp.zeros((1, ROUTE_LANES), F32).at[0, :ne].set(b_router).at[0, ne:ne + ng].set(b_group)
    return jnp.concatenate([hi, lo], axis=1), bias


def _moe(h, nw, w_group, b_group, w_router, b_router, w_gate, w_up, w_down, layer, final_w):
    t, d = h.shape
    assert d == ROW_TILE * 128, "row-as-tile layout needs d_model == 1024"
    _, ne, _, ff = w_gate.shape
    tm = TM_MOE
    te_rows = TM_EXPERT
    nw2 = nw.reshape(1, d)
    wr3, bias = _router_weights(w_group, b_group, w_router, b_router)
    tri = jnp.asarray(np.tril(np.ones((tm, tm), np.float32), -1), BF16)
    nt = t // tm
    sel = jnp.asarray(np.eye(8, ROUTE_LANES, dtype=np.float32), BF16)
    info, pos, tile_info = pl.pallas_call(
        _route_body, grid=(nt,),
        in_specs=[pl.BlockSpec((tm, d), lambda i: (i, 0)),
                  pl.BlockSpec((1, d), lambda i: (0, 0)),
                  pl.BlockSpec((d, 2 * ROUTE_LANES), lambda i: (0, 0)),
                  pl.BlockSpec((1, ROUTE_LANES), lambda i: (0, 0)),
                  pl.BlockSpec((tm, tm), lambda i: (0, 0)),
                  pl.BlockSpec((8, ROUTE_LANES), lambda i: (0, 0))],
        out_specs=[pl.BlockSpec((tm, ROUTE_LANES), lambda i: (i, 0)), pl.BlockSpec((8, tm), lambda i: (0, i)),
                   pl.BlockSpec((8, ROUTE_LANES), lambda i: (i, 0))],
        out_shape=[jax.ShapeDtypeStruct((t, ROUTE_LANES), F32), jax.ShapeDtypeStruct((8, t), jnp.int32),
                   jax.ShapeDtypeStruct((nt * 8, ROUTE_LANES), F32)],
        compiler_params=_cparams(("parallel",)), name="moe_route",
    )(h, nw2, wr3, bias, tri, sel)

    tinfo = tile_info.reshape(nt, 8, ROUTE_LANES).astype(jnp.int32)
    tcnt = tinfo[:, 0, :]
    cnt_i = jnp.sum(tcnt, axis=0)[:ne]
    ntile = (cnt_i + te_rows - 1) // te_rows
    tile_end = jnp.cumsum(ntile)
    off_i = (tile_end - ntile) * te_rows
    n_tiles = (MOE_TOP_K * t) // te_rows + ne
    rows_total = n_tiles * te_rows
    ti = jnp.arange(n_tiles, dtype=jnp.int32)
    tot = tile_end[-1:]
    ti_c = jnp.minimum(ti, tot[0] - 1)
    tile_e = jnp.sum((ti_c[:, None] >= tile_end[None, :]).astype(jnp.int32), axis=1)
    off_pad = jnp.zeros((ROUTE_LANES,), jnp.int32).at[:ne].set(off_i)
    gstart = off_pad[None, :] + jnp.cumsum(tcnt, axis=0) - tcnt
    runs = jnp.zeros((nt, 8, ROUTE_LANES), jnp.int32).at[:, 0].set(tinfo[:, 1, :]).at[:, 1].set(gstart).at[:, 2].set(tcnt)

    xs = pl.pallas_call(
        functools.partial(_dispatch_body, tm=tm, tile_rows=te_rows, ne=ne, n_tiles=n_tiles),
        grid_spec=pltpu.PrefetchScalarGridSpec(
            num_scalar_prefetch=3, grid=(nt // 2,),
            in_specs=[pl.BlockSpec((2 * tm, d), lambda i, *_: (i, 0)),
                      pl.BlockSpec((1, d), lambda i, *_: (0, 0)),
                      pl.BlockSpec((8, 2 * tm), lambda i, *_: (0, i), memory_space=pltpu.SMEM),
                      pl.BlockSpec((2, 8, ROUTE_LANES), lambda i, *_: (i, 0, 0), memory_space=pltpu.SMEM)],
            out_specs=pl.BlockSpec(memory_space=pl.ANY),
            scratch_shapes=[pltpu.VMEM((tm * ROW_TILE, 128), F32),
                            pltpu.VMEM((2, MOE_TOP_K * tm * ROW_TILE, 128), F32),
                            pltpu.VMEM((te_rows * ROW_TILE, 128), F32), pltpu.SemaphoreType.DMA((3,))]),
        out_shape=jax.ShapeDtypeStruct((rows_total * ROW_TILE, 128), F32),
        compiler_params=_cparams(("arbitrary",)), name="moe_dispatch",
    )(cnt_i, off_i, tot, h, nw2, pos, runs)

    ys = pl.pallas_call(
        _expert_body,
        grid_spec=pltpu.PrefetchScalarGridSpec(
            num_scalar_prefetch=3, grid=(n_tiles,),
            in_specs=[pl.BlockSpec((te_rows * ROW_TILE, 128), lambda i, e, b, v: (b[i], 0)),
                      pl.BlockSpec((1, 1, d, ff), lambda i, e, b, v: (layer, e[i], 0, 0)),
                      pl.BlockSpec((1, 1, d, ff), lambda i, e, b, v: (layer, e[i], 0, 0)),
                      pl.BlockSpec((1, 1, ff, d), lambda i, e, b, v: (layer, e[i], 0, 0))],
            out_specs=pl.BlockSpec((te_rows * ROW_TILE, 128), lambda i, e, b, v: (i, 0))),
        out_shape=jax.ShapeDtypeStruct((rows_total * ROW_TILE, 128), F32),
        compiler_params=_cparams(("arbitrary",)), name="moe_expert",
    )(tile_e, ti_c, tot, xs, w_gate, w_up, w_down)

    final = final_w is not None
    fw = (final_w if final else nw).reshape(1, d)
    return pl.pallas_call(
        functools.partial(_combine_body, tm=tm, ne=ne, final=final), grid=(nt // 2,),
        in_specs=[pl.BlockSpec((2 * tm, d), lambda i: (i, 0)),
                  pl.BlockSpec((2 * tm, ROUTE_LANES), lambda i: (i, 0)),
                  pl.BlockSpec((8, 2 * tm), lambda i: (0, i), memory_space=pltpu.SMEM),
                  pl.BlockSpec((2, 8, ROUTE_LANES), lambda i: (i, 0, 0), memory_space=pltpu.SMEM),
                  pl.BlockSpec((2, 8, ROUTE_LANES), lambda i: (jnp.minimum(i + 1, nt // 2 - 1), 0, 0),
                               memory_space=pltpu.SMEM),
                  pl.BlockSpec(memory_space=pl.ANY),
                  pl.BlockSpec((1, d), lambda i: (0, 0))],
        out_specs=pl.BlockSpec((2 * tm, d), lambda i: (i, 0)),
        out_shape=jax.ShapeDtypeStruct((t, d), F32),
        scratch_shapes=[pltpu.VMEM((2, MOE_TOP_K * tm * ROW_TILE, 128), F32), pltpu.VMEM((tm * ROW_TILE, 128), F32),
                        pltpu.VMEM((tm * ROW_TILE, 128), F32), pltpu.SemaphoreType.DMA((2,))],
        compiler_params=_cparams(("arbitrary",)), name="moe_combine",
    )(h, info, pos, runs, runs, ys, fw)


def _s5_weights(lam_re, lam_im, b_re, b_im, c_re, c_im, log_dt, nsteps):
    L = S5_CHUNK
    g, p = lam_re.shape
    ch = b_re.shape[-1]
    lam = lax.complex(lam_re.astype(F32), lam_im.astype(F32))
    dt = jnp.exp(log_dt.astype(F32))[:, None]
    lam_bar = jnp.exp(lam * dt)
    b_bar = ((lam_bar - 1.0) / lam)[..., None] * lax.complex(b_re.astype(F32), b_im.astype(F32))
    cmat = lax.complex(c_re.astype(F32), c_im.astype(F32))
    pows = [jnp.ones_like(lam_bar)]
    for _ in range(L):
        pows.append(pows[-1] * lam_bar)
    pw = jnp.stack(pows, axis=1)
    kern = jnp.real(jnp.einsum('gtop,gpi->gito', cmat[:, None, :, :] * pw[:, :L, None, :], b_bar))
    kpad = jnp.pad(kern.reshape(g, ch, L * ch), ((0, 0), (0, 0), ((L - 1) * ch, 0)))
    toep = jnp.stack([kpad[:, :, (L - 1 - s) * ch:(2 * L - 1 - s) * ch] for s in range(L)], axis=1)
    toep = toep.reshape(g, L * ch, L * ch)
    wst = pw[:, L - 1 - np.arange(L)][:, :, :, None] * b_bar[:, None, :, :]
    wst = wst.transpose(0, 1, 3, 2).reshape(g, L * ch, p)
    wst = jnp.concatenate([jnp.real(wst), jnp.imag(wst)], axis=-1)
    mo = cmat.transpose(0, 2, 1)[:, :, None, :] * pw[:, 1:L + 1].transpose(0, 2, 1)[:, :, :, None]
    mo = mo.reshape(g, p, L * ch)
    wout = jnp.concatenate([jnp.real(mo), -jnp.imag(mo)], axis=1)
    a = pw[:, L]
    ars, ais = [], []
    for _ in range(nsteps):
        ars.append(jnp.concatenate([jnp.real(a), jnp.real(a)], axis=-1))
        ais.append(jnp.concatenate([-jnp.imag(a), jnp.imag(a)], axis=-1))
        a = a * a
    wcat = jnp.concatenate([toep, wst], axis=-1).astype(BF16)
    return wcat, wout.astype(BF16), jnp.stack(ars, axis=1), jnp.stack(ais, axis=1)


def _s5_body(u_ref, wcat_ref, wout_ref, ar_ref, ai_ref, y_ref, us_ref, ys_ref, *, nsteps):
    L = S5_CHUNK
    ch = S5_GROUP_CH
    gpc = 128 // ch
    ny = L * ch
    nc = u_ref.shape[0] // L
    for s in range(L):
        us_ref[s] = u_ref[pl.ds(s, nc, stride=L), :]
    lane = lax.broadcasted_iota(jnp.int32, (nc, 128), 1)
    ridx = lax.broadcasted_iota(jnp.int32, (nc, 128), 0)

    def shift(x, k):
        return jnp.where(ridx >= k, pltpu.roll(x, k, axis=0), 0.0)

    for gi in range(gpc):
        halves = []
        for hh in range(ny // 128):
            acc = None
            for s8 in range(gpc):
                rot = ((s8 - gi) * ch) % 128
                src = us_ref[hh * gpc + s8]
                if rot:
                    src = pltpu.roll(src, rot, axis=1)
                slot = (lane >= s8 * ch) & (lane < (s8 + 1) * ch)
                acc = jnp.where(slot, src, 0.0) if acc is None else jnp.where(slot, src, acc)
            halves.append(acc)
        ug = jnp.concatenate(halves, axis=1).astype(BF16)
        r = _dot(ug, wcat_ref[gi])
        y1 = r[:, :ny]
        z = r[:, ny:]
        w = shift(z, 1)
        wx = pltpu.roll(w, z.shape[1] // 2, axis=1)
        for k in range(nsteps):
            if (1 << k) >= nc:
                break
            sk = shift(w, 1 << k)
            sx = shift(wx, 1 << k)
            ar = ar_ref[gi, k:k + 1, :]
            ai = ai_ref[gi, k:k + 1, :]
            w, wx = w + sk * ar + sx * ai, wx + sx * ar - sk * ai
        yg = y1 + _dot(w.astype(BF16), wout_ref[gi])
        slot = (lane >= gi * ch) & (lane < (gi + 1) * ch)
        for t in range(L):
            src = yg[:, (t // gpc) * 128:(t // gpc + 1) * 128]
            rot = ((gi - t % gpc) * ch) % 128
            if rot:
                src = pltpu.roll(src, rot, axis=1)
            ys_ref[t] = jnp.where(slot, src, 0.0) if gi == 0 else jnp.where(slot, src, ys_ref[t])
    for t in range(L):
        y_ref[pl.ds(t, nc, stride=L), :] = ys_ref[t]


def _s5(main, col0, width, nb, seq, lam_re, lam_im, b_re, b_im, c_re, c_im, log_dt):
    t = main.shape[0]
    L = S5_CHUNK
    ch = S5_GROUP_CH
    g = width // ch
    nc = seq // L
    gpc = 128 // ch
    ncol = width // 128
    assert col0 % 128 == 0 and width % 128 == 0 and (L * ch) % 128 == 0
    nsteps = max(1, (nc - 1).bit_length())
    wcat, wout, ar, ai = _s5_weights(lam_re, lam_im, b_re, b_im, c_re, c_im, log_dt, nsteps)
    body = functools.partial(_s5_body, nsteps=nsteps)
    return pl.pallas_call(
        body, grid=(ncol, nb),
        in_specs=[pl.BlockSpec((seq, 128), lambda j, b: (b, col0 // 128 + j)),
                  pl.BlockSpec((gpc,) + wcat.shape[1:], lambda j, b: (j, 0, 0)),
                  pl.BlockSpec((gpc,) + wout.shape[1:], lambda j, b: (j, 0, 0)),
                  pl.BlockSpec((gpc,) + ar.shape[1:], lambda j, b: (j, 0, 0)),
                  pl.BlockSpec((gpc,) + ai.shape[1:], lambda j, b: (j, 0, 0))],
        out_specs=pl.BlockSpec((seq, 128), lambda j, b: (b, j)),
        out_shape=jax.ShapeDtypeStruct((t, width), F32),
        scratch_shapes=[pltpu.VMEM((L, nc, 128), F32), pltpu.VMEM((L, nc, 128), F32)],
        compiler_params=_cparams(("parallel", "parallel")), name="s5",
    )(main, wcat, wout, ar, ai)


def _hgrn_gmat():
    L = HGRN_CHUNK
    blocks = 2 + int(np.log2(L))
    gm = np.zeros((blocks * L, L), np.float32)
    for j in range(L):
        gm[j, :j + 1] = 1.0
        gm[L + j, j + 1:] = 1.0
    li, m = 2, L
    while m >= 2:
        half = m // 2
        for j in range(L):
            pos = j % m
            r = j - pos + half - 1
            if pos >= half:
                gm[li * L + j, r + 1:j + 1] = 1.0
            else:
                gm[li * L + j, j + 1:r + 1] = 1.0
        li += 1
        m //= 2
    return gm


def _hgrn_body(mh_ref, fg_ref, gm_ref, lb_ref, nw_ref, out_ref, st_ref, *, nb, nh, dh):
    L = HGRN_CHUNK
    w = nh * dh

    @pl.when(pl.program_id(0) == 0)
    def _init():
        st_ref[...] = jnp.zeros_like(st_ref)

    row = lax.broadcasted_iota(jnp.int32, (L, 2 * L), 0)
    col = lax.broadcasted_iota(jnp.int32, (L, 2 * L), 1) & (L - 1)
    rowd = lax.broadcasted_iota(jnp.int32, (L, 2 * dh), 0)
    laned = lax.broadcasted_iota(jnp.int32, (L, 2 * dh), 1)
    first = laned < dh
    eye = row == col

    def blockdiag(x):
        z = jnp.zeros_like(x)
        return jnp.concatenate([jnp.where(first, x, z), jnp.where(first, z, x)], axis=0)

    gm2 = gm_ref[...]
    lb = lb_ref[...]
    zst = jnp.zeros((dh, dh), BF16)
    for b in range(nb):
        fg = fg_ref[b]
        f = lb + (1.0 - lb) * jax.nn.sigmoid(fg)
        kk = (1.0 - lb) * jax.nn.sigmoid(-fg)
        lf = jnp.log(f)
        hi = lf.astype(BF16)
        mid = (lf - hi.astype(F32)).astype(BF16)
        p_all = jnp.exp(_dot(gm2, jnp.concatenate([hi, mid], axis=0)))
        for hp in range(nh // 2):
            i0 = b * nh + 2 * hp
            cs = slice(2 * hp * dh, (2 * hp + 2) * dh)
            qb = mh_ref[b, :, 2 * hp * dh:(2 * hp + 2) * dh]
            vb = mh_ref[b, :, w + 2 * hp * dh:w + (2 * hp + 2) * dh]
            og = mh_ref[b, :, 2 * w + 2 * hp * dh:2 * w + (2 * hp + 2) * dh].astype(F32)
            q = qb.astype(F32)
            k = kk[:, cs]
            pb = p_all[0:L, cs]
            pe = p_all[L:2 * L, cs]
            st0 = st_ref[i0]
            st1 = st_ref[i0 + 1]
            stbd = jnp.concatenate([jnp.concatenate([st0.astype(BF16), zst], axis=1),
                                    jnp.concatenate([zst, st1.astype(BF16)], axis=1)], axis=0)
            o = _dot_nt((q * pb).astype(BF16), stbd)
            attn = jnp.where(eye, _dot_nt(qb, blockdiag(k.astype(BF16))), 0.0)
            li, m = 2, L
            while m >= 2:
                pl_ = p_all[li * L:(li + 1) * L, cs]
                up = (rowd & (m - 1)) >= (m // 2)
                ql = jnp.where(up, q * pl_, 0.0).astype(BF16)
                kl = jnp.where(up, 0.0, k * pl_).astype(BF16)
                same = (row & ~(m - 1)) == (col & ~(m - 1))
                attn = attn + jnp.where(same, _dot_nt(ql, blockdiag(kl)), 0.0)
                li += 1
                m //= 2
            o = o + _dot(attn.astype(BF16), blockdiag(vb))
            kh = (k * pe).astype(BF16)
            for j in range(2):
                hs = slice(j * dh, (j + 1) * dh)
                gs = slice((2 * hp + j) * dh, (2 * hp + j + 1) * dh)
                st = st0 if j == 0 else st1
                st_ref[i0 + j] = st * pb[L - 1:L, hs] + _dot_tn(vb[:, hs], kh[:, hs])
                oj = o[:, hs]
                on = oj * lax.rsqrt(jnp.mean(oj * oj, axis=-1, keepdims=True) + RMS_EPS)
                out_ref[b, :, gs] = (on * nw_ref[:, gs] * jax.nn.silu(og[:, hs])).astype(out_ref.dtype)


def _hgrn(mh, mf, lower_bound, out_norm, nb, seq):
    t = mh.shape[0]
    nh = HGRN_HEADS
    w = out_norm.shape[0]
    dh = w // nh
    L = HGRN_CHUNK
    nc = seq // L
    gm = _hgrn_gmat()
    gm = jnp.asarray(np.concatenate([gm, gm], axis=1), BF16)
    body = functools.partial(_hgrn_body, nb=nb, nh=nh, dh=dh)
    out = pl.pallas_call(
        body, grid=(nc,),
        in_specs=[pl.BlockSpec((nb, L, 3 * w), lambda c: (0, c, 0)),
                  pl.BlockSpec((nb, L, w), lambda c: (0, c, 0)),
                  pl.BlockSpec(gm.shape, lambda c: (0, 0)),
                  pl.BlockSpec((1, w), lambda c: (0, 0)),
                  pl.BlockSpec((1, w), lambda c: (0, 0))],
        out_specs=pl.BlockSpec((nb, L, w), lambda c: (0, c, 0)),
        out_shape=jax.ShapeDtypeStruct((nb, seq, w), BF16),
        scratch_shapes=[pltpu.VMEM((nb * nh, dh, dh), F32)],
        compiler_params=_cparams(("arbitrary",)), name="hgrn",
    )(mh.reshape(nb, seq, mh.shape[1]), mf.reshape(nb, seq, mf.shape[1]), gm, lower_bound.reshape(1, w),
      out_norm.reshape(1, w))
    return out.reshape(t, w)


def _out_c_body(h_ref, ys_ref, u_ref, oh_ref, d_ref, wglu_ref, bglu_ref, w_ref, out_ref):
    ws = ys_ref.shape[1]
    z = jax.nn.gelu(ys_ref[...] + d_ref[...] * u_ref[...])
    gate = jax.nn.sigmoid(_dot(z.astype(BF16), wglu_ref[...]) + bglu_ref[...])
    out_ref[...] = (h_ref[...] + _dot((z * gate).astype(BF16), w_ref[:ws, :]) + _dot(oh_ref[...], w_ref[ws:, :]))


def _out_c(h, ys, main, oh, d_skip, w_glu, b_glu, w_out):
    t, d = h.shape
    ws = ys.shape[1]
    wh = oh.shape[1]
    tm = TM_PROJ
    ub = (main.shape[1] - ws) // ws
    return pl.pallas_call(
        _out_c_body, grid=(t // tm,),
        in_specs=[pl.BlockSpec((tm, d), lambda i: (i, 0)),
                  pl.BlockSpec((tm, ws), lambda i: (i, 0)),
                  pl.BlockSpec((tm, ws), lambda i: (i, ub)),
                  pl.BlockSpec((tm, wh), lambda i: (i, 0)),
                  pl.BlockSpec((1, ws), lambda i: (0, 0)),
                  pl.BlockSpec(w_glu.shape, lambda i: (0, 0)),
                  pl.BlockSpec((1, ws), lambda i: (0, 0)),
                  pl.BlockSpec(w_out.shape, lambda i: (0, 0))],
        out_specs=pl.BlockSpec((tm, d), lambda i: (i, 0)),
        out_shape=jax.ShapeDtypeStruct((t, d), F32),
        compiler_params=_cparams(("parallel",)), name="out_c",
    )(h, ys, main, oh, d_skip.reshape(1, ws), w_glu, b_glu.reshape(1, ws), w_out)


def kernel(x, norm_mix, norm_ffn, norm_final, ab_w_in, ab_gate_bias, ab_head_norm, ab_conv_w, ab_w_out, cd_w_in, s5_lambda_re, s5_lambda_im, s5_b_re, s5_b_im, s5_c_re, s5_c_im, s5_d, s5_log_dt, s5_w_glu, s5_b_glu, hgrn_lb, hgrn_out_norm, cd_w_out, moe_w_group, moe_b_group, moe_w_router, moe_b_router, moe_w_gate, moe_w_up, moe_w_down):
    nb, seq, d = x.shape
    depth = norm_mix.shape[0]
    h = x.reshape(nb * seq, d)
    for layer in range(depth):
        j = layer // 2
        if layer % 2 == 0:
            wm = ab_head_norm.shape[1]
            ng = ab_gate_bias.shape[1]
            w_in = ab_w_in[j]
            w_main = jnp.concatenate([w_in[:, :4 * wm], w_in[:, 4 * wm + ng:]], axis=1).astype(BF16)
            w_gates = w_in[:, 4 * wm:4 * wm + ng].astype(BF16)
            main, g, gt = _proj(h, norm_mix[layer], w_main, w_gates)
            hm = _mlstm(main, g, gt, ab_gate_bias[j], ab_head_norm[j], nb, seq)
            h = _out_a(h, hm, main, ab_conv_w[j], ab_w_out[j].astype(BF16), seq)
        else:
            ws = s5_d.shape[1]
            w_in = cd_w_in[j]
            wh = hgrn_out_norm.shape[1]
            cu, cq, cf, ci, co = (w_in[:, :ws], w_in[:, ws:ws + wh], w_in[:, ws + wh:ws + 2 * wh],
                                  w_in[:, ws + 2 * wh:ws + 3 * wh], w_in[:, ws + 3 * wh:])
            w_main = jnp.concatenate([cq, ci, co, cf, cu], axis=1).astype(BF16)
            mh, mf = _proj(h, norm_mix[layer], w_main, n_lo=3 * wh)
            sm = jax.nn.softmax(hgrn_lb.astype(F32), axis=0)
            lower_bound = jnp.cumsum(sm, axis=0)[layer] - sm[0]
            ys = _s5(mf, wh, ws, nb, seq, s5_lambda_re[j], s5_lambda_im[j], s5_b_re[j], s5_b_im[j],
                     s5_c_re[j], s5_c_im[j], s5_log_dt[j])
            oh = _hgrn(mh, mf, lower_bound, hgrn_out_norm[j], nb, seq)
            h = _out_c(h, ys, mf, oh, s5_d[j], s5_w_glu[j].astype(BF16), s5_b_glu[j], cd_w_out[j].astype(BF16))
        h = _moe(h, norm_ffn[layer], moe_w_group[layer], moe_b_group[layer], moe_w_router[layer], moe_b_router[layer],
                 moe_w_gate, moe_w_up, moe_w_down, layer,
                 norm_final if layer == depth - 1 else None)
    return h.reshape(nb, seq, d)
```

```python
import functools

import numpy as np
import jax
import jax.numpy as jnp
from jax import lax
from jax.experimental import pallas as pl
from jax.experimental.pallas import tpu as pltpu

F32 = jnp.float32
BF16 = jnp.bfloat16
RMS_EPS = 1e-6
MLSTM_CHUNK = 512
HGRN_CHUNK = 128
S5_CHUNK = 16
S5_GROUP_CH = 16
S5_STATE = 64
MLSTM_HEADS = 4
HGRN_HEADS = 4
MOE_GROUPS = 4
MOE_EPG = 8
ROUTE_LANES = 128
TM_PROJ = 1024
TM_MOE = 512
TM_EXPERT = 512
MOE_TOP_K = 2
VMEM_LIMIT = 56 * 1024 * 1024

_NT = (((1,), (1,)), ((), ()))
_TN = (((0,), (0,)), ((), ()))


def _cparams(sem):
    return pltpu.CompilerParams(dimension_semantics=sem, vmem_limit_bytes=VMEM_LIMIT)


def _rms(x, w):
    return x * lax.rsqrt(jnp.mean(x * x, axis=-1, keepdims=True) + RMS_EPS) * w


def _split3(x):
    hi = x.astype(BF16)
    r = x - hi.astype(F32)
    mid = r.astype(BF16)
    lo = (r - mid.astype(F32)).astype(BF16)
    return hi, mid, lo


def _dot(a, b):
    return jnp.dot(a, b, preferred_element_type=F32)


def _dot_nt(a, b):
    return lax.dot_general(a, b, _NT, preferred_element_type=F32)


def _dot_tn(a, b):
    return lax.dot_general(a, b, _TN, preferred_element_type=F32)


def _proj_gates_body(x_ref, nw_ref, w_ref, wg_ref, wgt_ref, main_ref, g_ref, gt_ref):
    xn = _rms(x_ref[...], nw_ref[...]).astype(BF16)
    main_ref[...] = _dot(xn, w_ref[...]).astype(main_ref.dtype)
    g_ref[...] = _dot(xn, wg_ref[...])[:, : g_ref.shape[1]]
    gt_ref[...] = _dot_nt(wgt_ref[...], xn)


def _proj_split_body(x_ref, nw_ref, w_ref, lo_ref, hi_ref):
    xn = _rms(x_ref[...], nw_ref[...]).astype(BF16)
    r = _dot(xn, w_ref[...])
    n_lo = lo_ref.shape[1]
    lo_ref[...] = r[:, :n_lo].astype(lo_ref.dtype)
    hi_ref[...] = r[:, n_lo:]


def _proj(h, nw, w_main, w_gates=None, n_lo=None):
    t, d = h.shape
    n = w_main.shape[1]
    tm = TM_PROJ
    x_spec = pl.BlockSpec((tm, d), lambda i: (i, 0))
    nw_spec = pl.BlockSpec((1, d), lambda i: (0, 0))
    w_spec = pl.BlockSpec((d, n), lambda i: (0, 0))
    if w_gates is None:
        return pl.pallas_call(
            _proj_split_body, grid=(t // tm,), in_specs=[x_spec, nw_spec, w_spec],
            out_specs=[pl.BlockSpec((tm, n_lo), lambda i: (i, 0)), pl.BlockSpec((tm, n - n_lo), lambda i: (i, 0))],
            out_shape=[jax.ShapeDtypeStruct((t, n_lo), BF16), jax.ShapeDtypeStruct((t, n - n_lo), F32)],
            compiler_params=_cparams(("parallel",)), name="proj",
        )(h, nw.reshape(1, d), w_main)
    ng = w_gates.shape[1]
    wg_pad = jnp.zeros((d, 128), BF16).at[:, :ng].set(w_gates)
    return pl.pallas_call(
        _proj_gates_body, grid=(t // tm,),
        in_specs=[x_spec, nw_spec, w_spec, pl.BlockSpec((d, 128), lambda i: (0, 0)),
                  pl.BlockSpec((ng, d), lambda i: (0, 0))],
        out_specs=[pl.BlockSpec((tm, n), lambda i: (i, 0)), pl.BlockSpec((tm, ng), lambda i: (i, 0)),
                   pl.BlockSpec((ng, tm), lambda i: (0, i))],
        out_shape=[jax.ShapeDtypeStruct((t, n), BF16), jax.ShapeDtypeStruct((t, ng), F32),
                   jax.ShapeDtypeStruct((ng, t), F32)],
        compiler_params=_cparams(("parallel",)), name="proj_gates",
    )(h, nw.reshape(1, d), w_main, wg_pad, w_gates.T)


def _mlstm_body(main_ref, g_ref, gt_ref, br_ref, bc_ref, hn_ref, out_ref, c_ref, n_ref, m_ref, *, nb, nh, dh):
    L = MLSTM_CHUNK
    w = nh * dh

    @pl.when(pl.program_id(0) == 0)
    def _init():
        c_ref[...] = jnp.zeros_like(c_ref)
        n_ref[...] = jnp.zeros_like(n_ref)
        m_ref[...] = jnp.full_like(m_ref, -1e30)

    row = lax.broadcasted_iota(jnp.int32, (L, L), 0)
    col = lax.broadcasted_iota(jnp.int32, (L, L), 1)
    causal = col <= row
    tril = causal.astype(BF16)
    triu = (row <= col).astype(BF16)
    scale = dh ** -0.5
    for b in range(nb):
        g = g_ref[b] + br_ref[...]
        gt = gt_ref[b, 0] + bc_ref[...]
        i_c = g[:, :nh]
        i_r = gt[:nh, :]
        lfc = _split3(jax.nn.log_sigmoid(g[:, nh:]))
        lfr = _split3(jax.nn.log_sigmoid(gt[nh:, :]))
        bc_all = _dot(tril, lfc[0]) + _dot(tril, lfc[1]) + _dot(tril, lfc[2])
        br_all = _dot(lfr[0], triu) + _dot(lfr[1], triu) + _dot(lfr[2], triu)
        for h in range(nh):
            idx = b * nh + h
            qb = main_ref[b, :, h * dh:(h + 1) * dh]
            vb = main_ref[b, :, 2 * w + h * dh:2 * w + (h + 1) * dh]
            q = qb.astype(F32)
            k = main_ref[b, :, w + h * dh:w + (h + 1) * dh].astype(F32) * scale
            v = vb.astype(F32)
            o = main_ref[b, :, 3 * w + h * dh:3 * w + (h + 1) * dh].astype(F32)
            bc = bc_all[:, h:h + 1]
            br = br_all[h:h + 1, :]
            ir = i_r[h:h + 1, :]
            ic = i_c[:, h:h + 1]
            m_prev = m_ref[idx]
            c_prev = c_ref[idx]
            n_prev = n_ref[idx]
            logw = jnp.where(causal, bc - br + ir, -jnp.inf)
            inter = bc + m_prev
            m_row = jnp.maximum(jnp.max(logw, axis=-1, keepdims=True), inter)
            kb = k.astype(BF16)
            s = _dot_nt(qb, kb) * jnp.exp(logw - m_row)
            isc = jnp.exp(inter - m_row)
            num = _dot(s.astype(BF16), vb) + isc * _dot_nt(qb, c_prev.astype(BF16))
            den = jnp.sum(s, axis=-1, keepdims=True) + isc * jnp.sum(q * n_prev, axis=-1, keepdims=True)
            hout = num / jnp.maximum(jnp.abs(den), jnp.exp(-m_row))
            b_end = bc[L - 1:L, :]
            logg = b_end - bc + ic
            m_new = jnp.maximum(b_end + m_prev, jnp.max(logg, axis=0, keepdims=True))
            wk = jnp.exp(logg - m_new)
            decay = jnp.exp(b_end + m_prev - m_new)
            c_ref[idx] = decay * c_prev + _dot_tn((v * wk).astype(BF16), kb)
            n_ref[idx] = decay * n_prev + jnp.sum(wk * k, axis=0, keepdims=True)
            m_ref[idx] = m_new
            hn = hout * lax.rsqrt(jnp.mean(hout * hout, axis=-1, keepdims=True) + RMS_EPS)
            out_ref[b, :, h * dh:(h + 1) * dh] = (hn * hn_ref[:, h * dh:(h + 1) * dh] * jax.nn.sigmoid(o)
                                                  ).astype(out_ref.dtype)


def _mlstm(main, g, gt, gate_bias, head_norm, nb, seq):
    t, n = main.shape
    nh = MLSTM_HEADS
    w = head_norm.shape[0]
    dh = w // nh
    L = MLSTM_CHUNK
    nc = seq // L
    main3 = main.reshape(nb, seq, n)
    g3 = g.reshape(nb, seq, 2 * nh)
    gt4 = gt.reshape(2 * nh, nb, nc, L).transpose(1, 2, 0, 3)
    body = functools.partial(_mlstm_body, nb=nb, nh=nh, dh=dh)
    out = pl.pallas_call(
        body, grid=(nc,),
        in_specs=[pl.BlockSpec((nb, L, 4 * w), lambda c: (0, c, 0)),
                  pl.BlockSpec((nb, L, 2 * nh), lambda c: (0, c, 0)),
                  pl.BlockSpec((nb, 1, 2 * nh, L), lambda c: (0, c, 0, 0)),
                  pl.BlockSpec((1, 2 * nh), lambda c: (0, 0)),
                  pl.BlockSpec((2 * nh, 1), lambda c: (0, 0)),
                  pl.BlockSpec((1, w), lambda c: (0, 0))],
        out_specs=pl.BlockSpec((nb, L, w), lambda c: (0, c, 0)),
        out_shape=jax.ShapeDtypeStruct((nb, seq, w), BF16),
        scratch_shapes=[pltpu.VMEM((nb * nh, dh, dh), F32), pltpu.VMEM((nb * nh, 1, dh), F32),
                        pltpu.VMEM((nb * nh, 1, 1), F32)],
        compiler_params=_cparams(("arbitrary",)), name="mlstm",
    )(main3, g3, gt4, gate_bias.reshape(1, 2 * nh), gate_bias.reshape(2 * nh, 1), head_norm.reshape(1, w))
    return out.reshape(t, w)


def _out_a_body(h_ref, hm_ref, gb_ref, gc_ref, xin_ref, pgc_ref, pxin_ref, cw_ref, w_ref, out_ref, *, tm, seq):
    i = pl.program_id(0)
    wm = hm_ref.shape[1]
    p = gc_ref[...].astype(F32) * xin_ref[...].astype(F32)
    first = (i * tm) % seq == 0
    hr = pgc_ref.shape[0]
    pp = jnp.where(first, 0.0, pgc_ref[...].astype(F32) * pxin_ref[...].astype(F32))
    rowi = lax.broadcasted_iota(jnp.int32, p.shape, 0)
    p1 = jnp.where(rowi == 0, pp[hr - 1:hr, :], pltpu.roll(p, 1, axis=0))
    p2 = jnp.where(rowi == 0, pp[hr - 2:hr - 1, :], jnp.where(rowi == 1, pp[hr - 1:hr, :], pltpu.roll(p, 2, axis=0)))
    yc = gb_ref[...].astype(F32) * (cw_ref[0:1, :] * p2 + cw_ref[1:2, :] * p1 + cw_ref[2:3, :] * p)
    out_ref[...] = (h_ref[...] + _dot(hm_ref[...], w_ref[:wm, :]) + _dot(yc.astype(BF16), w_ref[wm:, :]))


def _out_a(h, hm, main, conv_w, w_out, seq):
    t, d = h.shape
    wm = hm.shape[1]
    wc = conv_w.shape[1]
    tm = TM_PROJ
    cb = (4 * wm) // wc
    assert seq % tm == 0, "a token tile must not straddle two sequences (causal conv halo)"
    halo = 16
    rb = tm // halo
    prev = lambda i: jnp.maximum(i * rb - 1, 0)
    body = functools.partial(_out_a_body, tm=tm, seq=seq)
    return pl.pallas_call(
        body, grid=(t // tm,),
        in_specs=[pl.BlockSpec((tm, d), lambda i: (i, 0)),
                  pl.BlockSpec((tm, wm), lambda i: (i, 0)),
                  pl.BlockSpec((tm, wc), lambda i: (i, cb)),
                  pl.BlockSpec((tm, wc), lambda i: (i, cb + 1)),
                  pl.BlockSpec((tm, wc), lambda i: (i, cb + 2)),
                  pl.BlockSpec((halo, wc), lambda i: (prev(i), cb + 1)),
                  pl.BlockSpec((halo, wc), lambda i: (prev(i), cb + 2)),
                  pl.BlockSpec(conv_w.shape, lambda i: (0, 0)),
                  pl.BlockSpec(w_out.shape, lambda i: (0, 0))],
        out_specs=pl.BlockSpec((tm, d), lambda i: (i, 0)),
        out_shape=jax.ShapeDtypeStruct((t, d), F32),
        compiler_params=_cparams(("parallel",)), name="out_a",
    )(h, hm, main, main, main, main, main, conv_w, w_out)


def _route(logits, lane):
    ne = MOE_GROUPS * MOE_EPG
    big = 1e9
    gl = jnp.where((lane >= ne) & (lane < ne + MOE_GROUPS), logits, -jnp.inf)
    gmax = jnp.max(gl, axis=-1, keepdims=True)
    gidx = jnp.min(jnp.where(gl == gmax, lane - ne, big), axis=-1, keepdims=True)
    gval = 1.0 / jnp.sum(jnp.exp(gl - gmax), axis=-1, keepdims=True)
    lo = gidx * MOE_EPG
    sel = jnp.where((lane >= lo) & (lane < lo + MOE_EPG), logits, -jnp.inf)
    l1 = jnp.max(sel, axis=-1, keepdims=True)
    i1 = jnp.min(jnp.where(sel == l1, lane, big), axis=-1, keepdims=True)
    sel2 = jnp.where(lane == i1, -jnp.inf, sel)
    l2 = jnp.max(sel2, axis=-1, keepdims=True)
    i2 = jnp.min(jnp.where(sel2 == l2, lane, big), axis=-1, keepdims=True)
    r = jnp.exp(l2 - l1)
    w1 = gval / (1.0 + r)
    return i1, i2, w1, w1 * r


ROW_TILE = 8


def _rows_to_tiles(ref, x):
    n = x.shape[0]
    for c in range(ROW_TILE):
        ref[pl.ds(c, n, stride=ROW_TILE), :] = x[:, c * 128:(c + 1) * 128]


def _tiles_to_rows(ref, n):
    return jnp.concatenate([ref[pl.ds(c, n, stride=ROW_TILE), :] for c in range(ROW_TILE)], axis=1)


def _lane_put(lane, cols):
    out = jnp.where(lane == 0.0, cols[0], 0.0)
    for k in range(1, len(cols)):
        out = out + jnp.where(lane == float(k), cols[k], 0.0)
    return out


def _lane_get(lane, x, idx_col):
    return jnp.sum(jnp.where(lane == idx_col, x, 0.0), axis=-1, keepdims=True)


def _route_body(h_ref, nw_ref, wr_ref, br_ref, tri_ref, sel_ref, info_ref, pos_ref, tile_ref):
    xn = _rms(h_ref[...], nw_ref[...])
    hi = xn.astype(BF16)
    lo = (xn - hi.astype(F32)).astype(BF16)
    tm = hi.shape[0]
    r = _dot(jnp.concatenate([hi, lo], axis=0), wr_ref[...])
    logits = r[:tm, :ROUTE_LANES] + r[tm:, :ROUTE_LANES] + r[:tm, ROUTE_LANES:] + br_ref[...]
    lane = lax.broadcasted_iota(jnp.int32, logits.shape, 1).astype(F32)
    i1, i2, w1, w2 = _route(logits, lane)
    ind = jnp.where((lane == i1) | (lane == i2), 1.0, 0.0)
    ahead = _dot(tri_ref[...], ind.astype(BF16))
    tcnt = jnp.broadcast_to(jnp.sum(ind, axis=0, keepdims=True), tile_ref.shape)
    lane8 = lax.broadcasted_iota(jnp.int32, tile_ref.shape, 1)
    sub8 = lax.broadcasted_iota(jnp.int32, tile_ref.shape, 0)
    incl = tcnt
    sh = 1
    while sh < tile_ref.shape[1]:
        incl = incl + jnp.where(lane8 >= sh, pltpu.roll(incl, sh, axis=1), 0.0)
        sh *= 2
    toff = incl - tcnt
    local = (ahead + toff[0:1, :]) * ROW_TILE
    info_ref[...] = _lane_put(lane, [w1, w2])
    tile_ref[...] = jnp.where(sub8 == 0, tcnt, jnp.where(sub8 == 1, toff, 0.0))
    cols = []
    for v in (_lane_get(lane, local, i1), _lane_get(lane, local, i2)):
        vh = jnp.floor(v * (1.0 / 256.0))
        cols += [vh, v - 256.0 * vh]
    tr = _dot_nt(sel_ref[...], _lane_put(lane, cols).astype(BF16))
    subt = lax.broadcasted_iota(jnp.int32, tr.shape, 0)
    pos_ref[...] = jnp.where(subt == 0, tr[0:1, :] * 256.0 + tr[1:2, :],
                             jnp.where(subt == 1, tr[2:3, :] * 256.0 + tr[3:4, :], 0.0)).astype(jnp.int32)


def _dispatch_body(cnt_s, off_s, tot_s, h_ref, nw_ref, pos_s, runs_s, xs_ref, xn_buf, blk, zbuf, sem, *,
                   tm, tile_rows, ne, n_tiles):
    i = pl.program_id(0)
    last = pl.num_programs(0) - 1
    slots = MOE_TOP_K * tm * ROW_TILE

    def wait_runs(half):
        pltpu.make_async_copy(blk.at[half], xs_ref.at[pl.ds(0, slots)], sem.at[half]).wait()

    for half in range(2):
        rs = slice(half * tm, (half + 1) * tm)

        @pl.when(i > 0)
        def _drain_previous(half=half):
            wait_runs(half)

        _rows_to_tiles(xn_buf, _rms(h_ref[rs, :], nw_ref[...]))

        def place(t, carry, half=half):
            row = xn_buf[pl.ds(pl.multiple_of(t * ROW_TILE, ROW_TILE), ROW_TILE), :]
            for k in range(MOE_TOP_K):
                blk[half, pl.ds(pl.multiple_of(pos_s[k, half * tm + t], ROW_TILE), ROW_TILE), :] = row
            return carry

        lax.fori_loop(0, tm, place, 0, unroll=16)
        for e in range(ne):
            n = runs_s[half, 2, e] * ROW_TILE

            @pl.when(n > 0)
            def _send(e=e, n=n, half=half):
                src = pl.multiple_of(runs_s[half, 0, e] * ROW_TILE, ROW_TILE)
                dst = pl.multiple_of(runs_s[half, 1, e] * ROW_TILE, ROW_TILE)
                pltpu.make_async_copy(blk.at[half, pl.ds(src, n)], xs_ref.at[pl.ds(dst, n)], sem.at[half]
                                      ).start(priority=e % 2)

    @pl.when(i == last)
    def _zero_unused_rows():
        wait_runs(0)
        wait_runs(1)
        zbuf[...] = jnp.zeros_like(zbuf)

        def fill(row, nrows):
            at = pl.multiple_of(row * ROW_TILE, ROW_TILE)
            n = nrows * ROW_TILE
            c = pltpu.make_async_copy(zbuf.at[pl.ds(0, n)], xs_ref.at[pl.ds(at, n)], sem.at[2])
            c.start()
            c.wait()

        for e in range(ne):
            n_pad = (tile_rows - cnt_s[e] % tile_rows) % tile_rows

            @pl.when(n_pad > 0)
            def _fill(e=e, n_pad=n_pad):
                fill(off_s[e] + cnt_s[e], n_pad)

        def zero_tile(j, carry):
            fill(j * tile_rows, tile_rows)
            return carry

        lax.fori_loop(tot_s[0], n_tiles, zero_tile, 0)


def _expert_body(te_s, blk_s, tot_s, xs_ref, wg_ref, wu_ref, wd_ref, ys_ref):
    valid = pl.program_id(0) < tot_s[0]
    rows = xs_ref.shape[0] // ROW_TILE

    @pl.when(valid)
    def _run():
        x = _tiles_to_rows(xs_ref, rows).astype(BF16)
        hid = jax.nn.silu(_dot(x, wg_ref[0, 0].astype(BF16))) * _dot(x, wu_ref[0, 0].astype(BF16))
        _rows_to_tiles(ys_ref, _dot(hid.astype(BF16), wd_ref[0, 0].astype(BF16)))

    @pl.when(jnp.logical_not(valid))
    def _unused_tile():
        ys_ref[...] = jnp.zeros_like(ys_ref)


def _combine_body(h_ref, info_ref, pos_s, runs_s, next_s, ys_ref, fw_ref, out_ref, blk, y1, y2, sem, *,
                  tm, ne, final):
    i = pl.program_id(0)
    slots = MOE_TOP_K * tm * ROW_TILE
    bufs = (y1, y2)

    def fetch(runs, half):
        for e in range(ne):
            n = runs[half, 2, e] * ROW_TILE

            @pl.when(n > 0)
            def _fetch(e=e, n=n):
                dst = pl.multiple_of(runs[half, 0, e] * ROW_TILE, ROW_TILE)
                src = pl.multiple_of(runs[half, 1, e] * ROW_TILE, ROW_TILE)
                pltpu.make_async_copy(ys_ref.at[pl.ds(src, n)], blk.at[half, pl.ds(dst, n)], sem.at[half]
                                      ).start(priority=e % 2)

    def finish(half):
        pltpu.make_async_copy(ys_ref.at[pl.ds(0, slots)], blk.at[half], sem.at[half]).wait()
        rs = slice(half * tm, (half + 1) * tm)

        def pick(t, carry):
            dst = pl.ds(pl.multiple_of(t * ROW_TILE, ROW_TILE), ROW_TILE)
            for k in range(MOE_TOP_K):
                at = pl.multiple_of(pos_s[k, half * tm + t], ROW_TILE)
                bufs[k][dst, :] = blk[half, pl.ds(at, ROW_TILE), :]
            return carry

        lax.fori_loop(0, tm, pick, 0, unroll=8)
        o = (h_ref[rs, :] + info_ref[rs, 0:1] * _tiles_to_rows(y1, tm) + info_ref[rs, 1:2] * _tiles_to_rows(y2, tm))
        if final:
            o = _rms(o, fw_ref[...])
        out_ref[rs, :] = o

    @pl.when(i == 0)
    def _first():
        fetch(runs_s, 0)

    fetch(runs_s, 1)
    finish(0)

    @pl.when(i < pl.num_programs(0) - 1)
    def _prefetch():
        fetch(next_s, 0)

    finish(1)


def _router_weights(w_group, b_group, w_router, b_router):
    d, ne = w_router.shape
    ng = w_group.shape[1]
    w = jnp.zeros((d, ROUTE_LANES), F32).at[:, :ne].set(w_router).at[:, ne:ne + ng].set(w_group)
    hi = w.astype(BF16)
    lo = (w - hi.astype(F32)).astype(BF16)
    bias = jnp.zeros((1, ROUTE_LANES), F32).at[0, :ne].set(b_router).at[0, ne:ne + ng].set(b_group)
    return jnp.concatenate([hi, lo], axis=1), bias


def _moe(h, nw, w_group, b_group, w_router, b_router, w_gate, w_up, w_down, layer, final_w):
    t, d = h.shape
    assert d == ROW_TILE * 128, "row-as-tile layout needs d_model == 1024"
    _, ne, _, ff = w_gate.shape
    tm = TM_MOE
    te_rows = TM_EXPERT
    nw2 = nw.reshape(1, d)
    wr3, bias = _router_weights(w_group, b_group, w_router, b_router)
    tri = jnp.asarray(np.tril(np.ones((tm, tm), np.float32), -1), BF16)
    nt = t // tm
    sel = jnp.asarray(np.eye(8, ROUTE_LANES, dtype=np.float32), BF16)
    info, pos, tile_info = pl.pallas_call(
        _route_body, grid=(nt,),
        in_specs=[pl.BlockSpec((tm, d), lambda i: (i, 0)),
                  pl.BlockSpec((1, d), lambda i: (0, 0)),
                  pl.BlockSpec((d, 2 * ROUTE_LANES), lambda i: (0, 0)),
                  pl.BlockSpec((1, ROUTE_LANES), lambda i: (0, 0)),
                  pl.BlockSpec((tm, tm), lambda i: (0, 0)),
                  pl.BlockSpec((8, ROUTE_LANES), lambda i: (0, 0))],
        out_specs=[pl.BlockSpec((tm, ROUTE_LANES), lambda i: (i, 0)), pl.BlockSpec((8, tm), lambda i: (0, i)),
                   pl.BlockSpec((8, ROUTE_LANES), lambda i: (i, 0))],
        out_shape=[jax.ShapeDtypeStruct((t, ROUTE_LANES), F32), jax.ShapeDtypeStruct((8, t), jnp.int32),
                   jax.ShapeDtypeStruct((nt * 8, ROUTE_LANES), F32)],
        compiler_params=_cparams(("parallel",)), name="moe_route",
    )(h, nw2, wr3, bias, tri, sel)

    tinfo = tile_info.reshape(nt, 8, ROUTE_LANES).astype(jnp.int32)
    tcnt = tinfo[:, 0, :]
    cnt_i = jnp.sum(tcnt, axis=0)[:ne]
    ntile = (cnt_i + te_rows - 1) // te_rows
    tile_end = jnp.cumsum(ntile)
    off_i = (tile_end - ntile) * te_rows
    n_tiles = (MOE_TOP_K * t) // te_rows + ne
    rows_total = n_tiles * te_rows
    ti = jnp.arange(n_tiles, dtype=jnp.int32)
    tot = tile_end[-1:]
    ti_c = jnp.minimum(ti, tot[0] - 1)
    tile_e = jnp.sum((ti_c[:, None] >= tile_end[None, :]).astype(jnp.int32), axis=1)
    off_pad = jnp.zeros((ROUTE_LANES,), jnp.int32).at[:ne].set(off_i)
    gstart = off_pad[None, :] + jnp.cumsum(tcnt, axis=0) - tcnt
    runs = jnp.zeros((nt, 8, ROUTE_LANES), jnp.int32).at[:, 0].set(tinfo[:, 1, :]).at[:, 1].set(gstart).at[:, 2].set(tcnt)

    xs = pl.pallas_call(
        functools.partial(_dispatch_body, tm=tm, tile_rows=te_rows, ne=ne, n_tiles=n_tiles),
        grid_spec=pltpu.PrefetchScalarGridSpec(
            num_scalar_prefetch=3, grid=(nt // 2,),
            in_specs=[pl.BlockSpec((2 * tm, d), lambda i, *_: (i, 0)),
                      pl.BlockSpec((1, d), lambda i, *_: (0, 0)),
                      pl.BlockSpec((8, 2 * tm), lambda i, *_: (0, i), memory_space=pltpu.SMEM),
                      pl.BlockSpec((2, 8, ROUTE_LANES), lambda i, *_: (i, 0, 0), memory_space=pltpu.SMEM)],
            out_specs=pl.BlockSpec(memory_space=pl.ANY),
            scratch_shapes=[pltpu.VMEM((tm * ROW_TILE, 128), F32),
                            pltpu.VMEM((2, MOE_TOP_K * tm * ROW_TILE, 128), F32),
                            pltpu.VMEM((te_rows * ROW_TILE, 128), F32), pltpu.SemaphoreType.DMA((3,))]),
        out_shape=jax.ShapeDtypeStruct((rows_total * ROW_TILE, 128), F32),
        compiler_params=_cparams(("arbitrary",)), name="moe_dispatch",
    )(cnt_i, off_i, tot, h, nw2, pos, runs)

    ys = pl.pallas_call(
        _expert_body,
        grid_spec=pltpu.PrefetchScalarGridSpec(
            num_scalar_prefetch=3, grid=(n_tiles,),
            in_specs=[pl.BlockSpec((te_rows * ROW_TILE, 128), lambda i, e, b, v: (b[i], 0)),
                      pl.BlockSpec((1, 1, d, ff), lambda i, e, b, v: (layer, e[i], 0, 0)),
                      pl.BlockSpec((1, 1, d, ff), lambda i, e, b, v: (layer, e[i], 0, 0)),
                      pl.BlockSpec((1, 1, ff, d), lambda i, e, b, v: (layer, e[i], 0, 0))],
            out_specs=pl.BlockSpec((te_rows * ROW_TILE, 128), lambda i, e, b, v: (i, 0))),
        out_shape=jax.ShapeDtypeStruct((rows_total * ROW_TILE, 128), F32),
        compiler_params=_cparams(("arbitrary",)), name="moe_expert",
    )(tile_e, ti_c, tot, xs, w_gate, w_up, w_down)

    final = final_w is not None
    fw = (final_w if final else nw).reshape(1, d)
    return pl.pallas_call(
        functools.partial(_combine_body, tm=tm, ne=ne, final=final), grid=(nt // 2,),
        in_specs=[pl.BlockSpec((2 * tm, d), lambda i: (i, 0)),
                  pl.BlockSpec((2 * tm, ROUTE_LANES), lambda i: (i, 0)),
                  pl.BlockSpec((8, 2 * tm), lambda i: (0, i), memory_space=pltpu.SMEM),
                  pl.BlockSpec((2, 8, ROUTE_LANES), lambda i: (i, 0, 0), memory_space=pltpu.SMEM),
                  pl.BlockSpec((2, 8, ROUTE_LANES), lambda i: (jnp.minimum(i + 1, nt // 2 - 1), 0, 0),
                               memory_space=pltpu.SMEM),
                  pl.BlockSpec(memory_space=pl.ANY),
                  pl.BlockSpec((1, d), lambda i: (0, 0))],
        out_specs=pl.BlockSpec((2 * tm, d), lambda i: (i, 0)),
        out_shape=jax.ShapeDtypeStruct((t, d), F32),
        scratch_shapes=[pltpu.VMEM((2, MOE_TOP_K * tm * ROW_TILE, 128), F32), pltpu.VMEM((tm * ROW_TILE, 128), F32),
                        pltpu.VMEM((tm * ROW_TILE, 128), F32), pltpu.SemaphoreType.DMA((2,))],
        compiler_params=_cparams(("arbitrary",)), name="moe_combine",
    )(h, info, pos, runs, runs, ys, fw)


def _s5_weights(lam_re, lam_im, b_re, b_im, c_re, c_im, log_dt, nsteps):
    L = S5_CHUNK
    g, p = lam_re.shape
    ch = b_re.shape[-1]
    lam = lax.complex(lam_re.astype(F32), lam_im.astype(F32))
    dt = jnp.exp(log_dt.astype(F32))[:, None]
    lam_bar = jnp.exp(lam * dt)
    b_bar = ((lam_bar - 1.0) / lam)[..., None] * lax.complex(b_re.astype(F32), b_im.astype(F32))
    cmat = lax.complex(c_re.astype(F32), c_im.astype(F32))
    pows = [jnp.ones_like(lam_bar)]
    for _ in range(L):
        pows.append(pows[-1] * lam_bar)
    pw = jnp.stack(pows, axis=1)
    kern = jnp.real(jnp.einsum('gtop,gpi->gito', cmat[:, None, :, :] * pw[:, :L, None, :], b_bar))
    kpad = jnp.pad(kern.reshape(g, ch, L * ch), ((0, 0), (0, 0), ((L - 1) * ch, 0)))
    toep = jnp.stack([kpad[:, :, (L - 1 - s) * ch:(2 * L - 1 - s) * ch] for s in range(L)], axis=1)
    toep = toep.reshape(g, L * ch, L * ch)
    wst = pw[:, L - 1 - np.arange(L)][:, :, :, None] * b_bar[:, None, :, :]
    wst = wst.transpose(0, 1, 3, 2).reshape(g, L * ch, p)
    wst = jnp.concatenate([jnp.real(wst), jnp.imag(wst)], axis=-1)
    mo = cmat.transpose(0, 2, 1)[:, :, None, :] * pw[:, 1:L + 1].transpose(0, 2, 1)[:, :, :, None]
    mo = mo.reshape(g, p, L * ch)
    wout = jnp.concatenate([jnp.real(mo), -jnp.imag(mo)], axis=1)
    a = pw[:, L]
    ars, ais = [], []
    for _ in range(nsteps):
        ars.append(jnp.concatenate([jnp.real(a), jnp.real(a)], axis=-1))
        ais.append(jnp.concatenate([-jnp.imag(a), jnp.imag(a)], axis=-1))
        a = a * a
    wcat = jnp.concatenate([toep, wst], axis=-1).astype(BF16)
    return wcat, wout.astype(BF16), jnp.stack(ars, axis=1), jnp.stack(ais, axis=1)


def _s5_body(u_ref, wcat_ref, wout_ref, ar_ref, ai_ref, y_ref, us_ref, ys_ref, *, nsteps):
    L = S5_CHUNK
    ch = S5_GROUP_CH
    gpc = 128 // ch
    ny = L * ch
    nc = u_ref.shape[0] // L
    for s in range(L):
        us_ref[s] = u_ref[pl.ds(s, nc, stride=L), :]
    lane = lax.broadcasted_iota(jnp.int32, (nc, 128), 1)
    ridx = lax.broadcasted_iota(jnp.int32, (nc, 128), 0)

    def shift(x, k):
        return jnp.where(ridx >= k, pltpu.roll(x, k, axis=0), 0.0)

    for gi in range(gpc):
        halves = []
        for hh in range(ny // 128):
            acc = None
            for s8 in range(gpc):
                rot = ((s8 - gi) * ch) % 128
                src = us_ref[hh * gpc + s8]
                if rot:
                    src = pltpu.roll(src, rot, axis=1)
                slot = (lane >= s8 * ch) & (lane < (s8 + 1) * ch)
                acc = jnp.where(slot, src, 0.0) if acc is None else jnp.where(slot, src, acc)
            halves.append(acc)
        ug = jnp.concatenate(halves, axis=1).astype(BF16)
        r = _dot(ug, wcat_ref[gi])
        y1 = r[:, :ny]
        z = r[:, ny:]
        w = shift(z, 1)
        wx = pltpu.roll(w, z.shape[1] // 2, axis=1)
        for k in range(nsteps):
            if (1 << k) >= nc:
                break
            sk = shift(w, 1 << k)
            sx = shift(wx, 1 << k)
            ar = ar_ref[gi, k:k + 1, :]
            ai = ai_ref[gi, k:k + 1, :]
            w, wx = w + sk * ar + sx * ai, wx + sx * ar - sk * ai
        yg = y1 + _dot(w.astype(BF16), wout_ref[gi])
        slot = (lane >= gi * ch) & (lane < (gi + 1) * ch)
        for t in range(L):
            src = yg[:, (t // gpc) * 128:(t // gpc + 1) * 128]
            rot = ((gi - t % gpc) * ch) % 128
            if rot:
                src = pltpu.roll(src, rot, axis=1)
            ys_ref[t] = jnp.where(slot, src, 0.0) if gi == 0 else jnp.where(slot, src, ys_ref[t])
    for t in range(L):
        y_ref[pl.ds(t, nc, stride=L), :] = ys_ref[t]


def _s5(main, col0, width, nb, seq, lam_re, lam_im, b_re, b_im, c_re, c_im, log_dt):
    t = main.shape[0]
    L = S5_CHUNK
    ch = S5_GROUP_CH
    g = width // ch
    nc = seq // L
    gpc = 128 // ch
    ncol = width // 128
    assert col0 % 128 == 0 and width % 128 == 0 and (L * ch) % 128 == 0
    nsteps = max(1, (nc - 1).bit_length())
    wcat, wout, ar, ai = _s5_weights(lam_re, lam_im, b_re, b_im, c_re, c_im, log_dt, nsteps)
    body = functools.partial(_s5_body, nsteps=nsteps)
    return pl.pallas_call(
        body, grid=(ncol, nb),
        in_specs=[pl.BlockSpec((seq, 128), lambda j, b: (b, col0 // 128 + j)),
                  pl.BlockSpec((gpc,) + wcat.shape[1:], lambda j, b: (j, 0, 0)),
                  pl.BlockSpec((gpc,) + wout.shape[1:], lambda j, b: (j, 0, 0)),
                  pl.BlockSpec((gpc,) + ar.shape[1:], lambda j, b: (j, 0, 0)),
                  pl.BlockSpec((gpc,) + ai.shape[1:], lambda j, b: (j, 0, 0))],
        out_specs=pl.BlockSpec((seq, 128), lambda j, b: (b, j)),
        out_shape=jax.ShapeDtypeStruct((t, width), F32),
        scratch_shapes=[pltpu.VMEM((L, nc, 128), F32), pltpu.VMEM((L, nc, 128), F32)],
        compiler_params=_cparams(("parallel", "parallel")), name="s5",
    )(main, wcat, wout, ar, ai)


def _hgrn_gmat():
    L = HGRN_CHUNK
    blocks = 2 + int(np.log2(L))
    gm = np.zeros((blocks * L, L), np.float32)
    for j in range(L):
        gm[j, :j + 1] = 1.0
        gm[L + j, j + 1:] = 1.0
    li, m = 2, L
    while m >= 2:
        half = m // 2
        for j in range(L):
            pos = j % m
            r = j - pos + half - 1
            if pos >= half:
                gm[li * L + j, r + 1:j + 1] = 1.0
            else:
                gm[li * L + j, j + 1:r + 1] = 1.0
        li += 1
        m //= 2
    return gm


def _hgrn_body(mh_ref, fg_ref, gm_ref, lb_ref, nw_ref, out_ref, st_ref, *, nb, nh, dh):
    L = HGRN_CHUNK
    w = nh * dh

    @pl.when(pl.program_id(0) == 0)
    def _init():
        st_ref[...] = jnp.zeros_like(st_ref)

    row = lax.broadcasted_iota(jnp.int32, (L, 2 * L), 0)
    col = lax.broadcasted_iota(jnp.int32, (L, 2 * L), 1) & (L - 1)
    rowd = lax.broadcasted_iota(jnp.int32, (L, 2 * dh), 0)
    laned = lax.broadcasted_iota(jnp.int32, (L, 2 * dh), 1)
    first = laned < dh
    eye = row == col

    def blockdiag(x):
        z = jnp.zeros_like(x)
        return jnp.concatenate([jnp.where(first, x, z), jnp.where(first, z, x)], axis=0)

    gm2 = gm_ref[...]
    lb = lb_ref[...]
    zst = jnp.zeros((dh, dh), BF16)
    for b in range(nb):
        fg = fg_ref[b]
        f = lb + (1.0 - lb) * jax.nn.sigmoid(fg)
        kk = (1.0 - lb) * jax.nn.sigmoid(-fg)
        lf = jnp.log(f)
        hi = lf.astype(BF16)
        mid = (lf - hi.astype(F32)).astype(BF16)
        p_all = jnp.exp(_dot(gm2, jnp.concatenate([hi, mid], axis=0)))
        for hp in range(nh // 2):
            i0 = b * nh + 2 * hp
            cs = slice(2 * hp * dh, (2 * hp + 2) * dh)
            qb = mh_ref[b, :, 2 * hp * dh:(2 * hp + 2) * dh]
            vb = mh_ref[b, :, w + 2 * hp * dh:w + (2 * hp + 2) * dh]
            og = mh_ref[b, :, 2 * w + 2 * hp * dh:2 * w + (2 * hp + 2) * dh].astype(F32)
            q = qb.astype(F32)
            k = kk[:, cs]
            pb = p_all[0:L, cs]
            pe = p_all[L:2 * L, cs]
            st0 = st_ref[i0]
            st1 = st_ref[i0 + 1]
            stbd = jnp.concatenate([jnp.concatenate([st0.astype(BF16), zst], axis=1),
                                    jnp.concatenate([zst, st1.astype(BF16)], axis=1)], axis=0)
            o = _dot_nt((q * pb).astype(BF16), stbd)
            attn = jnp.where(eye, _dot_nt(qb, blockdiag(k.astype(BF16))), 0.0)
            li, m = 2, L
            while m >= 2:
                pl_ = p_all[li * L:(li + 1) * L, cs]
                up = (rowd & (m - 1)) >= (m // 2)
                ql = jnp.where(up, q * pl_, 0.0).astype(BF16)
                kl = jnp.where(up, 0.0, k * pl_).astype(BF16)
                same = (row & ~(m - 1)) == (col & ~(m - 1))
                attn = attn + jnp.where(same, _dot_nt(ql, blockdiag(kl)), 0.0)
                li += 1
                m //= 2
            o = o + _dot(attn.astype(BF16), blockdiag(vb))
            kh = (k * pe).astype(BF16)
            for j in range(2):
                hs = slice(j * dh, (j + 1) * dh)
                gs = slice((2 * hp + j) * dh, (2 * hp + j + 1) * dh)
                st = st0 if j == 0 else st1
                st_ref[i0 + j] = st * pb[L - 1:L, hs] + _dot_tn(vb[:, hs], kh[:, hs])
                oj = o[:, hs]
                on = oj * lax.rsqrt(jnp.mean(oj * oj, axis=-1, keepdims=True) + RMS_EPS)
                out_ref[b, :, gs] = (on * nw_ref[:, gs] * jax.nn.silu(og[:, hs])).astype(out_ref.dtype)


def _hgrn(mh, mf, lower_bound, out_norm, nb, seq):
    t = mh.shape[0]
    nh = HGRN_HEADS
    w = out_norm.shape[0]
    dh = w // nh
    L = HGRN_CHUNK
    nc = seq // L
    gm = _hgrn_gmat()
    gm = jnp.asarray(np.concatenate([gm, gm], axis=1), BF16)
    body = functools.partial(_hgrn_body, nb=nb, nh=nh, dh=dh)
    out = pl.pallas_call(
        body, grid=(nc,),
        in_specs=[pl.BlockSpec((nb, L, 3 * w), lambda c: (0, c, 0)),
                  pl.BlockSpec((nb, L, w), lambda c: (0, c, 0)),
                  pl.BlockSpec(gm.shape, lambda c: (0, 0)),
                  pl.BlockSpec((1, w), lambda c: (0, 0)),
                  pl.BlockSpec((1, w), lambda c: (0, 0))],
        out_specs=pl.BlockSpec((nb, L, w), lambda c: (0, c, 0)),
        out_shape=jax.ShapeDtypeStruct((nb, seq, w), BF16),
        scratch_shapes=[pltpu.VMEM((nb * nh, dh, dh), F32)],
        compiler_params=_cparams(("arbitrary",)), name="hgrn",
    )(mh.reshape(nb, seq, mh.shape[1]), mf.reshape(nb, seq, mf.shape[1]), gm, lower_bound.reshape(1, w),
      out_norm.reshape(1, w))
    return out.reshape(t, w)


def _out_c_body(h_ref, ys_ref, u_ref, oh_ref, d_ref, wglu_ref, bglu_ref, w_ref, out_ref):
    ws = ys_ref.shape[1]
    z = jax.nn.gelu(ys_ref[...] + d_ref[...] * u_ref[...])
    gate = jax.nn.sigmoid(_dot(z.astype(BF16), wglu_ref[...]) + bglu_ref[...])
    out_ref[...] = (h_ref[...] + _dot((z * gate).astype(BF16), w_ref[:ws, :]) + _dot(oh_ref[...], w_ref[ws:, :]))


def _out_c(h, ys, main, oh, d_skip, w_glu, b_glu, w_out):
    t, d = h.shape
    ws = ys.shape[1]
    wh = oh.shape[1]
    tm = TM_PROJ
    ub = (main.shape[1] - ws) // ws
    return pl.pallas_call(
        _out_c_body, grid=(t // tm,),
        in_specs=[pl.BlockSpec((tm, d), lambda i: (i, 0)),
                  pl.BlockSpec((tm, ws), lambda i: (i, 0)),
                  pl.BlockSpec((tm, ws), lambda i: (i, ub)),
                  pl.BlockSpec((tm, wh), lambda i: (i, 0)),
                  pl.BlockSpec((1, ws), lambda i: (0, 0)),
                  pl.BlockSpec(w_glu.shape, lambda i: (0, 0)),
                  pl.BlockSpec((1, ws), lambda i: (0, 0)),
                  pl.BlockSpec(w_out.shape, lambda i: (0, 0))],
        out_specs=pl.BlockSpec((tm, d), lambda i: (i, 0)),
        out_shape=jax.ShapeDtypeStruct((t, d), F32),
        compiler_params=_cparams(("parallel",)), name="out_c",
    )(h, ys, main, oh, d_skip.reshape(1, ws), w_glu, b_glu.reshape(1, ws), w_out)


def kernel(x, norm_mix, norm_ffn, norm_final, ab_w_in, ab_gate_bias, ab_head_norm, ab_conv_w, ab_w_out, cd_w_in, s5_lambda_re, s5_lambda_im, s5_b_re, s5_b_im, s5_c_re, s5_c_im, s5_d, s5_log_dt, s5_w_glu, s5_b_glu, hgrn_lb, hgrn_out_norm, cd_w_out, moe_w_group, moe_b_group, moe_w_router, moe_b_router, moe_w_gate, moe_w_up, moe_w_down):
    nb, seq, d = x.shape
    depth = norm_mix.shape[0]
    h = x.reshape(nb * seq, d)
    for layer in range(depth):
        j = layer // 2
        if layer % 2 == 0:
            wm = ab_head_norm.shape[1]
            ng = ab_gate_bias.shape[1]
            w_in = ab_w_in[j]
            w_main = jnp.concatenate([w_in[:, :4 * wm], w_in[:, 4 * wm + ng:]], axis=1).astype(BF16)
            w_gates = w_in[:, 4 * wm:4 * wm + ng].astype(BF16)
            main, g, gt = _proj(h, norm_mix[layer], w_main, w_gates)
            hm = _mlstm(main, g, gt, ab_gate_bias[j], ab_head_norm[j], nb, seq)
            h = _out_a(h, hm, main, ab_conv_w[j], ab_w_out[j].astype(BF16), seq)
        else:
            ws = s5_d.shape[1]
            w_in = cd_w_in[j]
            wh = hgrn_out_norm.shape[1]
            cu, cq, cf, ci, co = (w_in[:, :ws], w_in[:, ws:ws + wh], w_in[:, ws + wh:ws + 2 * wh],
                                  w_in[:, ws + 2 * wh:ws + 3 * wh], w_in[:, ws + 3 * wh:])
            w_main = jnp.concatenate([cq, ci, co, cf, cu], axis=1).astype(BF16)
            mh, mf = _proj(h, norm_mix[layer], w_main, n_lo=3 * wh)
            sm = jax.nn.softmax(hgrn_lb.astype(F32), axis=0)
            lower_bound = jnp.cumsum(sm, axis=0)[layer] - sm[0]
            ys = _s5(mf, wh, ws, nb, seq, s5_lambda_re[j], s5_lambda_im[j], s5_b_re[j], s5_b_im[j],
                     s5_c_re[j], s5_c_im[j], s5_log_dt[j])
            oh = _hgrn(mh, mf, lower_bound, hgrn_out_norm[j], nb, seq)
            h = _out_c(h, ys, mf, oh, s5_d[j], s5_w_glu[j].astype(BF16), s5_b_glu[j], cd_w_out[j].astype(BF16))
        h = _moe(h, norm_ffn[layer], moe_w_group[layer], moe_b_group[layer], moe_w_router[layer], moe_b_router[layer],
                 moe_w_gate, moe_w_up, moe_w_down, layer,
                 norm_final if layer == depth - 1 else None)
    return h.reshape(nb, seq, d)
```

```python
import functools

import numpy as np
import jax
import jax.numpy as jnp
from jax import lax
from jax.experimental import pallas as pl
from jax.experimental.pallas import tpu as pltpu

F32 = jnp.float32
BF16 = jnp.bfloat16
RMS_EPS = 1e-6
MLSTM_CHUNK = 512
HGRN_CHUNK = 128
S5_CHUNK = 16
S5_GROUP_CH = 16
MLSTM_HEADS = 4
HGRN_HEADS = 4
MOE_GROUPS = 4
MOE_EPG = 8
ROUTE_LANES = 128
TM_PROJ = 1024
TM_MOE = 512
TM_EXPERT = 512
MOE_TOP_K = 2
VMEM_LIMIT = 56 * 1024 * 1024

_NT = (((1,), (1,)), ((), ()))
_TN = (((0,), (0,)), ((), ()))


def _cparams(sem):
    return pltpu.CompilerParams(dimension_semantics=sem, vmem_limit_bytes=VMEM_LIMIT)


def _rms(x, w):
    return x * lax.rsqrt(jnp.mean(x * x, axis=-1, keepdims=True) + RMS_EPS) * w


def _split3(x):
    hi = x.astype(BF16)
    r = x - hi.astype(F32)
    mid = r.astype(BF16)
    lo = (r - mid.astype(F32)).astype(BF16)
    return hi, mid, lo


def _dot(a, b):
    return jnp.dot(a, b, preferred_element_type=F32)


def _dot_nt(a, b):
    return lax.dot_general(a, b, _NT, preferred_element_type=F32)


def _dot_tn(a, b):
    return lax.dot_general(a, b, _TN, preferred_element_type=F32)


def _proj_gates_body(x_ref, nw_ref, w_ref, wg_ref, wgt_ref, main_ref, g_ref, gt_ref):
    xn = _rms(x_ref[...], nw_ref[...]).astype(BF16)
    main_ref[...] = _dot(xn, w_ref[...]).astype(main_ref.dtype)
    g_ref[...] = _dot(xn, wg_ref[...])[:, : g_ref.shape[1]]
    gt_ref[...] = _dot_nt(wgt_ref[...], xn)


def _proj_split_body(x_ref, nw_ref, w_ref, lo_ref, hi_ref):
    xn = _rms(x_ref[...], nw_ref[...]).astype(BF16)
    r = _dot(xn, w_ref[...])
    n_lo = lo_ref.shape[1]
    lo_ref[...] = r[:, :n_lo].astype(lo_ref.dtype)
    hi_ref[...] = r[:, n_lo:]


def _proj(h, nw, w_main, w_gates=None, n_lo=None):
    t, d = h.shape
    n = w_main.shape[1]
    tm = TM_PROJ
    x_spec = pl.BlockSpec((tm, d), lambda i: (i, 0))
    nw_spec = pl.BlockSpec((1, d), lambda i: (0, 0))
    w_spec = pl.BlockSpec((d, n), lambda i: (0, 0))
    if w_gates is None:
        return pl.pallas_call(
            _proj_split_body, grid=(t // tm,), in_specs=[x_spec, nw_spec, w_spec],
            out_specs=[pl.BlockSpec((tm, n_lo), lambda i: (i, 0)), pl.BlockSpec((tm, n - n_lo), lambda i: (i, 0))],
            out_shape=[jax.ShapeDtypeStruct((t, n_lo), BF16), jax.ShapeDtypeStruct((t, n - n_lo), F32)],
            compiler_params=_cparams(("parallel",)), name="proj",
        )(h, nw.reshape(1, d), w_main)
    ng = w_gates.shape[1]
    wg_pad = jnp.pad(w_gates, ((0, 0), (0, 128 - ng)))
    return pl.pallas_call(
        _proj_gates_body, grid=(t // tm,),
        in_specs=[x_spec, nw_spec, w_spec, pl.BlockSpec((d, 128), lambda i: (0, 0)),
                  pl.BlockSpec((ng, d), lambda i: (0, 0))],
        out_specs=[pl.BlockSpec((tm, n), lambda i: (i, 0)), pl.BlockSpec((tm, ng), lambda i: (i, 0)),
                   pl.BlockSpec((ng, tm), lambda i: (0, i))],
        out_shape=[jax.ShapeDtypeStruct((t, n), BF16), jax.ShapeDtypeStruct((t, ng), F32),
                   jax.ShapeDtypeStruct((ng, t), F32)],
        compiler_params=_cparams(("parallel",)), name="proj_gates",
    )(h, nw.reshape(1, d), w_main, wg_pad, w_gates.T)


def _mlstm_body(main_ref, g_ref, gt_ref, br_ref, bc_ref, hn_ref, out_ref, c_ref, n_ref, m_ref, *, nb, nh, dh):
    L = MLSTM_CHUNK
    w = nh * dh

    @pl.when(pl.program_id(0) == 0)
    def _init():
        c_ref[...] = jnp.zeros_like(c_ref)
        n_ref[...] = jnp.zeros_like(n_ref)
        m_ref[...] = jnp.full_like(m_ref, -1e30)

    row = lax.broadcasted_iota(jnp.int32, (L, L), 0)
    col = lax.broadcasted_iota(jnp.int32, (L, L), 1)
    causal = col <= row
    tril = causal.astype(BF16)
    triu = (row <= col).astype(BF16)
    scale = dh ** -0.5
    for b in range(nb):
        g = g_ref[b] + br_ref[...]
        gt = gt_ref[b, 0] + bc_ref[...]
        i_c = g[:, :nh]
        i_r = gt[:nh, :]
        lfc = _split3(jax.nn.log_sigmoid(g[:, nh:]))
        lfr = _split3(jax.nn.log_sigmoid(gt[nh:, :]))
        bc_all = _dot(tril, lfc[0]) + _dot(tril, lfc[1]) + _dot(tril, lfc[2])
        br_all = _dot(lfr[0], triu) + _dot(lfr[1], triu) + _dot(lfr[2], triu)
        for h in range(nh):
            idx = b * nh + h
            qb = main_ref[b, :, h * dh:(h + 1) * dh]
            vb = main_ref[b, :, 2 * w + h * dh:2 * w + (h + 1) * dh]
            q = qb.astype(F32)
            k = main_ref[b, :, w + h * dh:w + (h + 1) * dh].astype(F32) * scale
            v = vb.astype(F32)
            o = main_ref[b, :, 3 * w + h * dh:3 * w + (h + 1) * dh].astype(F32)
            bc = bc_all[:, h:h + 1]
            br = br_all[h:h + 1, :]
            ir = i_r[h:h + 1, :]
            ic = i_c[:, h:h + 1]
            m_prev = m_ref[idx]
            c_prev = c_ref[idx]
            n_prev = n_ref[idx]
            logw = jnp.where(causal, bc - br + ir, -jnp.inf)
            inter = bc + m_prev
            m_row = jnp.maximum(jnp.max(logw, axis=-1, keepdims=True), inter)
            kb = k.astype(BF16)
            s = _dot_nt(qb, kb) * jnp.exp(logw - m_row)
            isc = jnp.exp(inter - m_row)
            num = _dot(s.astype(BF16), vb) + isc * _dot_nt(qb, c_prev.astype(BF16))
            den = jnp.sum(s, axis=-1, keepdims=True) + isc * jnp.sum(q * n_prev, axis=-1, keepdims=True)
            hout = num / jnp.maximum(jnp.abs(den), jnp.exp(-m_row))
            b_end = bc[L - 1:L, :]
            logg = b_end - bc + ic
            m_new = jnp.maximum(b_end + m_prev, jnp.max(logg, axis=0, keepdims=True))
            wk = jnp.exp(logg - m_new)
            decay = jnp.exp(b_end + m_prev - m_new)
            c_ref[idx] = decay * c_prev + _dot_tn((v * wk).astype(BF16), kb)
            n_ref[idx] = decay * n_prev + jnp.sum(wk * k, axis=0, keepdims=True)
            m_ref[idx] = m_new
            hn = hout * lax.rsqrt(jnp.mean(hout * hout, axis=-1, keepdims=True) + RMS_EPS)
            out_ref[b, :, h * dh:(h + 1) * dh] = (hn * hn_ref[:, h * dh:(h + 1) * dh] * jax.nn.sigmoid(o)
                                                  ).astype(out_ref.dtype)


def _mlstm(main, g, gt, gate_bias, head_norm, nb, seq):
    t, n = main.shape
    nh = MLSTM_HEADS
    w = head_norm.shape[0]
    dh = w // nh
    L = MLSTM_CHUNK
    nc = seq // L
    main3 = main.reshape(nb, seq, n)
    g3 = g.reshape(nb, seq, 2 * nh)
    gt4 = gt.reshape(2 * nh, nb, nc, L).transpose(1, 2, 0, 3)
    body = functools.partial(_mlstm_body, nb=nb, nh=nh, dh=dh)
    out = pl.pallas_call(
        body, grid=(nc,),
        in_specs=[pl.BlockSpec((nb, L, 4 * w), lambda c: (0, c, 0)),
                  pl.BlockSpec((nb, L, 2 * nh), lambda c: (0, c, 0)),
                  pl.BlockSpec((nb, 1, 2 * nh, L), lambda c: (0, c, 0, 0)),
                  pl.BlockSpec((1, 2 * nh), lambda c: (0, 0)),
                  pl.BlockSpec((2 * nh, 1), lambda c: (0, 0)),
                  pl.BlockSpec((1, w), lambda c: (0, 0))],
        out_specs=pl.BlockSpec((nb, L, w), lambda c: (0, c, 0)),
        out_shape=jax.ShapeDtypeStruct((nb, seq, w), BF16),
        scratch_shapes=[pltpu.VMEM((nb * nh, dh, dh), F32), pltpu.VMEM((nb * nh, 1, dh), F32),
                        pltpu.VMEM((nb * nh, 1, 1), F32)],
        compiler_params=_cparams(("arbitrary",)), name="mlstm",
    )(main3, g3, gt4, gate_bias.reshape(1, 2 * nh), gate_bias.reshape(2 * nh, 1), head_norm.reshape(1, w))
    return out.reshape(t, w)


def _out_a_body(h_ref, hm_ref, gb_ref, gc_ref, xin_ref, pgc_ref, pxin_ref, cw_ref, w_ref, out_ref, *, tm, seq):
    i = pl.program_id(0)
    wm = hm_ref.shape[1]
    p = gc_ref[...].astype(F32) * xin_ref[...].astype(F32)
    first = (i * tm) % seq == 0
    hr = pgc_ref.shape[0]
    pp = jnp.where(first, 0.0, pgc_ref[...].astype(F32) * pxin_ref[...].astype(F32))
    rowi = lax.broadcasted_iota(jnp.int32, p.shape, 0)
    p1 = jnp.where(rowi == 0, pp[hr - 1:hr, :], pltpu.roll(p, 1, axis=0))
    p2 = jnp.where(rowi == 0, pp[hr - 2:hr - 1, :], jnp.where(rowi == 1, pp[hr - 1:hr, :], pltpu.roll(p, 2, axis=0)))
    yc = gb_ref[...].astype(F32) * (cw_ref[0:1, :] * p2 + cw_ref[1:2, :] * p1 + cw_ref[2:3, :] * p)
    out_ref[...] = (h_ref[...] + _dot(hm_ref[...], w_ref[:wm, :]) + _dot(yc.astype(BF16), w_ref[wm:, :]))


def _out_a(h, hm, main, conv_w, w_out, seq):
    t, d = h.shape
    wm = hm.shape[1]
    wc = conv_w.shape[1]
    tm = TM_PROJ
    cb = (4 * wm) // wc
    assert seq % tm == 0, "a token tile must not straddle two sequences (causal conv halo)"
    halo = 16
    rb = tm // halo
    prev = lambda i: jnp.maximum(i * rb - 1, 0)
    body = functools.partial(_out_a_body, tm=tm, seq=seq)
    return pl.pallas_call(
        body, grid=(t // tm,),
        in_specs=[pl.BlockSpec((tm, d), lambda i: (i, 0)),
                  pl.BlockSpec((tm, wm), lambda i: (i, 0)),
                  pl.BlockSpec((tm, wc), lambda i: (i, cb)),
                  pl.BlockSpec((tm, wc), lambda i: (i, cb + 1)),
                  pl.BlockSpec((tm, wc), lambda i: (i, cb + 2)),
                  pl.BlockSpec((halo, wc), lambda i: (prev(i), cb + 1)),
                  pl.BlockSpec((halo, wc), lambda i: (prev(i), cb + 2)),
                  pl.BlockSpec(conv_w.shape, lambda i: (0, 0)),
                  pl.BlockSpec(w_out.shape, lambda i: (0, 0))],
        out_specs=pl.BlockSpec((tm, d), lambda i: (i, 0)),
        out_shape=jax.ShapeDtypeStruct((t, d), F32),
        compiler_params=_cparams(("parallel",)), name="out_a",
    )(h, hm, main, main, main, main, main, conv_w, w_out)


def _route(logits, lane):
    ne = MOE_GROUPS * MOE_EPG
    big = 1e9
    gl = jnp.where((lane >= ne) & (lane < ne + MOE_GROUPS), logits, -jnp.inf)
    gmax = jnp.max(gl, axis=-1, keepdims=True)
    gidx = jnp.min(jnp.where(gl == gmax, lane - ne, big), axis=-1, keepdims=True)
    gval = 1.0 / jnp.sum(jnp.exp(gl - gmax), axis=-1, keepdims=True)
    lo = gidx * MOE_EPG
    sel = jnp.where((lane >= lo) & (lane < lo + MOE_EPG), logits, -jnp.inf)
    l1 = jnp.max(sel, axis=-1, keepdims=True)
    i1 = jnp.min(jnp.where(sel == l1, lane, big), axis=-1, keepdims=True)
    sel2 = jnp.where(lane == i1, -jnp.inf, sel)
    l2 = jnp.max(sel2, axis=-1, keepdims=True)
    i2 = jnp.min(jnp.where(sel2 == l2, lane, big), axis=-1, keepdims=True)
    r = jnp.exp(l2 - l1)
    w1 = gval / (1.0 + r)
    return i1, i2, w1, w1 * r


ROW_TILE = 8


def _rows_to_tiles(ref, x):
    n = x.shape[0]
    for c in range(ROW_TILE):
        ref[pl.ds(c, n, stride=ROW_TILE), :] = x[:, c * 128:(c + 1) * 128]


def _tiles_to_rows(ref, n):
    return jnp.concatenate([ref[pl.ds(c, n, stride=ROW_TILE), :] for c in range(ROW_TILE)], axis=1)


def _lane_put(lane, cols):
    out = jnp.where(lane == 0.0, cols[0], 0.0)
    for k in range(1, len(cols)):
        out = out + jnp.where(lane == float(k), cols[k], 0.0)
    return out


def _lane_get(lane, x, idx_col):
    return jnp.sum(jnp.where(lane == idx_col, x, 0.0), axis=-1, keepdims=True)


def _route_body(h_ref, nw_ref, wr_ref, br_ref, tri_ref, sel_ref, info_ref, pos_ref, tile_ref):
    xn = _rms(h_ref[...], nw_ref[...])
    hi = xn.astype(BF16)
    lo = (xn - hi.astype(F32)).astype(BF16)
    tm = hi.shape[0]
    r = _dot(jnp.concatenate([hi, lo], axis=0), wr_ref[...])
    logits = r[:tm, :ROUTE_LANES] + r[tm:, :ROUTE_LANES] + r[:tm, ROUTE_LANES:] + br_ref[...]
    lane = lax.broadcasted_iota(jnp.int32, logits.shape, 1).astype(F32)
    i1, i2, w1, w2 = _route(logits, lane)
    ind = jnp.where((lane == i1) | (lane == i2), 1.0, 0.0)
    ahead = _dot(tri_ref[...], ind.astype(BF16))
    tcnt = jnp.broadcast_to(jnp.sum(ind, axis=0, keepdims=True), tile_ref.shape)
    lane8 = lax.broadcasted_iota(jnp.int32, tile_ref.shape, 1)
    sub8 = lax.broadcasted_iota(jnp.int32, tile_ref.shape, 0)
    incl = tcnt
    sh = 1
    while sh < tile_ref.shape[1]:
        incl = incl + jnp.where(lane8 >= sh, pltpu.roll(incl, sh, axis=1), 0.0)
        sh *= 2
    toff = incl - tcnt
    local = (ahead + toff[0:1, :]) * ROW_TILE
    info_ref[...] = _lane_put(lane, [w1, w2])
    tile_ref[...] = jnp.where(sub8 == 0, tcnt, jnp.where(sub8 == 1, toff, 0.0))
    cols = []
    for v in (_lane_get(lane, local, i1), _lane_get(lane, local, i2)):
        vh = jnp.floor(v * (1.0 / 256.0))
        cols += [vh, v - 256.0 * vh]
    tr = _dot_nt(sel_ref[...], _lane_put(lane, cols).astype(BF16))
    subt = lax.broadcasted_iota(jnp.int32, tr.shape, 0)
    pos_ref[...] = jnp.where(subt == 0, tr[0:1, :] * 256.0 + tr[1:2, :],
                             jnp.where(subt == 1, tr[2:3, :] * 256.0 + tr[3:4, :], 0.0)).astype(jnp.int32)


def _dispatch_body(cnt_s, off_s, tot_s, h_ref, nw_ref, pos_s, runs_s, xs_ref, xn_buf, blk, zbuf, sem, *,
                   tm, tile_rows, ne, n_tiles):
    i = pl.program_id(0)
    last = pl.num_programs(0) - 1
    slots = MOE_TOP_K * tm * ROW_TILE

    def wait_runs(half):
        pltpu.make_async_copy(blk.at[half], xs_ref.at[pl.ds(0, slots)], sem.at[half]).wait()

    for half in range(2):
        rs = slice(half * tm, (half + 1) * tm)

        @pl.when(i > 0)
        def _drain_previous(half=half):
            wait_runs(half)

        _rows_to_tiles(xn_buf, _rms(h_ref[rs, :], nw_ref[...]))

        def place(t, carry, half=half):
            row = xn_buf[pl.ds(pl.multiple_of(t * ROW_TILE, ROW_TILE), ROW_TILE), :]
            for k in range(MOE_TOP_K):
                blk[half, pl.ds(pl.multiple_of(pos_s[k, half * tm + t], ROW_TILE), ROW_TILE), :] = row
            return carry

        lax.fori_loop(0, tm, place, 0, unroll=16)
        for e in range(ne):
            n = runs_s[half, 2, e] * ROW_TILE

            @pl.when(n > 0)
            def _send(e=e, n=n, half=half):
                src = pl.multiple_of(runs_s[half, 0, e] * ROW_TILE, ROW_TILE)
                dst = pl.multiple_of(runs_s[half, 1, e] * ROW_TILE, ROW_TILE)
                pltpu.make_async_copy(blk.at[half, pl.ds(src, n)], xs_ref.at[pl.ds(dst, n)], sem.at[half]
                                      ).start(priority=e % 2)

    @pl.when(i == last)
    def _zero_unused_rows():
        wait_runs(0)
        wait_runs(1)
        zbuf[...] = jnp.zeros_like(zbuf)

        def fill(row, nrows):
            at = pl.multiple_of(row * ROW_TILE, ROW_TILE)
            n = nrows * ROW_TILE
            c = pltpu.make_async_copy(zbuf.at[pl.ds(0, n)], xs_ref.at[pl.ds(at, n)], sem.at[2])
            c.start()
            c.wait()

        for e in range(ne):
            n_pad = (tile_rows - cnt_s[e] % tile_rows) % tile_rows

            @pl.when(n_pad > 0)
            def _fill(e=e, n_pad=n_pad):
                fill(off_s[e] + cnt_s[e], n_pad)

        def zero_tile(j, carry):
            fill(j * tile_rows, tile_rows)
            return carry

        lax.fori_loop(tot_s[0], n_tiles, zero_tile, 0)


def _expert_body(te_s, blk_s, tot_s, xs_ref, wg_ref, wu_ref, wd_ref, ys_ref):
    valid = pl.program_id(0) < tot_s[0]
    rows = xs_ref.shape[0] // ROW_TILE

    @pl.when(valid)
    def _run():
        x = _tiles_to_rows(xs_ref, rows).astype(BF16)
        hid = jax.nn.silu(_dot(x, wg_ref[0, 0].astype(BF16))) * _dot(x, wu_ref[0, 0].astype(BF16))
        _rows_to_tiles(ys_ref, _dot(hid.astype(BF16), wd_ref[0, 0].astype(BF16)))

    @pl.when(jnp.logical_not(valid))
    def _unused_tile():
        ys_ref[...] = jnp.zeros_like(ys_ref)


def _combine_body(h_ref, info_ref, pos_s, runs_s, next_s, ys_ref, fw_ref, out_ref, blk, y1, y2, sem, *,
                  tm, ne, final):
    i = pl.program_id(0)
    slots = MOE_TOP_K * tm * ROW_TILE
    bufs = (y1, y2)

    def fetch(runs, half):
        for e in range(ne):
            n = runs[half, 2, e] * ROW_TILE

            @pl.when(n > 0)
            def _fetch(e=e, n=n):
                dst = pl.multiple_of(runs[half, 0, e] * ROW_TILE, ROW_TILE)
                src = pl.multiple_of(runs[half, 1, e] * ROW_TILE, ROW_TILE)
                pltpu.make_async_copy(ys_ref.at[pl.ds(src, n)], blk.at[half, pl.ds(dst, n)], sem.at[half]
                                      ).start(priority=e % 2)

    def finish(half):
        pltpu.make_async_copy(ys_ref.at[pl.ds(0, slots)], blk.at[half], sem.at[half]).wait()
        rs = slice(half * tm, (half + 1) * tm)

        def pick(t, carry):
            dst = pl.ds(pl.multiple_of(t * ROW_TILE, ROW_TILE), ROW_TILE)
            for k in range(MOE_TOP_K):
                at = pl.multiple_of(pos_s[k, half * tm + t], ROW_TILE)
                bufs[k][dst, :] = blk[half, pl.ds(at, ROW_TILE), :]
            return carry

        lax.fori_loop(0, tm, pick, 0, unroll=8)
        o = (h_ref[rs, :] + info_ref[rs, 0:1] * _tiles_to_rows(y1, tm) + info_ref[rs, 1:2] * _tiles_to_rows(y2, tm))
        if final:
            o = _rms(o, fw_ref[...])
        out_ref[rs, :] = o

    @pl.when(i == 0)
    def _first():
        fetch(runs_s, 0)

    fetch(runs_s, 1)
    finish(0)

    @pl.when(i < pl.num_programs(0) - 1)
    def _prefetch():
        fetch(next_s, 0)

    finish(1)


def _router_weights(w_group, b_group, w_router, b_router):
    d, ne = w_router.shape
    ng = w_group.shape[1]
    w = jnp.pad(jnp.concatenate([w_router, w_group], axis=1).astype(F32), ((0, 0), (0, ROUTE_LANES - ne - ng)))
    hi = w.astype(BF16)
    lo = (w - hi.astype(F32)).astype(BF16)
    bias = jnp.pad(jnp.concatenate([b_router, b_group]).astype(F32), (0, ROUTE_LANES - ne - ng)).reshape(1, -1)
    return jnp.concatenate([hi, lo], axis=1), bias


def _moe(h, nw, w_group, b_group, w_router, b_router, w_gate, w_up, w_down, layer, final_w):
    t, d = h.shape
    assert d == ROW_TILE * 128, "row-as-tile layout needs d_model == 1024"
    _, ne, _, ff = w_gate.shape
    tm = TM_MOE
    te_rows = TM_EXPERT
    nw2 = nw.reshape(1, d)
    wr3, bias = _router_weights(w_group, b_group, w_router, b_router)
    tri = jnp.asarray(np.tril(np.ones((tm, tm), np.float32), -1), BF16)
    nt = t // tm
    sel = jnp.asarray(np.eye(8, ROUTE_LANES, dtype=np.float32), BF16)
    info, pos, tile_info = pl.pallas_call(
        _route_body, grid=(nt,),
        in_specs=[pl.BlockSpec((tm, d), lambda i: (i, 0)),
                  pl.BlockSpec((1, d), lambda i: (0, 0)),
                  pl.BlockSpec((d, 2 * ROUTE_LANES), lambda i: (0, 0)),
                  pl.BlockSpec((1, ROUTE_LANES), lambda i: (0, 0)),
                  pl.BlockSpec((tm, tm), lambda i: (0, 0)),
                  pl.BlockSpec((8, ROUTE_LANES), lambda i: (0, 0))],
        out_specs=[pl.BlockSpec((tm, ROUTE_LANES), lambda i: (i, 0)), pl.BlockSpec((8, tm), lambda i: (0, i)),
                   pl.BlockSpec((8, ROUTE_LANES), lambda i: (i, 0))],
        out_shape=[jax.ShapeDtypeStruct((t, ROUTE_LANES), F32), jax.ShapeDtypeStruct((8, t), jnp.int32),
                   jax.ShapeDtypeStruct((nt * 8, ROUTE_LANES), F32)],
        compiler_params=_cparams(("parallel",)), name="moe_route",
    )(h, nw2, wr3, bias, tri, sel)

    tinfo = tile_info.reshape(nt, 8, ROUTE_LANES).astype(jnp.int32)
    tcnt = tinfo[:, 0, :]
    cnt_i = jnp.sum(tcnt, axis=0)[:ne]
    ntile = (cnt_i + te_rows - 1) // te_rows
    tile_end = jnp.cumsum(ntile)
    off_i = (tile_end - ntile) * te_rows
    n_tiles = (MOE_TOP_K * t) // te_rows + ne
    rows_total = n_tiles * te_rows
    ti = jnp.arange(n_tiles, dtype=jnp.int32)
    tot = tile_end[-1:]
    ti_c = jnp.minimum(ti, tot[0] - 1)
    tile_e = jnp.sum((ti_c[:, None] >= tile_end[None, :]).astype(jnp.int32), axis=1)
    gstart = jnp.pad(off_i, (0, ROUTE_LANES - ne))[None, :] + jnp.cumsum(tcnt, axis=0) - tcnt
    runs = jnp.pad(jnp.stack([tinfo[:, 1, :], gstart, tcnt], axis=1), ((0, 0), (0, 5), (0, 0)))

    xs = pl.pallas_call(
        functools.partial(_dispatch_body, tm=tm, tile_rows=te_rows, ne=ne, n_tiles=n_tiles),
        grid_spec=pltpu.PrefetchScalarGridSpec(
            num_scalar_prefetch=3, grid=(nt // 2,),
            in_specs=[pl.BlockSpec((2 * tm, d), lambda i, *_: (i, 0)),
                      pl.BlockSpec((1, d), lambda i, *_: (0, 0)),
                      pl.BlockSpec((8, 2 * tm), lambda i, *_: (0, i), memory_space=pltpu.SMEM),
                      pl.BlockSpec((2, 8, ROUTE_LANES), lambda i, *_: (i, 0, 0), memory_space=pltpu.SMEM)],
            out_specs=pl.BlockSpec(memory_space=pl.ANY),
            scratch_shapes=[pltpu.VMEM((tm * ROW_TILE, 128), F32),
                            pltpu.VMEM((2, MOE_TOP_K * tm * ROW_TILE, 128), F32),
                            pltpu.VMEM((te_rows * ROW_TILE, 128), F32), pltpu.SemaphoreType.DMA((3,))]),
        out_shape=jax.ShapeDtypeStruct((rows_total * ROW_TILE, 128), F32),
        compiler_params=_cparams(("arbitrary",)), name="moe_dispatch",
    )(cnt_i, off_i, tot, h, nw2, pos, runs)

    ys = pl.pallas_call(
        _expert_body,
        grid_spec=pltpu.PrefetchScalarGridSpec(
            num_scalar_prefetch=3, grid=(n_tiles,),
            in_specs=[pl.BlockSpec((te_rows * ROW_TILE, 128), lambda i, e, b, v: (b[i], 0)),
                      pl.BlockSpec((1, 1, d, ff), lambda i, e, b, v: (layer, e[i], 0, 0)),
                      pl.BlockSpec((1, 1, d, ff), lambda i, e, b, v: (layer, e[i], 0, 0)),
                      pl.BlockSpec((1, 1, ff, d), lambda i, e, b, v: (layer, e[i], 0, 0))],
            out_specs=pl.BlockSpec((te_rows * ROW_TILE, 128), lambda i, e, b, v: (i, 0))),
        out_shape=jax.ShapeDtypeStruct((rows_total * ROW_TILE, 128), F32),
        compiler_params=_cparams(("arbitrary",)), name="moe_expert",
    )(tile_e, ti_c, tot, xs, w_gate, w_up, w_down)

    final = final_w is not None
    fw = (final_w if final else nw).reshape(1, d)
    return pl.pallas_call(
        functools.partial(_combine_body, tm=tm, ne=ne, final=final), grid=(nt // 2,),
        in_specs=[pl.BlockSpec((2 * tm, d), lambda i: (i, 0)),
                  pl.BlockSpec((2 * tm, ROUTE_LANES), lambda i: (i, 0)),
                  pl.BlockSpec((8, 2 * tm), lambda i: (0, i), memory_space=pltpu.SMEM),
                  pl.BlockSpec((2, 8, ROUTE_LANES), lambda i: (i, 0, 0), memory_space=pltpu.SMEM),
                  pl.BlockSpec((2, 8, ROUTE_LANES), lambda i: (jnp.minimum(i + 1, nt // 2 - 1), 0, 0),
                               memory_space=pltpu.SMEM),
                  pl.BlockSpec(memory_space=pl.ANY),
                  pl.BlockSpec((1, d), lambda i: (0, 0))],
        out_specs=pl.BlockSpec((2 * tm, d), lambda i: (i, 0)),
        out_shape=jax.ShapeDtypeStruct((t, d), F32),
        scratch_shapes=[pltpu.VMEM((2, MOE_TOP_K * tm * ROW_TILE, 128), F32), pltpu.VMEM((tm * ROW_TILE, 128), F32),
                        pltpu.VMEM((tm * ROW_TILE, 128), F32), pltpu.SemaphoreType.DMA((2,))],
        compiler_params=_cparams(("arbitrary",)), name="moe_combine",
    )(h, info, pos, runs, runs, ys, fw)


def _s5_weights(lam_re, lam_im, b_re, b_im, c_re, c_im, log_dt, nsteps):
    L = S5_CHUNK
    g, p = lam_re.shape
    ch = b_re.shape[-1]
    lam = lax.complex(lam_re.astype(F32), lam_im.astype(F32))
    dt = jnp.exp(log_dt.astype(F32))[:, None]
    lam_bar = jnp.exp(lam * dt)
    b_bar = ((lam_bar - 1.0) / lam)[..., None] * lax.complex(b_re.astype(F32), b_im.astype(F32))
    cmat = lax.complex(c_re.astype(F32), c_im.astype(F32))
    pows = [jnp.ones_like(lam_bar)]
    for _ in range(L):
        pows.append(pows[-1] * lam_bar)
    pw = jnp.stack(pows, axis=1)
    kern = jnp.real(jnp.einsum('gtop,gpi->gito', cmat[:, None, :, :] * pw[:, :L, None, :], b_bar))
    lead = (L - 1) * ch
    plen = lead + L * ch
    kpad = jnp.pad(kern.reshape(g, ch, L * ch).astype(BF16), ((0, 0), (0, 0), (lead, 0)))
    win = jnp.tile(kpad, (1, 1, L))[:, :, lead:lead + L * (plen - ch)].reshape(g, ch, L, plen - ch)[..., :L * ch]
    toep = win.transpose(0, 2, 1, 3).reshape(g, L * ch, L * ch)
    wst = pw[:, L - 1 - np.arange(L)][:, :, :, None] * b_bar[:, None, :, :]
    wst = wst.transpose(0, 1, 3, 2).reshape(g, L * ch, p)
    wst = jnp.concatenate([jnp.real(wst), jnp.imag(wst)], axis=-1)
    mo = cmat.transpose(0, 2, 1)[:, :, None, :] * pw[:, 1:L + 1].transpose(0, 2, 1)[:, :, :, None]
    mo = mo.reshape(g, p, L * ch)
    wout = jnp.concatenate([jnp.real(mo), -jnp.imag(mo)], axis=1)
    a = pw[:, L]
    ars, ais = [], []
    for _ in range(nsteps):
        ars.append(jnp.concatenate([jnp.real(a), jnp.real(a)], axis=-1))
        ais.append(jnp.concatenate([-jnp.imag(a), jnp.imag(a)], axis=-1))
        a = a * a
    wcat = jnp.concatenate([toep, wst.astype(BF16)], axis=-1)
    return wcat, wout.astype(BF16), jnp.stack(ars, axis=1), jnp.stack(ais, axis=1)


def _s5_body(u_ref, wcat_ref, wout_ref, ar_ref, ai_ref, y_ref, us_ref, ys_ref, *, nsteps):
    L = S5_CHUNK
    ch = S5_GROUP_CH
    gpc = 128 // ch
    ny = L * ch
    nc = u_ref.shape[0] // L
    for s in range(L):
        us_ref[s] = u_ref[pl.ds(s, nc, stride=L), :]
    lane = lax.broadcasted_iota(jnp.int32, (nc, 128), 1)
    ridx = lax.broadcasted_iota(jnp.int32, (nc, 128), 0)

    def shift(x, k):
        return jnp.where(ridx >= k, pltpu.roll(x, k, axis=0), 0.0)

    for gi in range(gpc):
        halves = []
        for hh in range(ny // 128):
            acc = None
            for s8 in range(gpc):
                rot = ((s8 - gi) * ch) % 128
                src = us_ref[hh * gpc + s8]
                if rot:
                    src = pltpu.roll(src, rot, axis=1)
                slot = (lane >= s8 * ch) & (lane < (s8 + 1) * ch)
                acc = jnp.where(slot, src, 0.0) if acc is None else jnp.where(slot, src, acc)
            halves.append(acc)
        ug = jnp.concatenate(halves, axis=1).astype(BF16)
        r = _dot(ug, wcat_ref[gi])
        y1 = r[:, :ny]
        z = r[:, ny:]
        w = shift(z, 1)
        wx = pltpu.roll(w, z.shape[1] // 2, axis=1)
        for k in range(nsteps):
            if (1 << k) >= nc:
                break
            sk = shift(w, 1 << k)
            sx = shift(wx, 1 << k)
            ar = ar_ref[gi, k:k + 1, :]
            ai = ai_ref[gi, k:k + 1, :]
            w, wx = w + sk * ar + sx * ai, wx + sx * ar - sk * ai
        yg = y1 + _dot(w.astype(BF16), wout_ref[gi])
        slot = (lane >= gi * ch) & (lane < (gi + 1) * ch)
        for t in range(L):
            src = yg[:, (t // gpc) * 128:(t // gpc + 1) * 128]
            rot = ((gi - t % gpc) * ch) % 128
            if rot:
                src = pltpu.roll(src, rot, axis=1)
            ys_ref[t] = jnp.where(slot, src, 0.0) if gi == 0 else jnp.where(slot, src, ys_ref[t])
    for t in range(L):
        y_ref[pl.ds(t, nc, stride=L), :] = ys_ref[t]


def _s5(main, col0, width, nb, seq, lam_re, lam_im, b_re, b_im, c_re, c_im, log_dt):
    t = main.shape[0]
    L = S5_CHUNK
    ch = S5_GROUP_CH
    g = width // ch
    nc = seq // L
    gpc = 128 // ch
    ncol = width // 128
    assert col0 % 128 == 0 and width % 128 == 0 and (L * ch) % 128 == 0
    nsteps = max(1, (nc - 1).bit_length())
    wcat, wout, ar, ai = _s5_weights(lam_re, lam_im, b_re, b_im, c_re, c_im, log_dt, nsteps)
    body = functools.partial(_s5_body, nsteps=nsteps)
    return pl.pallas_call(
        body, grid=(ncol, nb),
        in_specs=[pl.BlockSpec((seq, 128), lambda j, b: (b, col0 // 128 + j)),
                  pl.BlockSpec((gpc,) + wcat.shape[1:], lambda j, b: (j, 0, 0)),
                  pl.BlockSpec((gpc,) + wout.shape[1:], lambda j, b: (j, 0, 0)),
                  pl.BlockSpec((gpc,) + ar.shape[1:], lambda j, b: (j, 0, 0)),
                  pl.BlockSpec((gpc,) + ai.shape[1:], lambda j, b: (j, 0, 0))],
        out_specs=pl.BlockSpec((seq, 128), lambda j, b: (b, j)),
        out_shape=jax.ShapeDtypeStruct((t, width), F32),
        scratch_shapes=[pltpu.VMEM((L, nc, 128), F32), pltpu.VMEM((L, nc, 128), F32)],
        compiler_params=_cparams(("parallel", "parallel")), name="s5",
    )(main, wcat, wout, ar, ai)


def _hgrn_gmat():
    L = HGRN_CHUNK
    blocks = 2 + int(np.log2(L))
    gm = np.zeros((blocks * L, L), np.float32)
    for j in range(L):
        gm[j, :j + 1] = 1.0
        gm[L + j, j + 1:] = 1.0
    li, m = 2, L
    while m >= 2:
        half = m // 2
        for j in range(L):
            pos = j % m
            r = j - pos + half - 1
            if pos >= half:
                gm[li * L + j, r + 1:j + 1] = 1.0
            else:
                gm[li * L + j, j + 1:r + 1] = 1.0
        li += 1
        m //= 2
    return gm


def _hgrn_body(mh_ref, fg_ref, gm_ref, lb_ref, nw_ref, out_ref, st_ref, *, nb, nh, dh):
    L = HGRN_CHUNK
    w = nh * dh

    @pl.when(pl.program_id(0) == 0)
    def _init():
        st_ref[...] = jnp.zeros_like(st_ref)

    row = lax.broadcasted_iota(jnp.int32, (L, 2 * L), 0)
    col = lax.broadcasted_iota(jnp.int32, (L, 2 * L), 1) & (L - 1)
    rowd = lax.broadcasted_iota(jnp.int32, (L, 2 * dh), 0)
    laned = lax.broadcasted_iota(jnp.int32, (L, 2 * dh), 1)
    first = laned < dh
    eye = row == col

    def blockdiag(x):
        z = jnp.zeros_like(x)
        return jnp.concatenate([jnp.where(first, x, z), jnp.where(first, z, x)], axis=0)

    gm2 = gm_ref[...]
    lb = lb_ref[...]
    zst = jnp.zeros((dh, dh), BF16)
    for b in range(nb):
        fg = fg_ref[b]
        f = lb + (1.0 - lb) * jax.nn.sigmoid(fg)
        kk = (1.0 - lb) * jax.nn.sigmoid(-fg)
        lf = jnp.log(f)
        hi = lf.astype(BF16)
        mid = (lf - hi.astype(F32)).astype(BF16)
        p_all = jnp.exp(_dot(gm2, jnp.concatenate([hi, mid], axis=0)))
        for hp in range(nh // 2):
            i0 = b * nh + 2 * hp
            cs = slice(2 * hp * dh, (2 * hp + 2) * dh)
            qb = mh_ref[b, :, 2 * hp * dh:(2 * hp + 2) * dh]
            vb = mh_ref[b, :, w + 2 * hp * dh:w + (2 * hp + 2) * dh]
            og = mh_ref[b, :, 2 * w + 2 * hp * dh:2 * w + (2 * hp + 2) * dh].astype(F32)
            q = qb.astype(F32)
            k = kk[:, cs]
            pb = p_all[0:L, cs]
            pe = p_all[L:2 * L, cs]
            st0 = st_ref[i0]
            st1 = st_ref[i0 + 1]
            stbd = jnp.concatenate([jnp.concatenate([st0.astype(BF16), zst], axis=1),
                                    jnp.concatenate([zst, st1.astype(BF16)], axis=1)], axis=0)
            o = _dot_nt((q * pb).astype(BF16), stbd)
            attn = jnp.where(eye, _dot_nt(qb, blockdiag(k.astype(BF16))), 0.0)
            li, m = 2, L
            while m >= 2:
                pl_ = p_all[li * L:(li + 1) * L, cs]
                up = (rowd & (m - 1)) >= (m // 2)
                ql = jnp.where(up, q * pl_, 0.0).astype(BF16)
                kl = jnp.where(up, 0.0, k * pl_).astype(BF16)
                same = (row & ~(m - 1)) == (col & ~(m - 1))
                attn = attn + jnp.where(same, _dot_nt(ql, blockdiag(kl)), 0.0)
                li += 1
                m //= 2
            o = o + _dot(attn.astype(BF16), blockdiag(vb))
            kh = (k * pe).astype(BF16)
            for j in range(2):
                hs = slice(j * dh, (j + 1) * dh)
                gs = slice((2 * hp + j) * dh, (2 * hp + j + 1) * dh)
                st = st0 if j == 0 else st1
                st_ref[i0 + j] = st * pb[L - 1:L, hs] + _dot_tn(vb[:, hs], kh[:, hs])
                oj = o[:, hs]
                on = oj * lax.rsqrt(jnp.mean(oj * oj, axis=-1, keepdims=True) + RMS_EPS)
                out_ref[b, :, gs] = (on * nw_ref[:, gs] * jax.nn.silu(og[:, hs])).astype(out_ref.dtype)


def _hgrn(mh, mf, lower_bound, out_norm, nb, seq):
    t = mh.shape[0]
    nh = HGRN_HEADS
    w = out_norm.shape[0]
    dh = w // nh
    L = HGRN_CHUNK
    nc = seq // L
    gm = _hgrn_gmat()
    gm = jnp.asarray(np.concatenate([gm, gm], axis=1), BF16)
    body = functools.partial(_hgrn_body, nb=nb, nh=nh, dh=dh)
    out = pl.pallas_call(
        body, grid=(nc,),
        in_specs=[pl.BlockSpec((nb, L, 3 * w), lambda c: (0, c, 0)),
                  pl.BlockSpec((nb, L, w), lambda c: (0, c, 0)),
                  pl.BlockSpec(gm.shape, lambda c: (0, 0)),
                  pl.BlockSpec((1, w), lambda c: (0, 0)),
                  pl.BlockSpec((1, w), lambda c: (0, 0))],
        out_specs=pl.BlockSpec((nb, L, w), lambda c: (0, c, 0)),
        out_shape=jax.ShapeDtypeStruct((nb, seq, w), BF16),
        scratch_shapes=[pltpu.VMEM((nb * nh, dh, dh), F32)],
        compiler_params=_cparams(("arbitrary",)), name="hgrn",
    )(mh.reshape(nb, seq, mh.shape[1]), mf.reshape(nb, seq, mf.shape[1]), gm, lower_bound.reshape(1, w),
      out_norm.reshape(1, w))
    return out.reshape(t, w)


def _out_c_body(h_ref, ys_ref, u_ref, oh_ref, d_ref, wglu_ref, bglu_ref, w_ref, out_ref):
    ws = ys_ref.shape[1]
    z = jax.nn.gelu(ys_ref[...] + d_ref[...] * u_ref[...])
    gate = jax.nn.sigmoid(_dot(z.astype(BF16), wglu_ref[...]) + bglu_ref[...])
    out_ref[...] = (h_ref[...] + _dot((z * gate).astype(BF16), w_ref[:ws, :]) + _dot(oh_ref[...], w_ref[ws:, :]))


def _out_c(h, ys, main, oh, d_skip, w_glu, b_glu, w_out):
    t, d = h.shape
    ws = ys.shape[1]
    wh = oh.shape[1]
    tm = TM_PROJ
    ub = (main.shape[1] - ws) // ws
    return pl.pallas_call(
        _out_c_body, grid=(t // tm,),
        in_specs=[pl.BlockSpec((tm, d), lambda i: (i, 0)),
                  pl.BlockSpec((tm, ws), lambda i: (i, 0)),
                  pl.BlockSpec((tm, ws), lambda i: (i, ub)),
                  pl.BlockSpec((tm, wh), lambda i: (i, 0)),
                  pl.BlockSpec((1, ws), lambda i: (0, 0)),
                  pl.BlockSpec(w_glu.shape, lambda i: (0, 0)),
                  pl.BlockSpec((1, ws), lambda i: (0, 0)),
                  pl.BlockSpec(w_out.shape, lambda i: (0, 0))],
        out_specs=pl.BlockSpec((tm, d), lambda i: (i, 0)),
        out_shape=jax.ShapeDtypeStruct((t, d), F32),
        compiler_params=_cparams(("parallel",)), name="out_c",
    )(h, ys, main, oh, d_skip.reshape(1, ws), w_glu, b_glu.reshape(1, ws), w_out)


def kernel(x, norm_mix, norm_ffn, norm_final, ab_w_in, ab_gate_bias, ab_head_norm, ab_conv_w, ab_w_out, cd_w_in, s5_lambda_re, s5_lambda_im, s5_b_re, s5_b_im, s5_c_re, s5_c_im, s5_d, s5_log_dt, s5_w_glu, s5_b_glu, hgrn_lb, hgrn_out_norm, cd_w_out, moe_w_group, moe_b_group, moe_w_router, moe_b_router, moe_w_gate, moe_w_up, moe_w_down):
    nb, seq, d = x.shape
    depth = norm_mix.shape[0]
    h = x.reshape(nb * seq, d)
    for layer in range(depth):
        j = layer // 2
        if layer % 2 == 0:
            wm = ab_head_norm.shape[1]
            ng = ab_gate_bias.shape[1]
            w_in = ab_w_in[j]
            w_main = jnp.concatenate([w_in[:, :4 * wm], w_in[:, 4 * wm + ng:]], axis=1).astype(BF16)
            w_gates = w_in[:, 4 * wm:4 * wm + ng].astype(BF16)
            main, g, gt = _proj(h, norm_mix[layer], w_main, w_gates)
            hm = _mlstm(main, g, gt, ab_gate_bias[j], ab_head_norm[j], nb, seq)
            h = _out_a(h, hm, main, ab_conv_w[j], ab_w_out[j].astype(BF16), seq)
        else:
            ws = s5_d.shape[1]
            w_in = cd_w_in[j]
            wh = hgrn_out_norm.shape[1]
            cu, cq, cf, ci, co = (w_in[:, :ws], w_in[:, ws:ws + wh], w_in[:, ws + wh:ws + 2 * wh],
                                  w_in[:, ws + 2 * wh:ws + 3 * wh], w_in[:, ws + 3 * wh:])
            w_main = jnp.concatenate([cq, ci, co, cf, cu], axis=1).astype(BF16)
            mh, mf = _proj(h, norm_mix[layer], w_main, n_lo=3 * wh)
            sm = jax.nn.softmax(hgrn_lb.astype(F32), axis=0)
            lower_bound = jnp.cumsum(sm, axis=0)[layer] - sm[0]
            ys = _s5(mf, wh, ws, nb, seq, s5_lambda_re[j], s5_lambda_im[j], s5_b_re[j], s5_b_im[j],
                     s5_c_re[j], s5_c_im[j], s5_log_dt[j])
            oh = _hgrn(mh, mf, lower_bound, hgrn_out_norm[j], nb, seq)
            h = _out_c(h, ys, mf, oh, s5_d[j], s5_w_glu[j].astype(BF16), s5_b_glu[j], cd_w_out[j].astype(BF16))
        h = _moe(h, norm_ffn[layer], moe_w_group[layer], moe_b_group[layer], moe_w_router[layer], moe_b_router[layer],
                 moe_w_gate, moe_w_up, moe_w_down, layer,
                 norm_final if layer == depth - 1 else None)
    return h.reshape(nb, seq, d)
```

```python
import functools

import numpy as np
import jax
import jax.numpy as jnp
from jax import lax
from jax.experimental import pallas as pl
from jax.experimental.pallas import tpu as pltpu

F32 = jnp.float32
BF16 = jnp.bfloat16
RMS_EPS = 1e-6
MLSTM_CHUNK = 512
HGRN_CHUNK = 128
S5_CHUNK = 16
S5_GROUP_CH = 16
MLSTM_HEADS = 4
HGRN_HEADS = 4
MOE_GROUPS = 4
MOE_EPG = 8
ROUTE_LANES = 128
TM_PROJ = 1024
TM_MOE = 512
TM_EXPERT = 512
MOE_TOP_K = 2
VMEM_LIMIT = 56 * 1024 * 1024

_NT = (((1,), (1,)), ((), ()))
_TN = (((0,), (0,)), ((), ()))


def _cparams(sem):
    return pltpu.CompilerParams(dimension_semantics=sem, vmem_limit_bytes=VMEM_LIMIT)


def _rms(x, w):
    return x * lax.rsqrt(jnp.mean(x * x, axis=-1, keepdims=True) + RMS_EPS) * w


def _split3(x):
    hi = x.astype(BF16)
    r = x - hi.astype(F32)
    mid = r.astype(BF16)
    lo = (r - mid.astype(F32)).astype(BF16)
    return hi, mid, lo


def _dot(a, b):
    return jnp.dot(a, b, preferred_element_type=F32)


def _dot_nt(a, b):
    return lax.dot_general(a, b, _NT, preferred_element_type=F32)


def _dot_tn(a, b):
    return lax.dot_general(a, b, _TN, preferred_element_type=F32)


def _proj_gates_body(x_ref, nw_ref, w_ref, wg_ref, wgt_ref, main_ref, g_ref, gt_ref):
    xn = _rms(x_ref[...], nw_ref[...]).astype(BF16)
    main_ref[...] = _dot(xn, w_ref[...]).astype(main_ref.dtype)
    g_ref[...] = _dot(xn, wg_ref[...])[:, : g_ref.shape[1]]
    gt_ref[...] = _dot_nt(wgt_ref[...], xn)


def _proj_split_body(x_ref, nw_ref, w_ref, lo_ref, hi_ref):
    xn = _rms(x_ref[...], nw_ref[...]).astype(BF16)
    r = _dot(xn, w_ref[...])
    n_lo = lo_ref.shape[1]
    lo_ref[...] = r[:, :n_lo].astype(lo_ref.dtype)
    hi_ref[...] = r[:, n_lo:]


def _proj(h, nw, w_main, w_gates=None, n_lo=None):
    t, d = h.shape
    n = w_main.shape[1]
    tm = TM_PROJ
    x_spec = pl.BlockSpec((tm, d), lambda i: (i, 0))
    nw_spec = pl.BlockSpec((1, d), lambda i: (0, 0))
    w_spec = pl.BlockSpec((d, n), lambda i: (0, 0))
    if w_gates is None:
        return pl.pallas_call(
            _proj_split_body, grid=(t // tm,), in_specs=[x_spec, nw_spec, w_spec],
            out_specs=[pl.BlockSpec((tm, n_lo), lambda i: (i, 0)), pl.BlockSpec((tm, n - n_lo), lambda i: (i, 0))],
            out_shape=[jax.ShapeDtypeStruct((t, n_lo), BF16), jax.ShapeDtypeStruct((t, n - n_lo), F32)],
            compiler_params=_cparams(("parallel",)), name="proj",
        )(h, nw.reshape(1, d), w_main)
    ng = w_gates.shape[1]
    wg_pad = jnp.pad(w_gates, ((0, 0), (0, 128 - ng)))
    return pl.pallas_call(
        _proj_gates_body, grid=(t // tm,),
        in_specs=[x_spec, nw_spec, w_spec, pl.BlockSpec((d, 128), lambda i: (0, 0)),
                  pl.BlockSpec((ng, d), lambda i: (0, 0))],
        out_specs=[pl.BlockSpec((tm, n), lambda i: (i, 0)), pl.BlockSpec((tm, ng), lambda i: (i, 0)),
                   pl.BlockSpec((ng, tm), lambda i: (0, i))],
        out_shape=[jax.ShapeDtypeStruct((t, n), BF16), jax.ShapeDtypeStruct((t, ng), F32),
                   jax.ShapeDtypeStruct((ng, t), F32)],
        compiler_params=_cparams(("parallel",)), name="proj_gates",
    )(h, nw.reshape(1, d), w_main, wg_pad, w_gates.T)


def _mlstm_body(main_ref, g_ref, gt_ref, br_ref, bc_ref, hn_ref, out_ref, c_ref, n_ref, m_ref, *, nb, nh, dh):
    L = MLSTM_CHUNK
    w = nh * dh

    @pl.when(pl.program_id(0) == 0)
    def _init():
        c_ref[...] = jnp.zeros_like(c_ref)
        n_ref[...] = jnp.zeros_like(n_ref)
        m_ref[...] = jnp.full_like(m_ref, -1e30)

    row = lax.broadcasted_iota(jnp.int32, (L, L), 0)
    col = lax.broadcasted_iota(jnp.int32, (L, L), 1)
    causal = col <= row
    tril = causal.astype(BF16)
    triu = (row <= col).astype(BF16)
    scale = dh ** -0.5
    for b in range(nb):
        g = g_ref[b] + br_ref[...]
        gt = gt_ref[b, 0] + bc_ref[...]
        i_c = g[:, :nh]
        i_r = gt[:nh, :]
        lfc = _split3(jax.nn.log_sigmoid(g[:, nh:]))
        lfr = _split3(jax.nn.log_sigmoid(gt[nh:, :]))
        bc_all = _dot(tril, lfc[0]) + _dot(tril, lfc[1]) + _dot(tril, lfc[2])
        br_all = _dot(lfr[0], triu) + _dot(lfr[1], triu) + _dot(lfr[2], triu)
        for h in range(nh):
            idx = b * nh + h
            qb = main_ref[b, :, h * dh:(h + 1) * dh]
            vb = main_ref[b, :, 2 * w + h * dh:2 * w + (h + 1) * dh]
            q = qb.astype(F32)
            k = main_ref[b, :, w + h * dh:w + (h + 1) * dh].astype(F32) * scale
            v = vb.astype(F32)
            o = main_ref[b, :, 3 * w + h * dh:3 * w + (h + 1) * dh].astype(F32)
            bc = bc_all[:, h:h + 1]
            br = br_all[h:h + 1, :]
            ir = i_r[h:h + 1, :]
            ic = i_c[:, h:h + 1]
            m_prev = m_ref[idx]
            c_prev = c_ref[idx]
            n_prev = n_ref[idx]
            logw = jnp.where(causal, bc - br + ir, -jnp.inf)
            inter = bc + m_prev
            m_row = jnp.maximum(jnp.max(logw, axis=-1, keepdims=True), inter)
            kb = k.astype(BF16)
            s = _dot_nt(qb, kb) * jnp.exp(logw - m_row)
            isc = jnp.exp(inter - m_row)
            num = _dot(s.astype(BF16), vb) + isc * _dot_nt(qb, c_prev.astype(BF16))
            den = jnp.sum(s, axis=-1, keepdims=True) + isc * jnp.sum(q * n_prev, axis=-1, keepdims=True)
            hout = num / jnp.maximum(jnp.abs(den), jnp.exp(-m_row))
            b_end = bc[L - 1:L, :]
            logg = b_end - bc + ic
            m_new = jnp.maximum(b_end + m_prev, jnp.max(logg, axis=0, keepdims=True))
            wk = jnp.exp(logg - m_new)
            decay = jnp.exp(b_end + m_prev - m_new)
            c_ref[idx] = decay * c_prev + _dot_tn((v * wk).astype(BF16), kb)
            n_ref[idx] = decay * n_prev + jnp.sum(wk * k, axis=0, keepdims=True)
            m_ref[idx] = m_new
            hn = hout * lax.rsqrt(jnp.mean(hout * hout, axis=-1, keepdims=True) + RMS_EPS)
            out_ref[b, :, h * dh:(h + 1) * dh] = (hn * hn_ref[:, h * dh:(h + 1) * dh] * jax.nn.sigmoid(o)
                                                  ).astype(out_ref.dtype)


def _mlstm(main, g, gt, gate_bias, head_norm, nb, seq):
    t, n = main.shape
    nh = MLSTM_HEADS
    w = head_norm.shape[0]
    dh = w // nh
    L = MLSTM_CHUNK
    nc = seq // L
    main3 = main.reshape(nb, seq, n)
    g3 = g.reshape(nb, seq, 2 * nh)
    gt4 = gt.reshape(2 * nh, nb, nc, L).transpose(1, 2, 0, 3)
    body = functools.partial(_mlstm_body, nb=nb, nh=nh, dh=dh)
    out = pl.pallas_call(
        body, grid=(nc,),
        in_specs=[pl.BlockSpec((nb, L, 4 * w), lambda c: (0, c, 0)),
                  pl.BlockSpec((nb, L, 2 * nh), lambda c: (0, c, 0)),
                  pl.BlockSpec((nb, 1, 2 * nh, L), lambda c: (0, c, 0, 0)),
                  pl.BlockSpec((1, 2 * nh), lambda c: (0, 0)),
                  pl.BlockSpec((2 * nh, 1), lambda c: (0, 0)),
                  pl.BlockSpec((1, w), lambda c: (0, 0))],
        out_specs=pl.BlockSpec((nb, L, w), lambda c: (0, c, 0)),
        out_shape=jax.ShapeDtypeStruct((nb, seq, w), BF16),
        scratch_shapes=[pltpu.VMEM((nb * nh, dh, dh), F32), pltpu.VMEM((nb * nh, 1, dh), F32),
                        pltpu.VMEM((nb * nh, 1, 1), F32)],
        compiler_params=_cparams(("arbitrary",)), name="mlstm",
    )(main3, g3, gt4, gate_bias.reshape(1, 2 * nh), gate_bias.reshape(2 * nh, 1), head_norm.reshape(1, w))
    return out.reshape(t, w)


def _out_a_body(h_ref, hm_ref, gb_ref, gc_ref, xin_ref, pgc_ref, pxin_ref, cw_ref, w_ref, out_ref, *, tm, seq):
    i = pl.program_id(0)
    wm = hm_ref.shape[1]
    p = gc_ref[...].astype(F32) * xin_ref[...].astype(F32)
    first = (i * tm) % seq == 0
    hr = pgc_ref.shape[0]
    pp = jnp.where(first, 0.0, pgc_ref[...].astype(F32) * pxin_ref[...].astype(F32))
    rowi = lax.broadcasted_iota(jnp.int32, p.shape, 0)
    p1 = jnp.where(rowi == 0, pp[hr - 1:hr, :], pltpu.roll(p, 1, axis=0))
    p2 = jnp.where(rowi == 0, pp[hr - 2:hr - 1, :], jnp.where(rowi == 1, pp[hr - 1:hr, :], pltpu.roll(p, 2, axis=0)))
    yc = gb_ref[...].astype(F32) * (cw_ref[0:1, :] * p2 + cw_ref[1:2, :] * p1 + cw_ref[2:3, :] * p)
    out_ref[...] = (h_ref[...] + _dot(hm_ref[...], w_ref[:wm, :]) + _dot(yc.astype(BF16), w_ref[wm:, :]))


def _out_a(h, hm, main, conv_w, w_out, seq):
    t, d = h.shape
    wm = hm.shape[1]
    wc = conv_w.shape[1]
    tm = TM_PROJ
    cb = (4 * wm) // wc
    assert seq % tm == 0, "a token tile must not straddle two sequences (causal conv halo)"
    halo = 16
    rb = tm // halo
    prev = lambda i: jnp.maximum(i * rb - 1, 0)
    body = functools.partial(_out_a_body, tm=tm, seq=seq)
    return pl.pallas_call(
        body, grid=(t // tm,),
        in_specs=[pl.BlockSpec((tm, d), lambda i: (i, 0)),
                  pl.BlockSpec((tm, wm), lambda i: (i, 0)),
                  pl.BlockSpec((tm, wc), lambda i: (i, cb)),
                  pl.BlockSpec((tm, wc), lambda i: (i, cb + 1)),
                  pl.BlockSpec((tm, wc), lambda i: (i, cb + 2)),
                  pl.BlockSpec((halo, wc), lambda i: (prev(i), cb + 1)),
                  pl.BlockSpec((halo, wc), lambda i: (prev(i), cb + 2)),
                  pl.BlockSpec(conv_w.shape, lambda i: (0, 0)),
                  pl.BlockSpec(w_out.shape, lambda i: (0, 0))],
        out_specs=pl.BlockSpec((tm, d), lambda i: (i, 0)),
        out_shape=jax.ShapeDtypeStruct((t, d), F32),
        compiler_params=_cparams(("parallel",)), name="out_a",
    )(h, hm, main, main, main, main, main, conv_w, w_out)


def _route(logits, lane):
    ne = MOE_GROUPS * MOE_EPG
    big = 1e9
    gl = jnp.where((lane >= ne) & (lane < ne + MOE_GROUPS), logits, -jnp.inf)
    gmax = jnp.max(gl, axis=-1, keepdims=True)
    gidx = jnp.min(jnp.where(gl == gmax, lane - ne, big), axis=-1, keepdims=True)
    gval = 1.0 / jnp.sum(jnp.exp(gl - gmax), axis=-1, keepdims=True)
    lo = gidx * MOE_EPG
    sel = jnp.where((lane >= lo) & (lane < lo + MOE_EPG), logits, -jnp.inf)
    l1 = jnp.max(sel, axis=-1, keepdims=True)
    i1 = jnp.min(jnp.where(sel == l1, lane, big), axis=-1, keepdims=True)
    sel2 = jnp.where(lane == i1, -jnp.inf, sel)
    l2 = jnp.max(sel2, axis=-1, keepdims=True)
    i2 = jnp.min(jnp.where(sel2 == l2, lane, big), axis=-1, keepdims=True)
    r = jnp.exp(l2 - l1)
    w1 = gval / (1.0 + r)
    return i1, i2, w1, w1 * r


ROW_TILE = 8


def _rows_to_tiles(ref, x):
    n = x.shape[0]
    for c in range(ROW_TILE):
        ref[pl.ds(c, n, stride=ROW_TILE), :] = x[:, c * 128:(c + 1) * 128]


def _tiles_to_rows(ref, n):
    return jnp.concatenate([ref[pl.ds(c, n, stride=ROW_TILE), :] for c in range(ROW_TILE)], axis=1)


def _lane_put(lane, cols):
    out = jnp.where(lane == 0.0, cols[0], 0.0)
    for k in range(1, len(cols)):
        out = out + jnp.where(lane == float(k), cols[k], 0.0)
    return out


def _lane_get(lane, x, idx_col):
    return jnp.sum(jnp.where(lane == idx_col, x, 0.0), axis=-1, keepdims=True)


def _route_body(h_ref, nw_ref, wr_ref, br_ref, tri_ref, sel_ref, info_ref, pos_ref, tile_ref):
    xn = _rms(h_ref[...], nw_ref[...])
    hi = xn.astype(BF16)
    lo = (xn - hi.astype(F32)).astype(BF16)
    tm = hi.shape[0]
    r = _dot(jnp.concatenate([hi, lo], axis=0), wr_ref[...])
    logits = r[:tm, :ROUTE_LANES] + r[tm:, :ROUTE_LANES] + r[:tm, ROUTE_LANES:] + br_ref[...]
    lane = lax.broadcasted_iota(jnp.int32, logits.shape, 1).astype(F32)
    i1, i2, w1, w2 = _route(logits, lane)
    ind = jnp.where((lane == i1) | (lane == i2), 1.0, 0.0)
    ahead = _dot(tri_ref[...], ind.astype(BF16))
    tcnt = jnp.broadcast_to(jnp.sum(ind, axis=0, keepdims=True), tile_ref.shape)
    lane8 = lax.broadcasted_iota(jnp.int32, tile_ref.shape, 1)
    sub8 = lax.broadcasted_iota(jnp.int32, tile_ref.shape, 0)
    incl = tcnt
    sh = 1
    while sh < tile_ref.shape[1]:
        incl = incl + jnp.where(lane8 >= sh, pltpu.roll(incl, sh, axis=1), 0.0)
        sh *= 2
    toff = incl - tcnt
    local = (ahead + toff[0:1, :]) * ROW_TILE
    info_ref[...] = _lane_put(lane, [w1, w2])
    tile_ref[...] = jnp.where(sub8 == 0, tcnt, jnp.where(sub8 == 1, toff, 0.0))
    cols = []
    for v in (_lane_get(lane, local, i1), _lane_get(lane, local, i2)):
        vh = jnp.floor(v * (1.0 / 256.0))
        cols += [vh, v - 256.0 * vh]
    tr = _dot_nt(sel_ref[...], _lane_put(lane, cols).astype(BF16))
    subt = lax.broadcasted_iota(jnp.int32, tr.shape, 0)
    pos_ref[...] = jnp.where(subt == 0, tr[0:1, :] * 256.0 + tr[1:2, :],
                             jnp.where(subt == 1, tr[2:3, :] * 256.0 + tr[3:4, :], 0.0)).astype(jnp.int32)


def _dispatch_body(cnt_s, off_s, tot_s, h_ref, nw_ref, pos_s, runs_s, xs_ref, xn_buf, blk, zbuf, sem, *,
                   tm, tile_rows, ne, n_tiles):
    i = pl.program_id(0)
    last = pl.num_programs(0) - 1
    slots = MOE_TOP_K * tm * ROW_TILE

    def wait_runs(half):
        pltpu.make_async_copy(blk.at[half], xs_ref.at[pl.ds(0, slots)], sem.at[half]).wait()

    for half in range(2):
        rs = slice(half * tm, (half + 1) * tm)

        @pl.when(i > 0)
        def _drain_previous(half=half):
            wait_runs(half)

        _rows_to_tiles(xn_buf, _rms(h_ref[rs, :], nw_ref[...]))

        def place(t, carry, half=half):
            row = xn_buf[pl.ds(pl.multiple_of(t * ROW_TILE, ROW_TILE), ROW_TILE), :]
            for k in range(MOE_TOP_K):
                blk[half, pl.ds(pl.multiple_of(pos_s[k, half * tm + t], ROW_TILE), ROW_TILE), :] = row
            return carry

        lax.fori_loop(0, tm, place, 0, unroll=16)
        for e in range(ne):
            n = runs_s[half, 2, e] * ROW_TILE

            @pl.when(n > 0)
            def _send(e=e, n=n, half=half):
                src = pl.multiple_of(runs_s[half, 0, e] * ROW_TILE, ROW_TILE)
                dst = pl.multiple_of(runs_s[half, 1, e] * ROW_TILE, ROW_TILE)
                pltpu.make_async_copy(blk.at[half, pl.ds(src, n)], xs_ref.at[pl.ds(dst, n)], sem.at[half]
                                      ).start(priority=e % 2)

    @pl.when(i == last)
    def _zero_unused_rows():
        wait_runs(0)
        wait_runs(1)
        zbuf[...] = jnp.zeros_like(zbuf)

        def fill(row, nrows):
            at = pl.multiple_of(row * ROW_TILE, ROW_TILE)
            n = nrows * ROW_TILE
            c = pltpu.make_async_copy(zbuf.at[pl.ds(0, n)], xs_ref.at[pl.ds(at, n)], sem.at[2])
            c.start()
            c.wait()

        for e in range(ne):
            n_pad = (tile_rows - cnt_s[e] % tile_rows) % tile_rows

            @pl.when(n_pad > 0)
            def _fill(e=e, n_pad=n_pad):
                fill(off_s[e] + cnt_s[e], n_pad)

        def zero_tile(j, carry):
            fill(j * tile_rows, tile_rows)
            return carry

        lax.fori_loop(tot_s[0], n_tiles, zero_tile, 0)


def _expert_body(te_s, blk_s, tot_s, xs_ref, wg_ref, wu_ref, wd_ref, ys_ref, wg_b, wu_b, wd_b):
    i = pl.program_id(0)
    valid = i < tot_s[0]
    rows = xs_ref.shape[0] // ROW_TILE

    @pl.when(valid & ((i == 0) | (te_s[i] != te_s[jnp.maximum(i - 1, 0)])))
    def _new_expert():
        wg_b[...] = wg_ref[0, 0].astype(BF16)
        wu_b[...] = wu_ref[0, 0].astype(BF16)
        wd_b[...] = wd_ref[0, 0].astype(BF16)

    @pl.when(valid)
    def _run():
        x = _tiles_to_rows(xs_ref, rows).astype(BF16)
        hid = jax.nn.silu(_dot(x, wg_b[...])) * _dot(x, wu_b[...])
        _rows_to_tiles(ys_ref, _dot(hid.astype(BF16), wd_b[...]))

    @pl.when(jnp.logical_not(valid))
    def _unused_tile():
        ys_ref[...] = jnp.zeros_like(ys_ref)


def _combine_body(h_ref, info_ref, pos_s, runs_s, next_s, ys_ref, fw_ref, out_ref, blk, y1, y2, sem, *,
                  tm, ne, final):
    i = pl.program_id(0)
    slots = MOE_TOP_K * tm * ROW_TILE
    bufs = (y1, y2)

    def fetch(runs, half):
        for e in range(ne):
            n = runs[half, 2, e] * ROW_TILE

            @pl.when(n > 0)
            def _fetch(e=e, n=n):
                dst = pl.multiple_of(runs[half, 0, e] * ROW_TILE, ROW_TILE)
                src = pl.multiple_of(runs[half, 1, e] * ROW_TILE, ROW_TILE)
                pltpu.make_async_copy(ys_ref.at[pl.ds(src, n)], blk.at[half, pl.ds(dst, n)], sem.at[half]
                                      ).start(priority=e % 2)

    def finish(half):
        pltpu.make_async_copy(ys_ref.at[pl.ds(0, slots)], blk.at[half], sem.at[half]).wait()
        rs = slice(half * tm, (half + 1) * tm)

        def pick(t, carry):
            dst = pl.ds(pl.multiple_of(t * ROW_TILE, ROW_TILE), ROW_TILE)
            for k in range(MOE_TOP_K):
                at = pl.multiple_of(pos_s[k, half * tm + t], ROW_TILE)
                bufs[k][dst, :] = blk[half, pl.ds(at, ROW_TILE), :]
            return carry

        lax.fori_loop(0, tm, pick, 0, unroll=8)
        o = (h_ref[rs, :] + info_ref[rs, 0:1] * _tiles_to_rows(y1, tm) + info_ref[rs, 1:2] * _tiles_to_rows(y2, tm))
        if final:
            o = _rms(o, fw_ref[...])
        out_ref[rs, :] = o

    @pl.when(i == 0)
    def _first():
        fetch(runs_s, 0)

    fetch(runs_s, 1)
    finish(0)

    @pl.when(i < pl.num_programs(0) - 1)
    def _prefetch():
        fetch(next_s, 0)

    finish(1)


def _router_weights(w_group, b_group, w_router, b_router):
    d, ne = w_router.shape
    ng = w_group.shape[1]
    w = jnp.pad(jnp.concatenate([w_router, w_group], axis=1).astype(F32), ((0, 0), (0, ROUTE_LANES - ne - ng)))
    hi = w.astype(BF16)
    lo = (w - hi.astype(F32)).astype(BF16)
    bias = jnp.pad(jnp.concatenate([b_router, b_group]).astype(F32), (0, ROUTE_LANES - ne - ng)).reshape(1, -1)
    return jnp.concatenate([hi, lo], axis=1), bias


def _moe(h, nw, w_group, b_group, w_router, b_router, w_gate, w_up, w_down, layer, final_w):
    t, d = h.shape
    assert d == ROW_TILE * 128, "row-as-tile layout needs d_model == 1024"
    _, ne, _, ff = w_gate.shape
    tm = TM_MOE
    te_rows = TM_EXPERT
    nw2 = nw.reshape(1, d)
    wr3, bias = _router_weights(w_group, b_group, w_router, b_router)
    tri = jnp.asarray(np.tril(np.ones((tm, tm), np.float32), -1), BF16)
    nt = t // tm
    sel = jnp.asarray(np.eye(8, ROUTE_LANES, dtype=np.float32), BF16)
    info, pos, tile_info = pl.pallas_call(
        _route_body, grid=(nt,),
        in_specs=[pl.BlockSpec((tm, d), lambda i: (i, 0)),
                  pl.BlockSpec((1, d), lambda i: (0, 0)),
                  pl.BlockSpec((d, 2 * ROUTE_LANES), lambda i: (0, 0)),
                  pl.BlockSpec((1, ROUTE_LANES), lambda i: (0, 0)),
                  pl.BlockSpec((tm, tm), lambda i: (0, 0)),
                  pl.BlockSpec((8, ROUTE_LANES), lambda i: (0, 0))],
        out_specs=[pl.BlockSpec((tm, ROUTE_LANES), lambda i: (i, 0)), pl.BlockSpec((8, tm), lambda i: (0, i)),
                   pl.BlockSpec((8, ROUTE_LANES), lambda i: (i, 0))],
        out_shape=[jax.ShapeDtypeStruct((t, ROUTE_LANES), F32), jax.ShapeDtypeStruct((8, t), jnp.int32),
                   jax.ShapeDtypeStruct((nt * 8, ROUTE_LANES), F32)],
        compiler_params=_cparams(("parallel",)), name="moe_route",
    )(h, nw2, wr3, bias, tri, sel)

    tinfo = tile_info.reshape(nt, 8, ROUTE_LANES).astype(jnp.int32)
    tcnt = tinfo[:, 0, :]
    cnt_i = jnp.sum(tcnt, axis=0)[:ne]
    ntile = (cnt_i + te_rows - 1) // te_rows
    tile_end = jnp.cumsum(ntile)
    off_i = (tile_end - ntile) * te_rows
    n_tiles = (MOE_TOP_K * t) // te_rows + ne
    rows_total = n_tiles * te_rows
    ti = jnp.arange(n_tiles, dtype=jnp.int32)
    tot = tile_end[-1:]
    ti_c = jnp.minimum(ti, tot[0] - 1)
    tile_e = jnp.sum((ti_c[:, None] >= tile_end[None, :]).astype(jnp.int32), axis=1)
    gstart = jnp.pad(off_i, (0, ROUTE_LANES - ne))[None, :] + jnp.cumsum(tcnt, axis=0) - tcnt
    runs = jnp.pad(jnp.stack([tinfo[:, 1, :], gstart, tcnt], axis=1), ((0, 0), (0, 5), (0, 0)))

    xs = pl.pallas_call(
        functools.partial(_dispatch_body, tm=tm, tile_rows=te_rows, ne=ne, n_tiles=n_tiles),
        grid_spec=pltpu.PrefetchScalarGridSpec(
            num_scalar_prefetch=3, grid=(nt // 2,),
            in_specs=[pl.BlockSpec((2 * tm, d), lambda i, *_: (i, 0)),
                      pl.BlockSpec((1, d), lambda i, *_: (0, 0)),
                      pl.BlockSpec((8, 2 * tm), lambda i, *_: (0, i), memory_space=pltpu.SMEM),
                      pl.BlockSpec((2, 8, ROUTE_LANES), lambda i, *_: (i, 0, 0), memory_space=pltpu.SMEM)],
            out_specs=pl.BlockSpec(memory_space=pl.ANY),
            scratch_shapes=[pltpu.VMEM((tm * ROW_TILE, 128), F32),
                            pltpu.VMEM((2, MOE_TOP_K * tm * ROW_TILE, 128), F32),
                            pltpu.VMEM((te_rows * ROW_TILE, 128), F32), pltpu.SemaphoreType.DMA((3,))]),
        out_shape=jax.ShapeDtypeStruct((rows_total * ROW_TILE, 128), F32),
        compiler_params=_cparams(("arbitrary",)), name="moe_dispatch",
    )(cnt_i, off_i, tot, h, nw2, pos, runs)

    ys = pl.pallas_call(
        _expert_body,
        grid_spec=pltpu.PrefetchScalarGridSpec(
            num_scalar_prefetch=3, grid=(n_tiles,),
            in_specs=[pl.BlockSpec((te_rows * ROW_TILE, 128), lambda i, e, b, v: (b[i], 0)),
                      pl.BlockSpec((1, 1, d, ff), lambda i, e, b, v: (layer, e[i], 0, 0)),
                      pl.BlockSpec((1, 1, d, ff), lambda i, e, b, v: (layer, e[i], 0, 0)),
                      pl.BlockSpec((1, 1, ff, d), lambda i, e, b, v: (layer, e[i], 0, 0))],
            out_specs=pl.BlockSpec((te_rows * ROW_TILE, 128), lambda i, e, b, v: (i, 0)),
            scratch_shapes=[pltpu.VMEM((d, ff), BF16), pltpu.VMEM((d, ff), BF16), pltpu.VMEM((ff, d), BF16)]),
        out_shape=jax.ShapeDtypeStruct((rows_total * ROW_TILE, 128), F32),
        compiler_params=_cparams(("arbitrary",)), name="moe_expert",
    )(tile_e, ti_c, tot, xs, w_gate, w_up, w_down)

    final = final_w is not None
    fw = (final_w if final else nw).reshape(1, d)
    return pl.pallas_call(
        functools.partial(_combine_body, tm=tm, ne=ne, final=final), grid=(nt // 2,),
        in_specs=[pl.BlockSpec((2 * tm, d), lambda i: (i, 0)),
                  pl.BlockSpec((2 * tm, ROUTE_LANES), lambda i: (i, 0)),
                  pl.BlockSpec((8, 2 * tm), lambda i: (0, i), memory_space=pltpu.SMEM),
                  pl.BlockSpec((2, 8, ROUTE_LANES), lambda i: (i, 0, 0), memory_space=pltpu.SMEM),
                  pl.BlockSpec((2, 8, ROUTE_LANES), lambda i: (jnp.minimum(i + 1, nt // 2 - 1), 0, 0),
                               memory_space=pltpu.SMEM),
                  pl.BlockSpec(memory_space=pl.ANY),
                  pl.BlockSpec((1, d), lambda i: (0, 0))],
        out_specs=pl.BlockSpec((2 * tm, d), lambda i: (i, 0)),
        out_shape=jax.ShapeDtypeStruct((t, d), F32),
        scratch_shapes=[pltpu.VMEM((2, MOE_TOP_K * tm * ROW_TILE, 128), F32), pltpu.VMEM((tm * ROW_TILE, 128), F32),
                        pltpu.VMEM((tm * ROW_TILE, 128), F32), pltpu.SemaphoreType.DMA((2,))],
        compiler_params=_cparams(("arbitrary",)), name="moe_combine",
    )(h, info, pos, runs, runs, ys, fw)


def _s5_weights(lam_re, lam_im, b_re, b_im, c_re, c_im, log_dt, nsteps):
    L = S5_CHUNK
    g, p = lam_re.shape
    ch = b_re.shape[-1]
    lam = lax.complex(lam_re.astype(F32), lam_im.astype(F32))
    dt = jnp.exp(log_dt.astype(F32))[:, None]
    lam_bar = jnp.exp(lam * dt)
    b_bar = ((lam_bar - 1.0) / lam)[..., None] * lax.complex(b_re.astype(F32), b_im.astype(F32))
    cmat = lax.complex(c_re.astype(F32), c_im.astype(F32))
    pows = [jnp.ones_like(lam_bar)]
    for _ in range(L):
        pows.append(pows[-1] * lam_bar)
    pw = jnp.stack(pows, axis=1)
    kern = jnp.real(jnp.einsum('gtop,gpi->gito', cmat[:, None, :, :] * pw[:, :L, None, :], b_bar))
    lead = (L - 1) * ch
    plen = lead + L * ch
    kpad = jnp.pad(kern.reshape(g, ch, L * ch).astype(BF16), ((0, 0), (0, 0), (lead, 0)))
    win = jnp.tile(kpad, (1, 1, L))[:, :, lead:lead + L * (plen - ch)].reshape(g, ch, L, plen - ch)[..., :L * ch]
    toep = win.transpose(0, 2, 1, 3).reshape(g, L * ch, L * ch)
    wst = pw[:, L - 1 - np.arange(L)][:, :, :, None] * b_bar[:, None, :, :]
    wst = wst.transpose(0, 1, 3, 2).reshape(g, L * ch, p)
    wst = jnp.concatenate([jnp.real(wst), jnp.imag(wst)], axis=-1)
    mo = cmat.transpose(0, 2, 1)[:, :, None, :] * pw[:, 1:L + 1].transpose(0, 2, 1)[:, :, :, None]
    mo = mo.reshape(g, p, L * ch)
    wout = jnp.concatenate([jnp.real(mo), -jnp.imag(mo)], axis=1)
    a = pw[:, L]
    ars, ais = [], []
    for _ in range(nsteps):
        ars.append(jnp.concatenate([jnp.real(a), jnp.real(a)], axis=-1))
        ais.append(jnp.concatenate([-jnp.imag(a), jnp.imag(a)], axis=-1))
        a = a * a
    wcat = jnp.concatenate([toep, wst.astype(BF16)], axis=-1)
    return wcat, wout.astype(BF16), jnp.stack(ars, axis=1), jnp.stack(ais, axis=1)


def _s5_body(u_ref, wcat_ref, wout_ref, ar_ref, ai_ref, y_ref, us_ref, ys_ref, *, nsteps):
    L = S5_CHUNK
    ch = S5_GROUP_CH
    gpc = 128 // ch
    ny = L * ch
    nc = u_ref.shape[0] // L
    for s in range(L):
        us_ref[s] = u_ref[pl.ds(s, nc, stride=L), :]
    lane = lax.broadcasted_iota(jnp.int32, (nc, 128), 1)
    ridx = lax.broadcasted_iota(jnp.int32, (nc, 128), 0)

    def shift(x, k):
        return jnp.where(ridx >= k, pltpu.roll(x, k, axis=0), 0.0)

    for gi in range(gpc):
        halves = []
        for hh in range(ny // 128):
            acc = None
            for s8 in range(gpc):
                rot = ((s8 - gi) * ch) % 128
                src = us_ref[hh * gpc + s8]
                if rot:
                    src = pltpu.roll(src, rot, axis=1)
                slot = (lane >= s8 * ch) & (lane < (s8 + 1) * ch)
                acc = jnp.where(slot, src, 0.0) if acc is None else jnp.where(slot, src, acc)
            halves.append(acc)
        ug = jnp.concatenate(halves, axis=1).astype(BF16)
        r = _dot(ug, wcat_ref[gi])
        y1 = r[:, :ny]
        z = r[:, ny:]
        w = shift(z, 1)
        wx = pltpu.roll(w, z.shape[1] // 2, axis=1)
        for k in range(nsteps):
            if (1 << k) >= nc:
                break
            sk = shift(w, 1 << k)
            sx = shift(wx, 1 << k)
            ar = ar_ref[gi, k:k + 1, :]
            ai = ai_ref[gi, k:k + 1, :]
            w, wx = w + sk * ar + sx * ai, wx + sx * ar - sk * ai
        yg = y1 + _dot(w.astype(BF16), wout_ref[gi])
        slot = (lane >= gi * ch) & (lane < (gi + 1) * ch)
        for t in range(L):
            src = yg[:, (t // gpc) * 128:(t // gpc + 1) * 128]
            rot = ((gi - t % gpc) * ch) % 128
            if rot:
                src = pltpu.roll(src, rot, axis=1)
            ys_ref[t] = jnp.where(slot, src, 0.0) if gi == 0 else jnp.where(slot, src, ys_ref[t])
    for t in range(L):
        y_ref[pl.ds(t, nc, stride=L), :] = ys_ref[t]


def _s5(main, col0, width, nb, seq, lam_re, lam_im, b_re, b_im, c_re, c_im, log_dt):
    t = main.shape[0]
    L = S5_CHUNK
    ch = S5_GROUP_CH
    g = width // ch
    nc = seq // L
    gpc = 128 // ch
    ncol = width // 128
    assert col0 % 128 == 0 and width % 128 == 0 and (L * ch) % 128 == 0
    nsteps = max(1, (nc - 1).bit_length())
    wcat, wout, ar, ai = _s5_weights(lam_re, lam_im, b_re, b_im, c_re, c_im, log_dt, nsteps)
    body = functools.partial(_s5_body, nsteps=nsteps)
    return pl.pallas_call(
        body, grid=(ncol, nb),
        in_specs=[pl.BlockSpec((seq, 128), lambda j, b: (b, col0 // 128 + j)),
                  pl.BlockSpec((gpc,) + wcat.shape[1:], lambda j, b: (j, 0, 0)),
                  pl.BlockSpec((gpc,) + wout.shape[1:], lambda j, b: (j, 0, 0)),
                  pl.BlockSpec((gpc,) + ar.shape[1:], lambda j, b: (j, 0, 0)),
                  pl.BlockSpec((gpc,) + ai.shape[1:], lambda j, b: (j, 0, 0))],
        out_specs=pl.BlockSpec((seq, 128), lambda j, b: (b, j)),
        out_shape=jax.ShapeDtypeStruct((t, width), F32),
        scratch_shapes=[pltpu.VMEM((L, nc, 128), F32), pltpu.VMEM((L, nc, 128), F32)],
        compiler_params=_cparams(("parallel", "parallel")), name="s5",
    )(main, wcat, wout, ar, ai)


def _hgrn_gmat():
    L = HGRN_CHUNK
    blocks = 2 + int(np.log2(L))
    gm = np.zeros((blocks * L, L), np.float32)
    for j in range(L):
        gm[j, :j + 1] = 1.0
        gm[L + j, j + 1:] = 1.0
    li, m = 2, L
    while m >= 2:
        half = m // 2
        for j in range(L):
            pos = j % m
            r = j - pos + half - 1
            if pos >= half:
                gm[li * L + j, r + 1:j + 1] = 1.0
            else:
                gm[li * L + j, j + 1:r + 1] = 1.0
        li += 1
        m //= 2
    return gm


def _hgrn_body(mh_ref, fg_ref, gm_ref, lb_ref, nw_ref, out_ref, st_ref, *, nb, nh, dh):
    L = HGRN_CHUNK
    w = nh * dh

    @pl.when(pl.program_id(0) == 0)
    def _init():
        st_ref[...] = jnp.zeros_like(st_ref)

    row = lax.broadcasted_iota(jnp.int32, (L, 2 * L), 0)
    col = lax.broadcasted_iota(jnp.int32, (L, 2 * L), 1) & (L - 1)
    rowd = lax.broadcasted_iota(jnp.int32, (L, 2 * dh), 0)
    laned = lax.broadcasted_iota(jnp.int32, (L, 2 * dh), 1)
    first = laned < dh
    eye = row == col

    def blockdiag(x):
        z = jnp.zeros_like(x)
        return jnp.concatenate([jnp.where(first, x, z), jnp.where(first, z, x)], axis=0)

    gm2 = gm_ref[...]
    lb = lb_ref[...]
    zst = jnp.zeros((dh, dh), BF16)
    for b in range(nb):
        fg = fg_ref[b]
        f = lb + (1.0 - lb) * jax.nn.sigmoid(fg)
        kk = (1.0 - lb) * jax.nn.sigmoid(-fg)
        lf = jnp.log(f)
        hi = lf.astype(BF16)
        mid = (lf - hi.astype(F32)).astype(BF16)
        p_all = jnp.exp(_dot(gm2, jnp.concatenate([hi, mid], axis=0)))
        for hp in range(nh // 2):
            i0 = b * nh + 2 * hp
            cs = slice(2 * hp * dh, (2 * hp + 2) * dh)
            qb = mh_ref[b, :, 2 * hp * dh:(2 * hp + 2) * dh]
            vb = mh_ref[b, :, w + 2 * hp * dh:w + (2 * hp + 2) * dh]
            og = mh_ref[b, :, 2 * w + 2 * hp * dh:2 * w + (2 * hp + 2) * dh].astype(F32)
            q = qb.astype(F32)
            k = kk[:, cs]
            pb = p_all[0:L, cs]
            pe = p_all[L:2 * L, cs]
            st0 = st_ref[i0]
            st1 = st_ref[i0 + 1]
            stbd = jnp.concatenate([jnp.concatenate([st0.astype(BF16), zst], axis=1),
                                    jnp.concatenate([zst, st1.astype(BF16)], axis=1)], axis=0)
            o = _dot_nt((q * pb).astype(BF16), stbd)
            attn = jnp.where(eye, _dot_nt(qb, blockdiag(k.astype(BF16))), 0.0)
            li, m = 2, L
            while m >= 2:
                pl_ = p_all[li * L:(li + 1) * L, cs]
                up = (rowd & (m - 1)) >= (m // 2)
                ql = jnp.where(up, q * pl_, 0.0).astype(BF16)
                kl = jnp.where(up, 0.0, k * pl_).astype(BF16)
                same = (row & ~(m - 1)) == (col & ~(m - 1))
                attn = attn + jnp.where(same, _dot_nt(ql, blockdiag(kl)), 0.0)
                li += 1
                m //= 2
            o = o + _dot(attn.astype(BF16), blockdiag(vb))
            kh = (k * pe).astype(BF16)
            for j in range(2):
                hs = slice(j * dh, (j + 1) * dh)
                gs = slice((2 * hp + j) * dh, (2 * hp + j + 1) * dh)
                st = st0 if j == 0 else st1
                st_ref[i0 + j] = st * pb[L - 1:L, hs] + _dot_tn(vb[:, hs], kh[:, hs])
                oj = o[:, hs]
                on = oj * lax.rsqrt(jnp.mean(oj * oj, axis=-1, keepdims=True) + RMS_EPS)
                out_ref[b, :, gs] = (on * nw_ref[:, gs] * jax.nn.silu(og[:, hs])).astype(out_ref.dtype)


def _hgrn(mh, mf, lower_bound, out_norm, nb, seq):
    t = mh.shape[0]
    nh = HGRN_HEADS
    w = out_norm.shape[0]
    dh = w // nh
    L = HGRN_CHUNK
    nc = seq // L
    gm = _hgrn_gmat()
    gm = jnp.asarray(np.concatenate([gm, gm], axis=1), BF16)
    body = functools.partial(_hgrn_body, nb=nb, nh=nh, dh=dh)
    out = pl.pallas_call(
        body, grid=(nc,),
        in_specs=[pl.BlockSpec((nb, L, 3 * w), lambda c: (0, c, 0)),
                  pl.BlockSpec((nb, L, w), lambda c: (0, c, 0)),
                  pl.BlockSpec(gm.shape, lambda c: (0, 0)),
                  pl.BlockSpec((1, w), lambda c: (0, 0)),
                  pl.BlockSpec((1, w), lambda c: (0, 0))],
        out_specs=pl.BlockSpec((nb, L, w), lambda c: (0, c, 0)),
        out_shape=jax.ShapeDtypeStruct((nb, seq, w), BF16),
        scratch_shapes=[pltpu.VMEM((nb * nh, dh, dh), F32)],
        compiler_params=_cparams(("arbitrary",)), name="hgrn",
    )(mh.reshape(nb, seq, mh.shape[1]), mf.reshape(nb, seq, mf.shape[1]), gm, lower_bound.reshape(1, w),
      out_norm.reshape(1, w))
    return out.reshape(t, w)


def _out_c_body(h_ref, ys_ref, u_ref, oh_ref, d_ref, wglu_ref, bglu_ref, w_ref, out_ref):
    ws = ys_ref.shape[1]
    z = jax.nn.gelu(ys_ref[...] + d_ref[...] * u_ref[...])
    gate = jax.nn.sigmoid(_dot(z.astype(BF16), wglu_ref[...]) + bglu_ref[...])
    out_ref[...] = (h_ref[...] + _dot((z * gate).astype(BF16), w_ref[:ws, :]) + _dot(oh_ref[...], w_ref[ws:, :]))


def _out_c(h, ys, main, oh, d_skip, w_glu, b_glu, w_out):
    t, d = h.shape
    ws = ys.shape[1]
    wh = oh.shape[1]
    tm = TM_PROJ
    ub = (main.shape[1] - ws) // ws
    return pl.pallas_call(
        _out_c_body, grid=(t // tm,),
        in_specs=[pl.BlockSpec((tm, d), lambda i: (i, 0)),
                  pl.BlockSpec((tm, ws), lambda i: (i, 0)),
                  pl.BlockSpec((tm, ws), lambda i: (i, ub)),
                  pl.BlockSpec((tm, wh), lambda i: (i, 0)),
                  pl.BlockSpec((1, ws), lambda i: (0, 0)),
                  pl.BlockSpec(w_glu.shape, lambda i: (0, 0)),
                  pl.BlockSpec((1, ws), lambda i: (0, 0)),
                  pl.BlockSpec(w_out.shape, lambda i: (0, 0))],
        out_specs=pl.BlockSpec((tm, d), lambda i: (i, 0)),
        out_shape=jax.ShapeDtypeStruct((t, d), F32),
        compiler_params=_cparams(("parallel",)), name="out_c",
    )(h, ys, main, oh, d_skip.reshape(1, ws), w_glu, b_glu.reshape(1, ws), w_out)


def kernel(x, norm_mix, norm_ffn, norm_final, ab_w_in, ab_gate_bias, ab_head_norm, ab_conv_w, ab_w_out, cd_w_in, s5_lambda_re, s5_lambda_im, s5_b_re, s5_b_im, s5_c_re, s5_c_im, s5_d, s5_log_dt, s5_w_glu, s5_b_glu, hgrn_lb, hgrn_out_norm, cd_w_out, moe_w_group, moe_b_group, moe_w_router, moe_b_router, moe_w_gate, moe_w_up, moe_w_down):
    nb, seq, d = x.shape
    depth = norm_mix.shape[0]
    h = x.reshape(nb * seq, d)
    for layer in range(depth):
        j = layer // 2
        if layer % 2 == 0:
            wm = ab_head_norm.shape[1]
            ng = ab_gate_bias.shape[1]
            w_in = ab_w_in[j]
            w_main = jnp.concatenate([w_in[:, :4 * wm], w_in[:, 4 * wm + ng:]], axis=1).astype(BF16)
            w_gates = w_in[:, 4 * wm:4 * wm + ng].astype(BF16)
            main, g, gt = _proj(h, norm_mix[layer], w_main, w_gates)
            hm = _mlstm(main, g, gt, ab_gate_bias[j], ab_head_norm[j], nb, seq)
            h = _out_a(h, hm, main, ab_conv_w[j], ab_w_out[j].astype(BF16), seq)
        else:
            ws = s5_d.shape[1]
            w_in = cd_w_in[j]
            wh = hgrn_out_norm.shape[1]
            cu, cq, cf, ci, co = (w_in[:, :ws], w_in[:, ws:ws + wh], w_in[:, ws + wh:ws + 2 * wh],
                                  w_in[:, ws + 2 * wh:ws + 3 * wh], w_in[:, ws + 3 * wh:])
            w_main = jnp.concatenate([cq, ci, co, cf, cu], axis=1).astype(BF16)
            mh, mf = _proj(h, norm_mix[layer], w_main, n_lo=3 * wh)
            sm = jax.nn.softmax(hgrn_lb.astype(F32), axis=0)
            lower_bound = jnp.cumsum(sm, axis=0)[layer] - sm[0]
            ys = _s5(mf, wh, ws, nb, seq, s5_lambda_re[j], s5_lambda_im[j], s5_b_re[j], s5_b_im[j],
                     s5_c_re[j], s5_c_im[j], s5_log_dt[j])
            oh = _hgrn(mh, mf, lower_bound, hgrn_out_norm[j], nb, seq)
            h = _out_c(h, ys, mf, oh, s5_d[j], s5_w_glu[j].astype(BF16), s5_b_glu[j], cd_w_out[j].astype(BF16))
        h = _moe(h, norm_ffn[layer], moe_w_group[layer], moe_b_group[layer], moe_w_router[layer], moe_b_router[layer],
                 moe_w_gate, moe_w_up, moe_w_down, layer,
                 norm_final if layer == depth - 1 else None)
    return h.reshape(nb, seq, d)
```

```python
import functools

import numpy as np
import jax
import jax.numpy as jnp
from jax import lax
from jax.experimental import pallas as pl
from jax.experimental.pallas import tpu as pltpu

F32 = jnp.float32
BF16 = jnp.bfloat16
RMS_EPS = 1e-6
MLSTM_CHUNK = 512
HGRN_CHUNK = 128
S5_CHUNK = 16
S5_GROUP_CH = 16
MLSTM_HEADS = 4
HGRN_HEADS = 4
MOE_GROUPS = 4
MOE_EPG = 8
ROUTE_LANES = 128
TM_PROJ = 1024
TM_MOE = 512
TM_EXPERT = 512
MOE_TOP_K = 2
EXPERT_RING = 3
VMEM_LIMIT = 56 * 1024 * 1024

_NT = (((1,), (1,)), ((), ()))
_TN = (((0,), (0,)), ((), ()))


def _cparams(sem):
    return pltpu.CompilerParams(dimension_semantics=sem, vmem_limit_bytes=VMEM_LIMIT)


def _rms(x, w):
    return x * lax.rsqrt(jnp.mean(x * x, axis=-1, keepdims=True) + RMS_EPS) * w


def _split3(x):
    hi = x.astype(BF16)
    r = x - hi.astype(F32)
    mid = r.astype(BF16)
    lo = (r - mid.astype(F32)).astype(BF16)
    return hi, mid, lo


def _dot(a, b):
    return jnp.dot(a, b, preferred_element_type=F32)


def _dot_nt(a, b):
    return lax.dot_general(a, b, _NT, preferred_element_type=F32)


def _dot_tn(a, b):
    return lax.dot_general(a, b, _TN, preferred_element_type=F32)


def _proj_gates_body(x_ref, nw_ref, w_ref, wg_ref, wgt_ref, main_ref, g_ref, gt_ref):
    xn = _rms(x_ref[...], nw_ref[...]).astype(BF16)
    main_ref[...] = _dot(xn, w_ref[...]).astype(main_ref.dtype)
    g_ref[...] = _dot(xn, wg_ref[...])[:, : g_ref.shape[1]]
    gt_ref[...] = _dot_nt(wgt_ref[...], xn)


def _proj_split_body(x_ref, nw_ref, w_ref, lo_ref, hi_ref):
    xn = _rms(x_ref[...], nw_ref[...]).astype(BF16)
    r = _dot(xn, w_ref[...])
    n_lo = lo_ref.shape[1]
    lo_ref[...] = r[:, :n_lo].astype(lo_ref.dtype)
    hi_ref[...] = r[:, n_lo:]


def _proj(h, nw, w_main, w_gates=None, n_lo=None):
    t, d = h.shape
    n = w_main.shape[1]
    tm = TM_PROJ
    x_spec = pl.BlockSpec((tm, d), lambda i: (i, 0))
    nw_spec = pl.BlockSpec((1, d), lambda i: (0, 0))
    w_spec = pl.BlockSpec((d, n), lambda i: (0, 0))
    if w_gates is None:
        return pl.pallas_call(
            _proj_split_body, grid=(t // tm,), in_specs=[x_spec, nw_spec, w_spec],
            out_specs=[pl.BlockSpec((tm, n_lo), lambda i: (i, 0)), pl.BlockSpec((tm, n - n_lo), lambda i: (i, 0))],
            out_shape=[jax.ShapeDtypeStruct((t, n_lo), BF16), jax.ShapeDtypeStruct((t, n - n_lo), F32)],
            compiler_params=_cparams(("parallel",)), name="proj",
        )(h, nw.reshape(1, d), w_main)
    ng = w_gates.shape[1]
    wg_pad = jnp.pad(w_gates, ((0, 0), (0, 128 - ng)))
    return pl.pallas_call(
        _proj_gates_body, grid=(t // tm,),
        in_specs=[x_spec, nw_spec, w_spec, pl.BlockSpec((d, 128), lambda i: (0, 0)),
                  pl.BlockSpec((ng, d), lambda i: (0, 0))],
        out_specs=[pl.BlockSpec((tm, n), lambda i: (i, 0)), pl.BlockSpec((tm, ng), lambda i: (i, 0)),
                   pl.BlockSpec((ng, tm), lambda i: (0, i))],
        out_shape=[jax.ShapeDtypeStruct((t, n), BF16), jax.ShapeDtypeStruct((t, ng), F32),
                   jax.ShapeDtypeStruct((ng, t), F32)],
        compiler_params=_cparams(("parallel",)), name="proj_gates",
    )(h, nw.reshape(1, d), w_main, wg_pad, w_gates.T)


def _mlstm_body(main_ref, g_ref, gt_ref, br_ref, bc_ref, hn_ref, out_ref, c_ref, n_ref, m_ref, *, nb, nh, dh):
    L = MLSTM_CHUNK
    w = nh * dh

    @pl.when(pl.program_id(0) == 0)
    def _init():
        c_ref[...] = jnp.zeros_like(c_ref)
        n_ref[...] = jnp.zeros_like(n_ref)
        m_ref[...] = jnp.full_like(m_ref, -1e30)

    row = lax.broadcasted_iota(jnp.int32, (L, L), 0)
    col = lax.broadcasted_iota(jnp.int32, (L, L), 1)
    causal = col <= row
    tril = causal.astype(BF16)
    triu = (row <= col).astype(BF16)
    scale = dh ** -0.5
    for b in range(nb):
        g = g_ref[b] + br_ref[...]
        gt = gt_ref[b, 0] + bc_ref[...]
        i_c = g[:, :nh]
        i_r = gt[:nh, :]
        lfc = _split3(jax.nn.log_sigmoid(g[:, nh:]))
        lfr = _split3(jax.nn.log_sigmoid(gt[nh:, :]))
        bc_all = _dot(tril, lfc[0]) + _dot(tril, lfc[1]) + _dot(tril, lfc[2])
        br_all = _dot(lfr[0], triu) + _dot(lfr[1], triu) + _dot(lfr[2], triu)
        for h in range(nh):
            idx = b * nh + h
            qb = main_ref[b, :, h * dh:(h + 1) * dh]
            vb = main_ref[b, :, 2 * w + h * dh:2 * w + (h + 1) * dh]
            q = qb.astype(F32)
            k = main_ref[b, :, w + h * dh:w + (h + 1) * dh].astype(F32) * scale
            v = vb.astype(F32)
            o = main_ref[b, :, 3 * w + h * dh:3 * w + (h + 1) * dh].astype(F32)
            bc = bc_all[:, h:h + 1]
            br = br_all[h:h + 1, :]
            ir = i_r[h:h + 1, :]
            ic = i_c[:, h:h + 1]
            m_prev = m_ref[idx]
            c_prev = c_ref[idx]
            n_prev = n_ref[idx]
            logw = jnp.where(causal, bc - br + ir, -jnp.inf)
            inter = bc + m_prev
            m_row = jnp.maximum(jnp.max(logw, axis=-1, keepdims=True), inter)
            kb = k.astype(BF16)
            s = _dot_nt(qb, kb) * jnp.exp(logw - m_row)
            isc = jnp.exp(inter - m_row)
            num = _dot(s.astype(BF16), vb) + isc * _dot_nt(qb, c_prev.astype(BF16))
            den = jnp.sum(s, axis=-1, keepdims=True) + isc * jnp.sum(q * n_prev, axis=-1, keepdims=True)
            hout = num / jnp.maximum(jnp.abs(den), jnp.exp(-m_row))
            b_end = bc[L - 1:L, :]
            logg = b_end - bc + ic
            m_new = jnp.maximum(b_end + m_prev, jnp.max(logg, axis=0, keepdims=True))
            wk = jnp.exp(logg - m_new)
            decay = jnp.exp(b_end + m_prev - m_new)
            c_ref[idx] = decay * c_prev + _dot_tn((v * wk).astype(BF16), kb)
            n_ref[idx] = decay * n_prev + jnp.sum(wk * k, axis=0, keepdims=True)
            m_ref[idx] = m_new
            hn = hout * lax.rsqrt(jnp.mean(hout * hout, axis=-1, keepdims=True) + RMS_EPS)
            out_ref[b, :, h * dh:(h + 1) * dh] = (hn * hn_ref[:, h * dh:(h + 1) * dh] * jax.nn.sigmoid(o)
                                                  ).astype(out_ref.dtype)


def _mlstm(main, g, gt, gate_bias, head_norm, nb, seq):
    t, n = main.shape
    nh = MLSTM_HEADS
    w = head_norm.shape[0]
    dh = w // nh
    L = MLSTM_CHUNK
    nc = seq // L
    main3 = main.reshape(nb, seq, n)
    g3 = g.reshape(nb, seq, 2 * nh)
    gt4 = gt.reshape(2 * nh, nb, nc, L).transpose(1, 2, 0, 3)
    body = functools.partial(_mlstm_body, nb=nb, nh=nh, dh=dh)
    out = pl.pallas_call(
        body, grid=(nc,),
        in_specs=[pl.BlockSpec((nb, L, 4 * w), lambda c: (0, c, 0)),
                  pl.BlockSpec((nb, L, 2 * nh), lambda c: (0, c, 0)),
                  pl.BlockSpec((nb, 1, 2 * nh, L), lambda c: (0, c, 0, 0)),
                  pl.BlockSpec((1, 2 * nh), lambda c: (0, 0)),
                  pl.BlockSpec((2 * nh, 1), lambda c: (0, 0)),
                  pl.BlockSpec((1, w), lambda c: (0, 0))],
        out_specs=pl.BlockSpec((nb, L, w), lambda c: (0, c, 0)),
        out_shape=jax.ShapeDtypeStruct((nb, seq, w), BF16),
        scratch_shapes=[pltpu.VMEM((nb * nh, dh, dh), F32), pltpu.VMEM((nb * nh, 1, dh), F32),
                        pltpu.VMEM((nb * nh, 1, 1), F32)],
        compiler_params=_cparams(("arbitrary",)), name="mlstm",
    )(main3, g3, gt4, gate_bias.reshape(1, 2 * nh), gate_bias.reshape(2 * nh, 1), head_norm.reshape(1, w))
    return out.reshape(t, w)


def _out_a_body(h_ref, hm_ref, gb_ref, gc_ref, xin_ref, pgc_ref, pxin_ref, cw_ref, w_ref, out_ref, *, tm, seq):
    i = pl.program_id(0)
    wm = hm_ref.shape[1]
    p = gc_ref[...].astype(F32) * xin_ref[...].astype(F32)
    first = (i * tm) % seq == 0
    hr = pgc_ref.shape[0]
    pp = jnp.where(first, 0.0, pgc_ref[...].astype(F32) * pxin_ref[...].astype(F32))
    rowi = lax.broadcasted_iota(jnp.int32, p.shape, 0)
    p1 = jnp.where(rowi == 0, pp[hr - 1:hr, :], pltpu.roll(p, 1, axis=0))
    p2 = jnp.where(rowi == 0, pp[hr - 2:hr - 1, :], jnp.where(rowi == 1, pp[hr - 1:hr, :], pltpu.roll(p, 2, axis=0)))
    yc = gb_ref[...].astype(F32) * (cw_ref[0:1, :] * p2 + cw_ref[1:2, :] * p1 + cw_ref[2:3, :] * p)
    out_ref[...] = (h_ref[...] + _dot(hm_ref[...], w_ref[:wm, :]) + _dot(yc.astype(BF16), w_ref[wm:, :]))


def _out_a(h, hm, main, conv_w, w_out, seq):
    t, d = h.shape
    wm = hm.shape[1]
    wc = conv_w.shape[1]
    tm = TM_PROJ
    cb = (4 * wm) // wc
    assert seq % tm == 0, "a token tile must not straddle two sequences (causal conv halo)"
    halo = 16
    rb = tm // halo
    prev = lambda i: jnp.maximum(i * rb - 1, 0)
    body = functools.partial(_out_a_body, tm=tm, seq=seq)
    return pl.pallas_call(
        body, grid=(t // tm,),
        in_specs=[pl.BlockSpec((tm, d), lambda i: (i, 0)),
                  pl.BlockSpec((tm, wm), lambda i: (i, 0)),
                  pl.BlockSpec((tm, wc), lambda i: (i, cb)),
                  pl.BlockSpec((tm, wc), lambda i: (i, cb + 1)),
                  pl.BlockSpec((tm, wc), lambda i: (i, cb + 2)),
                  pl.BlockSpec((halo, wc), lambda i: (prev(i), cb + 1)),
                  pl.BlockSpec((halo, wc), lambda i: (prev(i), cb + 2)),
                  pl.BlockSpec(conv_w.shape, lambda i: (0, 0)),
                  pl.BlockSpec(w_out.shape, lambda i: (0, 0))],
        out_specs=pl.BlockSpec((tm, d), lambda i: (i, 0)),
        out_shape=jax.ShapeDtypeStruct((t, d), F32),
        compiler_params=_cparams(("parallel",)), name="out_a",
    )(h, hm, main, main, main, main, main, conv_w, w_out)


def _route(logits, lane):
    ne = MOE_GROUPS * MOE_EPG
    big = 1e9
    gl = jnp.where((lane >= ne) & (lane < ne + MOE_GROUPS), logits, -jnp.inf)
    gmax = jnp.max(gl, axis=-1, keepdims=True)
    gidx = jnp.min(jnp.where(gl == gmax, lane - ne, big), axis=-1, keepdims=True)
    gval = 1.0 / jnp.sum(jnp.exp(gl - gmax), axis=-1, keepdims=True)
    lo = gidx * MOE_EPG
    sel = jnp.where((lane >= lo) & (lane < lo + MOE_EPG), logits, -jnp.inf)
    l1 = jnp.max(sel, axis=-1, keepdims=True)
    i1 = jnp.min(jnp.where(sel == l1, lane, big), axis=-1, keepdims=True)
    sel2 = jnp.where(lane == i1, -jnp.inf, sel)
    l2 = jnp.max(sel2, axis=-1, keepdims=True)
    i2 = jnp.min(jnp.where(sel2 == l2, lane, big), axis=-1, keepdims=True)
    r = jnp.exp(l2 - l1)
    w1 = gval / (1.0 + r)
    return i1, i2, w1, w1 * r


ROW_TILE = 8


def _rows_to_tiles(ref, x):
    n = x.shape[0]
    for c in range(ROW_TILE):
        ref[pl.ds(c, n, stride=ROW_TILE), :] = x[:, c * 128:(c + 1) * 128]


def _tiles_to_rows(ref, n):
    return jnp.concatenate([ref[pl.ds(c, n, stride=ROW_TILE), :] for c in range(ROW_TILE)], axis=1)


def _lane_put(lane, cols):
    out = jnp.where(lane == 0.0, cols[0], 0.0)
    for k in range(1, len(cols)):
        out = out + jnp.where(lane == float(k), cols[k], 0.0)
    return out


def _lane_get(lane, x, idx_col):
    return jnp.sum(jnp.where(lane == idx_col, x, 0.0), axis=-1, keepdims=True)


def _route_body(h_ref, nw_ref, wr_ref, br_ref, tri_ref, sel_ref, info_ref, pos_ref, tile_ref):
    xn = _rms(h_ref[...], nw_ref[...])
    hi = xn.astype(BF16)
    lo = (xn - hi.astype(F32)).astype(BF16)
    tm = hi.shape[0]
    r = _dot(jnp.concatenate([hi, lo], axis=0), wr_ref[...])
    logits = r[:tm, :ROUTE_LANES] + r[tm:, :ROUTE_LANES] + r[:tm, ROUTE_LANES:] + br_ref[...]
    lane = lax.broadcasted_iota(jnp.int32, logits.shape, 1).astype(F32)
    i1, i2, w1, w2 = _route(logits, lane)
    ind = jnp.where((lane == i1) | (lane == i2), 1.0, 0.0)
    ahead = _dot(tri_ref[...], ind.astype(BF16))
    tcnt = jnp.broadcast_to(jnp.sum(ind, axis=0, keepdims=True), tile_ref.shape)
    lane8 = lax.broadcasted_iota(jnp.int32, tile_ref.shape, 1)
    sub8 = lax.broadcasted_iota(jnp.int32, tile_ref.shape, 0)
    incl = tcnt
    sh = 1
    while sh < tile_ref.shape[1]:
        incl = incl + jnp.where(lane8 >= sh, pltpu.roll(incl, sh, axis=1), 0.0)
        sh *= 2
    toff = incl - tcnt
    local = (ahead + toff[0:1, :]) * ROW_TILE
    info_ref[...] = _lane_put(lane, [w1, w2])
    tile_ref[...] = jnp.where(sub8 == 0, tcnt, jnp.where(sub8 == 1, toff, 0.0))
    cols = []
    for v in (_lane_get(lane, local, i1), _lane_get(lane, local, i2)):
        vh = jnp.floor(v * (1.0 / 256.0))
        cols += [vh, v - 256.0 * vh]
    tr = _dot_nt(sel_ref[...], _lane_put(lane, cols).astype(BF16))
    subt = lax.broadcasted_iota(jnp.int32, tr.shape, 0)
    pos_ref[...] = jnp.where(subt == 0, tr[0:1, :] * 256.0 + tr[1:2, :],
                             jnp.where(subt == 1, tr[2:3, :] * 256.0 + tr[3:4, :], 0.0)).astype(jnp.int32)


def _dispatch_body(cnt_s, off_s, tot_s, h_ref, nw_ref, pos_s, runs_s, xs_ref, xn_buf, blk, zbuf, sem, *,
                   tm, tile_rows, ne, n_tiles):
    i = pl.program_id(0)
    last = pl.num_programs(0) - 1
    slots = MOE_TOP_K * tm * ROW_TILE

    def wait_runs(half):
        pltpu.make_async_copy(blk.at[half], xs_ref.at[pl.ds(0, slots)], sem.at[half]).wait()

    for half in range(2):
        rs = slice(half * tm, (half + 1) * tm)

        @pl.when(i > 0)
        def _drain_previous(half=half):
            wait_runs(half)

        _rows_to_tiles(xn_buf, _rms(h_ref[rs, :], nw_ref[...]))

        def place(t, carry, half=half):
            row = xn_buf[pl.ds(pl.multiple_of(t * ROW_TILE, ROW_TILE), ROW_TILE), :]
            for k in range(MOE_TOP_K):
                blk[half, pl.ds(pl.multiple_of(pos_s[k, half * tm + t], ROW_TILE), ROW_TILE), :] = row
            return carry

        lax.fori_loop(0, tm, place, 0, unroll=16)
        for e in range(ne):
            n = runs_s[half, 2, e] * ROW_TILE

            @pl.when(n > 0)
            def _send(e=e, n=n, half=half):
                src = pl.multiple_of(runs_s[half, 0, e] * ROW_TILE, ROW_TILE)
                dst = pl.multiple_of(runs_s[half, 1, e] * ROW_TILE, ROW_TILE)
                pltpu.make_async_copy(blk.at[half, pl.ds(src, n)], xs_ref.at[pl.ds(dst, n)], sem.at[half]
                                      ).start(priority=e % 2)

    @pl.when(i == last)
    def _zero_unused_rows():
        wait_runs(0)
        wait_runs(1)
        zbuf[...] = jnp.zeros_like(zbuf)

        def fill(row, nrows):
            at = pl.multiple_of(row * ROW_TILE, ROW_TILE)
            n = nrows * ROW_TILE
            c = pltpu.make_async_copy(zbuf.at[pl.ds(0, n)], xs_ref.at[pl.ds(at, n)], sem.at[2])
            c.start()
            c.wait()

        for e in range(ne):
            n_pad = (tile_rows - cnt_s[e] % tile_rows) % tile_rows

            @pl.when(n_pad > 0)
            def _fill(e=e, n_pad=n_pad):
                fill(off_s[e] + cnt_s[e], n_pad)

        def zero_tile(j, carry):
            fill(j * tile_rows, tile_rows)
            return carry

        lax.fori_loop(tot_s[0], n_tiles, zero_tile, 0)


def _expert_body(te_s, blk_s, tot_s, xs_hbm, wg_ref, wu_ref, wd_ref, ys_ref, xbuf, sem):
    i = pl.program_id(0)
    tot = tot_s[0]
    valid = i < tot
    rows = ys_ref.shape[0] // ROW_TILE
    nrow = rows * ROW_TILE

    def fetch(j, slot):
        at = pl.multiple_of(j * nrow, ROW_TILE)
        return pltpu.make_async_copy(xs_hbm.at[pl.ds(at, nrow)], xbuf.at[slot], sem.at[slot])

    for ahead in range(EXPERT_RING - 1):
        @pl.when((i == 0) & (ahead < tot))
        def _prime(ahead=ahead):
            fetch(ahead, ahead).start()

    for slot in range(EXPERT_RING):
        nxt = (slot + EXPERT_RING - 1) % EXPERT_RING

        @pl.when(valid & (i % EXPERT_RING == slot))
        def _run(slot=slot, nxt=nxt):
            @pl.when(i + EXPERT_RING - 1 < tot)
            def _prefetch():
                fetch(i + EXPERT_RING - 1, nxt).start()

            fetch(i, slot).wait()
            x = jnp.concatenate([xbuf[slot, pl.ds(c, rows, stride=ROW_TILE), :] for c in range(ROW_TILE)],
                                axis=1).astype(BF16)
            hid = jax.nn.silu(_dot(x, wg_ref[0, 0].astype(BF16))) * _dot(x, wu_ref[0, 0].astype(BF16))
            _rows_to_tiles(ys_ref, _dot(hid.astype(BF16), wd_ref[0, 0].astype(BF16)))

    @pl.when(jnp.logical_not(valid))
    def _unused_tile():
        ys_ref[...] = jnp.zeros_like(ys_ref)


def _combine_body(h_ref, info_ref, pos_s, runs_s, next_s, ys_ref, fw_ref, out_ref, blk, y1, y2, sem, *,
                  tm, ne, final):
    i = pl.program_id(0)
    slots = MOE_TOP_K * tm * ROW_TILE
    bufs = (y1, y2)

    def fetch(runs, half):
        for e in range(ne):
            n = runs[half, 2, e] * ROW_TILE

            @pl.when(n > 0)
            def _fetch(e=e, n=n):
                dst = pl.multiple_of(runs[half, 0, e] * ROW_TILE, ROW_TILE)
                src = pl.multiple_of(runs[half, 1, e] * ROW_TILE, ROW_TILE)
                pltpu.make_async_copy(ys_ref.at[pl.ds(src, n)], blk.at[half, pl.ds(dst, n)], sem.at[half]
                                      ).start(priority=e % 2)

    def finish(half):
        pltpu.make_async_copy(ys_ref.at[pl.ds(0, slots)], blk.at[half], sem.at[half]).wait()
        rs = slice(half * tm, (half + 1) * tm)

        def pick(t, carry):
            dst = pl.ds(pl.multiple_of(t * ROW_TILE, ROW_TILE), ROW_TILE)
            for k in range(MOE_TOP_K):
                at = pl.multiple_of(pos_s[k, half * tm + t], ROW_TILE)
                bufs[k][dst, :] = blk[half, pl.ds(at, ROW_TILE), :]
            return carry

        lax.fori_loop(0, tm, pick, 0, unroll=8)
        o = (h_ref[rs, :] + info_ref[rs, 0:1] * _tiles_to_rows(y1, tm) + info_ref[rs, 1:2] * _tiles_to_rows(y2, tm))
        if final:
            o = _rms(o, fw_ref[...])
        out_ref[rs, :] = o

    @pl.when(i == 0)
    def _first():
        fetch(runs_s, 0)

    fetch(runs_s, 1)
    finish(0)

    @pl.when(i < pl.num_programs(0) - 1)
    def _prefetch():
        fetch(next_s, 0)

    finish(1)


def _router_weights(w_group, b_group, w_router, b_router):
    d, ne = w_router.shape
    ng = w_group.shape[1]
    w = jnp.pad(jnp.concatenate([w_router, w_group], axis=1).astype(F32), ((0, 0), (0, ROUTE_LANES - ne - ng)))
    hi = w.astype(BF16)
    lo = (w - hi.astype(F32)).astype(BF16)
    bias = jnp.pad(jnp.concatenate([b_router, b_group]).astype(F32), (0, ROUTE_LANES - ne - ng)).reshape(1, -1)
    return jnp.concatenate([hi, lo], axis=1), bias


def _moe(h, nw, w_group, b_group, w_router, b_router, w_gate, w_up, w_down, layer, final_w):
    t, d = h.shape
    assert d == ROW_TILE * 128, "row-as-tile layout needs d_model == 1024"
    _, ne, _, ff = w_gate.shape
    tm = TM_MOE
    te_rows = TM_EXPERT
    nw2 = nw.reshape(1, d)
    wr3, bias = _router_weights(w_group, b_group, w_router, b_router)
    tri = jnp.asarray(np.tril(np.ones((tm, tm), np.float32), -1), BF16)
    nt = t // tm
    sel = jnp.asarray(np.eye(8, ROUTE_LANES, dtype=np.float32), BF16)
    info, pos, tile_info = pl.pallas_call(
        _route_body, grid=(nt,),
        in_specs=[pl.BlockSpec((tm, d), lambda i: (i, 0)),
                  pl.BlockSpec((1, d), lambda i: (0, 0)),
                  pl.BlockSpec((d, 2 * ROUTE_LANES), lambda i: (0, 0)),
                  pl.BlockSpec((1, ROUTE_LANES), lambda i: (0, 0)),
                  pl.BlockSpec((tm, tm), lambda i: (0, 0)),
                  pl.BlockSpec((8, ROUTE_LANES), lambda i: (0, 0))],
        out_specs=[pl.BlockSpec((tm, ROUTE_LANES), lambda i: (i, 0)), pl.BlockSpec((8, tm), lambda i: (0, i)),
                   pl.BlockSpec((8, ROUTE_LANES), lambda i: (i, 0))],
        out_shape=[jax.ShapeDtypeStruct((t, ROUTE_LANES), F32), jax.ShapeDtypeStruct((8, t), jnp.int32),
                   jax.ShapeDtypeStruct((nt * 8, ROUTE_LANES), F32)],
        compiler_params=_cparams(("parallel",)), name="moe_route",
    )(h, nw2, wr3, bias, tri, sel)

    tinfo = tile_info.reshape(nt, 8, ROUTE_LANES).astype(jnp.int32)
    tcnt = tinfo[:, 0, :]
    cnt_i = jnp.sum(tcnt, axis=0)[:ne]
    ntile = (cnt_i + te_rows - 1) // te_rows
    tile_end = jnp.cumsum(ntile)
    off_i = (tile_end - ntile) * te_rows
    n_tiles = (MOE_TOP_K * t) // te_rows + ne
    rows_total = n_tiles * te_rows
    ti = jnp.arange(n_tiles, dtype=jnp.int32)
    tot = tile_end[-1:]
    ti_c = jnp.minimum(ti, tot[0] - 1)
    tile_e = jnp.sum((ti_c[:, None] >= tile_end[None, :]).astype(jnp.int32), axis=1)
    gstart = jnp.pad(off_i, (0, ROUTE_LANES - ne))[None, :] + jnp.cumsum(tcnt, axis=0) - tcnt
    runs = jnp.pad(jnp.stack([tinfo[:, 1, :], gstart, tcnt], axis=1), ((0, 0), (0, 5), (0, 0)))

    xs = pl.pallas_call(
        functools.partial(_dispatch_body, tm=tm, tile_rows=te_rows, ne=ne, n_tiles=n_tiles),
        grid_spec=pltpu.PrefetchScalarGridSpec(
            num_scalar_prefetch=3, grid=(nt // 2,),
            in_specs=[pl.BlockSpec((2 * tm, d), lambda i, *_: (i, 0)),
                      pl.BlockSpec((1, d), lambda i, *_: (0, 0)),
                      pl.BlockSpec((8, 2 * tm), lambda i, *_: (0, i), memory_space=pltpu.SMEM),
                      pl.BlockSpec((2, 8, ROUTE_LANES), lambda i, *_: (i, 0, 0), memory_space=pltpu.SMEM)],
            out_specs=pl.BlockSpec(memory_space=pl.ANY),
            scratch_shapes=[pltpu.VMEM((tm * ROW_TILE, 128), F32),
                            pltpu.VMEM((2, MOE_TOP_K * tm * ROW_TILE, 128), F32),
                            pltpu.VMEM((te_rows * ROW_TILE, 128), F32), pltpu.SemaphoreType.DMA((3,))]),
        out_shape=jax.ShapeDtypeStruct((rows_total * ROW_TILE, 128), F32),
        compiler_params=_cparams(("arbitrary",)), name="moe_dispatch",
    )(cnt_i, off_i, tot, h, nw2, pos, runs)

    ys = pl.pallas_call(
        _expert_body,
        grid_spec=pltpu.PrefetchScalarGridSpec(
            num_scalar_prefetch=3, grid=(n_tiles,),
            in_specs=[pl.BlockSpec(memory_space=pl.ANY),
                      pl.BlockSpec((1, 1, d, ff), lambda i, e, b, v: (layer, e[i], 0, 0)),
                      pl.BlockSpec((1, 1, d, ff), lambda i, e, b, v: (layer, e[i], 0, 0)),
                      pl.BlockSpec((1, 1, ff, d), lambda i, e, b, v: (layer, e[i], 0, 0))],
            out_specs=pl.BlockSpec((te_rows * ROW_TILE, 128), lambda i, e, b, v: (i, 0)),
            scratch_shapes=[pltpu.VMEM((EXPERT_RING, te_rows * ROW_TILE, 128), F32),
                            pltpu.SemaphoreType.DMA((EXPERT_RING,))]),
        out_shape=jax.ShapeDtypeStruct((rows_total * ROW_TILE, 128), F32),
        compiler_params=_cparams(("arbitrary",)), name="moe_expert",
    )(tile_e, ti_c, tot, xs, w_gate, w_up, w_down)

    final = final_w is not None
    fw = (final_w if final else nw).reshape(1, d)
    return pl.pallas_call(
        functools.partial(_combine_body, tm=tm, ne=ne, final=final), grid=(nt // 2,),
        in_specs=[pl.BlockSpec((2 * tm, d), lambda i: (i, 0)),
                  pl.BlockSpec((2 * tm, ROUTE_LANES), lambda i: (i, 0)),
                  pl.BlockSpec((8, 2 * tm), lambda i: (0, i), memory_space=pltpu.SMEM),
                  pl.BlockSpec((2, 8, ROUTE_LANES), lambda i: (i, 0, 0), memory_space=pltpu.SMEM),
                  pl.BlockSpec((2, 8, ROUTE_LANES), lambda i: (jnp.minimum(i + 1, nt // 2 - 1), 0, 0),
                               memory_space=pltpu.SMEM),
                  pl.BlockSpec(memory_space=pl.ANY),
                  pl.BlockSpec((1, d), lambda i: (0, 0))],
        out_specs=pl.BlockSpec((2 * tm, d), lambda i: (i, 0)),
        out_shape=jax.ShapeDtypeStruct((t, d), F32),
        scratch_shapes=[pltpu.VMEM((2, MOE_TOP_K * tm * ROW_TILE, 128), F32), pltpu.VMEM((tm * ROW_TILE, 128), F32),
                        pltpu.VMEM((tm * ROW_TILE, 128), F32), pltpu.SemaphoreType.DMA((2,))],
        compiler_params=_cparams(("arbitrary",)), name="moe_combine",
    )(h, info, pos, runs, runs, ys, fw)


def _s5_weights(lam_re, lam_im, b_re, b_im, c_re, c_im, log_dt, nsteps):
    L = S5_CHUNK
    g, p = lam_re.shape
    ch = b_re.shape[-1]
    lam = lax.complex(lam_re.astype(F32), lam_im.astype(F32))
    dt = jnp.exp(log_dt.astype(F32))[:, None]
    lam_bar = jnp.exp(lam * dt)
    b_bar = ((lam_bar - 1.0) / lam)[..., None] * lax.complex(b_re.astype(F32), b_im.astype(F32))
    cmat = lax.complex(c_re.astype(F32), c_im.astype(F32))
    pows = [jnp.ones_like(lam_bar)]
    for _ in range(L):
        pows.append(pows[-1] * lam_bar)
    pw = jnp.stack(pows, axis=1)
    kern = jnp.real(jnp.einsum('gtop,gpi->gito', cmat[:, None, :, :] * pw[:, :L, None, :], b_bar))
    lead = (L - 1) * ch
    plen = lead + L * ch
    kpad = jnp.pad(kern.reshape(g, ch, L * ch).astype(BF16), ((0, 0), (0, 0), (lead, 0)))
    win = jnp.tile(kpad, (1, 1, L))[:, :, lead:lead + L * (plen - ch)].reshape(g, ch, L, plen - ch)[..., :L * ch]
    toep = win.transpose(0, 2, 1, 3).reshape(g, L * ch, L * ch)
    wst = pw[:, L - 1 - np.arange(L)][:, :, :, None] * b_bar[:, None, :, :]
    wst = wst.transpose(0, 1, 3, 2).reshape(g, L * ch, p)
    wst = jnp.concatenate([jnp.real(wst), jnp.imag(wst)], axis=-1)
    mo = cmat.transpose(0, 2, 1)[:, :, None, :] * pw[:, 1:L + 1].transpose(0, 2, 1)[:, :, :, None]
    mo = mo.reshape(g, p, L * ch)
    wout = jnp.concatenate([jnp.real(mo), -jnp.imag(mo)], axis=1)
    a = pw[:, L]
    ars, ais = [], []
    for _ in range(nsteps):
        ars.append(jnp.concatenate([jnp.real(a), jnp.real(a)], axis=-1))
        ais.append(jnp.concatenate([-jnp.imag(a), jnp.imag(a)], axis=-1))
        a = a * a
    wcat = jnp.concatenate([toep, wst.astype(BF16)], axis=-1)
    return wcat, wout.astype(BF16), jnp.stack(ars, axis=1), jnp.stack(ais, axis=1)


def _s5_body(u_ref, wcat_ref, wout_ref, ar_ref, ai_ref, y_ref, us_ref, ys_ref, *, nsteps):
    L = S5_CHUNK
    ch = S5_GROUP_CH
    gpc = 128 // ch
    ny = L * ch
    nc = u_ref.shape[0] // L
    for s in range(L):
        us_ref[s] = u_ref[pl.ds(s, nc, stride=L), :]
    lane = lax.broadcasted_iota(jnp.int32, (nc, 128), 1)
    ridx = lax.broadcasted_iota(jnp.int32, (nc, 128), 0)

    def shift(x, k):
        return jnp.where(ridx >= k, pltpu.roll(x, k, axis=0), 0.0)

    for gi in range(gpc):
        halves = []
        for hh in range(ny // 128):
            acc = None
            for s8 in range(gpc):
                rot = ((s8 - gi) * ch) % 128
                src = us_ref[hh * gpc + s8]
                if rot:
                    src = pltpu.roll(src, rot, axis=1)
                slot = (lane >= s8 * ch) & (lane < (s8 + 1) * ch)
                acc = jnp.where(slot, src, 0.0) if acc is None else jnp.where(slot, src, acc)
            halves.append(acc)
        ug = jnp.concatenate(halves, axis=1).astype(BF16)
        r = _dot(ug, wcat_ref[gi])
        y1 = r[:, :ny]
        z = r[:, ny:]
        w = shift(z, 1)
        wx = pltpu.roll(w, z.shape[1] // 2, axis=1)
        for k in range(nsteps):
            if (1 << k) >= nc:
                break
            sk = shift(w, 1 << k)
            sx = shift(wx, 1 << k)
            ar = ar_ref[gi, k:k + 1, :]
            ai = ai_ref[gi, k:k + 1, :]
            w, wx = w + sk * ar + sx * ai, wx + sx * ar - sk * ai
        yg = y1 + _dot(w.astype(BF16), wout_ref[gi])
        slot = (lane >= gi * ch) & (lane < (gi + 1) * ch)
        for t in range(L):
            src = yg[:, (t // gpc) * 128:(t // gpc + 1) * 128]
            rot = ((gi - t % gpc) * ch) % 128
            if rot:
                src = pltpu.roll(src, rot, axis=1)
            ys_ref[t] = jnp.where(slot, src, 0.0) if gi == 0 else jnp.where(slot, src, ys_ref[t])
    for t in range(L):
        y_ref[pl.ds(t, nc, stride=L), :] = ys_ref[t]


def _s5(main, col0, width, nb, seq, lam_re, lam_im, b_re, b_im, c_re, c_im, log_dt):
    t = main.shape[0]
    L = S5_CHUNK
    ch = S5_GROUP_CH
    g = width // ch
    nc = seq // L
    gpc = 128 // ch
    ncol = width // 128
    assert col0 % 128 == 0 and width % 128 == 0 and (L * ch) % 128 == 0
    nsteps = max(1, (nc - 1).bit_length())
    wcat, wout, ar, ai = _s5_weights(lam_re, lam_im, b_re, b_im, c_re, c_im, log_dt, nsteps)
    body = functools.partial(_s5_body, nsteps=nsteps)
    return pl.pallas_call(
        body, grid=(ncol, nb),
        in_specs=[pl.BlockSpec((seq, 128), lambda j, b: (b, col0 // 128 + j)),
                  pl.BlockSpec((gpc,) + wcat.shape[1:], lambda j, b: (j, 0, 0)),
                  pl.BlockSpec((gpc,) + wout.shape[1:], lambda j, b: (j, 0, 0)),
                  pl.BlockSpec((gpc,) + ar.shape[1:], lambda j, b: (j, 0, 0)),
                  pl.BlockSpec((gpc,) + ai.shape[1:], lambda j, b: (j, 0, 0))],
        out_specs=pl.BlockSpec((seq, 128), lambda j, b: (b, j)),
        out_shape=jax.ShapeDtypeStruct((t, width), F32),
        scratch_shapes=[pltpu.VMEM((L, nc, 128), F32), pltpu.VMEM((L, nc, 128), F32)],
        compiler_params=_cparams(("parallel", "parallel")), name="s5",
    )(main, wcat, wout, ar, ai)


def _hgrn_gmat():
    L = HGRN_CHUNK
    blocks = 2 + int(np.log2(L))
    gm = np.zeros((blocks * L, L), np.float32)
    for j in range(L):
        gm[j, :j + 1] = 1.0
        gm[L + j, j + 1:] = 1.0
    li, m = 2, L
    while m >= 2:
        half = m // 2
        for j in range(L):
            pos = j % m
            r = j - pos + half - 1
            if pos >= half:
                gm[li * L + j, r + 1:j + 1] = 1.0
            else:
                gm[li * L + j, j + 1:r + 1] = 1.0
        li += 1
        m //= 2
    return gm


def _hgrn_body(mh_ref, fg_ref, gm_ref, lb_ref, nw_ref, out_ref, st_ref, *, nb, nh, dh):
    L = HGRN_CHUNK
    w = nh * dh

    @pl.when(pl.program_id(0) == 0)
    def _init():
        st_ref[...] = jnp.zeros_like(st_ref)

    row = lax.broadcasted_iota(jnp.int32, (L, 2 * L), 0)
    col = lax.broadcasted_iota(jnp.int32, (L, 2 * L), 1) & (L - 1)
    rowd = lax.broadcasted_iota(jnp.int32, (L, 2 * dh), 0)
    laned = lax.broadcasted_iota(jnp.int32, (L, 2 * dh), 1)
    first = laned < dh
    eye = row == col

    def blockdiag(x):
        z = jnp.zeros_like(x)
        return jnp.concatenate([jnp.where(first, x, z), jnp.where(first, z, x)], axis=0)

    gm2 = gm_ref[...]
    lb = lb_ref[...]
    zst = jnp.zeros((dh, dh), BF16)
    for b in range(nb):
        fg = fg_ref[b]
        f = lb + (1.0 - lb) * jax.nn.sigmoid(fg)
        kk = (1.0 - lb) * jax.nn.sigmoid(-fg)
        lf = jnp.log(f)
        hi = lf.astype(BF16)
        mid = (lf - hi.astype(F32)).astype(BF16)
        p_all = jnp.exp(_dot(gm2, jnp.concatenate([hi, mid], axis=0)))
        for hp in range(nh // 2):
            i0 = b * nh + 2 * hp
            cs = slice(2 * hp * dh, (2 * hp + 2) * dh)
            qb = mh_ref[b, :, 2 * hp * dh:(2 * hp + 2) * dh]
            vb = mh_ref[b, :, w + 2 * hp * dh:w + (2 * hp + 2) * dh]
            og = mh_ref[b, :, 2 * w + 2 * hp * dh:2 * w + (2 * hp + 2) * dh].astype(F32)
            q = qb.astype(F32)
            k = kk[:, cs]
            pb = p_all[0:L, cs]
            pe = p_all[L:2 * L, cs]
            st0 = st_ref[i0]
            st1 = st_ref[i0 + 1]
            stbd = jnp.concatenate([jnp.concatenate([st0.astype(BF16), zst], axis=1),
                                    jnp.concatenate([zst, st1.astype(BF16)], axis=1)], axis=0)
            o = _dot_nt((q * pb).astype(BF16), stbd)
            attn = jnp.where(eye, _dot_nt(qb, blockdiag(k.astype(BF16))), 0.0)
            li, m = 2, L
            while m >= 2:
                pl_ = p_all[li * L:(li + 1) * L, cs]
                up = (rowd & (m - 1)) >= (m // 2)
                ql = jnp.where(up, q * pl_, 0.0).astype(BF16)
                kl = jnp.where(up, 0.0, k * pl_).astype(BF16)
                same = (row & ~(m - 1)) == (col & ~(m - 1))
                attn = attn + jnp.where(same, _dot_nt(ql, blockdiag(kl)), 0.0)
                li += 1
                m //= 2
            o = o + _dot(attn.astype(BF16), blockdiag(vb))
            kh = (k * pe).astype(BF16)
            for j in range(2):
                hs = slice(j * dh, (j + 1) * dh)
                gs = slice((2 * hp + j) * dh, (2 * hp + j + 1) * dh)
                st = st0 if j == 0 else st1
                st_ref[i0 + j] = st * pb[L - 1:L, hs] + _dot_tn(vb[:, hs], kh[:, hs])
                oj = o[:, hs]
                on = oj * lax.rsqrt(jnp.mean(oj * oj, axis=-1, keepdims=True) + RMS_EPS)
                out_ref[b, :, gs] = (on * nw_ref[:, gs] * jax.nn.silu(og[:, hs])).astype(out_ref.dtype)


def _hgrn(mh, mf, lower_bound, out_norm, nb, seq):
    t = mh.shape[0]
    nh = HGRN_HEADS
    w = out_norm.shape[0]
    dh = w // nh
    L = HGRN_CHUNK
    nc = seq // L
    gm = _hgrn_gmat()
    gm = jnp.asarray(np.concatenate([gm, gm], axis=1), BF16)
    body = functools.partial(_hgrn_body, nb=nb, nh=nh, dh=dh)
    out = pl.pallas_call(
        body, grid=(nc,),
        in_specs=[pl.BlockSpec((nb, L, 3 * w), lambda c: (0, c, 0)),
                  pl.BlockSpec((nb, L, w), lambda c: (0, c, 0)),
                  pl.BlockSpec(gm.shape, lambda c: (0, 0)),
                  pl.BlockSpec((1, w), lambda c: (0, 0)),
                  pl.BlockSpec((1, w), lambda c: (0, 0))],
        out_specs=pl.BlockSpec((nb, L, w), lambda c: (0, c, 0)),
        out_shape=jax.ShapeDtypeStruct((nb, seq, w), BF16),
        scratch_shapes=[pltpu.VMEM((nb * nh, dh, dh), F32)],
        compiler_params=_cparams(("arbitrary",)), name="hgrn",
    )(mh.reshape(nb, seq, mh.shape[1]), mf.reshape(nb, seq, mf.shape[1]), gm, lower_bound.reshape(1, w),
      out_norm.reshape(1, w))
    return out.reshape(t, w)


def _out_c_body(h_ref, ys_ref, u_ref, oh_ref, d_ref, wglu_ref, bglu_ref, w_ref, out_ref):
    ws = ys_ref.shape[1]
    z = jax.nn.gelu(ys_ref[...] + d_ref[...] * u_ref[...])
    gate = jax.nn.sigmoid(_dot(z.astype(BF16), wglu_ref[...]) + bglu_ref[...])
    out_ref[...] = (h_ref[...] + _dot((z * gate).astype(BF16), w_ref[:ws, :]) + _dot(oh_ref[...], w_ref[ws:, :]))


def _out_c(h, ys, main, oh, d_skip, w_glu, b_glu, w_out):
    t, d = h.shape
    ws = ys.shape[1]
    wh = oh.shape[1]
    tm = TM_PROJ
    ub = (main.shape[1] - ws) // ws
    return pl.pallas_call(
        _out_c_body, grid=(t // tm,),
        in_specs=[pl.BlockSpec((tm, d), lambda i: (i, 0)),
                  pl.BlockSpec((tm, ws), lambda i: (i, 0)),
                  pl.BlockSpec((tm, ws), lambda i: (i, ub)),
                  pl.BlockSpec((tm, wh), lambda i: (i, 0)),
                  pl.BlockSpec((1, ws), lambda i: (0, 0)),
                  pl.BlockSpec(w_glu.shape, lambda i: (0, 0)),
                  pl.BlockSpec((1, ws), lambda i: (0, 0)),
                  pl.BlockSpec(w_out.shape, lambda i: (0, 0))],
        out_specs=pl.BlockSpec((tm, d), lambda i: (i, 0)),
        out_shape=jax.ShapeDtypeStruct((t, d), F32),
        compiler_params=_cparams(("parallel",)), name="out_c",
    )(h, ys, main, oh, d_skip.reshape(1, ws), w_glu, b_glu.reshape(1, ws), w_out)


def kernel(x, norm_mix, norm_ffn, norm_final, ab_w_in, ab_gate_bias, ab_head_norm, ab_conv_w, ab_w_out, cd_w_in, s5_lambda_re, s5_lambda_im, s5_b_re, s5_b_im, s5_c_re, s5_c_im, s5_d, s5_log_dt, s5_w_glu, s5_b_glu, hgrn_lb, hgrn_out_norm, cd_w_out, moe_w_group, moe_b_group, moe_w_router, moe_b_router, moe_w_gate, moe_w_up, moe_w_down):
    nb, seq, d = x.shape
    depth = norm_mix.shape[0]
    h = x.reshape(nb * seq, d)
    for layer in range(depth):
        j = layer // 2
        if layer % 2 == 0:
            wm = ab_head_norm.shape[1]
            ng = ab_gate_bias.shape[1]
            w_in = ab_w_in[j]
            w_main = jnp.concatenate([w_in[:, :4 * wm], w_in[:, 4 * wm + ng:]], axis=1).astype(BF16)
            w_gates = w_in[:, 4 * wm:4 * wm + ng].astype(BF16)
            main, g, gt = _proj(h, norm_mix[layer], w_main, w_gates)
            hm = _mlstm(main, g, gt, ab_gate_bias[j], ab_head_norm[j], nb, seq)
            h = _out_a(h, hm, main, ab_conv_w[j], ab_w_out[j].astype(BF16), seq)
        else:
            ws = s5_d.shape[1]
            w_in = cd_w_in[j]
            wh = hgrn_out_norm.shape[1]
            cu, cq, cf, ci, co = (w_in[:, :ws], w_in[:, ws:ws + wh], w_in[:, ws + wh:ws + 2 * wh],
                                  w_in[:, ws + 2 * wh:ws + 3 * wh], w_in[:, ws + 3 * wh:])
            w_main = jnp.concatenate([cq, ci, co, cf, cu], axis=1).astype(BF16)
            mh, mf = _proj(h, norm_mix[layer], w_main, n_lo=3 * wh)
            sm = jax.nn.softmax(hgrn_lb.astype(F32), axis=0)
            lower_bound = jnp.cumsum(sm, axis=0)[layer] - sm[0]
            ys = _s5(mf, wh, ws, nb, seq, s5_lambda_re[j], s5_lambda_im[j], s5_b_re[j], s5_b_im[j],
                     s5_c_re[j], s5_c_im[j], s5_log_dt[j])
            oh = _hgrn(mh, mf, lower_bound, hgrn_out_norm[j], nb, seq)
            h = _out_c(h, ys, mf, oh, s5_d[j], s5_w_glu[j].astype(BF16), s5_b_glu[j], cd_w_out[j].astype(BF16))
        h = _moe(h, norm_ffn[layer], moe_w_group[layer], moe_b_group[layer], moe_w_router[layer], moe_b_router[layer],
                 moe_w_gate, moe_w_up, moe_w_down, layer,
                 norm_final if layer == depth - 1 else None)
    return h.reshape(nb, seq, d)
```

```python
import functools

import numpy as np
import jax
import jax.numpy as jnp
from jax import lax
from jax.experimental import pallas as pl
from jax.experimental.pallas import tpu as pltpu

F32 = jnp.float32
BF16 = jnp.bfloat16
RMS_EPS = 1e-6
MLSTM_CHUNK = 512
HGRN_CHUNK = 128
S5_CHUNK = 16
S5_GROUP_CH = 16
MLSTM_HEADS = 4
HGRN_HEADS = 4
MOE_GROUPS = 4
MOE_EPG = 8
ROUTE_LANES = 128
TM_PROJ = 1024
TM_MOE = 512
TM_EXPERT = 512
MOE_TOP_K = 2
EXPERT_RING = 4
VMEM_LIMIT = 56 * 1024 * 1024

_NT = (((1,), (1,)), ((), ()))
_TN = (((0,), (0,)), ((), ()))


def _cparams(sem):
    return pltpu.CompilerParams(dimension_semantics=sem, vmem_limit_bytes=VMEM_LIMIT)


def _rms(x, w):
    return x * lax.rsqrt(jnp.mean(x * x, axis=-1, keepdims=True) + RMS_EPS) * w


def _split3(x):
    hi = x.astype(BF16)
    r = x - hi.astype(F32)
    mid = r.astype(BF16)
    lo = (r - mid.astype(F32)).astype(BF16)
    return hi, mid, lo


def _dot(a, b):
    return jnp.dot(a, b, preferred_element_type=F32)


def _dot_nt(a, b):
    return lax.dot_general(a, b, _NT, preferred_element_type=F32)


def _dot_tn(a, b):
    return lax.dot_general(a, b, _TN, preferred_element_type=F32)


def _proj_gates_body(x_ref, nw_ref, w_ref, wg_ref, wgt_ref, main_ref, g_ref, gt_ref):
    xn = _rms(x_ref[...], nw_ref[...]).astype(BF16)
    main_ref[...] = _dot(xn, w_ref[...]).astype(main_ref.dtype)
    g_ref[...] = _dot(xn, wg_ref[...])[:, : g_ref.shape[1]]
    gt_ref[...] = _dot_nt(wgt_ref[...], xn)


def _proj_split_body(x_ref, nw_ref, w_ref, lo_ref, hi_ref):
    xn = _rms(x_ref[...], nw_ref[...]).astype(BF16)
    r = _dot(xn, w_ref[...])
    n_lo = lo_ref.shape[1]
    lo_ref[...] = r[:, :n_lo].astype(lo_ref.dtype)
    hi_ref[...] = r[:, n_lo:]


def _proj(h, nw, w_main, w_gates=None, n_lo=None):
    t, d = h.shape
    n = w_main.shape[1]
    tm = TM_PROJ
    x_spec = pl.BlockSpec((tm, d), lambda i: (i, 0))
    nw_spec = pl.BlockSpec((1, d), lambda i: (0, 0))
    w_spec = pl.BlockSpec((d, n), lambda i: (0, 0))
    if w_gates is None:
        return pl.pallas_call(
            _proj_split_body, grid=(t // tm,), in_specs=[x_spec, nw_spec, w_spec],
            out_specs=[pl.BlockSpec((tm, n_lo), lambda i: (i, 0)), pl.BlockSpec((tm, n - n_lo), lambda i: (i, 0))],
            out_shape=[jax.ShapeDtypeStruct((t, n_lo), BF16), jax.ShapeDtypeStruct((t, n - n_lo), F32)],
            compiler_params=_cparams(("parallel",)), name="proj",
        )(h, nw.reshape(1, d), w_main)
    ng = w_gates.shape[1]
    wg_pad = jnp.pad(w_gates, ((0, 0), (0, 128 - ng)))
    return pl.pallas_call(
        _proj_gates_body, grid=(t // tm,),
        in_specs=[x_spec, nw_spec, w_spec, pl.BlockSpec((d, 128), lambda i: (0, 0)),
                  pl.BlockSpec((ng, d), lambda i: (0, 0))],
        out_specs=[pl.BlockSpec((tm, n), lambda i: (i, 0)), pl.BlockSpec((tm, ng), lambda i: (i, 0)),
                   pl.BlockSpec((ng, tm), lambda i: (0, i))],
        out_shape=[jax.ShapeDtypeStruct((t, n), BF16), jax.ShapeDtypeStruct((t, ng), F32),
                   jax.ShapeDtypeStruct((ng, t), F32)],
        compiler_params=_cparams(("parallel",)), name="proj_gates",
    )(h, nw.reshape(1, d), w_main, wg_pad, w_gates.T)


def _mlstm_body(main_ref, g_ref, gt_ref, br_ref, bc_ref, hn_ref, out_ref, c_ref, n_ref, m_ref, *, nb, nh, dh):
    L = MLSTM_CHUNK
    w = nh * dh

    @pl.when(pl.program_id(0) == 0)
    def _init():
        c_ref[...] = jnp.zeros_like(c_ref)
        n_ref[...] = jnp.zeros_like(n_ref)
        m_ref[...] = jnp.full_like(m_ref, -1e30)

    row = lax.broadcasted_iota(jnp.int32, (L, L), 0)
    col = lax.broadcasted_iota(jnp.int32, (L, L), 1)
    causal = col <= row
    tril = causal.astype(BF16)
    triu = (row <= col).astype(BF16)
    scale = dh ** -0.5
    for b in range(nb):
        g = g_ref[b] + br_ref[...]
        gt = gt_ref[b, 0] + bc_ref[...]
        i_c = g[:, :nh]
        i_r = gt[:nh, :]
        lfc = _split3(jax.nn.log_sigmoid(g[:, nh:]))
        lfr = _split3(jax.nn.log_sigmoid(gt[nh:, :]))
        bc_all = _dot(tril, lfc[0]) + _dot(tril, lfc[1]) + _dot(tril, lfc[2])
        br_all = _dot(lfr[0], triu) + _dot(lfr[1], triu) + _dot(lfr[2], triu)
        for h in range(nh):
            idx = b * nh + h
            qb = main_ref[b, :, h * dh:(h + 1) * dh]
            vb = main_ref[b, :, 2 * w + h * dh:2 * w + (h + 1) * dh]
            q = qb.astype(F32)
            k = main_ref[b, :, w + h * dh:w + (h + 1) * dh].astype(F32) * scale
            v = vb.astype(F32)
            o = main_ref[b, :, 3 * w + h * dh:3 * w + (h + 1) * dh].astype(F32)
            bc = bc_all[:, h:h + 1]
            br = br_all[h:h + 1, :]
            ir = i_r[h:h + 1, :]
            ic = i_c[:, h:h + 1]
            m_prev = m_ref[idx]
            c_prev = c_ref[idx]
            n_prev = n_ref[idx]
            logw = jnp.where(causal, bc - br + ir, -jnp.inf)
            inter = bc + m_prev
            m_row = jnp.maximum(jnp.max(logw, axis=-1, keepdims=True), inter)
            kb = k.astype(BF16)
            s = _dot_nt(qb, kb) * jnp.exp(logw - m_row)
            isc = jnp.exp(inter - m_row)
            num = _dot(s.astype(BF16), vb) + isc * _dot_nt(qb, c_prev.astype(BF16))
            den = jnp.sum(s, axis=-1, keepdims=True) + isc * jnp.sum(q * n_prev, axis=-1, keepdims=True)
            hout = num / jnp.maximum(jnp.abs(den), jnp.exp(-m_row))
            b_end = bc[L - 1:L, :]
            logg = b_end - bc + ic
            m_new = jnp.maximum(b_end + m_prev, jnp.max(logg, axis=0, keepdims=True))
            wk = jnp.exp(logg - m_new)
            decay = jnp.exp(b_end + m_prev - m_new)
            c_ref[idx] = decay * c_prev + _dot_tn((v * wk).astype(BF16), kb)
            n_ref[idx] = decay * n_prev + jnp.sum(wk * k, axis=0, keepdims=True)
            m_ref[idx] = m_new
            hn = hout * lax.rsqrt(jnp.mean(hout * hout, axis=-1, keepdims=True) + RMS_EPS)
            out_ref[b, :, h * dh:(h + 1) * dh] = (hn * hn_ref[:, h * dh:(h + 1) * dh] * jax.nn.sigmoid(o)
                                                  ).astype(out_ref.dtype)


def _mlstm(main, g, gt, gate_bias, head_norm, nb, seq):
    t, n = main.shape
    nh = MLSTM_HEADS
    w = head_norm.shape[0]
    dh = w // nh
    L = MLSTM_CHUNK
    nc = seq // L
    main3 = main.reshape(nb, seq, n)
    g3 = g.reshape(nb, seq, 2 * nh)
    gt4 = gt.reshape(2 * nh, nb, nc, L).transpose(1, 2, 0, 3)
    body = functools.partial(_mlstm_body, nb=nb, nh=nh, dh=dh)
    out = pl.pallas_call(
        body, grid=(nc,),
        in_specs=[pl.BlockSpec((nb, L, 4 * w), lambda c: (0, c, 0)),
                  pl.BlockSpec((nb, L, 2 * nh), lambda c: (0, c, 0)),
                  pl.BlockSpec((nb, 1, 2 * nh, L), lambda c: (0, c, 0, 0)),
                  pl.BlockSpec((1, 2 * nh), lambda c: (0, 0)),
                  pl.BlockSpec((2 * nh, 1), lambda c: (0, 0)),
                  pl.BlockSpec((1, w), lambda c: (0, 0))],
        out_specs=pl.BlockSpec((nb, L, w), lambda c: (0, c, 0)),
        out_shape=jax.ShapeDtypeStruct((nb, seq, w), BF16),
        scratch_shapes=[pltpu.VMEM((nb * nh, dh, dh), F32), pltpu.VMEM((nb * nh, 1, dh), F32),
                        pltpu.VMEM((nb * nh, 1, 1), F32)],
        compiler_params=_cparams(("arbitrary",)), name="mlstm",
    )(main3, g3, gt4, gate_bias.reshape(1, 2 * nh), gate_bias.reshape(2 * nh, 1), head_norm.reshape(1, w))
    return out.reshape(t, w)


def _out_a_body(h_ref, hm_ref, gb_ref, gc_ref, xin_ref, pgc_ref, pxin_ref, cw_ref, w_ref, out_ref, *, tm, seq):
    i = pl.program_id(0)
    wm = hm_ref.shape[1]
    p = gc_ref[...].astype(F32) * xin_ref[...].astype(F32)
    first = (i * tm) % seq == 0
    hr = pgc_ref.shape[0]
    pp = jnp.where(first, 0.0, pgc_ref[...].astype(F32) * pxin_ref[...].astype(F32))
    rowi = lax.broadcasted_iota(jnp.int32, p.shape, 0)
    p1 = jnp.where(rowi == 0, pp[hr - 1:hr, :], pltpu.roll(p, 1, axis=0))
    p2 = jnp.where(rowi == 0, pp[hr - 2:hr - 1, :], jnp.where(rowi == 1, pp[hr - 1:hr, :], pltpu.roll(p, 2, axis=0)))
    yc = gb_ref[...].astype(F32) * (cw_ref[0:1, :] * p2 + cw_ref[1:2, :] * p1 + cw_ref[2:3, :] * p)
    out_ref[...] = (h_ref[...] + _dot(hm_ref[...], w_ref[:wm, :]) + _dot(yc.astype(BF16), w_ref[wm:, :]))


def _out_a(h, hm, main, conv_w, w_out, seq):
    t, d = h.shape
    wm = hm.shape[1]
    wc = conv_w.shape[1]
    tm = TM_PROJ
    cb = (4 * wm) // wc
    assert seq % tm == 0, "a token tile must not straddle two sequences (causal conv halo)"
    halo = 16
    rb = tm // halo
    prev = lambda i: jnp.maximum(i * rb - 1, 0)
    body = functools.partial(_out_a_body, tm=tm, seq=seq)
    return pl.pallas_call(
        body, grid=(t // tm,),
        in_specs=[pl.BlockSpec((tm, d), lambda i: (i, 0)),
                  pl.BlockSpec((tm, wm), lambda i: (i, 0)),
                  pl.BlockSpec((tm, wc), lambda i: (i, cb)),
                  pl.BlockSpec((tm, wc), lambda i: (i, cb + 1)),
                  pl.BlockSpec((tm, wc), lambda i: (i, cb + 2)),
                  pl.BlockSpec((halo, wc), lambda i: (prev(i), cb + 1)),
                  pl.BlockSpec((halo, wc), lambda i: (prev(i), cb + 2)),
                  pl.BlockSpec(conv_w.shape, lambda i: (0, 0)),
                  pl.BlockSpec(w_out.shape, lambda i: (0, 0))],
        out_specs=pl.BlockSpec((tm, d), lambda i: (i, 0)),
        out_shape=jax.ShapeDtypeStruct((t, d), F32),
        compiler_params=_cparams(("parallel",)), name="out_a",
    )(h, hm, main, main, main, main, main, conv_w, w_out)


def _route(logits, lane):
    ne = MOE_GROUPS * MOE_EPG
    big = 1e9
    gl = jnp.where((lane >= ne) & (lane < ne + MOE_GROUPS), logits, -jnp.inf)
    gmax = jnp.max(gl, axis=-1, keepdims=True)
    gidx = jnp.min(jnp.where(gl == gmax, lane - ne, big), axis=-1, keepdims=True)
    gval = 1.0 / jnp.sum(jnp.exp(gl - gmax), axis=-1, keepdims=True)
    lo = gidx * MOE_EPG
    sel = jnp.where((lane >= lo) & (lane < lo + MOE_EPG), logits, -jnp.inf)
    l1 = jnp.max(sel, axis=-1, keepdims=True)
    i1 = jnp.min(jnp.where(sel == l1, lane, big), axis=-1, keepdims=True)
    sel2 = jnp.where(lane == i1, -jnp.inf, sel)
    l2 = jnp.max(sel2, axis=-1, keepdims=True)
    i2 = jnp.min(jnp.where(sel2 == l2, lane, big), axis=-1, keepdims=True)
    r = jnp.exp(l2 - l1)
    w1 = gval / (1.0 + r)
    return i1, i2, w1, w1 * r


ROW_TILE = 8


def _rows_to_tiles(ref, x):
    n = x.shape[0]
    for c in range(ROW_TILE):
        ref[pl.ds(c, n, stride=ROW_TILE), :] = x[:, c * 128:(c + 1) * 128]


def _tiles_to_rows(ref, n):
    return jnp.concatenate([ref[pl.ds(c, n, stride=ROW_TILE), :] for c in range(ROW_TILE)], axis=1)


def _lane_put(lane, cols):
    out = jnp.where(lane == 0.0, cols[0], 0.0)
    for k in range(1, len(cols)):
        out = out + jnp.where(lane == float(k), cols[k], 0.0)
    return out


def _lane_get(lane, x, idx_col):
    return jnp.sum(jnp.where(lane == idx_col, x, 0.0), axis=-1, keepdims=True)


def _route_body(h_ref, nw_ref, wr_ref, br_ref, tri_ref, sel_ref, info_ref, pos_ref, tile_ref):
    xn = _rms(h_ref[...], nw_ref[...])
    hi = xn.astype(BF16)
    lo = (xn - hi.astype(F32)).astype(BF16)
    tm = hi.shape[0]
    r = _dot(jnp.concatenate([hi, lo], axis=0), wr_ref[...])
    logits = r[:tm, :ROUTE_LANES] + r[tm:, :ROUTE_LANES] + r[:tm, ROUTE_LANES:] + br_ref[...]
    lane = lax.broadcasted_iota(jnp.int32, logits.shape, 1).astype(F32)
    i1, i2, w1, w2 = _route(logits, lane)
    ind = jnp.where((lane == i1) | (lane == i2), 1.0, 0.0)
    ahead = _dot(tri_ref[...], ind.astype(BF16))
    tcnt = jnp.broadcast_to(jnp.sum(ind, axis=0, keepdims=True), tile_ref.shape)
    lane8 = lax.broadcasted_iota(jnp.int32, tile_ref.shape, 1)
    sub8 = lax.broadcasted_iota(jnp.int32, tile_ref.shape, 0)
    incl = tcnt
    sh = 1
    while sh < tile_ref.shape[1]:
        incl = incl + jnp.where(lane8 >= sh, pltpu.roll(incl, sh, axis=1), 0.0)
        sh *= 2
    toff = incl - tcnt
    local = (ahead + toff[0:1, :]) * ROW_TILE
    info_ref[...] = _lane_put(lane, [w1, w2])
    tile_ref[...] = jnp.where(sub8 == 0, tcnt, jnp.where(sub8 == 1, toff, 0.0))
    cols = []
    for v in (_lane_get(lane, local, i1), _lane_get(lane, local, i2)):
        vh = jnp.floor(v * (1.0 / 256.0))
        cols += [vh, v - 256.0 * vh]
    tr = _dot_nt(sel_ref[...], _lane_put(lane, cols).astype(BF16))
    subt = lax.broadcasted_iota(jnp.int32, tr.shape, 0)
    pos_ref[...] = jnp.where(subt == 0, tr[0:1, :] * 256.0 + tr[1:2, :],
                             jnp.where(subt == 1, tr[2:3, :] * 256.0 + tr[3:4, :], 0.0)).astype(jnp.int32)


def _dispatch_body(cnt_s, off_s, tot_s, h_ref, nw_ref, pos_s, runs_s, xs_ref, xn_buf, blk, zbuf, sem, *,
                   tm, tile_rows, ne, n_tiles):
    i = pl.program_id(0)
    last = pl.num_programs(0) - 1
    slots = MOE_TOP_K * tm * ROW_TILE

    def wait_runs(half):
        pltpu.make_async_copy(blk.at[half], xs_ref.at[pl.ds(0, slots)], sem.at[half]).wait()

    for half in range(2):
        rs = slice(half * tm, (half + 1) * tm)

        @pl.when(i > 0)
        def _drain_previous(half=half):
            wait_runs(half)

        _rows_to_tiles(xn_buf, _rms(h_ref[rs, :], nw_ref[...]))

        def place(t, carry, half=half):
            row = xn_buf[pl.ds(pl.multiple_of(t * ROW_TILE, ROW_TILE), ROW_TILE), :]
            for k in range(MOE_TOP_K):
                blk[half, pl.ds(pl.multiple_of(pos_s[k, half * tm + t], ROW_TILE), ROW_TILE), :] = row
            return carry

        lax.fori_loop(0, tm, place, 0, unroll=16)
        for e in range(ne):
            n = runs_s[half, 2, e] * ROW_TILE

            @pl.when(n > 0)
            def _send(e=e, n=n, half=half):
                src = pl.multiple_of(runs_s[half, 0, e] * ROW_TILE, ROW_TILE)
                dst = pl.multiple_of(runs_s[half, 1, e] * ROW_TILE, ROW_TILE)
                pltpu.make_async_copy(blk.at[half, pl.ds(src, n)], xs_ref.at[pl.ds(dst, n)], sem.at[half]
                                      ).start(priority=e % 2)

    @pl.when(i == last)
    def _zero_unused_rows():
        wait_runs(0)
        wait_runs(1)
        zbuf[...] = jnp.zeros_like(zbuf)

        def fill(row, nrows):
            at = pl.multiple_of(row * ROW_TILE, ROW_TILE)
            n = nrows * ROW_TILE
            c = pltpu.make_async_copy(zbuf.at[pl.ds(0, n)], xs_ref.at[pl.ds(at, n)], sem.at[2])
            c.start()
            c.wait()

        for e in range(ne):
            n_pad = (tile_rows - cnt_s[e] % tile_rows) % tile_rows

            @pl.when(n_pad > 0)
            def _fill(e=e, n_pad=n_pad):
                fill(off_s[e] + cnt_s[e], n_pad)

        def zero_tile(j, carry):
            fill(j * tile_rows, tile_rows)
            return carry

        lax.fori_loop(tot_s[0], n_tiles, zero_tile, 0)


def _expert_body(te_s, blk_s, tot_s, xs_hbm, wg_ref, wu_ref, wd_ref, ys_ref, xbuf, sem):
    i = pl.program_id(0)
    tot = tot_s[0]
    valid = i < tot
    rows = ys_ref.shape[0] // ROW_TILE
    nrow = rows * ROW_TILE

    def fetch(j, slot):
        at = pl.multiple_of(j * nrow, ROW_TILE)
        return pltpu.make_async_copy(xs_hbm.at[pl.ds(at, nrow)], xbuf.at[slot], sem.at[slot])

    for ahead in range(EXPERT_RING - 1):
        @pl.when((i == 0) & (ahead < tot))
        def _prime(ahead=ahead):
            fetch(ahead, ahead).start()

    for slot in range(EXPERT_RING):
        nxt = (slot + EXPERT_RING - 1) % EXPERT_RING

        @pl.when(valid & (i % EXPERT_RING == slot))
        def _run(slot=slot, nxt=nxt):
            @pl.when(i + EXPERT_RING - 1 < tot)
            def _prefetch():
                fetch(i + EXPERT_RING - 1, nxt).start()

            fetch(i, slot).wait()
            x = jnp.concatenate([xbuf[slot, pl.ds(c, rows, stride=ROW_TILE), :] for c in range(ROW_TILE)],
                                axis=1).astype(BF16)
            hid = jax.nn.silu(_dot(x, wg_ref[0, 0].astype(BF16))) * _dot(x, wu_ref[0, 0].astype(BF16))
            _rows_to_tiles(ys_ref, _dot(hid.astype(BF16), wd_ref[0, 0].astype(BF16)))

    @pl.when(jnp.logical_not(valid))
    def _unused_tile():
        ys_ref[...] = jnp.zeros_like(ys_ref)


def _combine_body(h_ref, info_ref, pos_s, runs_s, next_s, ys_ref, fw_ref, out_ref, blk, y1, y2, sem, *,
                  tm, ne, final):
    i = pl.program_id(0)
    slots = MOE_TOP_K * tm * ROW_TILE
    bufs = (y1, y2)

    def fetch(runs, half):
        for e in range(ne):
            n = runs[half, 2, e] * ROW_TILE

            @pl.when(n > 0)
            def _fetch(e=e, n=n):
                dst = pl.multiple_of(runs[half, 0, e] * ROW_TILE, ROW_TILE)
                src = pl.multiple_of(runs[half, 1, e] * ROW_TILE, ROW_TILE)
                pltpu.make_async_copy(ys_ref.at[pl.ds(src, n)], blk.at[half, pl.ds(dst, n)], sem.at[half]
                                      ).start(priority=e % 2)

    def finish(half):
        pltpu.make_async_copy(ys_ref.at[pl.ds(0, slots)], blk.at[half], sem.at[half]).wait()
        rs = slice(half * tm, (half + 1) * tm)

        def pick(t, carry):
            dst = pl.ds(pl.multiple_of(t * ROW_TILE, ROW_TILE), ROW_TILE)
            for k in range(MOE_TOP_K):
                at = pl.multiple_of(pos_s[k, half * tm + t], ROW_TILE)
                bufs[k][dst, :] = blk[half, pl.ds(at, ROW_TILE), :]
            return carry

        lax.fori_loop(0, tm, pick, 0, unroll=8)
        o = (h_ref[rs, :] + info_ref[rs, 0:1] * _tiles_to_rows(y1, tm) + info_ref[rs, 1:2] * _tiles_to_rows(y2, tm))
        if final:
            o = _rms(o, fw_ref[...])
        out_ref[rs, :] = o

    @pl.when(i == 0)
    def _first():
        fetch(runs_s, 0)

    fetch(runs_s, 1)
    finish(0)

    @pl.when(i < pl.num_programs(0) - 1)
    def _prefetch():
        fetch(next_s, 0)

    finish(1)


def _router_weights(w_group, b_group, w_router, b_router):
    d, ne = w_router.shape
    ng = w_group.shape[1]
    w = jnp.pad(jnp.concatenate([w_router, w_group], axis=1).astype(F32), ((0, 0), (0, ROUTE_LANES - ne - ng)))
    hi = w.astype(BF16)
    lo = (w - hi.astype(F32)).astype(BF16)
    bias = jnp.pad(jnp.concatenate([b_router, b_group]).astype(F32), (0, ROUTE_LANES - ne - ng)).reshape(1, -1)
    return jnp.concatenate([hi, lo], axis=1), bias


def _moe(h, nw, w_group, b_group, w_router, b_router, w_gate, w_up, w_down, layer, final_w):
    t, d = h.shape
    assert d == ROW_TILE * 128, "row-as-tile layout needs d_model == 1024"
    _, ne, _, ff = w_gate.shape
    tm = TM_MOE
    te_rows = TM_EXPERT
    nw2 = nw.reshape(1, d)
    wr3, bias = _router_weights(w_group, b_group, w_router, b_router)
    tri = jnp.asarray(np.tril(np.ones((tm, tm), np.float32), -1), BF16)
    nt = t // tm
    sel = jnp.asarray(np.eye(8, ROUTE_LANES, dtype=np.float32), BF16)
    info, pos, tile_info = pl.pallas_call(
        _route_body, grid=(nt,),
        in_specs=[pl.BlockSpec((tm, d), lambda i: (i, 0)),
                  pl.BlockSpec((1, d), lambda i: (0, 0)),
                  pl.BlockSpec((d, 2 * ROUTE_LANES), lambda i: (0, 0)),
                  pl.BlockSpec((1, ROUTE_LANES), lambda i: (0, 0)),
                  pl.BlockSpec((tm, tm), lambda i: (0, 0)),
                  pl.BlockSpec((8, ROUTE_LANES), lambda i: (0, 0))],
        out_specs=[pl.BlockSpec((tm, ROUTE_LANES), lambda i: (i, 0)), pl.BlockSpec((8, tm), lambda i: (0, i)),
                   pl.BlockSpec((8, ROUTE_LANES), lambda i: (i, 0))],
        out_shape=[jax.ShapeDtypeStruct((t, ROUTE_LANES), F32), jax.ShapeDtypeStruct((8, t), jnp.int32),
                   jax.ShapeDtypeStruct((nt * 8, ROUTE_LANES), F32)],
        compiler_params=_cparams(("parallel",)), name="moe_route",
    )(h, nw2, wr3, bias, tri, sel)

    tinfo = tile_info.reshape(nt, 8, ROUTE_LANES).astype(jnp.int32)
    tcnt = tinfo[:, 0, :]
    cnt_i = jnp.sum(tcnt, axis=0)[:ne]
    ntile = (cnt_i + te_rows - 1) // te_rows
    tile_end = jnp.cumsum(ntile)
    off_i = (tile_end - ntile) * te_rows
    n_tiles = (MOE_TOP_K * t) // te_rows + ne
    rows_total = n_tiles * te_rows
    ti = jnp.arange(n_tiles, dtype=jnp.int32)
    tot = tile_end[-1:]
    ti_c = jnp.minimum(ti, tot[0] - 1)
    tile_e = jnp.sum((ti_c[:, None] >= tile_end[None, :]).astype(jnp.int32), axis=1)
    gstart = jnp.pad(off_i, (0, ROUTE_LANES - ne))[None, :] + jnp.cumsum(tcnt, axis=0) - tcnt
    runs = jnp.pad(jnp.stack([tinfo[:, 1, :], gstart, tcnt], axis=1), ((0, 0), (0, 5), (0, 0)))

    xs = pl.pallas_call(
        functools.partial(_dispatch_body, tm=tm, tile_rows=te_rows, ne=ne, n_tiles=n_tiles),
        grid_spec=pltpu.PrefetchScalarGridSpec(
            num_scalar_prefetch=3, grid=(nt // 2,),
            in_specs=[pl.BlockSpec((2 * tm, d), lambda i, *_: (i, 0)),
                      pl.BlockSpec((1, d), lambda i, *_: (0, 0)),
                      pl.BlockSpec((8, 2 * tm), lambda i, *_: (0, i), memory_space=pltpu.SMEM),
                      pl.BlockSpec((2, 8, ROUTE_LANES), lambda i, *_: (i, 0, 0), memory_space=pltpu.SMEM)],
            out_specs=pl.BlockSpec(memory_space=pl.ANY),
            scratch_shapes=[pltpu.VMEM((tm * ROW_TILE, 128), F32),
                            pltpu.VMEM((2, MOE_TOP_K * tm * ROW_TILE, 128), F32),
                            pltpu.VMEM((te_rows * ROW_TILE, 128), F32), pltpu.SemaphoreType.DMA((3,))]),
        out_shape=jax.ShapeDtypeStruct((rows_total * ROW_TILE, 128), F32),
        compiler_params=_cparams(("arbitrary",)), name="moe_dispatch",
    )(cnt_i, off_i, tot, h, nw2, pos, runs)

    ys = pl.pallas_call(
        _expert_body,
        grid_spec=pltpu.PrefetchScalarGridSpec(
            num_scalar_prefetch=3, grid=(n_tiles,),
            in_specs=[pl.BlockSpec(memory_space=pl.ANY),
                      pl.BlockSpec((1, 1, d, ff), lambda i, e, b, v: (layer, e[i], 0, 0)),
                      pl.BlockSpec((1, 1, d, ff), lambda i, e, b, v: (layer, e[i], 0, 0)),
                      pl.BlockSpec((1, 1, ff, d), lambda i, e, b, v: (layer, e[i], 0, 0))],
            out_specs=pl.BlockSpec((te_rows * ROW_TILE, 128), lambda i, e, b, v: (i, 0)),
            scratch_shapes=[pltpu.VMEM((EXPERT_RING, te_rows * ROW_TILE, 128), F32),
                            pltpu.SemaphoreType.DMA((EXPERT_RING,))]),
        out_shape=jax.ShapeDtypeStruct((rows_total * ROW_TILE, 128), F32),
        compiler_params=_cparams(("arbitrary",)), name="moe_expert",
    )(tile_e, ti_c, tot, xs, w_gate, w_up, w_down)

    final = final_w is not None
    fw = (final_w if final else nw).reshape(1, d)
    return pl.pallas_call(
        functools.partial(_combine_body, tm=tm, ne=ne, final=final), grid=(nt // 2,),
        in_specs=[pl.BlockSpec((2 * tm, d), lambda i: (i, 0)),
                  pl.BlockSpec((2 * tm, ROUTE_LANES), lambda i: (i, 0)),
                  pl.BlockSpec((8, 2 * tm), lambda i: (0, i), memory_space=pltpu.SMEM),
                  pl.BlockSpec((2, 8, ROUTE_LANES), lambda i: (i, 0, 0), memory_space=pltpu.SMEM),
                  pl.BlockSpec((2, 8, ROUTE_LANES), lambda i: (jnp.minimum(i + 1, nt // 2 - 1), 0, 0),
                               memory_space=pltpu.SMEM),
                  pl.BlockSpec(memory_space=pl.ANY),
                  pl.BlockSpec((1, d), lambda i: (0, 0))],
        out_specs=pl.BlockSpec((2 * tm, d), lambda i: (i, 0)),
        out_shape=jax.ShapeDtypeStruct((t, d), F32),
        scratch_shapes=[pltpu.VMEM((2, MOE_TOP_K * tm * ROW_TILE, 128), F32), pltpu.VMEM((tm * ROW_TILE, 128), F32),
                        pltpu.VMEM((tm * ROW_TILE, 128), F32), pltpu.SemaphoreType.DMA((2,))],
        compiler_params=_cparams(("arbitrary",)), name="moe_combine",
    )(h, info, pos, runs, runs, ys, fw)


def _s5_weights(lam_re, lam_im, b_re, b_im, c_re, c_im, log_dt, nsteps):
    L = S5_CHUNK
    g, p = lam_re.shape
    ch = b_re.shape[-1]
    lam = lax.complex(lam_re.astype(F32), lam_im.astype(F32))
    dt = jnp.exp(log_dt.astype(F32))[:, None]
    lam_bar = jnp.exp(lam * dt)
    b_bar = ((lam_bar - 1.0) / lam)[..., None] * lax.complex(b_re.astype(F32), b_im.astype(F32))
    cmat = lax.complex(c_re.astype(F32), c_im.astype(F32))
    pows = [jnp.ones_like(lam_bar)]
    for _ in range(L):
        pows.append(pows[-1] * lam_bar)
    pw = jnp.stack(pows, axis=1)
    kern = jnp.real(jnp.einsum('gtop,gpi->gito', cmat[:, None, :, :] * pw[:, :L, None, :], b_bar))
    lead = (L - 1) * ch
    plen = lead + L * ch
    kpad = jnp.pad(kern.reshape(g, ch, L * ch).astype(BF16), ((0, 0), (0, 0), (lead, 0)))
    win = jnp.tile(kpad, (1, 1, L))[:, :, lead:lead + L * (plen - ch)].reshape(g, ch, L, plen - ch)[..., :L * ch]
    toep = win.transpose(0, 2, 1, 3).reshape(g, L * ch, L * ch)
    wst = pw[:, L - 1 - np.arange(L)][:, :, :, None] * b_bar[:, None, :, :]
    wst = wst.transpose(0, 1, 3, 2).reshape(g, L * ch, p)
    wst = jnp.concatenate([jnp.real(wst), jnp.imag(wst)], axis=-1)
    mo = cmat.transpose(0, 2, 1)[:, :, None, :] * pw[:, 1:L + 1].transpose(0, 2, 1)[:, :, :, None]
    mo = mo.reshape(g, p, L * ch)
    wout = jnp.concatenate([jnp.real(mo), -jnp.imag(mo)], axis=1)
    a = pw[:, L]
    ars, ais = [], []
    for _ in range(nsteps):
        ars.append(jnp.concatenate([jnp.real(a), jnp.real(a)], axis=-1))
        ais.append(jnp.concatenate([-jnp.imag(a), jnp.imag(a)], axis=-1))
        a = a * a
    wcat = jnp.concatenate([toep, wst.astype(BF16)], axis=-1)
    return wcat, wout.astype(BF16), jnp.stack(ars, axis=1), jnp.stack(ais, axis=1)


def _s5_body(u_ref, wcat_ref, wout_ref, ar_ref, ai_ref, y_ref, us_ref, ys_ref, *, nsteps):
    L = S5_CHUNK
    ch = S5_GROUP_CH
    gpc = 128 // ch
    ny = L * ch
    nc = u_ref.shape[0] // L
    for s in range(L):
        us_ref[s] = u_ref[pl.ds(s, nc, stride=L), :]
    lane = lax.broadcasted_iota(jnp.int32, (nc, 128), 1)
    ridx = lax.broadcasted_iota(jnp.int32, (nc, 128), 0)

    def shift(x, k):
        return jnp.where(ridx >= k, pltpu.roll(x, k, axis=0), 0.0)

    for gi in range(gpc):
        halves = []
        for hh in range(ny // 128):
            acc = None
            for s8 in range(gpc):
                rot = ((s8 - gi) * ch) % 128
                src = us_ref[hh * gpc + s8]
                if rot:
                    src = pltpu.roll(src, rot, axis=1)
                slot = (lane >= s8 * ch) & (lane < (s8 + 1) * ch)
                acc = jnp.where(slot, src, 0.0) if acc is None else jnp.where(slot, src, acc)
            halves.append(acc)
        ug = jnp.concatenate(halves, axis=1).astype(BF16)
        r = _dot(ug, wcat_ref[gi])
        y1 = r[:, :ny]
        z = r[:, ny:]
        w = shift(z, 1)
        wx = pltpu.roll(w, z.shape[1] // 2, axis=1)
        for k in range(nsteps):
            if (1 << k) >= nc:
                break
            sk = shift(w, 1 << k)
            sx = shift(wx, 1 << k)
            ar = ar_ref[gi, k:k + 1, :]
            ai = ai_ref[gi, k:k + 1, :]
            w, wx = w + sk * ar + sx * ai, wx + sx * ar - sk * ai
        yg = y1 + _dot(w.astype(BF16), wout_ref[gi])
        slot = (lane >= gi * ch) & (lane < (gi + 1) * ch)
        for t in range(L):
            src = yg[:, (t // gpc) * 128:(t // gpc + 1) * 128]
            rot = ((gi - t % gpc) * ch) % 128
            if rot:
                src = pltpu.roll(src, rot, axis=1)
            ys_ref[t] = jnp.where(slot, src, 0.0) if gi == 0 else jnp.where(slot, src, ys_ref[t])
    for t in range(L):
        y_ref[pl.ds(t, nc, stride=L), :] = ys_ref[t]


def _s5(main, col0, width, nb, seq, lam_re, lam_im, b_re, b_im, c_re, c_im, log_dt):
    t = main.shape[0]
    L = S5_CHUNK
    ch = S5_GROUP_CH
    g = width // ch
    nc = seq // L
    gpc = 128 // ch
    ncol = width // 128
    assert col0 % 128 == 0 and width % 128 == 0 and (L * ch) % 128 == 0
    nsteps = max(1, (nc - 1).bit_length())
    wcat, wout, ar, ai = _s5_weights(lam_re, lam_im, b_re, b_im, c_re, c_im, log_dt, nsteps)
    body = functools.partial(_s5_body, nsteps=nsteps)
    return pl.pallas_call(
        body, grid=(ncol, nb),
        in_specs=[pl.BlockSpec((seq, 128), lambda j, b: (b, col0 // 128 + j)),
                  pl.BlockSpec((gpc,) + wcat.shape[1:], lambda j, b: (j, 0, 0)),
                  pl.BlockSpec((gpc,) + wout.shape[1:], lambda j, b: (j, 0, 0)),
                  pl.BlockSpec((gpc,) + ar.shape[1:], lambda j, b: (j, 0, 0)),
                  pl.BlockSpec((gpc,) + ai.shape[1:], lambda j, b: (j, 0, 0))],
        out_specs=pl.BlockSpec((seq, 128), lambda j, b: (b, j)),
        out_shape=jax.ShapeDtypeStruct((t, width), F32),
        scratch_shapes=[pltpu.VMEM((L, nc, 128), F32), pltpu.VMEM((L, nc, 128), F32)],
        compiler_params=_cparams(("parallel", "parallel")), name="s5",
    )(main, wcat, wout, ar, ai)


def _hgrn_gmat():
    L = HGRN_CHUNK
    blocks = 2 + int(np.log2(L))
    gm = np.zeros((blocks * L, L), np.float32)
    for j in range(L):
        gm[j, :j + 1] = 1.0
        gm[L + j, j + 1:] = 1.0
    li, m = 2, L
    while m >= 2:
        half = m // 2
        for j in range(L):
            pos = j % m
            r = j - pos + half - 1
            if pos >= half:
                gm[li * L + j, r + 1:j + 1] = 1.0
            else:
                gm[li * L + j, j + 1:r + 1] = 1.0
        li += 1
        m //= 2
    return gm


def _hgrn_body(mh_ref, fg_ref, gm_ref, lb_ref, nw_ref, out_ref, st_ref, *, nb, nh, dh):
    L = HGRN_CHUNK
    w = nh * dh

    @pl.when(pl.program_id(0) == 0)
    def _init():
        st_ref[...] = jnp.zeros_like(st_ref)

    row = lax.broadcasted_iota(jnp.int32, (L, 2 * L), 0)
    col = lax.broadcasted_iota(jnp.int32, (L, 2 * L), 1) & (L - 1)
    rowd = lax.broadcasted_iota(jnp.int32, (L, 2 * dh), 0)
    laned = lax.broadcasted_iota(jnp.int32, (L, 2 * dh), 1)
    first = laned < dh
    eye = row == col

    def blockdiag(x):
        z = jnp.zeros_like(x)
        return jnp.concatenate([jnp.where(first, x, z), jnp.where(first, z, x)], axis=0)

    gm2 = gm_ref[...]
    lb = lb_ref[...]
    zst = jnp.zeros((dh, dh), BF16)
    for b in range(nb):
        fg = fg_ref[b]
        f = lb + (1.0 - lb) * jax.nn.sigmoid(fg)
        kk = (1.0 - lb) * jax.nn.sigmoid(-fg)
        lf = jnp.log(f)
        hi = lf.astype(BF16)
        mid = (lf - hi.astype(F32)).astype(BF16)
        p_all = jnp.exp(_dot(gm2, jnp.concatenate([hi, mid], axis=0)))
        for hp in range(nh // 2):
            i0 = b * nh + 2 * hp
            cs = slice(2 * hp * dh, (2 * hp + 2) * dh)
            qb = mh_ref[b, :, 2 * hp * dh:(2 * hp + 2) * dh]
            vb = mh_ref[b, :, w + 2 * hp * dh:w + (2 * hp + 2) * dh]
            og = mh_ref[b, :, 2 * w + 2 * hp * dh:2 * w + (2 * hp + 2) * dh].astype(F32)
            q = qb.astype(F32)
            k = kk[:, cs]
            pb = p_all[0:L, cs]
            pe = p_all[L:2 * L, cs]
            st0 = st_ref[i0]
            st1 = st_ref[i0 + 1]
            stbd = jnp.concatenate([jnp.concatenate([st0.astype(BF16), zst], axis=1),
                                    jnp.concatenate([zst, st1.astype(BF16)], axis=1)], axis=0)
            o = _dot_nt((q * pb).astype(BF16), stbd)
            attn = jnp.where(eye, _dot_nt(qb, blockdiag(k.astype(BF16))), 0.0)
            li, m = 2, L
            while m >= 2:
                pl_ = p_all[li * L:(li + 1) * L, cs]
                up = (rowd & (m - 1)) >= (m // 2)
                ql = jnp.where(up, q * pl_, 0.0).astype(BF16)
                kl = jnp.where(up, 0.0, k * pl_).astype(BF16)
                same = (row & ~(m - 1)) == (col & ~(m - 1))
                attn = attn + jnp.where(same, _dot_nt(ql, blockdiag(kl)), 0.0)
                li += 1
                m //= 2
            o = o + _dot(attn.astype(BF16), blockdiag(vb))
            kh = (k * pe).astype(BF16)
            for j in range(2):
                hs = slice(j * dh, (j + 1) * dh)
                gs = slice((2 * hp + j) * dh, (2 * hp + j + 1) * dh)
                st = st0 if j == 0 else st1
                st_ref[i0 + j] = st * pb[L - 1:L, hs] + _dot_tn(vb[:, hs], kh[:, hs])
                oj = o[:, hs]
                on = oj * lax.rsqrt(jnp.mean(oj * oj, axis=-1, keepdims=True) + RMS_EPS)
                out_ref[b, :, gs] = (on * nw_ref[:, gs] * jax.nn.silu(og[:, hs])).astype(out_ref.dtype)


def _hgrn(mh, mf, lower_bound, out_norm, nb, seq):
    t = mh.shape[0]
    nh = HGRN_HEADS
    w = out_norm.shape[0]
    dh = w // nh
    L = HGRN_CHUNK
    nc = seq // L
    gm = _hgrn_gmat()
    gm = jnp.asarray(np.concatenate([gm, gm], axis=1), BF16)
    body = functools.partial(_hgrn_body, nb=nb, nh=nh, dh=dh)
    out = pl.pallas_call(
        body, grid=(nc,),
        in_specs=[pl.BlockSpec((nb, L, 3 * w), lambda c: (0, c, 0)),
                  pl.BlockSpec((nb, L, w), lambda c: (0, c, 0)),
                  pl.BlockSpec(gm.shape, lambda c: (0, 0)),
                  pl.BlockSpec((1, w), lambda c: (0, 0)),
                  pl.BlockSpec((1, w), lambda c: (0, 0))],
        out_specs=pl.BlockSpec((nb, L, w), lambda c: (0, c, 0)),
        out_shape=jax.ShapeDtypeStruct((nb, seq, w), BF16),
        scratch_shapes=[pltpu.VMEM((nb * nh, dh, dh), F32)],
        compiler_params=_cparams(("arbitrary",)), name="hgrn",
    )(mh.reshape(nb, seq, mh.shape[1]), mf.reshape(nb, seq, mf.shape[1]), gm, lower_bound.reshape(1, w),
      out_norm.reshape(1, w))
    return out.reshape(t, w)


def _out_c_body(h_ref, ys_ref, u_ref, oh_ref, d_ref, wglu_ref, bglu_ref, w_ref, out_ref):
    ws = ys_ref.shape[1]
    z = jax.nn.gelu(ys_ref[...] + d_ref[...] * u_ref[...])
    gate = jax.nn.sigmoid(_dot(z.astype(BF16), wglu_ref[...]) + bglu_ref[...])
    out_ref[...] = (h_ref[...] + _dot((z * gate).astype(BF16), w_ref[:ws, :]) + _dot(oh_ref[...], w_ref[ws:, :]))


def _out_c(h, ys, main, oh, d_skip, w_glu, b_glu, w_out):
    t, d = h.shape
    ws = ys.shape[1]
    wh = oh.shape[1]
    tm = TM_PROJ
    ub = (main.shape[1] - ws) // ws
    return pl.pallas_call(
        _out_c_body, grid=(t // tm,),
        in_specs=[pl.BlockSpec((tm, d), lambda i: (i, 0)),
                  pl.BlockSpec((tm, ws), lambda i: (i, 0)),
                  pl.BlockSpec((tm, ws), lambda i: (i, ub)),
                  pl.BlockSpec((tm, wh), lambda i: (i, 0)),
                  pl.BlockSpec((1, ws), lambda i: (0, 0)),
                  pl.BlockSpec(w_glu.shape, lambda i: (0, 0)),
                  pl.BlockSpec((1, ws), lambda i: (0, 0)),
                  pl.BlockSpec(w_out.shape, lambda i: (0, 0))],
        out_specs=pl.BlockSpec((tm, d), lambda i: (i, 0)),
        out_shape=jax.ShapeDtypeStruct((t, d), F32),
        compiler_params=_cparams(("parallel",)), name="out_c",
    )(h, ys, main, oh, d_skip.reshape(1, ws), w_glu, b_glu.reshape(1, ws), w_out)


def kernel(x, norm_mix, norm_ffn, norm_final, ab_w_in, ab_gate_bias, ab_head_norm, ab_conv_w, ab_w_out, cd_w_in, s5_lambda_re, s5_lambda_im, s5_b_re, s5_b_im, s5_c_re, s5_c_im, s5_d, s5_log_dt, s5_w_glu, s5_b_glu, hgrn_lb, hgrn_out_norm, cd_w_out, moe_w_group, moe_b_group, moe_w_router, moe_b_router, moe_w_gate, moe_w_up, moe_w_down):
    nb, seq, d = x.shape
    depth = norm_mix.shape[0]
    h = x.reshape(nb * seq, d)
    for layer in range(depth):
        j = layer // 2
        if layer % 2 == 0:
            wm = ab_head_norm.shape[1]
            ng = ab_gate_bias.shape[1]
            w_in = ab_w_in[j]
            w_main = jnp.concatenate([w_in[:, :4 * wm], w_in[:, 4 * wm + ng:]], axis=1).astype(BF16)
            w_gates = w_in[:, 4 * wm:4 * wm + ng].astype(BF16)
            main, g, gt = _proj(h, norm_mix[layer], w_main, w_gates)
            hm = _mlstm(main, g, gt, ab_gate_bias[j], ab_head_norm[j], nb, seq)
            h = _out_a(h, hm, main, ab_conv_w[j], ab_w_out[j].astype(BF16), seq)
        else:
            ws = s5_d.shape[1]
            w_in = cd_w_in[j]
            wh = hgrn_out_norm.shape[1]
            cu, cq, cf, ci, co = (w_in[:, :ws], w_in[:, ws:ws + wh], w_in[:, ws + wh:ws + 2 * wh],
                                  w_in[:, ws + 2 * wh:ws + 3 * wh], w_in[:, ws + 3 * wh:])
            w_main = jnp.concatenate([cq, ci, co, cf, cu], axis=1).astype(BF16)
            mh, mf = _proj(h, norm_mix[layer], w_main, n_lo=3 * wh)
            sm = jax.nn.softmax(hgrn_lb.astype(F32), axis=0)
            lower_bound = jnp.cumsum(sm, axis=0)[layer] - sm[0]
            ys = _s5(mf, wh, ws, nb, seq, s5_lambda_re[j], s5_lambda_im[j], s5_b_re[j], s5_b_im[j],
                     s5_c_re[j], s5_c_im[j], s5_log_dt[j])
            oh = _hgrn(mh, mf, lower_bound, hgrn_out_norm[j], nb, seq)
            h = _out_c(h, ys, mf, oh, s5_d[j], s5_w_glu[j].astype(BF16), s5_b_glu[j], cd_w_out[j].astype(BF16))
        h = _moe(h, norm_ffn[layer], moe_w_group[layer], moe_b_group[layer], moe_w_router[layer], moe_b_router[layer],
                 moe_w_gate, moe_w_up, moe_w_down, layer,
                 norm_final if layer == depth - 1 else None)
    return h.reshape(nb, seq, d)
```
